```python
import jax, jax.numpy as jnp
from jax import lax
import numpy as np

D_MODEL = 2048
BATCH = 1
SEQ = 16384
DEPTH = 1
DEC_BATCH = 32
DEC_SEQ = 4
PAST_LEN = 16384
PAGE_SIZE = 128

N_HEADS = 16
HEAD_DIM = 64
KV_HEADS = 4
GROUP = N_HEADS // KV_HEADS
ATTN_W = N_HEADS * HEAD_DIM
KV_W = KV_HEADS * HEAD_DIM
N_NSA_BRANCH = 3
CMP_BLOCK = 32
CMP_HID = HEAD_DIM
SEL_BLOCK = 64
N_SEL = 16
WINDOW = 512
Q_BLOCK = 128
D_RNN = D_MODEL // 2
RNN_BLOCKS = 16
RNN_BLK = D_RNN // RNN_BLOCKS
CONV_W = 4
LRU_C = 8.0
D_FF = 256 * ((8 * D_MODEL // 3 + 255) // 256)
LN_EPS = 1e-5
IN_SPLITS = (D_RNN, D_RNN, ATTN_W, KV_W, KV_W, KV_W, KV_W, KV_W, KV_W, N_NSA_BRANCH * N_HEADS, D_MODEL, D_MODEL)
D_IN = 2 * D_RNN + ATTN_W + 6 * KV_W + N_NSA_BRANCH * N_HEADS + 2 * D_MODEL

kernel_name = 'hawk_nsa_macaron_deepnorm_step'


def layer_norm(x, g, b):
    xf = x.astype(jnp.float32)
    mu = jnp.mean(xf, axis=-1, keepdims=True)
    var = jnp.mean(jnp.square(xf - mu), axis=-1, keepdims=True)
    return ((xf - mu) * lax.rsqrt(var + LN_EPS) * g + b).astype(x.dtype)


def swiglu(x, w_up, w_down):
    gate, up = jnp.split(x @ w_up, 2, axis=-1)
    return (jax.nn.silu(gate) * up) @ w_down


def masked_softmax(s, mask, axes):
    s = jnp.where(mask, s.astype(jnp.float32), -jnp.inf)
    m = jnp.max(s, axis=axes, keepdims=True)
    m = jnp.where(jnp.isfinite(m), m, 0.0)
    e = jnp.where(mask, jnp.exp(s - m), 0.0)
    den = jnp.sum(e, axis=axes, keepdims=True)
    return e / jnp.maximum(den, 1e-30)


def split_in(z):
    idx = np.cumsum(IN_SPLITS)[:-1].tolist()
    return jnp.split(z, idx, axis=-1)


def rglru_branch(xr, gate, conv_prev, h0, is_start, p):
    b, t, _ = xr.shape
    xp = jnp.concatenate([conv_prev, xr], axis=1)
    xc = p['b_conv'] + sum(p['w_conv'][k] * xp[:, k:k + t] for k in range(CONV_W))
    xb = xc.reshape(b, t, RNN_BLOCKS, RNN_BLK)
    r = jax.nn.sigmoid(jnp.einsum('btnc,ncd->btnd', xb, p['w_rg_a']).reshape(b, t, D_RNN) + p['b_rg_a'])
    i = jax.nn.sigmoid(jnp.einsum('btnc,ncd->btnd', xb, p['w_rg_x']).reshape(b, t, D_RNN) + p['b_rg_x'])
    log_a = -LRU_C * r.astype(jnp.float32) * jax.nn.softplus(-p['rg_lambda'].astype(jnp.float32))
    a = jnp.exp(log_a)
    mult = jnp.where(is_start[None, :, None], 1.0, jnp.sqrt(-jnp.expm1(2.0 * log_a)))
    u = mult * (i * xc).astype(jnp.float32)

    def step(h, au):
        a_t, u_t = au
        h = a_t * h + u_t
        return h, h

    h_last, hs = lax.scan(step, h0.astype(jnp.float32), (jnp.swapaxes(a, 0, 1), jnp.swapaxes(u, 0, 1)))
    hs = jnp.swapaxes(hs, 0, 1).astype(xr.dtype)
    y = jax.nn.gelu(gate) * hs
    return y, xp[:, -(CONV_W - 1):], h_last.astype(xr.dtype)


def compress(k, w1, w2, pe):
    b, t = k.shape[:2]
    nb = t // CMP_BLOCK
    kb = k[:, :nb * CMP_BLOCK].reshape(b, nb, CMP_BLOCK, KV_HEADS, HEAD_DIM) + pe[:, None, :]
    hid = jax.nn.gelu(jnp.einsum('bnlkd,lde->bnke', kb, w1))
    return jnp.einsum('bnke,ed->bnkd', hid, w2)


def nsa_block(q, qpos, kc, vc, ks_blk, vs_blk, kw, vw, kwpos):
    b, nq = q.shape[:2]
    qg = (q * HEAD_DIM ** -0.5).reshape(b, nq, KV_HEADS, GROUP, HEAD_DIM)
    nbc = kc.shape[1]
    cmp_end = (jnp.arange(nbc) + 1) * CMP_BLOCK - 1
    mask_c = cmp_end[None, :] <= qpos[:, None]
    p_c = masked_softmax(jnp.einsum('bqkgd,bnkd->bqkgn', qg, kc), mask_c[None, :, None, None, :], -1)
    o_c = jnp.einsum('bqkgn,bnkd->bqkgd', p_c.astype(vc.dtype), vc)
    nbs = ks_blk.shape[2]
    ratio = SEL_BLOCK // CMP_BLOCK
    imp = jnp.pad(jnp.sum(p_c, axis=3), ((0, 0), (0, 0), (0, 0), (0, nbs * ratio - nbc)))
    imp = jnp.sum(imp.reshape(b, nq, KV_HEADS, nbs, ratio), axis=-1)
    blk = jnp.arange(nbs)[None, :]
    cur = (qpos // SEL_BLOCK)[:, None]
    valid = blk * SEL_BLOCK <= qpos[:, None]
    forced = (blk == 0) | (blk == cur) | (blk == cur - 1)
    imp = jnp.where(valid[None, :, None, :], jnp.where(forced[None, :, None, :], jnp.inf, imp), -jnp.inf)
    n_sel = min(N_SEL, nbs)
    _, sel = lax.top_k(imp, n_sel)
    sel_t = jnp.transpose(sel, (0, 2, 1, 3))
    gather = jax.vmap(jax.vmap(lambda blocks, idx: blocks[idx]))
    flat = sel_t.reshape(b, KV_HEADS, nq * n_sel)
    ks = gather(ks_blk, flat).reshape(b, KV_HEADS, nq, n_sel, SEL_BLOCK, HEAD_DIM)
    vs = gather(vs_blk, flat).reshape(b, KV_HEADS, nq, n_sel, SEL_BLOCK, HEAD_DIM)
    kpos = sel_t[..., None] * SEL_BLOCK + jnp.arange(SEL_BLOCK)
    mask_s = kpos <= qpos[None, None, :, None, None]
    p_s = masked_softmax(jnp.einsum('bqkgd,bkqnld->bkqgnl', qg, ks), mask_s[:, :, :, None], (-2, -1))
    o_s = jnp.einsum('bkqgnl,bkqnld->bqkgd', p_s.astype(vs.dtype), vs)
    dpos = qpos[:, None] - kwpos[None, :]
    mask_w = (dpos >= 0) & (dpos <= WINDOW) & (kwpos[None, :] >= 0)
    p_w = masked_softmax(jnp.einsum('bqkgd,blkd->bqkgl', qg, kw), mask_w[None, :, None, None, :], -1)
    o_w = jnp.einsum('bqkgl,blkd->bqkgd', p_w.astype(vw.dtype), vw)
    shp = (b, nq, N_HEADS, HEAD_DIM)
    return o_c.reshape(shp), o_s.reshape(shp), o_w.reshape(shp)


def nsa_attend(q, qpos0, k_cmp, v_cmp, k_sel, v_sel, kw, vw, kw_pos0, g_nsa, p):
    b, tq = q.shape[:2]
    t = k_sel.shape[1]
    kc = compress(k_cmp, p['w_ck1'], p['w_ck2'], p['cmp_pe'])
    vc = compress(v_cmp, p['w_cv1'], p['w_cv2'], p['cmp_pe'])
    nbs = -(-t // SEL_BLOCK)

    def to_blocks(a):
        a = jnp.pad(a, ((0, 0), (0, nbs * SEL_BLOCK - t), (0, 0), (0, 0)))
        return jnp.transpose(a.reshape(b, nbs, SEL_BLOCK, KV_HEADS, HEAD_DIM), (0, 3, 1, 2, 4))

    ks_blk, vs_blk = to_blocks(k_sel), to_blocks(v_sel)
    qb = min(Q_BLOCK, tq)

    def one_block(c):
        start = c * qb
        q_c = lax.dynamic_slice_in_dim(q, start, qb, axis=1)
        qpos = qpos0 + start + jnp.arange(qb)
        w0 = qpos0 + start - WINDOW - kw_pos0
        kw_c = lax.dynamic_slice_in_dim(kw, w0, WINDOW + qb, axis=1)
        vw_c = lax.dynamic_slice_in_dim(vw, w0, WINDOW + qb, axis=1)
        kwpos = qpos0 + start - WINDOW + jnp.arange(WINDOW + qb)
        return nsa_block(q_c, qpos, kc, vc, ks_blk, vs_blk, kw_c, vw_c, kwpos)

    o_c, o_s, o_w = lax.map(one_block, jnp.arange(tq // qb))
    unblock = lambda o: jnp.moveaxis(o, 0, 1).reshape(b, tq, N_HEADS, HEAD_DIM)
    g = jax.nn.sigmoid(g_nsa).reshape(b, tq, N_NSA_BRANCH, N_HEADS, 1)
    o = g[:, :, 0] * unblock(o_c) + g[:, :, 1] * unblock(o_s) + g[:, :, 2] * unblock(o_w)
    return o.reshape(b, tq, ATTN_W)


def token_mixer(x, past_kv, win_prev, conv_prev, h0, p):
    b, t, _ = x.shape
    qpos0 = past_kv[0].shape[1]
    xr, gt, q, kc, vc, ksel, vsel, kw, vw, g_nsa, g_rec, g_att = split_in(x @ p['w_in'])
    pos = qpos0 + jnp.arange(t)
    y_rec, conv_new, h_new = rglru_branch(xr, gt, conv_prev, h0, pos == 0, p)
    new_kv = [a.reshape(b, t, KV_HEADS, HEAD_DIM) for a in (kc, vc, ksel, vsel, kw, vw)]
    full = [jnp.concatenate([pk, nk], axis=1) for pk, nk in zip(past_kv, new_kv[:4])]
    kw_all = jnp.concatenate([win_prev[0], new_kv[4]], axis=1)
    vw_all = jnp.concatenate([win_prev[1], new_kv[5]], axis=1)
    o_attn = nsa_attend(q.reshape(b, t, N_HEADS, HEAD_DIM), qpos0, full[0], full[1], full[2], full[3],
                        kw_all, vw_all, qpos0 - win_prev[0].shape[1], g_nsa, p)
    u = jax.nn.sigmoid(g_rec) * (y_rec @ p['w_rec_o']) + jax.nn.sigmoid(g_att) * (o_attn @ p['w_attn_o'])
    states = (new_kv[0], new_kv[1], new_kv[2], new_kv[3],
              kw_all[:, -WINDOW:], vw_all[:, -WINDOW:], conv_new, h_new)
    return u @ p['w_out'], states


def setup_inputs(seed: int = 0) -> dict:
    key = jax.random.key(seed)
    keys = iter(jax.random.split(key, 48))

    def nrm(shape, scale):
        return scale * jax.random.normal(next(keys), shape, jnp.float32)

    beta = (8.0 * DEPTH) ** -0.25
    n_pages = PAST_LEN // PAGE_SIZE
    n_used = DEC_BATCH * n_pages
    n_pool = n_used + (n_used + 3) // 4
    page_table = jax.random.permutation(next(keys), n_pool)[:n_used].reshape(DEC_BATCH, n_pages).astype(jnp.int32)
    cache_shape = (n_pool, PAGE_SIZE, KV_HEADS, HEAD_DIM)
    offs = np.concatenate([[0], np.cumsum(IN_SPLITS)])
    col_scale = np.ones((D_IN,), np.float32)
    for i in (4, 6, 8):
        col_scale[offs[i]:offs[i + 1]] = beta
    u = jax.random.uniform(next(keys), (D_RNN,), jnp.float32, 0.9, 0.999)
    a0 = u ** (1.0 / LRU_C)
    rg_lambda = jnp.log(a0) - jnp.log1p(-a0)
    return {
        'x_prompt': nrm((BATCH, SEQ, D_MODEL), 1.0),
        'x_sample': nrm((DEC_BATCH, DEC_SEQ, D_MODEL), 1.0),
        'cache_k_cmp': nrm(cache_shape, 1.0),
        'cache_v_cmp': nrm(cache_shape, 1.0),
        'cache_k_sel': nrm(cache_shape, 1.0),
        'cache_v_sel': nrm(cache_shape, 1.0),
        'page_table': page_table,
        'state_win_k': nrm((DEC_BATCH, WINDOW, KV_HEADS, HEAD_DIM), 1.0),
        'state_win_v': nrm((DEC_BATCH, WINDOW, KV_HEADS, HEAD_DIM), 1.0),
        'state_conv': nrm((DEC_BATCH, CONV_W - 1, D_RNN), 1.0),
        'state_h': nrm((DEC_BATCH, D_RNN), 0.5),
        'ln1_g': 1.0 + nrm((D_MODEL,), 0.01),
        'ln1_b': nrm((D_MODEL,), 0.01),
        'w_ffn1_up': nrm((D_MODEL, 2 * D_FF), D_MODEL ** -0.5),
        'w_ffn1_down': nrm((D_FF, D_MODEL), beta * D_FF ** -0.5),
        'w_in': nrm((D_MODEL, D_IN), D_MODEL ** -0.5) * jnp.asarray(col_scale),
        'w_conv': nrm((CONV_W, D_RNN), CONV_W ** -0.5),
        'b_conv': nrm((D_RNN,), 0.01),
        'w_rg_a': nrm((RNN_BLOCKS, RNN_BLK, RNN_BLK), RNN_BLK ** -0.5),
        'b_rg_a': nrm((D_RNN,), 0.01),
        'w_rg_x': nrm((RNN_BLOCKS, RNN_BLK, RNN_BLK), RNN_BLK ** -0.5),
        'b_rg_x': nrm((D_RNN,), 0.01),
        'rg_lambda': rg_lambda,
        'cmp_pe': nrm((CMP_BLOCK, HEAD_DIM), 0.1),
        'w_ck1': nrm((CMP_BLOCK, HEAD_DIM, CMP_HID), (CMP_BLOCK * HEAD_DIM) ** -0.5),
        'w_ck2': nrm((CMP_HID, HEAD_DIM), CMP_HID ** -0.5),
        'w_cv1': nrm((CMP_BLOCK, HEAD_DIM, CMP_HID), (CMP_BLOCK * HEAD_DIM) ** -0.5),
        'w_cv2': nrm((CMP_HID, HEAD_DIM), CMP_HID ** -0.5),
        'w_rec_o': nrm((D_RNN, D_MODEL), beta * D_RNN ** -0.5),
        'w_attn_o': nrm((ATTN_W, D_MODEL), beta * ATTN_W ** -0.5),
        'w_out': nrm((D_MODEL, D_MODEL), beta * D_MODEL ** -0.5),
        'ln2_g': 1.0 + nrm((D_MODEL,), 0.01),
        'ln2_b': nrm((D_MODEL,), 0.01),
        'w_ffn2_up': nrm((D_MODEL, 2 * D_FF), D_MODEL ** -0.5),
        'w_ffn2_down': nrm((D_FF, D_MODEL), beta * D_FF ** -0.5),
        'ln3_g': 1.0 + nrm((D_MODEL,), 0.01),
        'ln3_b': nrm((D_MODEL,), 0.01),
    }


def reference(x_prompt, x_sample, cache_k_cmp, cache_v_cmp, cache_k_sel, cache_v_sel, page_table,
              state_win_k, state_win_v, state_conv, state_h,
              ln1_g, ln1_b, w_ffn1_up, w_ffn1_down, w_in, w_conv, b_conv, w_rg_a, b_rg_a, w_rg_x, b_rg_x,
              rg_lambda, cmp_pe, w_ck1, w_ck2, w_cv1, w_cv2, w_rec_o, w_attn_o, w_out,
              ln2_g, ln2_b, w_ffn2_up, w_ffn2_down, ln3_g, ln3_b):
    p = dict(w_in=w_in, w_conv=w_conv, b_conv=b_conv, w_rg_a=w_rg_a, b_rg_a=b_rg_a, w_rg_x=w_rg_x,
             b_rg_x=b_rg_x, rg_lambda=rg_lambda, cmp_pe=cmp_pe, w_ck1=w_ck1, w_ck2=w_ck2, w_cv1=w_cv1,
             w_cv2=w_cv2, w_rec_o=w_rec_o, w_attn_o=w_attn_o, w_out=w_out)
    alpha = (2.0 * DEPTH) ** 0.25

    def macaron_pre(x):
        return layer_norm(alpha * x + 0.5 * swiglu(x, w_ffn1_up, w_ffn1_down), ln1_g, ln1_b)

    def macaron_post(x, m):
        x = layer_norm(alpha * x + m, ln2_g, ln2_b)
        return layer_norm(alpha * x + 0.5 * swiglu(x, w_ffn2_up, w_ffn2_down), ln3_g, ln3_b)

    xp = macaron_pre(x_prompt)
    bp = xp.shape[0]
    empty = jnp.zeros((bp, 0, KV_HEADS, HEAD_DIM), xp.dtype)
    zero_win = jnp.zeros((bp, WINDOW, KV_HEADS, HEAD_DIM), xp.dtype)
    m_p, (p_k_cmp, p_v_cmp, p_k_sel, p_v_sel, p_win_k, p_win_v, p_conv, p_h) = token_mixer(
        xp, (empty, empty, empty, empty), (zero_win, zero_win),
        jnp.zeros((bp, CONV_W - 1, D_RNN), xp.dtype), jnp.zeros((bp, D_RNN), xp.dtype), p)
    y_prompt = macaron_post(xp, m_p)

    xs = macaron_pre(x_sample)
    bs = xs.shape[0]
    past_len = page_table.shape[1] * PAGE_SIZE
    gather_past = lambda c: c[page_table].reshape(bs, past_len, KV_HEADS, HEAD_DIM)
    m_s, (s_k_cmp, s_v_cmp, s_k_sel, s_v_sel, s_win_k, s_win_v, s_conv, s_h) = token_mixer(
        xs, (gather_past(cache_k_cmp), gather_past(cache_v_cmp), gather_past(cache_k_sel), gather_past(cache_v_sel)),
        (state_win_k, state_win_v), state_conv, state_h, p)
    y_sample = macaron_post(xs, m_s)

    return (y_prompt, y_sample, p_k_cmp, p_v_cmp, p_k_sel, p_v_sel, p_win_k, p_win_v, p_conv, p_h,
            s_k_cmp, s_v_cmp, s_k_sel, s_v_sel, s_win_k, s_win_v, s_conv, s_h)
```

```python
import functools

import numpy as np
import jax
import jax.numpy as jnp
from jax import lax
from jax.experimental import pallas as pl
from jax.experimental.pallas import tpu as pltpu

F32 = jnp.float32
BF = jnp.bfloat16

DEPTH = 1
ALPHA = (2.0 * DEPTH) ** 0.25
N_HEADS = 16
HEAD_DIM = 64
KV_HEADS = 4
GROUP = N_HEADS // KV_HEADS
KV_W = KV_HEADS * HEAD_DIM
ATTN_W = N_HEADS * HEAD_DIM
N_NSA_BRANCH = 3
CMP_BLOCK = 32
SEL_BLOCK = 64
N_SEL = 16
WINDOW = 512
Q_BLOCK = 128
CONV_W = 4
LRU_C = 8.0
RNN_BLOCKS = 16
LN_EPS = 1e-5
PAGE_SIZE = 128

LANES = 128
SUBLANES = 8
VMEM_LIMIT = 56 * 1024 * 1024

NEG = -1e30
SEL_TILE = 512
BLOCKS_PER_TILE = SEL_TILE // SEL_BLOCK
CMP_ROW = 16
CMP_ROW_W = CMP_ROW * KV_W

_NT = (((1,), (1,)), ((), ()))


def _dot(a, b):
    return jnp.dot(a, b, preferred_element_type=F32)


def _dot_nt(a, b):
    return lax.dot_general(a, b, _NT, preferred_element_type=F32)


def _params(*sem):
    return pltpu.CompilerParams(dimension_semantics=sem, vmem_limit_bytes=VMEM_LIMIT)


def _layer_norm(y, g, b):
    mu = jnp.mean(y, axis=-1, keepdims=True)
    d = y - mu
    var = jnp.mean(d * d, axis=-1, keepdims=True)
    return d * lax.rsqrt(var + LN_EPS) * g + b


def _masked_softmax(s, mask):
    s = jnp.where(mask, s, -jnp.inf)
    m = jnp.max(s, axis=-1, keepdims=True)
    m = jnp.where(m == -jnp.inf, 0.0, m)
    e = jnp.where(mask, jnp.exp(s - m), 0.0)
    den = jnp.sum(e, axis=-1, keepdims=True)
    return e / jnp.maximum(den, 1e-30)


def _split3(x):
    hi = x.astype(BF)
    r1 = x - hi.astype(F32)
    mid = r1.astype(BF)
    lo = (r1 - mid.astype(F32)).astype(BF)
    return hi, mid, lo


def _expand_f32(x, e):
    hi, mid, lo = _split3(x)
    return _dot(hi, e) + _dot(mid, e) + _dot(lo, e)


def _ffn_kernel(x_ref, wg_ref, wu_ref, wd_ref, g_ref, b_ref, o_ref, xb_ref, acc_ref):
    j = pl.program_id(1)

    @pl.when(j == 0)
    def _():
        xb_ref[...] = x_ref[...].astype(BF)
        acc_ref[...] = jnp.zeros_like(acc_ref)

    xb = xb_ref[...]
    gate = _dot(xb, wg_ref[...])
    up = _dot(xb, wu_ref[...])
    h = (gate * jax.nn.sigmoid(gate) * up).astype(BF)
    acc_ref[...] += _dot(h, wd_ref[...])

    @pl.when(j == pl.num_programs(1) - 1)
    def _():
        y = ALPHA * x_ref[...] + 0.5 * acc_ref[...]
        o_ref[...] = _layer_norm(y, g_ref[...], b_ref[...])


def _ffn_ln(x, w_up, w_down, g, b):
    m, d = x.shape
    f = w_down.shape[0]
    tm = min(512, m)
    tn = 512
    nj = f // tn
    return pl.pallas_call(
        _ffn_kernel,
        grid=(m // tm, nj),
        in_specs=[
            pl.BlockSpec((tm, d), lambda i, j: (i, 0)),
            pl.BlockSpec((d, tn), lambda i, j: (0, j)),
            pl.BlockSpec((d, tn), lambda i, j: (0, j + nj)),
            pl.BlockSpec((tn, d), lambda i, j: (j, 0)),
            pl.BlockSpec((1, d), lambda i, j: (0, 0)),
            pl.BlockSpec((1, d), lambda i, j: (0, 0)),
        ],
        out_specs=pl.BlockSpec((tm, d), lambda i, j: (i, 0)),
        out_shape=jax.ShapeDtypeStruct((m, d), F32),
        scratch_shapes=[pltpu.VMEM((tm, d), BF), pltpu.VMEM((tm, d), F32)],
        compiler_params=_params("parallel", "arbitrary"),
        name="ffn_ln",
    )(x, w_up, w_up, w_down, g.reshape(1, d), b.reshape(1, d))


def _mm_kernel(x_ref, w_ref, o_ref):
    o_ref[...] = _dot(x_ref[...].astype(BF), w_ref[...])


def _matmul(x, w, tn):
    m, d = x.shape
    n = w.shape[1]
    tm = min(512, m)
    return pl.pallas_call(
        _mm_kernel,
        grid=(m // tm, n // tn),
        in_specs=[pl.BlockSpec((tm, d), lambda i, j: (i, 0)),
                  pl.BlockSpec((d, tn), lambda i, j: (0, j))],
        out_specs=pl.BlockSpec((tm, tn), lambda i, j: (i, j)),
        out_shape=jax.ShapeDtypeStruct((m, n), F32),
        compiler_params=_params("parallel", "arbitrary"),
        name="proj",
    )(x, w)


def _proj_attn_kernel(x_ref, wq_ref, wkv_ref, wgn_ref, pq_ref, pk_ref,
                      kc_ref, vc_ref, ks_ref, vs_ref, kw_ref, vw_ref, gn_ref,
                      qa_ref, ksa_ref, vsa_ref, kwa_ref, vwa_ref):
    tm = x_ref.shape[0]
    xb = x_ref[...].astype(BF)
    zq = _dot(xb, wq_ref[...]).astype(BF)
    for h in range(N_HEADS):
        pair = zq[:, (h // 2) * LANES:(h // 2 + 1) * LANES]
        qa_ref[h] = _dot(pair, pq_ref[h % 2]).astype(BF)
    zkv = _dot(xb, wkv_ref[...])
    parts = [zkv[:, n * KV_W:(n + 1) * KV_W] for n in range(6)]
    for ref, part in zip((kc_ref, vc_ref, ks_ref, vs_ref, kw_ref, vw_ref), parts):
        ref[...] = part
    gn_ref[...] = _dot(xb, wgn_ref[...])
    t = pl.program_id(0) * tm + lax.broadcasted_iota(jnp.int32, (tm, 1), 0)
    lane = lax.broadcasted_iota(jnp.int32, (1, LANES), 1)
    onehot = jnp.where(lane == HEAD_DIM + (t // SEL_BLOCK) % BLOCKS_PER_TILE, 1.0, 0.0)
    for ref, part, extra in ((ksa_ref, parts[2], onehot), (vsa_ref, parts[3], None),
                             (kwa_ref, parts[4], None), (vwa_ref, parts[5], None)):
        pb = part.astype(BF)
        for k in range(KV_HEADS):
            v = _dot(pb, pk_ref[k])
            if extra is not None:
                v = v + extra
            ref[k] = v.astype(BF)


def _proj_attn(x, wq, wkv, wgn, pq, pk):
    m, d = x.shape
    tm = 256
    row = lambda w: pl.BlockSpec((tm, w), lambda i: (i, 0))
    const = lambda a: pl.BlockSpec(a.shape, lambda i: (0,) * a.ndim)
    head = lambda n: pl.BlockSpec((n, tm, LANES), lambda i: (0, i, 0))
    f32 = lambda w: jax.ShapeDtypeStruct((m, w), F32)
    aug = lambda n: jax.ShapeDtypeStruct((n, m, LANES), BF)
    return pl.pallas_call(
        _proj_attn_kernel,
        grid=(m // tm,),
        in_specs=[row(d), const(wq), const(wkv), const(wgn), const(pq), const(pk)],
        out_specs=[row(KV_W)] * 6 + [row(LANES), head(N_HEADS)] + [head(KV_HEADS)] * 4,
        out_shape=[f32(KV_W)] * 6 + [f32(LANES), aug(N_HEADS)] + [aug(KV_HEADS)] * 4,
        compiler_params=_params("parallel"),
        name="proj_attn",
    )(x, wq, wkv, wgn, pq, pk)


def _rglru_gates(xc, wa, wx, ba, bx, lam, is_start):
    xcb = xc.astype(BF)
    r = jax.nn.sigmoid(_dot(xcb, wa) + ba)
    ig = jax.nn.sigmoid(_dot(xcb, wx) + bx)
    log_a = -LRU_C * r * jax.nn.softplus(-lam)
    a = jnp.exp(log_a)
    if is_start is True:
        return a, ig * xc
    th = jnp.tanh(log_a)
    mult = jnp.sqrt(-2.0 * th / (1.0 - th))
    if is_start is not None:
        mult = jnp.where(is_start, 1.0, mult)
    return a, mult * (ig * xc)


def _rglru_prompt_kernel(x_ref, gate_ref, wc_ref, bc_ref, wa_ref, wx_ref, ba_ref, bx_ref, lam_ref,
                         y_ref, tail_ref, hl_ref, h_scr, tail_scr):
    c = pl.program_id(1)
    tc = x_ref.shape[0]

    @pl.when(c == 0)
    def _():
        h_scr[...] = jnp.zeros_like(h_scr)
        tail_scr[...] = jnp.zeros_like(tail_scr)

    x = x_ref[...]
    prev = tail_scr[...]
    row8 = lax.broadcasted_iota(jnp.int32, (SUBLANES, 1), 0)
    row = lax.broadcasted_iota(jnp.int32, (tc, 1), 0)

    def shifted(s):
        rolled = pltpu.roll(x, s, 0)
        top = jnp.where(row8 < s, pltpu.roll(prev, s, 0), rolled[:SUBLANES])
        return jnp.concatenate([top, rolled[SUBLANES:]], axis=0)

    wc = wc_ref[...]
    conv = wc[0:1] * shifted(3)
    conv = conv + wc[1:2] * shifted(2)
    conv = conv + wc[2:3] * shifted(1)
    conv = conv + wc[3:4] * x
    xc = bc_ref[...] + conv

    is_start = (row + c * tc) == 0
    a, u = _rglru_gates(xc, wa_ref[0], wx_ref[0], ba_ref[...], bx_ref[...], lam_ref[...], is_start)

    d = 1
    while d < tc:
        keep = row >= d
        a_sh = jnp.where(keep, pltpu.roll(a, d, 0), 1.0)
        u_sh = jnp.where(keep, pltpu.roll(u, d, 0), 0.0)
        u = a * u_sh + u
        a = a * a_sh
        d *= 2
    h = a * h_scr[...] + u

    y_ref[...] = (jax.nn.gelu(gate_ref[...]) * h).astype(BF)
    h_last = h[tc - 1:tc]
    h_scr[...] = h_last
    tail_scr[...] = x[tc - SUBLANES:]
    hl_ref[...] = h_last
    tail_ref[...] = x[tc - SUBLANES:]


def _rglru_prompt(xrg, w_conv, b_conv, wa_bd, wx_bd, b_a, b_x, lam):
    t = xrg.shape[0]
    d_rnn = w_conv.shape[1]
    gw = 2 * LANES
    ng = d_rnn // gw
    tc = min(512, t)
    vec = lambda: pl.BlockSpec((1, gw), lambda g, c: (0, g))
    return pl.pallas_call(
        _rglru_prompt_kernel,
        grid=(ng, t // tc),
        in_specs=[
            pl.BlockSpec((tc, gw), lambda g, c: (c, g)),
            pl.BlockSpec((tc, gw), lambda g, c: (c, g + ng)),
            pl.BlockSpec((CONV_W, gw), lambda g, c: (0, g)),
            vec(),
            pl.BlockSpec((1, gw, gw), lambda g, c: (g, 0, 0)),
            pl.BlockSpec((1, gw, gw), lambda g, c: (g, 0, 0)),
            vec(), vec(), vec(),
        ],
        out_specs=[
            pl.BlockSpec((tc, gw), lambda g, c: (c, g)),
            pl.BlockSpec((SUBLANES, gw), lambda g, c: (0, g)),
            pl.BlockSpec((1, gw), lambda g, c: (0, g)),
        ],
        out_shape=[
            jax.ShapeDtypeStruct((t, d_rnn), BF),
            jax.ShapeDtypeStruct((SUBLANES, d_rnn), F32),
            jax.ShapeDtypeStruct((1, d_rnn), F32),
        ],
        scratch_shapes=[pltpu.VMEM((1, gw), F32), pltpu.VMEM((SUBLANES, gw), F32)],
        compiler_params=_params("parallel", "arbitrary"),
        name="rglru_prompt",
    )(xrg, xrg, w_conv, b_conv.reshape(1, -1), wa_bd, wx_bd,
      b_a.reshape(1, -1), b_x.reshape(1, -1), lam.reshape(1, -1))


def _rglru_sample_kernel(xrg_ref, cp_ref, h0_ref, wc_ref, bc_ref, wa_ref, wx_ref, ba_ref, bx_ref, lam_ref,
                         y_ref, cn_ref, hl_ref, *, n_tok, start0):
    d_rnn = h0_ref.shape[1]
    gw = wa_ref.shape[1]
    wc = wc_ref[...]
    xp = [cp_ref[:, k * d_rnn:(k + 1) * d_rnn] for k in range(CONV_W - 1)]
    xp += [xrg_ref[:, t * 2 * d_rnn:t * 2 * d_rnn + d_rnn] for t in range(n_tok)]
    h = h0_ref[...]
    for t in range(n_tok):
        conv = wc[0:1] * xp[t]
        for k in range(1, CONV_W):
            conv = conv + wc[k:k + 1] * xp[t + k]
        xc = bc_ref[...] + conv
        a_parts, u_parts = [], []
        for g in range(d_rnn // gw):
            sl = slice(g * gw, (g + 1) * gw)
            a_g, u_g = _rglru_gates(xc[:, sl], wa_ref[g], wx_ref[g], ba_ref[:, sl], bx_ref[:, sl],
                                    lam_ref[:, sl], True if (start0 and t == 0) else None)
            a_parts.append(a_g)
            u_parts.append(u_g)
        a = jnp.concatenate(a_parts, axis=1)
        u = jnp.concatenate(u_parts, axis=1)
        h = a * h + u
        gate = xrg_ref[:, t * 2 * d_rnn + d_rnn:(t + 1) * 2 * d_rnn]
        y_ref[:, t * d_rnn:(t + 1) * d_rnn] = (jax.nn.gelu(gate) * h).astype(BF)
    hl_ref[...] = h
    tail = xp[-(CONV_W - 1):]
    for k in range(CONV_W - 1):
        cn_ref[:, k * d_rnn:(k + 1) * d_rnn] = tail[k]


def _rglru_sample(xrg, conv_prev, h0, w_conv, b_conv, wa_bd, wx_bd, b_a, b_x, lam, start0):
    b, d_rnn = h0.shape
    n_tok = xrg.shape[1] // (2 * d_rnn)
    args = (xrg, conv_prev, h0, w_conv, b_conv.reshape(1, -1), wa_bd, wx_bd,
            b_a.reshape(1, -1), b_x.reshape(1, -1), lam.reshape(1, -1))
    full = lambda a: pl.BlockSpec(a.shape, lambda i: (0,) * a.ndim)
    outs = [jax.ShapeDtypeStruct((b, n_tok * d_rnn), BF),
            jax.ShapeDtypeStruct((b, (CONV_W - 1) * d_rnn), F32),
            jax.ShapeDtypeStruct((b, d_rnn), F32)]
    return pl.pallas_call(
        functools.partial(_rglru_sample_kernel, n_tok=n_tok, start0=start0),
        grid=(1,),
        in_specs=[full(a) for a in args],
        out_specs=[full(o) for o in outs],
        out_shape=outs,
        compiler_params=_params("arbitrary"),
        name="rglru_sample",
    )(*args)


def _compress_kernel(*refs, n_in, per_head):
    if not per_head:
        refs = refs[1:]
    x_refs = refs[:n_in]
    pe_ref, w1_ref, w2_ref = refs[n_in:n_in + 3]
    rest = refs[n_in + 3:]
    if per_head:
        pk_ref, o_ref, scr = rest
    else:
        o_ref, scr = rest
    if n_in == 1:
        x = x_refs[0][...]
    else:
        x = jnp.concatenate([r[0] for r in x_refs], axis=0)
    rows = x.shape[0]
    xb = (x.reshape(rows // SUBLANES, SUBLANES, CMP_ROW_W) + pe_ref[...][None]).reshape(rows, CMP_ROW_W)
    full = _dot(xb.astype(BF), w1_ref[...])
    hid = full[:, :KV_W] + pltpu.roll(full[:, KV_W:], rows - 1, 0)
    out = _dot(jax.nn.gelu(hid).astype(BF), w2_ref[...])
    nb = rows // 4
    if per_head:
        ob = out.astype(BF)
        for k in range(KV_HEADS):
            scr[...] = _dot(ob, pk_ref[k])
            o_ref[k, 0] = scr[pl.ds(0, nb, stride=4), :].astype(BF)
            o_ref[k, 1] = scr[pl.ds(2, nb, stride=4), :].astype(BF)
    else:
        for c in range(KV_W // LANES):
            scr[...] = out[:, c * LANES:(c + 1) * LANES]
            o_ref[0, 0, :, c * LANES:(c + 1) * LANES] = scr[pl.ds(0, nb, stride=4), :].astype(BF)
            o_ref[0, 1, :, c * LANES:(c + 1) * LANES] = scr[pl.ds(2, nb, stride=4), :].astype(BF)


def _compress_prompt(kv, pe2, w1, w2, pk):
    t = kv.shape[0]
    rows = 128
    x = kv.reshape(t // CMP_ROW, CMP_ROW_W)
    nb = rows // 4
    nbs = t // SEL_BLOCK
    const = lambda a: pl.BlockSpec(a.shape, lambda i: (0,) * a.ndim)
    return pl.pallas_call(
        functools.partial(_compress_kernel, n_in=1, per_head=True),
        grid=(x.shape[0] // rows,),
        in_specs=[pl.BlockSpec((rows, CMP_ROW_W), lambda i: (i, 0)),
                  const(pe2), const(w1), const(w2), const(pk)],
        out_specs=pl.BlockSpec((KV_HEADS, 2, nb, LANES), lambda i: (0, 0, i, 0)),
        out_shape=jax.ShapeDtypeStruct((KV_HEADS, 2, nbs, LANES), BF),
        scratch_shapes=[pltpu.VMEM((rows, LANES), F32)],
        compiler_params=_params("parallel"),
        name="compress_prompt",
    )(x, pe2, w1, w2, pk)


def _compress_paged(cache, page_table, pe2, w1, w2):
    b, n_pages = page_table.shape
    pages = 16
    rows_per_page = PAGE_SIZE // CMP_ROW
    x = cache.reshape(cache.shape[0], rows_per_page, CMP_ROW_W)
    rows = pages * rows_per_page
    nb = rows // 4
    nbs = n_pages * PAGE_SIZE // SEL_BLOCK
    const = lambda a: pl.BlockSpec(a.shape, lambda i, s, pt: (0,) * a.ndim)
    page_specs = [
        pl.BlockSpec((1, rows_per_page, CMP_ROW_W),
                     functools.partial(lambda i, s, pt, p: (pt[i, s * pages + p], 0, 0), p=p))
        for p in range(pages)]
    return pl.pallas_call(
        functools.partial(_compress_kernel, n_in=pages, per_head=False),
        grid_spec=pltpu.PrefetchScalarGridSpec(
            num_scalar_prefetch=1,
            grid=(b, n_pages // pages),
            in_specs=page_specs + [const(pe2), const(w1), const(w2)],
            out_specs=pl.BlockSpec((1, 2, nb, KV_W), lambda i, s, pt: (i, 0, s, 0)),
            scratch_shapes=[pltpu.VMEM((rows, LANES), F32)],
        ),
        out_shape=jax.ShapeDtypeStruct((b, 2, nbs, KV_W), BF),
        compiler_params=_params("parallel", "arbitrary"),
        name="compress_paged",
    )(page_table, *([x] * pages), pe2, w1, w2)


def _topk_columns(imp_t, n_pick):
    nblk = imp_t.shape[0]
    idx = lax.broadcasted_iota(jnp.int32, imp_t.shape, 0)
    sel = jnp.zeros(imp_t.shape, F32)
    for _ in range(n_pick):
        mx = jnp.max(imp_t, axis=0, keepdims=True)
        first = jnp.min(jnp.where(imp_t == mx, idx, nblk), axis=0, keepdims=True)
        hit = idx == first
        sel = jnp.where(hit, 1.0, sel)
        imp_t = jnp.where(hit, -jnp.inf, imp_t)
    return sel


def _nsa_prompt_kernel(*refs):
    (q_ref, kc_ref, vc_ref, ks_ref, vs_ref) = refs[:5]
    kw_refs = refs[5:10]
    vw_refs = refs[10:15]
    gn_ref, eg_ref, rp_ref, pa_ref, o_ref, bias_scr = refs[15:]
    i = pl.program_id(0)
    qb_rows = Q_BLOCK
    r = GROUP * qb_rows
    q = q_ref[...].reshape(r, LANES)
    row = lax.broadcasted_iota(jnp.int32, (r, 1), 0)
    qpos = i * qb_rows + row % qb_rows

    kc = kc_ref[0]
    n2 = kc.shape[0]
    nbs = n2 // 2
    s = _dot_nt(q, kc)
    col = lax.broadcasted_iota(jnp.int32, (1, n2), 1)
    blk = 2 * (col % nbs) + col // nbs
    p_c = _masked_softmax(s, blk * CMP_BLOCK + (CMP_BLOCK - 1) <= qpos)
    o_c = _dot(p_c.astype(BF), vc_ref[0])

    imp = p_c[0:qb_rows]
    for g in range(1, GROUP):
        imp = imp + p_c[g * qb_rows:(g + 1) * qb_rows]
    imp = imp[:, :nbs] + imp[:, nbs:]
    qp = i * qb_rows + lax.broadcasted_iota(jnp.int32, (qb_rows, 1), 0)
    jb = lax.broadcasted_iota(jnp.int32, (1, nbs), 1)
    cur = qp // SEL_BLOCK
    valid = jb * SEL_BLOCK <= qp
    forced = (jb == 0) | (jb == cur) | (jb == cur - 1)
    imp = jnp.where(valid, jnp.where(forced, jnp.inf, imp), -jnp.inf)
    sel = _topk_columns(imp.T, min(N_SEL, nbs)).T
    bias = jnp.where(valid & (sel > 0.5), 0.0, NEG).astype(BF)
    bias_all = _dot(bias, pa_ref[...]).astype(BF)
    n_tiles = bias_scr.shape[0]
    for t in range(n_tiles):
        bias_scr[t] = bias_all[:, t * LANES:(t + 1) * LANES]

    def tile(t, carry, diagonal):
        m, l, acc = carry
        k0 = pl.multiple_of(t * SEL_TILE, SEL_TILE)
        kt = ks_ref[0, pl.ds(k0, SEL_TILE), :]
        vt = vs_ref[0, pl.ds(k0, SEL_TILE), :]
        bt = bias_scr[t]
        qa = q + jnp.concatenate([bt] * GROUP, axis=0)
        st = _dot_nt(qa, kt)
        if diagonal:
            kpos = k0 + lax.broadcasted_iota(jnp.int32, (1, SEL_TILE), 1)
            st = jnp.where(kpos <= qpos, st, NEG)
        m_new = jnp.maximum(m, jnp.max(st, axis=1, keepdims=True))
        scale = jnp.exp(m - m_new)
        p = jnp.exp(st - m_new)
        l = scale * l + jnp.sum(p, axis=1, keepdims=True)
        acc = scale * acc + _dot(p.astype(BF), vt)
        return m_new, l, acc

    t_diag = (i * qb_rows) // SEL_TILE
    init = (jnp.full((r, 1), NEG, F32), jnp.zeros((r, 1), F32), jnp.zeros((r, LANES), F32))
    carry = lax.fori_loop(0, t_diag, lambda t, c: tile(t, c, False), init)
    _, l, acc = tile(t_diag, carry, True)
    o_s = acc / l

    kw = jnp.concatenate([ref[0] for ref in kw_refs], axis=0)
    vw = jnp.concatenate([ref[0] for ref in vw_refs], axis=0)
    n_w = kw.shape[0]
    kwpos = (i - WINDOW // qb_rows) * qb_rows + lax.broadcasted_iota(jnp.int32, (1, n_w), 1)
    dpos = qpos - kwpos
    p_w = _masked_softmax(_dot_nt(q, kw), (dpos >= 0) & (dpos <= WINDOW) & (kwpos >= 0))
    o_w = _dot(p_w.astype(BF), vw)

    gexp = _expand_f32(jax.nn.sigmoid(gn_ref[...]), eg_ref[0])

    def gate(br):
        return jnp.concatenate(
            [gexp[:, (br * GROUP + g) * LANES:(br * GROUP + g + 1) * LANES] for g in range(GROUP)], axis=0)

    comb = (gate(0) * o_c + gate(1) * o_s + gate(2) * o_w).astype(BF)
    out = _dot(comb[0:qb_rows], rp_ref[0])
    for g in range(1, GROUP):
        out = out + _dot(comb[g * qb_rows:(g + 1) * qb_rows], rp_ref[g])
    o_ref[...] = out.astype(BF)


def _nsa_prompt(qa, kc, vc, ksa, vsa, kwa, vwa, gn, egate, rplace, pall):
    t = qa.shape[1]
    n_tiles = t // SEL_TILE
    wb = WINDOW // Q_BLOCK + 1
    kv_full = lambda a: pl.BlockSpec((1,) + a.shape[1:], lambda i, k: (k, 0, 0))
    win = [pl.BlockSpec((1, Q_BLOCK, LANES),
                        functools.partial(lambda i, k, m: (k, jnp.maximum(i - (wb - 1) + m, 0), 0), m=m))
           for m in range(wb)]
    const = lambda a: pl.BlockSpec(a.shape, lambda i, k: (0,) * a.ndim)
    return pl.pallas_call(
        _nsa_prompt_kernel,
        grid=(t // Q_BLOCK, KV_HEADS),
        in_specs=[pl.BlockSpec((GROUP, Q_BLOCK, LANES), lambda i, k: (k, i, 0)),
                  kv_full(kc), kv_full(vc), kv_full(ksa), kv_full(vsa)]
                 + win + win
                 + [pl.BlockSpec((Q_BLOCK, LANES), lambda i, k: (i, 0)),
                    pl.BlockSpec((1,) + egate.shape[1:], lambda i, k: (k, 0, 0)),
                    const(rplace), const(pall)],
        out_specs=pl.BlockSpec((Q_BLOCK, KV_W), lambda i, k: (i, k)),
        out_shape=jax.ShapeDtypeStruct((t, ATTN_W), BF),
        scratch_shapes=[pltpu.VMEM((n_tiles, Q_BLOCK, LANES), BF)],
        compiler_params=_params("parallel", "arbitrary"),
        name="nsa_prompt",
    )(qa, kc, vc, ksa, vsa, *([kwa] * wb), *([vwa] * wb), gn, egate, rplace, pall)


def _nsa_sample_kernel(*refs, pages, past, n_tok):
    pt_ref = refs[0]
    q_ref, kc_ref, vc_ref = refs[1:4]
    ks_refs = refs[4:4 + pages]
    vs_refs = refs[4 + pages:4 + 2 * pages]
    (ksn_ref, vsn_ref, kwn_ref, vwn_ref, wk_ref, wv_ref, gn_ref,
     pg_ref, rp_ref, eg_ref, o_ref, q_scr, bias_scr, m_scr, l_scr, acc_scr, oc_scr) = refs[4 + 2 * pages:]
    del pt_ref
    s_id = pl.program_id(1)
    tp = SUBLANES
    r = N_HEADS * tp
    row = lax.broadcasted_iota(jnp.int32, (r, 1), 0)
    qpos = past + row % tp
    nbs = bias_scr.shape[1]

    @pl.when(s_id == 0)
    def _():
        qb = q_ref[0].astype(BF)
        for h in range(N_HEADS):
            k = h // GROUP
            q_scr[h * tp:(h + 1) * tp, :] = _dot(qb[:, k * KV_W:(k + 1) * KV_W], pg_ref[h]).astype(BF)
        q = q_scr[...]
        kc = kc_ref[0].reshape(2 * nbs, KV_W)
        vc = vc_ref[0].reshape(2 * nbs, KV_W)
        col = lax.broadcasted_iota(jnp.int32, (1, 2 * nbs), 1)
        blk = 2 * (col % nbs) + col // nbs
        p_c = _masked_softmax(_dot_nt(q, kc), blk * CMP_BLOCK + (CMP_BLOCK - 1) <= qpos)
        oc_scr[...] = _dot(p_c.astype(BF), vc)
        imps = []
        for k in range(KV_HEADS):
            base = k * GROUP * tp
            imp = p_c[base:base + tp]
            for g in range(1, GROUP):
                imp = imp + p_c[base + g * tp:base + (g + 1) * tp]
            imps.append(imp)
        imp = jnp.concatenate(imps, axis=0)
        imp = imp[:, :nbs] + imp[:, nbs:]
        jb = lax.broadcasted_iota(jnp.int32, (1, nbs), 1)
        imp = jnp.where((jb == 0) | (jb == nbs - 1), jnp.inf, imp)
        idx = lax.broadcasted_iota(jnp.int32, imp.shape, 1)
        sel = jnp.zeros(imp.shape, F32)
        for _ in range(min(N_SEL, nbs + 1) - 1):
            mx = jnp.max(imp, axis=1, keepdims=True)
            first = jnp.min(jnp.where(imp == mx, idx, nbs), axis=1, keepdims=True)
            hit = idx == first
            sel = jnp.where(hit, 1.0, sel)
            imp = jnp.where(hit, -jnp.inf, imp)
        bias = jnp.where(sel > 0.5, 0.0, NEG).astype(BF)
        for k in range(KV_HEADS):
            for g in range(GROUP):
                h = k * GROUP + g
                bias_scr[h * tp:(h + 1) * tp, :] = bias[k * tp:(k + 1) * tp]
        m_scr[...] = jnp.full(m_scr.shape, NEG, F32)
        l_scr[...] = jnp.zeros_like(l_scr)
        acc_scr[...] = jnp.zeros_like(acc_scr)

    def online(st, v):
        m = m_scr[...]
        m_new = jnp.maximum(m, jnp.max(st, axis=1, keepdims=True))
        scale = jnp.exp(m - m_new)
        p = jnp.exp(st - m_new)
        l_scr[...] = scale * l_scr[...] + jnp.sum(p, axis=1, keepdims=True)
        acc_scr[...] = scale * acc_scr[...] + _dot(p.astype(BF), v)
        m_scr[...] = m_new

    q = q_scr[...]
    kt = jnp.concatenate([ref[0].astype(BF) for ref in ks_refs], axis=0)
    vt = jnp.concatenate([ref[0].astype(BF) for ref in vs_refs], axis=0)
    n_keys = kt.shape[0]
    jb = lax.broadcasted_iota(jnp.int32, (nbs, 1), 0)
    key_blk = s_id * (n_keys // SEL_BLOCK) + lax.broadcasted_iota(jnp.int32, (1, n_keys), 1) // SEL_BLOCK
    expand = jnp.where(jb == key_blk, 1.0, 0.0).astype(BF)
    online(_dot_nt(q, kt) + _dot(bias_scr[...], expand), vt)

    @pl.when(s_id == pl.num_programs(1) - 1)
    def _():
        zeros = jnp.zeros((LANES - tp, KV_W), BF)
        tok = lax.broadcasted_iota(jnp.int32, (1, LANES), 1)
        ksn = jnp.concatenate([ksn_ref[0].astype(BF), zeros], axis=0)
        vsn = jnp.concatenate([vsn_ref[0].astype(BF), zeros], axis=0)
        online(jnp.where((past + tok <= qpos) & (tok < n_tok), _dot_nt(q, ksn), NEG), vsn)
        o_s = acc_scr[...] / l_scr[...]
        kw = jnp.concatenate([wk_ref[0].astype(BF), kwn_ref[0].astype(BF), zeros], axis=0)
        vw = jnp.concatenate([wv_ref[0].astype(BF), vwn_ref[0].astype(BF), zeros], axis=0)
        n_w = kw.shape[0]
        wcol = lax.broadcasted_iota(jnp.int32, (1, n_w), 1)
        kwpos = past - WINDOW + wcol
        dpos = qpos - kwpos
        mask_w = (dpos >= 0) & (dpos <= WINDOW) & (kwpos >= 0) & (wcol < WINDOW + n_tok)
        p_w = _masked_softmax(_dot_nt(q, kw), mask_w)
        o_w = _dot(p_w.astype(BF), vw)
        sg = jax.nn.sigmoid(gn_ref[0])
        gexp = _expand_f32(sg, eg_ref[...])

        def gate(br):
            return jnp.concatenate(
                [gexp[:, (br * N_HEADS + h) * KV_W:(br * N_HEADS + h + 1) * KV_W] for h in range(N_HEADS)],
                axis=0)

        comb = (gate(0) * oc_scr[...] + gate(1) * o_s + gate(2) * o_w).astype(BF)
        for k in range(KV_HEADS):
            out = None
            for g in range(GROUP):
                h = k * GROUP + g
                part = _dot(comb[h * tp:(h + 1) * tp], rp_ref[h])
                out = part if out is None else out + part
            o_ref[0, :, k * KV_W:(k + 1) * KV_W] = out


def _nsa_sample(page_table, q8, kc, vc, cache_ks, cache_vs, new8, win_k, win_v, gn8, pgk, rkg, egate, n_tok):
    b, n_pages = page_table.shape
    past = n_pages * PAGE_SIZE
    pages = 8
    tp = SUBLANES
    r = N_HEADS * tp
    nbs = past // SEL_BLOCK
    ck = cache_ks.reshape(cache_ks.shape[0], PAGE_SIZE, KV_W)
    cv = cache_vs.reshape(cache_vs.shape[0], PAGE_SIZE, KV_W)
    per_seq = lambda a: pl.BlockSpec((1,) + a.shape[1:], lambda i, s, pt: (i,) + (0,) * (a.ndim - 1))
    const = lambda a: pl.BlockSpec(a.shape, lambda i, s, pt: (0,) * a.ndim)
    page_specs = [
        pl.BlockSpec((1, PAGE_SIZE, KV_W),
                     functools.partial(lambda i, s, pt, p: (pt[i, s * pages + p], 0, 0), p=p))
        for p in range(pages)]
    return pl.pallas_call(
        functools.partial(_nsa_sample_kernel, pages=pages, past=past, n_tok=n_tok),
        grid_spec=pltpu.PrefetchScalarGridSpec(
            num_scalar_prefetch=1,
            grid=(b, n_pages // pages),
            in_specs=[per_seq(q8), per_seq(kc), per_seq(vc)] + page_specs + page_specs
                     + [per_seq(a) for a in new8] + [per_seq(win_k), per_seq(win_v), per_seq(gn8),
                                                     const(pgk), const(rkg), const(egate)],
            out_specs=pl.BlockSpec((1, tp, ATTN_W), lambda i, s, pt: (i, 0, 0)),
            scratch_shapes=[pltpu.VMEM((r, KV_W), BF), pltpu.VMEM((r, nbs), BF),
                            pltpu.VMEM((r, 1), F32), pltpu.VMEM((r, 1), F32),
                            pltpu.VMEM((r, KV_W), F32), pltpu.VMEM((r, KV_W), F32)],
        ),
        out_shape=jax.ShapeDtypeStruct((b, tp, ATTN_W), F32),
        compiler_params=_params("parallel", "arbitrary"),
        name="nsa_sample",
    )(page_table, q8, kc, vc, *([ck] * pages), *([cv] * pages), *new8, win_k, win_v, gn8, pgk, rkg, egate)


def _merge_kernel(x_ref, y_ref, o_ref, wgr_ref, wga_ref, wr_ref, wa_ref, wo_ref, g_ref, b_ref,
                  out_ref, xb_ref, acc_ref):
    j = pl.program_id(1)

    @pl.when(j == 0)
    def _():
        xb_ref[...] = x_ref[...].astype(BF)
        acc_ref[...] = jnp.zeros_like(acc_ref)

    xb = xb_ref[...]
    rec = _dot(y_ref[...], wr_ref[...])
    att = _dot(o_ref[...], wa_ref[...])
    u = jax.nn.sigmoid(_dot(xb, wgr_ref[...])) * rec + jax.nn.sigmoid(_dot(xb, wga_ref[...])) * att
    acc_ref[...] += _dot(u.astype(BF), wo_ref[...])

    @pl.when(j == pl.num_programs(1) - 1)
    def _():
        out_ref[...] = _layer_norm(ALPHA * x_ref[...] + acc_ref[...], g_ref[...], b_ref[...])


def _merge(x, y_rec, o_attn, w_grec, w_gatt, w_rec_o, w_attn_o, w_out, g, b):
    m, d = x.shape
    tm = min(512, m)
    tn = 512
    col = lambda a: pl.BlockSpec((a.shape[0], tn), lambda i, j: (0, j))
    rowb = lambda a: pl.BlockSpec((tm, a.shape[1]), lambda i, j: (i, 0))
    vec = pl.BlockSpec((1, d), lambda i, j: (0, 0))
    return pl.pallas_call(
        _merge_kernel,
        grid=(m // tm, d // tn),
        in_specs=[rowb(x), rowb(y_rec), rowb(o_attn), col(w_grec), col(w_gatt), col(w_rec_o), col(w_attn_o),
                  pl.BlockSpec((tn, d), lambda i, j: (j, 0)), vec, vec],
        out_specs=pl.BlockSpec((tm, d), lambda i, j: (i, 0)),
        out_shape=jax.ShapeDtypeStruct((m, d), F32),
        scratch_shapes=[pltpu.VMEM((tm, d), BF), pltpu.VMEM((tm, d), F32)],
        compiler_params=_params("parallel", "arbitrary"),
        name="merge_out",
    )(x, y_rec, o_attn, w_grec, w_gatt, w_rec_o, w_attn_o, w_out, g.reshape(1, d), b.reshape(1, d))


def _place(n_rows, n_cols, src0, dst0, width, value=1.0):
    m = np.zeros((n_rows, n_cols), np.float32)
    m[src0 + np.arange(width), dst0 + np.arange(width)] = value
    return m


def _layout_constants(n_sel_blocks_prompt):
    hd = HEAD_DIM
    scale = hd ** -0.5
    pq = np.stack([_place(LANES, LANES, e * hd, 0, hd, scale) for e in range(2)])
    pk = np.stack([_place(KV_W, LANES, k * hd, 0, hd) for k in range(KV_HEADS)])
    rplace = np.stack([_place(LANES, KV_W, 0, g * hd, hd) for g in range(GROUP)])
    egate = np.zeros((KV_HEADS, LANES, N_NSA_BRANCH * GROUP * LANES), np.float32)
    for k in range(KV_HEADS):
        for br in range(N_NSA_BRANCH):
            for g in range(GROUP):
                c = br * N_HEADS + k * GROUP + g
                egate[k, c, (br * GROUP + g) * LANES:(br * GROUP + g + 1) * LANES] = 1.0
    nbs = n_sel_blocks_prompt
    n_tiles = -(-nbs // BLOCKS_PER_TILE)
    pall = np.zeros((nbs, n_tiles * LANES), np.float32)
    j = np.arange(nbs)
    pall[j, (j // BLOCKS_PER_TILE) * LANES + hd + j % BLOCKS_PER_TILE] = 1.0
    pgk = np.stack([_place(KV_W, KV_W, (h % GROUP) * hd, (h // GROUP) * hd, hd, scale) for h in range(N_HEADS)])
    rkg = np.stack([_place(KV_W, KV_W, (h // GROUP) * hd, (h % GROUP) * hd, hd) for h in range(N_HEADS)])
    egs = np.zeros((LANES, N_NSA_BRANCH * N_HEADS * KV_W), np.float32)
    for c in range(N_NSA_BRANCH * N_HEADS):
        egs[c, c * KV_W:(c + 1) * KV_W] = 1.0
    as_bf = lambda a: jnp.asarray(a, BF)
    return dict(pq=as_bf(pq), pk=as_bf(pk), rplace=as_bf(rplace), egate=as_bf(egate), pall=as_bf(pall),
                pgk=as_bf(pgk), rkg=as_bf(rkg), egs=as_bf(egs))


def _block_diag(w, per_group):
    nb, c, _ = w.shape
    eye = jnp.eye(per_group, dtype=w.dtype)
    wg = w.reshape(nb // per_group, per_group, c, c)
    return jnp.einsum('gpcd,pq->gpcqd', wg, eye).reshape(nb // per_group, per_group * c, per_group * c)


def _compress_weights(w1, w2, pe):
    eye = jnp.eye(KV_HEADS, dtype=w1.dtype)
    big = jnp.einsum('lde,kq->lkdqe', w1, eye).reshape(CMP_BLOCK * KV_W, KV_HEADS * w1.shape[2])
    half = big.shape[0] // 2
    w1cat = jnp.concatenate([big[:half], big[half:]], axis=1).astype(BF)
    w2bd = _block_diag(jnp.broadcast_to(w2, (KV_HEADS,) + w2.shape), KV_HEADS)[0].astype(BF)
    pe_flat = jnp.broadcast_to(pe[:, None, :], (CMP_BLOCK, KV_HEADS, HEAD_DIM)).reshape(2, half)
    pe2 = jnp.tile(pe_flat, (SUBLANES // 2, 1))
    return w1cat, w2bd, pe2


def kernel(x_prompt, x_sample, cache_k_cmp, cache_v_cmp, cache_k_sel, cache_v_sel, page_table,
           state_win_k, state_win_v, state_conv, state_h,
           ln1_g, ln1_b, w_ffn1_up, w_ffn1_down, w_in, w_conv, b_conv, w_rg_a, b_rg_a, w_rg_x, b_rg_x,
           rg_lambda, cmp_pe, w_ck1, w_ck2, w_cv1, w_cv2, w_rec_o, w_attn_o, w_out,
           ln2_g, ln2_b, w_ffn2_up, w_ffn2_down, ln3_g, ln3_b):
    bp, t, d = x_prompt.shape
    bs, n_tok, _ = x_sample.shape
    d_rnn = w_conv.shape[1]
    assert bp == 1 and t % (128 * CMP_ROW) == 0 and t >= WINDOW
    past = page_table.shape[1] * PAGE_SIZE
    assert page_table.shape[1] % 16 == 0 and n_tok <= SUBLANES and n_tok >= CONV_W - 1
    assert state_win_k.shape[1] == WINDOW

    o_q = 2 * d_rnn
    o_kv = o_q + ATTN_W
    o_gn = o_kv + 6 * KV_W
    o_gr = o_gn + N_NSA_BRANCH * N_HEADS
    o_ga = o_gr + d
    w_in_b = w_in.astype(BF)
    w_rnn = w_in_b[:, :o_q]
    w_q = w_in_b[:, o_q:o_kv]
    w_kv = w_in_b[:, o_kv:o_gn]
    w_gn = jnp.pad(w_in_b[:, o_gn:o_gr], ((0, 0), (0, LANES - (o_gr - o_gn))))
    w_grec = w_in_b[:, o_gr:o_ga]
    w_gatt = w_in_b[:, o_ga:]
    w_attn_all = jnp.concatenate([w_in_b[:, :o_gn], jnp.pad(w_gn, ((0, 0), (0, 512 - LANES)))], axis=1)
    up1, down1 = w_ffn1_up.astype(BF), w_ffn1_down.astype(BF)
    up2, down2 = w_ffn2_up.astype(BF), w_ffn2_down.astype(BF)
    per_group = 2 * LANES // (d_rnn // RNN_BLOCKS)
    wa_bd = _block_diag(w_rg_a, per_group).astype(BF)
    wx_bd = _block_diag(w_rg_x, per_group).astype(BF)
    ck1, ck2, pe2 = _compress_weights(w_ck1, w_ck2, cmp_pe)
    cv1, cv2, _ = _compress_weights(w_cv1, w_cv2, cmp_pe)
    w_rec_b, w_attn_b, w_out_b = w_rec_o.astype(BF), w_attn_o.astype(BF), w_out.astype(BF)
    cst = _layout_constants(t // SEL_BLOCK)

    xp = _ffn_ln(x_prompt.reshape(t, d), up1, down1, ln1_g, ln1_b)
    xrg = _matmul(xp, w_rnn, 512)
    (p_kc, p_vc, p_ks, p_vs, p_kw, p_vw, p_gn, qa, ksa, vsa, kwa, vwa) = _proj_attn(
        xp, w_q, w_kv, w_gn, cst['pq'], cst['pk'])
    y_rec, p_tail, p_h = _rglru_prompt(xrg, w_conv, b_conv, wa_bd, wx_bd, b_rg_a, b_rg_x, rg_lambda)
    nbs = t // SEL_BLOCK
    kc = _compress_prompt(p_kc, pe2, ck1, ck2, cst['pk']).reshape(KV_HEADS, 2 * nbs, LANES)
    vc = _compress_prompt(p_vc, pe2, cv1, cv2, cst['pk']).reshape(KV_HEADS, 2 * nbs, LANES)
    o_attn = _nsa_prompt(qa, kc, vc, ksa, vsa, kwa, vwa, p_gn, cst['egate'], cst['rplace'], cst['pall'])
    x2 = _merge(xp, y_rec, o_attn, w_grec, w_gatt, w_rec_b, w_attn_b, w_out_b, ln2_g, ln2_b)
    y_prompt = _ffn_ln(x2, up2, down2, ln3_g, ln3_b).reshape(bp, t, d)

    kvh = lambda a: a.reshape(bp, -1, KV_HEADS, HEAD_DIM)
    p_states = (kvh(p_kc), kvh(p_vc), kvh(p_ks), kvh(p_vs), kvh(p_kw[t - WINDOW:]), kvh(p_vw[t - WINDOW:]),
                p_tail[SUBLANES - (CONV_W - 1):].reshape(bp, CONV_W - 1, d_rnn), p_h.reshape(bp, d_rnn))

    m_s = bs * n_tok
    xs = _ffn_ln(x_sample.reshape(m_s, d), up1, down1, ln1_g, ln1_b)
    zs = _matmul(xs, w_attn_all, 512)
    s_xrg = zs[:, :o_q].reshape(bs, n_tok * o_q)
    seq = lambda a: a.reshape(bs, n_tok, a.shape[-1])
    s_q = seq(zs[:, o_q:o_kv])
    s_kv = [seq(zs[:, o_kv + n * KV_W:o_kv + (n + 1) * KV_W]) for n in range(6)]
    s_gn = seq(zs[:, o_gn:o_gn + LANES])
    pad8 = lambda a: jnp.pad(a, ((0, 0), (0, SUBLANES - n_tok), (0, 0)))
    ys_rec, s_conv, s_h = _rglru_sample(
        s_xrg, state_conv.reshape(bs, -1), state_h, w_conv, b_conv, wa_bd, wx_bd, b_rg_a, b_rg_x, rg_lambda,
        start0=(past == 0))
    nbs_s = past // SEL_BLOCK
    kc_s = _compress_paged(cache_k_cmp, page_table, pe2, ck1, ck2)
    vc_s = _compress_paged(cache_v_cmp, page_table, pe2, cv1, cv2)
    o8 = _nsa_sample(page_table, pad8(s_q), kc_s, vc_s, cache_k_sel, cache_v_sel,
                     [pad8(s_kv[n]) for n in (2, 3, 4, 5)],
                     state_win_k.reshape(bs, WINDOW, KV_W), state_win_v.reshape(bs, WINDOW, KV_W),
                     pad8(s_gn), cst['pgk'], cst['rkg'], cst['egs'], n_tok)
    del nbs_s
    os_attn = o8[:, :n_tok].reshape(m_s, ATTN_W).astype(BF)
    x2s = _merge(xs, ys_rec.reshape(m_s, d_rnn), os_attn, w_grec, w_gatt, w_rec_b, w_attn_b, w_out_b, ln2_g, ln2_b)
    y_sample = _ffn_ln(x2s, up2, down2, ln3_g, ln3_b).reshape(bs, n_tok, d)

    kvs = lambda a: a.reshape(bs, n_tok, KV_HEADS, HEAD_DIM)
    win = lambda old, new: jnp.concatenate([old, kvs(new)], axis=1)[:, -WINDOW:]
    s_states = (kvs(s_kv[0]), kvs(s_kv[1]), kvs(s_kv[2]), kvs(s_kv[3]),
                win(state_win_k, s_kv[4]), win(state_win_v, s_kv[5]),
                s_conv.reshape(bs, CONV_W - 1, d_rnn), s_h)

    return (y_prompt, y_sample) + p_states + s_states
```

```python
import functools

import numpy as np
import jax
import jax.numpy as jnp
from jax import lax
from jax.experimental import pallas as pl
from jax.experimental.pallas import tpu as pltpu

F32 = jnp.float32
BF = jnp.bfloat16

DEPTH = 1
ALPHA = (2.0 * DEPTH) ** 0.25
N_HEADS = 16
HEAD_DIM = 64
KV_HEADS = 4
GROUP = N_HEADS // KV_HEADS
KV_W = KV_HEADS * HEAD_DIM
ATTN_W = N_HEADS * HEAD_DIM
N_NSA_BRANCH = 3
CMP_BLOCK = 32
SEL_BLOCK = 64
N_SEL = 16
WINDOW = 512
Q_BLOCK = 128
CONV_W = 4
LRU_C = 8.0
RNN_BLOCKS = 16
LN_EPS = 1e-5
PAGE_SIZE = 128

LANES = 128
SUBLANES = 8
VMEM_LIMIT = 56 * 1024 * 1024

NEG = -1e30
SEL_TILE = 512
BLOCKS_PER_TILE = SEL_TILE // SEL_BLOCK
CMP_ROW = 16
CMP_ROW_W = CMP_ROW * KV_W

_NT = (((1,), (1,)), ((), ()))


def _dot(a, b):
    return jnp.dot(a, b, preferred_element_type=F32)


def _dot_nt(a, b):
    return lax.dot_general(a, b, _NT, preferred_element_type=F32)


def _params(*sem):
    return pltpu.CompilerParams(dimension_semantics=sem, vmem_limit_bytes=VMEM_LIMIT)


def _layer_norm(y, g, b):
    mu = jnp.mean(y, axis=-1, keepdims=True)
    d = y - mu
    var = jnp.mean(d * d, axis=-1, keepdims=True)
    return d * lax.rsqrt(var + LN_EPS) * g + b


def _masked_softmax(s, mask):
    s = jnp.where(mask, s, -jnp.inf)
    m = jnp.max(s, axis=-1, keepdims=True)
    m = jnp.where(m == -jnp.inf, 0.0, m)
    e = jnp.where(mask, jnp.exp(s - m), 0.0)
    den = jnp.sum(e, axis=-1, keepdims=True)
    return e / jnp.maximum(den, 1e-30)


def _split3(x):
    hi = x.astype(BF)
    r1 = x - hi.astype(F32)
    mid = r1.astype(BF)
    lo = (r1 - mid.astype(F32)).astype(BF)
    return hi, mid, lo


def _expand_f32(x, e):
    hi, mid, lo = _split3(x)
    return _dot(hi, e) + _dot(mid, e) + _dot(lo, e)


def _ffn_kernel(x_ref, wg_ref, wu_ref, wd_ref, g_ref, b_ref, o_ref, xb_ref, acc_ref):
    j = pl.program_id(1)

    @pl.when(j == 0)
    def _():
        xb_ref[...] = x_ref[...].astype(BF)
        acc_ref[...] = jnp.zeros_like(acc_ref)

    xb = xb_ref[...]
    gate = _dot(xb, wg_ref[...])
    up = _dot(xb, wu_ref[...])
    h = (gate * jax.nn.sigmoid(gate) * up).astype(BF)
    acc_ref[...] += _dot(h, wd_ref[...])

    @pl.when(j == pl.num_programs(1) - 1)
    def _():
        y = ALPHA * x_ref[...] + 0.5 * acc_ref[...]
        o_ref[...] = _layer_norm(y, g_ref[...], b_ref[...])


def _ffn_ln(x, w_up, w_down, g, b):
    m, d = x.shape
    f = w_down.shape[0]
    tm = min(512, m)
    tn = 512
    nj = f // tn
    return pl.pallas_call(
        _ffn_kernel,
        grid=(m // tm, nj),
        in_specs=[
            pl.BlockSpec((tm, d), lambda i, j: (i, 0)),
            pl.BlockSpec((d, tn), lambda i, j: (0, j)),
            pl.BlockSpec((d, tn), lambda i, j: (0, j + nj)),
            pl.BlockSpec((tn, d), lambda i, j: (j, 0)),
            pl.BlockSpec((1, d), lambda i, j: (0, 0)),
            pl.BlockSpec((1, d), lambda i, j: (0, 0)),
        ],
        out_specs=pl.BlockSpec((tm, d), lambda i, j: (i, 0)),
        out_shape=jax.ShapeDtypeStruct((m, d), F32),
        scratch_shapes=[pltpu.VMEM((tm, d), BF), pltpu.VMEM((tm, d), F32)],
        compiler_params=_params("parallel", "arbitrary"),
        name="ffn_ln",
    )(x, w_up, w_up, w_down, g.reshape(1, d), b.reshape(1, d))


def _mm_kernel(x_ref, w_ref, o_ref):
    o_ref[...] = _dot(x_ref[...].astype(BF), w_ref[...])


def _matmul(x, w, tn):
    m, d = x.shape
    n = w.shape[1]
    tm = min(512, m)
    return pl.pallas_call(
        _mm_kernel,
        grid=(m // tm, n // tn),
        in_specs=[pl.BlockSpec((tm, d), lambda i, j: (i, 0)),
                  pl.BlockSpec((d, tn), lambda i, j: (0, j))],
        out_specs=pl.BlockSpec((tm, tn), lambda i, j: (i, j)),
        out_shape=jax.ShapeDtypeStruct((m, n), F32),
        compiler_params=_params("parallel", "arbitrary"),
        name="proj",
    )(x, w)


def _proj_attn_kernel(x_ref, wq_ref, wkv_ref, wgn_ref, pq_ref, pk_ref, pkt_ref,
                      kc_ref, vc_ref, ks_ref, vs_ref, kw_ref, vw_ref, gn_ref,
                      qa_ref, ksa_ref, vst_ref, kwa_ref, vwt_ref):
    tm = x_ref.shape[0]
    xb = x_ref[...].astype(BF)
    zq = _dot(xb, wq_ref[...]).astype(BF)
    for h in range(N_HEADS):
        pair = zq[:, (h // 2) * LANES:(h // 2 + 1) * LANES]
        qa_ref[h] = _dot(pair, pq_ref[h % 2]).astype(BF)
    zkv = _dot(xb, wkv_ref[...])
    parts = [zkv[:, n * KV_W:(n + 1) * KV_W] for n in range(6)]
    for ref, part in zip((kc_ref, vc_ref, ks_ref, vs_ref, kw_ref, vw_ref), parts):
        ref[...] = part
    gn_ref[...] = _dot(xb, wgn_ref[...])
    t = pl.program_id(0) * tm + lax.broadcasted_iota(jnp.int32, (tm, 1), 0)
    lane = lax.broadcasted_iota(jnp.int32, (1, LANES), 1)
    onehot = jnp.where(lane == HEAD_DIM + (t // SEL_BLOCK) % BLOCKS_PER_TILE, 1.0, 0.0)
    ksb = parts[2].astype(BF)
    kwb = parts[4].astype(BF)
    vsb = parts[3].astype(BF)
    vwb = parts[5].astype(BF)
    for k in range(KV_HEADS):
        ksa_ref[k] = (_dot(ksb, pk_ref[k]) + onehot).astype(BF)
        kwa_ref[k] = _dot(kwb, pk_ref[k]).astype(BF)
        vst_ref[k, 0] = _dot_nt(pkt_ref[k], vsb).astype(BF)
        vwt_ref[k] = _dot_nt(pkt_ref[k], vwb).astype(BF)


def _proj_attn(x, wq, wkv, wgn, pq, pk, pkt):
    m, d = x.shape
    tm = 256
    per_tile = SEL_TILE // tm
    row = lambda w: pl.BlockSpec((tm, w), lambda i: (i, 0))
    const = lambda a: pl.BlockSpec(a.shape, lambda i: (0,) * a.ndim)
    head = lambda n: pl.BlockSpec((n, tm, LANES), lambda i: (0, i, 0))
    f32 = lambda w: jax.ShapeDtypeStruct((m, w), F32)
    aug = lambda n: jax.ShapeDtypeStruct((n, m, LANES), BF)
    return pl.pallas_call(
        _proj_attn_kernel,
        grid=(m // tm,),
        in_specs=[row(d), const(wq), const(wkv), const(wgn), const(pq), const(pk), const(pkt)],
        out_specs=[row(KV_W)] * 6 + [row(LANES), head(N_HEADS), head(KV_HEADS),
                                     pl.BlockSpec((KV_HEADS, 1, LANES, tm), lambda i: (0, i // per_tile, 0, i % per_tile)),
                                     head(KV_HEADS),
                                     pl.BlockSpec((KV_HEADS, LANES, tm), lambda i: (0, 0, i))],
        out_shape=[f32(KV_W)] * 6 + [f32(LANES), aug(N_HEADS), aug(KV_HEADS),
                                     jax.ShapeDtypeStruct((KV_HEADS, m // SEL_TILE, LANES, SEL_TILE), BF),
                                     aug(KV_HEADS),
                                     jax.ShapeDtypeStruct((KV_HEADS, LANES, m), BF)],
        compiler_params=_params("parallel"),
        name="proj_attn",
    )(x, wq, wkv, wgn, pq, pk, pkt)


def _rglru_gates(xc, wa, wx, ba, bx, lam, is_start):
    xcb = xc.astype(BF)
    r = jax.nn.sigmoid(_dot(xcb, wa) + ba)
    ig = jax.nn.sigmoid(_dot(xcb, wx) + bx)
    log_a = -LRU_C * r * jax.nn.softplus(-lam)
    a = jnp.exp(log_a)
    if is_start is True:
        return a, ig * xc
    th = jnp.tanh(log_a)
    mult = jnp.sqrt(-2.0 * th / (1.0 - th))
    if is_start is not None:
        mult = jnp.where(is_start, 1.0, mult)
    return a, mult * (ig * xc)


def _rglru_prompt_kernel(x_ref, gate_ref, wc_ref, bc_ref, wa_ref, wx_ref, ba_ref, bx_ref, lam_ref,
                         y_ref, tail_ref, hl_ref, h_scr, tail_scr):
    c = pl.program_id(1)
    tc = x_ref.shape[0]

    @pl.when(c == 0)
    def _():
        h_scr[...] = jnp.zeros_like(h_scr)
        tail_scr[...] = jnp.zeros_like(tail_scr)

    x = x_ref[...]
    prev = tail_scr[...]
    row8 = lax.broadcasted_iota(jnp.int32, (SUBLANES, 1), 0)
    row = lax.broadcasted_iota(jnp.int32, (tc, 1), 0)

    def shifted(s):
        rolled = pltpu.roll(x, s, 0)
        top = jnp.where(row8 < s, pltpu.roll(prev, s, 0), rolled[:SUBLANES])
        return jnp.concatenate([top, rolled[SUBLANES:]], axis=0)

    wc = wc_ref[...]
    conv = wc[0:1] * shifted(3)
    conv = conv + wc[1:2] * shifted(2)
    conv = conv + wc[2:3] * shifted(1)
    conv = conv + wc[3:4] * x
    xc = bc_ref[...] + conv

    is_start = (row + c * tc) == 0
    a, u = _rglru_gates(xc, wa_ref[0], wx_ref[0], ba_ref[...], bx_ref[...], lam_ref[...], is_start)

    d = 1
    while d < tc:
        keep = row >= d
        a_sh = jnp.where(keep, pltpu.roll(a, d, 0), 1.0)
        u_sh = jnp.where(keep, pltpu.roll(u, d, 0), 0.0)
        u = a * u_sh + u
        a = a * a_sh
        d *= 2
    h = a * h_scr[...] + u

    y_ref[...] = (jax.nn.gelu(gate_ref[...]) * h).astype(BF)
    h_last = h[tc - 1:tc]
    h_scr[...] = h_last
    tail_scr[...] = x[tc - SUBLANES:]
    hl_ref[...] = h_last
    tail_ref[...] = x[tc - SUBLANES:]


def _rglru_prompt(xrg, w_conv, b_conv, wa_bd, wx_bd, b_a, b_x, lam):
    t = xrg.shape[0]
    d_rnn = w_conv.shape[1]
    gw = 2 * LANES
    ng = d_rnn // gw
    tc = min(512, t)
    vec = lambda: pl.BlockSpec((1, gw), lambda g, c: (0, g))
    return pl.pallas_call(
        _rglru_prompt_kernel,
        grid=(ng, t // tc),
        in_specs=[
            pl.BlockSpec((tc, gw), lambda g, c: (c, g)),
            pl.BlockSpec((tc, gw), lambda g, c: (c, g + ng)),
            pl.BlockSpec((CONV_W, gw), lambda g, c: (0, g)),
            vec(),
            pl.BlockSpec((1, gw, gw), lambda g, c: (g, 0, 0)),
            pl.BlockSpec((1, gw, gw), lambda g, c: (g, 0, 0)),
            vec(), vec(), vec(),
        ],
        out_specs=[
            pl.BlockSpec((tc, gw), lambda g, c: (c, g)),
            pl.BlockSpec((SUBLANES, gw), lambda g, c: (0, g)),
            pl.BlockSpec((1, gw), lambda g, c: (0, g)),
        ],
        out_shape=[
            jax.ShapeDtypeStruct((t, d_rnn), BF),
            jax.ShapeDtypeStruct((SUBLANES, d_rnn), F32),
            jax.ShapeDtypeStruct((1, d_rnn), F32),
        ],
        scratch_shapes=[pltpu.VMEM((1, gw), F32), pltpu.VMEM((SUBLANES, gw), F32)],
        compiler_params=_params("parallel", "arbitrary"),
        name="rglru_prompt",
    )(xrg, xrg, w_conv, b_conv.reshape(1, -1), wa_bd, wx_bd,
      b_a.reshape(1, -1), b_x.reshape(1, -1), lam.reshape(1, -1))


def _rglru_sample_kernel(xrg_ref, cp_ref, h0_ref, wc_ref, bc_ref, wa_ref, wx_ref, ba_ref, bx_ref, lam_ref,
                         y_ref, cn_ref, hl_ref, *, n_tok, start0):
    d_rnn = h0_ref.shape[1]
    gw = wa_ref.shape[1]
    wc = wc_ref[...]
    xp = [cp_ref[:, k * d_rnn:(k + 1) * d_rnn] for k in range(CONV_W - 1)]
    xp += [xrg_ref[:, t * 2 * d_rnn:t * 2 * d_rnn + d_rnn] for t in range(n_tok)]
    h = h0_ref[...]
    for t in range(n_tok):
        conv = wc[0:1] * xp[t]
        for k in range(1, CONV_W):
            conv = conv + wc[k:k + 1] * xp[t + k]
        xc = bc_ref[...] + conv
        a_parts, u_parts = [], []
        for g in range(d_rnn // gw):
            sl = slice(g * gw, (g + 1) * gw)
            a_g, u_g = _rglru_gates(xc[:, sl], wa_ref[g], wx_ref[g], ba_ref[:, sl], bx_ref[:, sl],
                                    lam_ref[:, sl], True if (start0 and t == 0) else None)
            a_parts.append(a_g)
            u_parts.append(u_g)
        a = jnp.concatenate(a_parts, axis=1)
        u = jnp.concatenate(u_parts, axis=1)
        h = a * h + u
        gate = xrg_ref[:, t * 2 * d_rnn + d_rnn:(t + 1) * 2 * d_rnn]
        y_ref[:, t * d_rnn:(t + 1) * d_rnn] = (jax.nn.gelu(gate) * h).astype(BF)
    hl_ref[...] = h
    tail = xp[-(CONV_W - 1):]
    for k in range(CONV_W - 1):
        cn_ref[:, k * d_rnn:(k + 1) * d_rnn] = tail[k]


def _rglru_sample(xrg, conv_prev, h0, w_conv, b_conv, wa_bd, wx_bd, b_a, b_x, lam, start0):
    b, d_rnn = h0.shape
    n_tok = xrg.shape[1] // (2 * d_rnn)
    args = (xrg, conv_prev, h0, w_conv, b_conv.reshape(1, -1), wa_bd, wx_bd,
            b_a.reshape(1, -1), b_x.reshape(1, -1), lam.reshape(1, -1))
    full = lambda a: pl.BlockSpec(a.shape, lambda i: (0,) * a.ndim)
    outs = [jax.ShapeDtypeStruct((b, n_tok * d_rnn), BF),
            jax.ShapeDtypeStruct((b, (CONV_W - 1) * d_rnn), F32),
            jax.ShapeDtypeStruct((b, d_rnn), F32)]
    return pl.pallas_call(
        functools.partial(_rglru_sample_kernel, n_tok=n_tok, start0=start0),
        grid=(1,),
        in_specs=[full(a) for a in args],
        out_specs=[full(o) for o in outs],
        out_shape=outs,
        compiler_params=_params("arbitrary"),
        name="rglru_sample",
    )(*args)


def _compress_kernel(*refs, n_in, per_head):
    if not per_head:
        refs = refs[1:]
    x_refs = refs[:n_in]
    pe_ref, w1_ref, w2_ref = refs[n_in:n_in + 3]
    rest = refs[n_in + 3:]
    if per_head:
        pk_ref, o_ref, scr = rest
    else:
        o_ref, scr = rest
    if n_in == 1:
        x = x_refs[0][...]
    else:
        x = jnp.concatenate([r[0] for r in x_refs], axis=0)
    rows = x.shape[0]
    xb = (x.reshape(rows // SUBLANES, SUBLANES, CMP_ROW_W) + pe_ref[...][None]).reshape(rows, CMP_ROW_W)
    full = _dot(xb.astype(BF), w1_ref[...])
    hid = full[:, :KV_W] + pltpu.roll(full[:, KV_W:], rows - 1, 0)
    out = _dot(jax.nn.gelu(hid).astype(BF), w2_ref[...])
    nb = rows // 4
    if per_head:
        ob = out.astype(BF)
        for k in range(KV_HEADS):
            scr[...] = _dot(ob, pk_ref[k])
            o_ref[k, 0] = scr[pl.ds(0, nb, stride=4), :].astype(BF)
            o_ref[k, 1] = scr[pl.ds(2, nb, stride=4), :].astype(BF)
    else:
        for c in range(KV_W // LANES):
            scr[...] = out[:, c * LANES:(c + 1) * LANES]
            o_ref[0, 0, :, c * LANES:(c + 1) * LANES] = scr[pl.ds(0, nb, stride=4), :].astype(BF)
            o_ref[0, 1, :, c * LANES:(c + 1) * LANES] = scr[pl.ds(2, nb, stride=4), :].astype(BF)


def _compress_prompt(kv, pe2, w1, w2, pk):
    t = kv.shape[0]
    rows = 128
    x = kv.reshape(t // CMP_ROW, CMP_ROW_W)
    nb = rows // 4
    nbs = t // SEL_BLOCK
    const = lambda a: pl.BlockSpec(a.shape, lambda i: (0,) * a.ndim)
    return pl.pallas_call(
        functools.partial(_compress_kernel, n_in=1, per_head=True),
        grid=(x.shape[0] // rows,),
        in_specs=[pl.BlockSpec((rows, CMP_ROW_W), lambda i: (i, 0)),
                  const(pe2), const(w1), const(w2), const(pk)],
        out_specs=pl.BlockSpec((KV_HEADS, 2, nb, LANES), lambda i: (0, 0, i, 0)),
        out_shape=jax.ShapeDtypeStruct((KV_HEADS, 2, nbs, LANES), BF),
        scratch_shapes=[pltpu.VMEM((rows, LANES), F32)],
        compiler_params=_params("parallel"),
        name="compress_prompt",
    )(x, pe2, w1, w2, pk)


def _compress_paged(cache, page_table, pe2, w1, w2):
    b, n_pages = page_table.shape
    pages = 16
    rows_per_page = PAGE_SIZE // CMP_ROW
    x = cache.reshape(cache.shape[0], rows_per_page, CMP_ROW_W)
    rows = pages * rows_per_page
    nb = rows // 4
    nbs = n_pages * PAGE_SIZE // SEL_BLOCK
    const = lambda a: pl.BlockSpec(a.shape, lambda i, s, pt: (0,) * a.ndim)
    page_specs = [
        pl.BlockSpec((1, rows_per_page, CMP_ROW_W),
                     functools.partial(lambda i, s, pt, p: (pt[i, s * pages + p], 0, 0), p=p))
        for p in range(pages)]
    return pl.pallas_call(
        functools.partial(_compress_kernel, n_in=pages, per_head=False),
        grid_spec=pltpu.PrefetchScalarGridSpec(
            num_scalar_prefetch=1,
            grid=(b, n_pages // pages),
            in_specs=page_specs + [const(pe2), const(w1), const(w2)],
            out_specs=pl.BlockSpec((1, 2, nb, KV_W), lambda i, s, pt: (i, 0, s, 0)),
            scratch_shapes=[pltpu.VMEM((rows, LANES), F32)],
        ),
        out_shape=jax.ShapeDtypeStruct((b, 2, nbs, KV_W), BF),
        compiler_params=_params("parallel", "arbitrary"),
        name="compress_paged",
    )(page_table, *([x] * pages), pe2, w1, w2)


def _topk_columns(imp_t, n_pick):
    nblk = imp_t.shape[0]
    idx = lax.broadcasted_iota(jnp.int32, imp_t.shape, 0)
    sel = jnp.zeros(imp_t.shape, F32)
    for _ in range(n_pick):
        mx = jnp.max(imp_t, axis=0, keepdims=True)
        first = jnp.min(jnp.where(imp_t == mx, idx, nblk), axis=0, keepdims=True)
        hit = idx == first
        sel = jnp.where(hit, 1.0, sel)
        imp_t = jnp.where(hit, -jnp.inf, imp_t)
    return sel


def _softmax_cols(s, mask):
    s = jnp.where(mask, s, -jnp.inf)
    m = jnp.max(s, axis=0, keepdims=True)
    m = jnp.where(m == -jnp.inf, 0.0, m)
    e = jnp.where(mask, jnp.exp(s - m), 0.0)
    den = jnp.sum(e, axis=0, keepdims=True)
    return e * (1.0 / jnp.maximum(den, 1e-30))


def _nsa_prompt_kernel(*refs):
    (q_ref, kc_ref, vc_ref, ks_ref, vst_ref) = refs[:5]
    kw_refs = refs[5:10]
    vwt_refs = refs[10:15]
    (gn_ref, rp_ref, pa_ref, o_ref,
     bias_scr, gate_scr, sa_scr, sb_scr, m_scr, l_scr, acc_scr) = refs[15:]
    i = pl.program_id(0)
    kvh = pl.program_id(1)
    nq = Q_BLOCK
    r = GROUP * nq
    q = q_ref[...].reshape(r, LANES)
    lane = lax.broadcasted_iota(jnp.int32, (1, r), 1)
    qpos = i * nq + lane % nq

    kc = kc_ref[0]
    n2 = kc.shape[0]
    nbs = n2 // 2
    rowc = lax.broadcasted_iota(jnp.int32, (n2, 1), 0)
    blk = 2 * (rowc % nbs) + rowc // nbs
    p_c = _softmax_cols(_dot_nt(kc, q), blk * CMP_BLOCK + (CMP_BLOCK - 1) <= qpos)
    vct = vc_ref[0].astype(F32).T.astype(BF)
    o_c = _dot(vct, p_c.astype(BF))

    imp = p_c[:, 0:nq]
    for g in range(1, GROUP):
        imp = imp + p_c[:, g * nq:(g + 1) * nq]
    imp = imp[:nbs] + imp[nbs:]
    qp = i * nq + lax.broadcasted_iota(jnp.int32, (1, nq), 1)
    jb = lax.broadcasted_iota(jnp.int32, (nbs, 1), 0)
    cur = qp // SEL_BLOCK
    valid = jb * SEL_BLOCK <= qp
    forced = (jb == 0) | (jb == cur) | (jb == cur - 1)
    imp = jnp.where(valid, jnp.where(forced, jnp.inf, imp), -jnp.inf)
    sel = _topk_columns(imp, min(N_SEL, nbs))
    bias = jnp.where(valid & (sel > 0.5), 0.0, NEG).T.astype(BF)
    bias_all = _dot(bias, pa_ref[...]).astype(BF)
    n_tiles = bias_scr.shape[0]
    for t in range(n_tiles):
        bias_scr[t] = bias_all[:, t * LANES:(t + 1) * LANES]

    def scores(t):
        k0 = pl.multiple_of(t * SEL_TILE, SEL_TILE)
        kt = ks_ref[0, pl.ds(k0, SEL_TILE), :]
        qa = q + jnp.concatenate([bias_scr[t]] * GROUP, axis=0)
        return _dot_nt(kt, qa)

    def update(t, s_ref, diagonal):
        vt = vst_ref[0, t]
        for g in range(GROUP):
            sl = slice(g * nq, (g + 1) * nq)
            sg = s_ref[:, sl]
            if diagonal:
                kpos = t * SEL_TILE + lax.broadcasted_iota(jnp.int32, (SEL_TILE, 1), 0)
                sg = jnp.where(kpos <= qpos[:, sl], sg, NEG)
            m_old = m_scr[:, sl]
            m_new = jnp.maximum(m_old, jnp.max(sg, axis=0, keepdims=True))
            scale = jnp.exp(m_old - m_new)
            p = jnp.exp(sg - m_new)
            l_scr[:, sl] = scale * l_scr[:, sl] + jnp.sum(p, axis=0, keepdims=True)
            acc_scr[:, sl] = scale * acc_scr[:, sl] + _dot(vt, p.astype(BF))
            m_scr[:, sl] = m_new

    t_diag = (i * nq) // SEL_TILE
    n_pairs = t_diag // 2
    m_scr[...] = jnp.full(m_scr.shape, NEG, F32)
    l_scr[...] = jnp.zeros_like(l_scr)
    acc_scr[...] = jnp.zeros_like(acc_scr)
    sa_scr[...] = scores(0)

    def pair(u, _):
        t0 = 2 * u
        sb_scr[...] = scores(t0 + 1)
        update(t0, sa_scr, False)
        sa_scr[...] = scores(t0 + 2)
        update(t0 + 1, sb_scr, False)
        return 0

    lax.fori_loop(0, n_pairs, pair, 0)

    @pl.when(t_diag % 2 == 1)
    def _():
        sb_scr[...] = scores(t_diag)
        update(t_diag - 1, sa_scr, False)
        update(t_diag, sb_scr, True)

    @pl.when(t_diag % 2 == 0)
    def _():
        update(t_diag, sa_scr, True)

    o_s = acc_scr[...] * (1.0 / l_scr[...])

    kw = jnp.concatenate([ref[0] for ref in kw_refs], axis=0)
    vwt = jnp.concatenate([ref[0] for ref in vwt_refs], axis=1)
    n_w = kw.shape[0]
    kwpos = (i - WINDOW // nq) * nq + lax.broadcasted_iota(jnp.int32, (n_w, 1), 0)
    dpos = qpos - kwpos
    p_w = _softmax_cols(_dot_nt(kw, q), (dpos >= 0) & (dpos <= WINDOW) & (kwpos >= 0))
    o_w = _dot(vwt, p_w.astype(BF))

    gate_scr[...] = jax.nn.sigmoid(gn_ref[...]).T

    def gate(br):
        return jnp.concatenate(
            [gate_scr[pl.ds(br * N_HEADS + kvh * GROUP + g, 1), :] for g in range(GROUP)], axis=1)

    comb = gate(0) * o_c + gate(1) * o_s + gate(2) * o_w
    out = None
    for g in range(GROUP):
        part = _dot(comb[:, g * nq:(g + 1) * nq].T.astype(BF), rp_ref[g])
        out = part if out is None else out + part
    o_ref[...] = out.astype(BF)


def _nsa_prompt(qa, kc, vc, ksa, vst, kwa, vwt, gn, rplace, pall):
    t = qa.shape[1]
    n_tiles = t // SEL_TILE
    wb = WINDOW // Q_BLOCK + 1
    kv_full = lambda a: pl.BlockSpec((1,) + a.shape[1:], lambda i, k: (k,) + (0,) * (a.ndim - 1))
    wblk = lambda i, m: jnp.maximum(i - (wb - 1) + m, 0)
    win_k = [pl.BlockSpec((1, Q_BLOCK, LANES), functools.partial(lambda i, k, m: (k, wblk(i, m), 0), m=m))
             for m in range(wb)]
    win_v = [pl.BlockSpec((1, LANES, Q_BLOCK), functools.partial(lambda i, k, m: (k, 0, wblk(i, m)), m=m))
             for m in range(wb)]
    const = lambda a: pl.BlockSpec(a.shape, lambda i, k: (0,) * a.ndim)
    return pl.pallas_call(
        _nsa_prompt_kernel,
        grid=(t // Q_BLOCK, KV_HEADS),
        in_specs=[pl.BlockSpec((GROUP, Q_BLOCK, LANES), lambda i, k: (k, i, 0)),
                  kv_full(kc), kv_full(vc), kv_full(ksa), kv_full(vst)]
                 + win_k + win_v
                 + [pl.BlockSpec((Q_BLOCK, LANES), lambda i, k: (i, 0)), const(rplace), const(pall)],
        out_specs=pl.BlockSpec((Q_BLOCK, KV_W), lambda i, k: (i, k)),
        out_shape=jax.ShapeDtypeStruct((t, ATTN_W), BF),
        scratch_shapes=[pltpu.VMEM((n_tiles, Q_BLOCK, LANES), BF), pltpu.VMEM((LANES, Q_BLOCK), F32),
                        pltpu.VMEM((SEL_TILE, GROUP * Q_BLOCK), F32), pltpu.VMEM((SEL_TILE, GROUP * Q_BLOCK), F32),
                        pltpu.VMEM((1, GROUP * Q_BLOCK), F32), pltpu.VMEM((1, GROUP * Q_BLOCK), F32),
                        pltpu.VMEM((LANES, GROUP * Q_BLOCK), F32)],
        compiler_params=_params("parallel", "arbitrary"),
        name="nsa_prompt",
    )(qa, kc, vc, ksa, vst, *([kwa] * wb), *([vwt] * wb), gn, rplace, pall)


def _nsa_sample_kernel(*refs, pages, past, n_tok):
    pt_ref = refs[0]
    q_ref, kc_ref, vc_ref = refs[1:4]
    ks_refs = refs[4:4 + pages]
    vs_refs = refs[4 + pages:4 + 2 * pages]
    (ksn_ref, vsn_ref, kwn_ref, vwn_ref, wk_ref, wv_ref, gn_ref,
     pg_ref, rp_ref, eg_ref, o_ref, q_scr, bias_scr, m_scr, l_scr, acc_scr, oc_scr) = refs[4 + 2 * pages:]
    del pt_ref
    s_id = pl.program_id(1)
    tp = SUBLANES
    r = N_HEADS * tp
    row = lax.broadcasted_iota(jnp.int32, (r, 1), 0)
    qpos = past + row % tp
    nbs = bias_scr.shape[1]

    @pl.when(s_id == 0)
    def _():
        qb = q_ref[0].astype(BF)
        for h in range(N_HEADS):
            k = h // GROUP
            q_scr[h * tp:(h + 1) * tp, :] = _dot(qb[:, k * KV_W:(k + 1) * KV_W], pg_ref[h]).astype(BF)
        q = q_scr[...]
        kc = kc_ref[0].reshape(2 * nbs, KV_W)
        vc = vc_ref[0].reshape(2 * nbs, KV_W)
        col = lax.broadcasted_iota(jnp.int32, (1, 2 * nbs), 1)
        blk = 2 * (col % nbs) + col // nbs
        p_c = _masked_softmax(_dot_nt(q, kc), blk * CMP_BLOCK + (CMP_BLOCK - 1) <= qpos)
        oc_scr[...] = _dot(p_c.astype(BF), vc)
        imps = []
        for k in range(KV_HEADS):
            base = k * GROUP * tp
            imp = p_c[base:base + tp]
            for g in range(1, GROUP):
                imp = imp + p_c[base + g * tp:base + (g + 1) * tp]
            imps.append(imp)
        imp = jnp.concatenate(imps, axis=0)
        imp = imp[:, :nbs] + imp[:, nbs:]
        jb = lax.broadcasted_iota(jnp.int32, (1, nbs), 1)
        imp = jnp.where((jb == 0) | (jb == nbs - 1), jnp.inf, imp)
        idx = lax.broadcasted_iota(jnp.int32, imp.shape, 1)
        sel = jnp.zeros(imp.shape, F32)
        for _ in range(min(N_SEL, nbs + 1) - 1):
            mx = jnp.max(imp, axis=1, keepdims=True)
            first = jnp.min(jnp.where(imp == mx, idx, nbs), axis=1, keepdims=True)
            hit = idx == first
            sel = jnp.where(hit, 1.0, sel)
            imp = jnp.where(hit, -jnp.inf, imp)
        bias = jnp.where(sel > 0.5, 0.0, NEG).astype(BF)
        for k in range(KV_HEADS):
            for g in range(GROUP):
                h = k * GROUP + g
                bias_scr[h * tp:(h + 1) * tp, :] = bias[k * tp:(k + 1) * tp]
        m_scr[...] = jnp.full(m_scr.shape, NEG, F32)
        l_scr[...] = jnp.zeros_like(l_scr)
        acc_scr[...] = jnp.zeros_like(acc_scr)

    def online(st, v, v_transposed):
        m = m_scr[...]
        m_new = jnp.maximum(m, jnp.max(st, axis=1, keepdims=True))
        scale = jnp.exp(m - m_new)
        p = jnp.exp(st - m_new)
        l_scr[...] = scale * l_scr[...] + jnp.sum(p, axis=1, keepdims=True)
        pv = _dot_nt(p.astype(BF), v) if v_transposed else _dot(p.astype(BF), v)
        acc_scr[...] = scale * acc_scr[...] + pv
        m_scr[...] = m_new

    q = q_scr[...]
    kt = jnp.concatenate([ref[0].astype(BF) for ref in ks_refs], axis=1)
    vt = jnp.concatenate([ref[0].astype(BF) for ref in vs_refs], axis=1)
    n_keys = kt.shape[1]
    jb = lax.broadcasted_iota(jnp.int32, (nbs, 1), 0)
    key_blk = s_id * (n_keys // SEL_BLOCK) + lax.broadcasted_iota(jnp.int32, (1, n_keys), 1) // SEL_BLOCK
    expand = jnp.where(jb == key_blk, 1.0, 0.0).astype(BF)
    online(_dot(q, kt) + _dot(bias_scr[...], expand), vt, True)

    @pl.when(s_id == pl.num_programs(1) - 1)
    def _():
        zeros = jnp.zeros((LANES - tp, KV_W), BF)
        tok = lax.broadcasted_iota(jnp.int32, (1, LANES), 1)
        ksn = jnp.concatenate([ksn_ref[0].astype(BF), zeros], axis=0)
        vsn = jnp.concatenate([vsn_ref[0].astype(BF), zeros], axis=0)
        online(jnp.where((past + tok <= qpos) & (tok < n_tok), _dot_nt(q, ksn), NEG), vsn, False)
        o_s = acc_scr[...] / l_scr[...]
        kw = jnp.concatenate([wk_ref[0].astype(BF), kwn_ref[0].astype(BF), zeros], axis=0)
        vw = jnp.concatenate([wv_ref[0].astype(BF), vwn_ref[0].astype(BF), zeros], axis=0)
        n_w = kw.shape[0]
        wcol = lax.broadcasted_iota(jnp.int32, (1, n_w), 1)
        kwpos = past - WINDOW + wcol
        dpos = qpos - kwpos
        mask_w = (dpos >= 0) & (dpos <= WINDOW) & (kwpos >= 0) & (wcol < WINDOW + n_tok)
        p_w = _masked_softmax(_dot_nt(q, kw), mask_w)
        o_w = _dot(p_w.astype(BF), vw)
        sg = jax.nn.sigmoid(gn_ref[0])
        gexp = _expand_f32(sg, eg_ref[...])

        def gate(br):
            return jnp.concatenate(
                [gexp[:, (br * N_HEADS + h) * KV_W:(br * N_HEADS + h + 1) * KV_W] for h in range(N_HEADS)],
                axis=0)

        comb = (gate(0) * oc_scr[...] + gate(1) * o_s + gate(2) * o_w).astype(BF)
        for k in range(KV_HEADS):
            out = None
            for g in range(GROUP):
                h = k * GROUP + g
                part = _dot(comb[h * tp:(h + 1) * tp], rp_ref[h])
                out = part if out is None else out + part
            o_ref[0, :, k * KV_W:(k + 1) * KV_W] = out


def _nsa_sample(page_table, q8, kc, vc, cache_ks, cache_vs, new8, win_k, win_v, gn8, pgk, rkg, egate, n_tok):
    b, n_pages = page_table.shape
    past = n_pages * PAGE_SIZE
    pages = 8
    tp = SUBLANES
    r = N_HEADS * tp
    nbs = past // SEL_BLOCK
    ck = jnp.transpose(cache_ks, (0, 2, 3, 1)).reshape(cache_ks.shape[0], KV_W, PAGE_SIZE)
    cv = jnp.transpose(cache_vs, (0, 2, 3, 1)).reshape(cache_vs.shape[0], KV_W, PAGE_SIZE)
    per_seq = lambda a: pl.BlockSpec((1,) + a.shape[1:], lambda i, s, pt: (i,) + (0,) * (a.ndim - 1))
    const = lambda a: pl.BlockSpec(a.shape, lambda i, s, pt: (0,) * a.ndim)
    page_specs = [
        pl.BlockSpec((1, KV_W, PAGE_SIZE),
                     functools.partial(lambda i, s, pt, p: (pt[i, s * pages + p], 0, 0), p=p))
        for p in range(pages)]
    return pl.pallas_call(
        functools.partial(_nsa_sample_kernel, pages=pages, past=past, n_tok=n_tok),
        grid_spec=pltpu.PrefetchScalarGridSpec(
            num_scalar_prefetch=1,
            grid=(b, n_pages // pages),
            in_specs=[per_seq(q8), per_seq(kc), per_seq(vc)] + page_specs + page_specs
                     + [per_seq(a) for a in new8] + [per_seq(win_k), per_seq(win_v), per_seq(gn8),
                                                     const(pgk), const(rkg), const(egate)],
            out_specs=pl.BlockSpec((1, tp, ATTN_W), lambda i, s, pt: (i, 0, 0)),
            scratch_shapes=[pltpu.VMEM((r, KV_W), BF), pltpu.VMEM((r, nbs), BF),
                            pltpu.VMEM((r, 1), F32), pltpu.VMEM((r, 1), F32),
                            pltpu.VMEM((r, KV_W), F32), pltpu.VMEM((r, KV_W), F32)],
        ),
        out_shape=jax.ShapeDtypeStruct((b, tp, ATTN_W), F32),
        compiler_params=_params("parallel", "arbitrary"),
        name="nsa_sample",
    )(page_table, q8, kc, vc, *([ck] * pages), *([cv] * pages), *new8, win_k, win_v, gn8, pgk, rkg, egate)


def _merge_kernel(x_ref, y_ref, o_ref, wgr_ref, wga_ref, wr_ref, wa_ref, wo_ref, g_ref, b_ref,
                  out_ref, xb_ref, acc_ref):
    j = pl.program_id(1)

    @pl.when(j == 0)
    def _():
        xb_ref[...] = x_ref[...].astype(BF)
        acc_ref[...] = jnp.zeros_like(acc_ref)

    xb = xb_ref[...]
    rec = _dot(y_ref[...], wr_ref[...])
    att = _dot(o_ref[...], wa_ref[...])
    u = jax.nn.sigmoid(_dot(xb, wgr_ref[...])) * rec + jax.nn.sigmoid(_dot(xb, wga_ref[...])) * att
    acc_ref[...] += _dot(u.astype(BF), wo_ref[...])

    @pl.when(j == pl.num_programs(1) - 1)
    def _():
        out_ref[...] = _layer_norm(ALPHA * x_ref[...] + acc_ref[...], g_ref[...], b_ref[...])


def _merge(x, y_rec, o_attn, w_grec, w_gatt, w_rec_o, w_attn_o, w_out, g, b):
    m, d = x.shape
    tm = min(512, m)
    tn = 512
    col = lambda a: pl.BlockSpec((a.shape[0], tn), lambda i, j: (0, j))
    rowb = lambda a: pl.BlockSpec((tm, a.shape[1]), lambda i, j: (i, 0))
    vec = pl.BlockSpec((1, d), lambda i, j: (0, 0))
    return pl.pallas_call(
        _merge_kernel,
        grid=(m // tm, d // tn),
        in_specs=[rowb(x), rowb(y_rec), rowb(o_attn), col(w_grec), col(w_gatt), col(w_rec_o), col(w_attn_o),
                  pl.BlockSpec((tn, d), lambda i, j: (j, 0)), vec, vec],
        out_specs=pl.BlockSpec((tm, d), lambda i, j: (i, 0)),
        out_shape=jax.ShapeDtypeStruct((m, d), F32),
        scratch_shapes=[pltpu.VMEM((tm, d), BF), pltpu.VMEM((tm, d), F32)],
        compiler_params=_params("parallel", "arbitrary"),
        name="merge_out",
    )(x, y_rec, o_attn, w_grec, w_gatt, w_rec_o, w_attn_o, w_out, g.reshape(1, d), b.reshape(1, d))


def _place(n_rows, n_cols, src0, dst0, width, value=1.0):
    m = np.zeros((n_rows, n_cols), np.float32)
    m[src0 + np.arange(width), dst0 + np.arange(width)] = value
    return m


def _layout_constants(n_sel_blocks_prompt):
    hd = HEAD_DIM
    scale = hd ** -0.5
    pq = np.stack([_place(LANES, LANES, e * hd, 0, hd, scale) for e in range(2)])
    pk = np.stack([_place(KV_W, LANES, k * hd, 0, hd) for k in range(KV_HEADS)])
    rplace = np.stack([_place(LANES, KV_W, 0, g * hd, hd) for g in range(GROUP)])
    pkt = np.transpose(pk, (0, 2, 1))
    nbs = n_sel_blocks_prompt
    n_tiles = -(-nbs // BLOCKS_PER_TILE)
    pall = np.zeros((nbs, n_tiles * LANES), np.float32)
    j = np.arange(nbs)
    pall[j, (j // BLOCKS_PER_TILE) * LANES + hd + j % BLOCKS_PER_TILE] = 1.0
    pgk = np.stack([_place(KV_W, KV_W, (h % GROUP) * hd, (h // GROUP) * hd, hd, scale) for h in range(N_HEADS)])
    rkg = np.stack([_place(KV_W, KV_W, (h // GROUP) * hd, (h % GROUP) * hd, hd) for h in range(N_HEADS)])
    egs = np.zeros((LANES, N_NSA_BRANCH * N_HEADS * KV_W), np.float32)
    for c in range(N_NSA_BRANCH * N_HEADS):
        egs[c, c * KV_W:(c + 1) * KV_W] = 1.0
    as_bf = lambda a: jnp.asarray(a, BF)
    return dict(pq=as_bf(pq), pk=as_bf(pk), pkt=as_bf(pkt), rplace=as_bf(rplace), pall=as_bf(pall),
                pgk=as_bf(pgk), rkg=as_bf(rkg), egs=as_bf(egs))


def _block_diag(w, per_group):
    nb, c, _ = w.shape
    eye = jnp.eye(per_group, dtype=w.dtype)
    wg = w.reshape(nb // per_group, per_group, c, c)
    return jnp.einsum('gpcd,pq->gpcqd', wg, eye).reshape(nb // per_group, per_group * c, per_group * c)


def _compress_weights(w1, w2, pe):
    eye = jnp.eye(KV_HEADS, dtype=w1.dtype)
    big = jnp.einsum('lde,kq->lkdqe', w1, eye).reshape(CMP_BLOCK * KV_W, KV_HEADS * w1.shape[2])
    half = big.shape[0] // 2
    w1cat = jnp.concatenate([big[:half], big[half:]], axis=1).astype(BF)
    w2bd = _block_diag(jnp.broadcast_to(w2, (KV_HEADS,) + w2.shape), KV_HEADS)[0].astype(BF)
    pe_flat = jnp.broadcast_to(pe[:, None, :], (CMP_BLOCK, KV_HEADS, HEAD_DIM)).reshape(2, half)
    pe2 = jnp.tile(pe_flat, (SUBLANES // 2, 1))
    return w1cat, w2bd, pe2


def kernel(x_prompt, x_sample, cache_k_cmp, cache_v_cmp, cache_k_sel, cache_v_sel, page_table,
           state_win_k, state_win_v, state_conv, state_h,
           ln1_g, ln1_b, w_ffn1_up, w_ffn1_down, w_in, w_conv, b_conv, w_rg_a, b_rg_a, w_rg_x, b_rg_x,
           rg_lambda, cmp_pe, w_ck1, w_ck2, w_cv1, w_cv2, w_rec_o, w_attn_o, w_out,
           ln2_g, ln2_b, w_ffn2_up, w_ffn2_down, ln3_g, ln3_b):
    bp, t, d = x_prompt.shape
    bs, n_tok, _ = x_sample.shape
    d_rnn = w_conv.shape[1]
    assert bp == 1 and t % (128 * CMP_ROW) == 0 and t >= WINDOW
    past = page_table.shape[1] * PAGE_SIZE
    assert page_table.shape[1] % 16 == 0 and n_tok <= SUBLANES and n_tok >= CONV_W - 1
    assert state_win_k.shape[1] == WINDOW

    o_q = 2 * d_rnn
    o_kv = o_q + ATTN_W
    o_gn = o_kv + 6 * KV_W
    o_gr = o_gn + N_NSA_BRANCH * N_HEADS
    o_ga = o_gr + d
    w_in_b = w_in.astype(BF)
    w_rnn = w_in_b[:, :o_q]
    w_q = w_in_b[:, o_q:o_kv]
    w_kv = w_in_b[:, o_kv:o_gn]
    w_gn = jnp.pad(w_in_b[:, o_gn:o_gr], ((0, 0), (0, LANES - (o_gr - o_gn))))
    w_grec = w_in_b[:, o_gr:o_ga]
    w_gatt = w_in_b[:, o_ga:]
    w_attn_all = jnp.concatenate([w_in_b[:, :o_gn], jnp.pad(w_gn, ((0, 0), (0, 512 - LANES)))], axis=1)
    up1, down1 = w_ffn1_up.astype(BF), w_ffn1_down.astype(BF)
    up2, down2 = w_ffn2_up.astype(BF), w_ffn2_down.astype(BF)
    per_group = 2 * LANES // (d_rnn // RNN_BLOCKS)
    wa_bd = _block_diag(w_rg_a, per_group).astype(BF)
    wx_bd = _block_diag(w_rg_x, per_group).astype(BF)
    ck1, ck2, pe2 = _compress_weights(w_ck1, w_ck2, cmp_pe)
    cv1, cv2, _ = _compress_weights(w_cv1, w_cv2, cmp_pe)
    w_rec_b, w_attn_b, w_out_b = w_rec_o.astype(BF), w_attn_o.astype(BF), w_out.astype(BF)
    cst = _layout_constants(t // SEL_BLOCK)

    xp = _ffn_ln(x_prompt.reshape(t, d), up1, down1, ln1_g, ln1_b)
    xrg = _matmul(xp, w_rnn, 512)
    (p_kc, p_vc, p_ks, p_vs, p_kw, p_vw, p_gn, qa, ksa, vst, kwa, vwt) = _proj_attn(
        xp, w_q, w_kv, w_gn, cst['pq'], cst['pk'], cst['pkt'])
    y_rec, p_tail, p_h = _rglru_prompt(xrg, w_conv, b_conv, wa_bd, wx_bd, b_rg_a, b_rg_x, rg_lambda)
    nbs = t // SEL_BLOCK
    kc = _compress_prompt(p_kc, pe2, ck1, ck2, cst['pk']).reshape(KV_HEADS, 2 * nbs, LANES)
    vc = _compress_prompt(p_vc, pe2, cv1, cv2, cst['pk']).reshape(KV_HEADS, 2 * nbs, LANES)
    o_attn = _nsa_prompt(qa, kc, vc, ksa, vst, kwa, vwt, p_gn, cst['rplace'], cst['pall'])
    x2 = _merge(xp, y_rec, o_attn, w_grec, w_gatt, w_rec_b, w_attn_b, w_out_b, ln2_g, ln2_b)
    y_prompt = _ffn_ln(x2, up2, down2, ln3_g, ln3_b).reshape(bp, t, d)

    kvh = lambda a: a.reshape(bp, -1, KV_HEADS, HEAD_DIM)
    p_states = (kvh(p_kc), kvh(p_vc), kvh(p_ks), kvh(p_vs), kvh(p_kw[t - WINDOW:]), kvh(p_vw[t - WINDOW:]),
                p_tail[SUBLANES - (CONV_W - 1):].reshape(bp, CONV_W - 1, d_rnn), p_h.reshape(bp, d_rnn))

    m_s = bs * n_tok
    xs = _ffn_ln(x_sample.reshape(m_s, d), up1, down1, ln1_g, ln1_b)
    zs = _matmul(xs, w_attn_all, 512)
    s_xrg = zs[:, :o_q].reshape(bs, n_tok * o_q)
    seq = lambda a: a.reshape(bs, n_tok, a.shape[-1])
    s_q = seq(zs[:, o_q:o_kv])
    s_kv = [seq(zs[:, o_kv + n * KV_W:o_kv + (n + 1) * KV_W]) for n in range(6)]
    s_gn = seq(zs[:, o_gn:o_gn + LANES])
    pad8 = lambda a: jnp.pad(a, ((0, 0), (0, SUBLANES - n_tok), (0, 0)))
    ys_rec, s_conv, s_h = _rglru_sample(
        s_xrg, state_conv.reshape(bs, -1), state_h, w_conv, b_conv, wa_bd, wx_bd, b_rg_a, b_rg_x, rg_lambda,
        start0=(past == 0))
    nbs_s = past // SEL_BLOCK
    kc_s = _compress_paged(cache_k_cmp, page_table, pe2, ck1, ck2)
    vc_s = _compress_paged(cache_v_cmp, page_table, pe2, cv1, cv2)
    o8 = _nsa_sample(page_table, pad8(s_q), kc_s, vc_s, cache_k_sel, cache_v_sel,
                     [pad8(s_kv[n]) for n in (2, 3, 4, 5)],
                     state_win_k.reshape(bs, WINDOW, KV_W), state_win_v.reshape(bs, WINDOW, KV_W),
                     pad8(s_gn), cst['pgk'], cst['rkg'], cst['egs'], n_tok)
    del nbs_s
    os_attn = o8[:, :n_tok].reshape(m_s, ATTN_W).astype(BF)
    x2s = _merge(xs, ys_rec.reshape(m_s, d_rnn), os_attn, w_grec, w_gatt, w_rec_b, w_attn_b, w_out_b, ln2_g, ln2_b)
    y_sample = _ffn_ln(x2s, up2, down2, ln3_g, ln3_b).reshape(bs, n_tok, d)

    kvs = lambda a: a.reshape(bs, n_tok, KV_HEADS, HEAD_DIM)
    win = lambda old, new: jnp.concatenate([old, kvs(new)], axis=1)[:, -WINDOW:]
    s_states = (kvs(s_kv[0]), kvs(s_kv[1]), kvs(s_kv[2]), kvs(s_kv[3]),
                win(state_win_k, s_kv[4]), win(state_win_v, s_kv[5]),
                s_conv.reshape(bs, CONV_W - 1, d_rnn), s_h)

    return (y_prompt, y_sample) + p_states + s_states
```

```python
import functools

import numpy as np
import jax
import jax.numpy as jnp
from jax import lax
from jax.experimental import pallas as pl
from jax.experimental.pallas import tpu as pltpu

F32 = jnp.float32
BF = jnp.bfloat16

DEPTH = 1
ALPHA = (2.0 * DEPTH) ** 0.25
N_HEADS = 16
HEAD_DIM = 64
KV_HEADS = 4
GROUP = N_HEADS // KV_HEADS
KV_W = KV_HEADS * HEAD_DIM
ATTN_W = N_HEADS * HEAD_DIM
N_NSA_BRANCH = 3
CMP_BLOCK = 32
SEL_BLOCK = 64
N_SEL = 16
WINDOW = 512
Q_BLOCK = 128
CONV_W = 4
LRU_C = 8.0
RNN_BLOCKS = 16
LN_EPS = 1e-5
PAGE_SIZE = 128

LANES = 128
SUBLANES = 8
VMEM_LIMIT = 56 * 1024 * 1024

NEG = -1e30
SEL_TILE = 512
BLOCKS_PER_TILE = SEL_TILE // SEL_BLOCK
CMP_ROW = 16
CMP_ROW_W = CMP_ROW * KV_W

_NT = (((1,), (1,)), ((), ()))


def _dot(a, b):
    return jnp.dot(a, b, preferred_element_type=F32)


def _dot_nt(a, b):
    return lax.dot_general(a, b, _NT, preferred_element_type=F32)


def _params(*sem):
    return pltpu.CompilerParams(dimension_semantics=sem, vmem_limit_bytes=VMEM_LIMIT)


def _layer_norm(y, g, b):
    mu = jnp.mean(y, axis=-1, keepdims=True)
    d = y - mu
    var = jnp.mean(d * d, axis=-1, keepdims=True)
    return d * lax.rsqrt(var + LN_EPS) * g + b


def _masked_softmax(s, mask):
    s = jnp.where(mask, s, -jnp.inf)
    m = jnp.max(s, axis=-1, keepdims=True)
    m = jnp.where(m == -jnp.inf, 0.0, m)
    e = jnp.where(mask, jnp.exp(s - m), 0.0)
    den = jnp.sum(e, axis=-1, keepdims=True)
    return e / jnp.maximum(den, 1e-30)


def _split3(x):
    hi = x.astype(BF)
    r1 = x - hi.astype(F32)
    mid = r1.astype(BF)
    lo = (r1 - mid.astype(F32)).astype(BF)
    return hi, mid, lo


def _expand_f32(x, e):
    hi, mid, lo = _split3(x)
    return _dot(hi, e) + _dot(mid, e) + _dot(lo, e)


def _ffn_kernel(x_ref, wg_ref, wu_ref, wd_ref, g_ref, b_ref, o_ref, xb_ref, acc_ref):
    j = pl.program_id(1)

    @pl.when(j == 0)
    def _():
        xb_ref[...] = x_ref[...].astype(BF)
        acc_ref[...] = jnp.zeros_like(acc_ref)

    xb = xb_ref[...]
    gate = _dot(xb, wg_ref[...])
    up = _dot(xb, wu_ref[...])
    h = (gate * jax.nn.sigmoid(gate) * up).astype(BF)
    acc_ref[...] += _dot(h, wd_ref[...])

    @pl.when(j == pl.num_programs(1) - 1)
    def _():
        y = ALPHA * x_ref[...] + 0.5 * acc_ref[...]
        o_ref[...] = _layer_norm(y, g_ref[...], b_ref[...])


def _ffn_ln(x, w_up, w_down, g, b):
    m, d = x.shape
    f = w_down.shape[0]
    tm = min(512, m)
    tn = 512
    nj = f // tn
    return pl.pallas_call(
        _ffn_kernel,
        grid=(m // tm, nj),
        in_specs=[
            pl.BlockSpec((tm, d), lambda i, j: (i, 0)),
            pl.BlockSpec((d, tn), lambda i, j: (0, j)),
            pl.BlockSpec((d, tn), lambda i, j: (0, j + nj)),
            pl.BlockSpec((tn, d), lambda i, j: (j, 0)),
            pl.BlockSpec((1, d), lambda i, j: (0, 0)),
            pl.BlockSpec((1, d), lambda i, j: (0, 0)),
        ],
        out_specs=pl.BlockSpec((tm, d), lambda i, j: (i, 0)),
        out_shape=jax.ShapeDtypeStruct((m, d), F32),
        scratch_shapes=[pltpu.VMEM((tm, d), BF), pltpu.VMEM((tm, d), F32)],
        compiler_params=_params("parallel", "arbitrary"),
        name="ffn_ln",
    )(x, w_up, w_up, w_down, g.reshape(1, d), b.reshape(1, d))


def _mm_kernel(x_ref, w_ref, o_ref):
    o_ref[...] = _dot(x_ref[...].astype(BF), w_ref[...])


def _matmul(x, w, tn):
    m, d = x.shape
    n = w.shape[1]
    tm = min(512, m)
    return pl.pallas_call(
        _mm_kernel,
        grid=(m // tm, n // tn),
        in_specs=[pl.BlockSpec((tm, d), lambda i, j: (i, 0)),
                  pl.BlockSpec((d, tn), lambda i, j: (0, j))],
        out_specs=pl.BlockSpec((tm, tn), lambda i, j: (i, j)),
        out_shape=jax.ShapeDtypeStruct((m, n), F32),
        compiler_params=_params("parallel", "arbitrary"),
        name="proj",
    )(x, w)


def _proj_attn_kernel(x_ref, wq_ref, wkv_ref, wgn_ref, pq_ref, pk_ref, pkt_ref,
                      kc_ref, vc_ref, ks_ref, vs_ref, kw_ref, vw_ref, gn_ref,
                      qa_ref, ksa_ref, vst_ref, kwa_ref, vwt_ref):
    tm = x_ref.shape[0]
    xb = x_ref[...].astype(BF)
    zq = _dot(xb, wq_ref[...]).astype(BF)
    for h in range(N_HEADS):
        pair = zq[:, (h // 2) * LANES:(h // 2 + 1) * LANES]
        qa_ref[h] = _dot(pair, pq_ref[h % 2]).astype(BF)
    zkv = _dot(xb, wkv_ref[...])
    parts = [zkv[:, n * KV_W:(n + 1) * KV_W] for n in range(6)]
    for ref, part in zip((kc_ref, vc_ref, ks_ref, vs_ref, kw_ref, vw_ref), parts):
        ref[...] = part
    gn_ref[...] = _dot(xb, wgn_ref[...])
    t = pl.program_id(0) * tm + lax.broadcasted_iota(jnp.int32, (tm, 1), 0)
    lane = lax.broadcasted_iota(jnp.int32, (1, LANES), 1)
    onehot = jnp.where(lane == HEAD_DIM + (t // SEL_BLOCK) % BLOCKS_PER_TILE, 1.0, 0.0)
    ksb = parts[2].astype(BF)
    kwb = parts[4].astype(BF)
    vsb = parts[3].astype(BF)
    vwb = parts[5].astype(BF)
    for k in range(KV_HEADS):
        ksa_ref[k] = (_dot(ksb, pk_ref[k]) + onehot).astype(BF)
        kwa_ref[k] = _dot(kwb, pk_ref[k]).astype(BF)
        vst_ref[k, 0] = _dot_nt(pkt_ref[k], vsb).astype(BF)
        vwt_ref[k] = _dot_nt(pkt_ref[k], vwb).astype(BF)


def _proj_attn(x, wq, wkv, wgn, pq, pk, pkt):
    m, d = x.shape
    tm = 256
    per_tile = SEL_TILE // tm
    row = lambda w: pl.BlockSpec((tm, w), lambda i: (i, 0))
    const = lambda a: pl.BlockSpec(a.shape, lambda i: (0,) * a.ndim)
    head = lambda n: pl.BlockSpec((n, tm, LANES), lambda i: (0, i, 0))
    f32 = lambda w: jax.ShapeDtypeStruct((m, w), F32)
    aug = lambda n: jax.ShapeDtypeStruct((n, m, LANES), BF)
    return pl.pallas_call(
        _proj_attn_kernel,
        grid=(m // tm,),
        in_specs=[row(d), const(wq), const(wkv), const(wgn), const(pq), const(pk), const(pkt)],
        out_specs=[row(KV_W)] * 6 + [row(LANES), head(N_HEADS), head(KV_HEADS),
                                     pl.BlockSpec((KV_HEADS, 1, HEAD_DIM, tm),
                                                  lambda i: (0, i // per_tile, 0, i % per_tile)),
                                     head(KV_HEADS),
                                     pl.BlockSpec((KV_HEADS, HEAD_DIM, tm), lambda i: (0, 0, i))],
        out_shape=[f32(KV_W)] * 6 + [f32(LANES), aug(N_HEADS), aug(KV_HEADS),
                                     jax.ShapeDtypeStruct((KV_HEADS, m // SEL_TILE, HEAD_DIM, SEL_TILE), BF),
                                     aug(KV_HEADS),
                                     jax.ShapeDtypeStruct((KV_HEADS, HEAD_DIM, m), BF)],
        compiler_params=_params("parallel"),
        name="proj_attn",
    )(x, wq, wkv, wgn, pq, pk, pkt)


def _rglru_gates(xc, wa, wx, ba, bx, lam, is_start):
    xcb = xc.astype(BF)
    r = jax.nn.sigmoid(_dot(xcb, wa) + ba)
    ig = jax.nn.sigmoid(_dot(xcb, wx) + bx)
    log_a = -LRU_C * r * jax.nn.softplus(-lam)
    a = jnp.exp(log_a)
    if is_start is True:
        return a, ig * xc
    th = jnp.tanh(log_a)
    mult = jnp.sqrt(-2.0 * th / (1.0 - th))
    if is_start is not None:
        mult = jnp.where(is_start, 1.0, mult)
    return a, mult * (ig * xc)


def _rglru_prompt_kernel(x_ref, gate_ref, wc_ref, bc_ref, wa_ref, wx_ref, ba_ref, bx_ref, lam_ref,
                         y_ref, tail_ref, hl_ref, h_scr, tail_scr):
    c = pl.program_id(1)
    tc = x_ref.shape[0]

    @pl.when(c == 0)
    def _():
        h_scr[...] = jnp.zeros_like(h_scr)
        tail_scr[...] = jnp.zeros_like(tail_scr)

    x = x_ref[...]
    prev = tail_scr[...]
    row8 = lax.broadcasted_iota(jnp.int32, (SUBLANES, 1), 0)
    row = lax.broadcasted_iota(jnp.int32, (tc, 1), 0)

    def shifted(s):
        rolled = pltpu.roll(x, s, 0)
        top = jnp.where(row8 < s, pltpu.roll(prev, s, 0), rolled[:SUBLANES])
        return jnp.concatenate([top, rolled[SUBLANES:]], axis=0)

    wc = wc_ref[...]
    conv = wc[0:1] * shifted(3)
    conv = conv + wc[1:2] * shifted(2)
    conv = conv + wc[2:3] * shifted(1)
    conv = conv + wc[3:4] * x
    xc = bc_ref[...] + conv

    is_start = (row + c * tc) == 0
    a, u = _rglru_gates(xc, wa_ref[0], wx_ref[0], ba_ref[...], bx_ref[...], lam_ref[...], is_start)

    d = 1
    while d < tc:
        keep = row >= d
        a_sh = jnp.where(keep, pltpu.roll(a, d, 0), 1.0)
        u_sh = jnp.where(keep, pltpu.roll(u, d, 0), 0.0)
        u = a * u_sh + u
        a = a * a_sh
        d *= 2
    h = a * h_scr[...] + u

    y_ref[...] = (jax.nn.gelu(gate_ref[...]) * h).astype(BF)
    h_last = h[tc - 1:tc]
    h_scr[...] = h_last
    tail_scr[...] = x[tc - SUBLANES:]
    hl_ref[...] = h_last
    tail_ref[...] = x[tc - SUBLANES:]


def _rglru_prompt(xrg, w_conv, b_conv, wa_bd, wx_bd, b_a, b_x, lam):
    t = xrg.shape[0]
    d_rnn = w_conv.shape[1]
    gw = 2 * LANES
    ng = d_rnn // gw
    tc = min(512, t)
    vec = lambda: pl.BlockSpec((1, gw), lambda g, c: (0, g))
    return pl.pallas_call(
        _rglru_prompt_kernel,
        grid=(ng, t // tc),
        in_specs=[
            pl.BlockSpec((tc, gw), lambda g, c: (c, g)),
            pl.BlockSpec((tc, gw), lambda g, c: (c, g + ng)),
            pl.BlockSpec((CONV_W, gw), lambda g, c: (0, g)),
            vec(),
            pl.BlockSpec((1, gw, gw), lambda g, c: (g, 0, 0)),
            pl.BlockSpec((1, gw, gw), lambda g, c: (g, 0, 0)),
            vec(), vec(), vec(),
        ],
        out_specs=[
            pl.BlockSpec((tc, gw), lambda g, c: (c, g)),
            pl.BlockSpec((SUBLANES, gw), lambda g, c: (0, g)),
            pl.BlockSpec((1, gw), lambda g, c: (0, g)),
        ],
        out_shape=[
            jax.ShapeDtypeStruct((t, d_rnn), BF),
            jax.ShapeDtypeStruct((SUBLANES, d_rnn), F32),
            jax.ShapeDtypeStruct((1, d_rnn), F32),
        ],
        scratch_shapes=[pltpu.VMEM((1, gw), F32), pltpu.VMEM((SUBLANES, gw), F32)],
        compiler_params=_params("parallel", "arbitrary"),
        name="rglru_prompt",
    )(xrg, xrg, w_conv, b_conv.reshape(1, -1), wa_bd, wx_bd,
      b_a.reshape(1, -1), b_x.reshape(1, -1), lam.reshape(1, -1))


def _rglru_sample_kernel(xrg_ref, cp_ref, h0_ref, wc_ref, bc_ref, wa_ref, wx_ref, ba_ref, bx_ref, lam_ref,
                         y_ref, cn_ref, hl_ref, *, n_tok, start0):
    d_rnn = h0_ref.shape[1]
    gw = wa_ref.shape[1]
    wc = wc_ref[...]
    xp = [cp_ref[:, k * d_rnn:(k + 1) * d_rnn] for k in range(CONV_W - 1)]
    xp += [xrg_ref[:, t * 2 * d_rnn:t * 2 * d_rnn + d_rnn] for t in range(n_tok)]
    h = h0_ref[...]
    for t in range(n_tok):
        conv = wc[0:1] * xp[t]
        for k in range(1, CONV_W):
            conv = conv + wc[k:k + 1] * xp[t + k]
        xc = bc_ref[...] + conv
        a_parts, u_parts = [], []
        for g in range(d_rnn // gw):
            sl = slice(g * gw, (g + 1) * gw)
            a_g, u_g = _rglru_gates(xc[:, sl], wa_ref[g], wx_ref[g], ba_ref[:, sl], bx_ref[:, sl],
                                    lam_ref[:, sl], True if (start0 and t == 0) else None)
            a_parts.append(a_g)
            u_parts.append(u_g)
        a = jnp.concatenate(a_parts, axis=1)
        u = jnp.concatenate(u_parts, axis=1)
        h = a * h + u
        gate = xrg_ref[:, t * 2 * d_rnn + d_rnn:(t + 1) * 2 * d_rnn]
        y_ref[:, t * d_rnn:(t + 1) * d_rnn] = (jax.nn.gelu(gate) * h).astype(BF)
    hl_ref[...] = h
    tail = xp[-(CONV_W - 1):]
    for k in range(CONV_W - 1):
        cn_ref[:, k * d_rnn:(k + 1) * d_rnn] = tail[k]


def _rglru_sample(xrg, conv_prev, h0, w_conv, b_conv, wa_bd, wx_bd, b_a, b_x, lam, start0):
    b, d_rnn = h0.shape
    n_tok = xrg.shape[1] // (2 * d_rnn)
    args = (xrg, conv_prev, h0, w_conv, b_conv.reshape(1, -1), wa_bd, wx_bd,
            b_a.reshape(1, -1), b_x.reshape(1, -1), lam.reshape(1, -1))
    full = lambda a: pl.BlockSpec(a.shape, lambda i: (0,) * a.ndim)
    outs = [jax.ShapeDtypeStruct((b, n_tok * d_rnn), BF),
            jax.ShapeDtypeStruct((b, (CONV_W - 1) * d_rnn), F32),
            jax.ShapeDtypeStruct((b, d_rnn), F32)]
    return pl.pallas_call(
        functools.partial(_rglru_sample_kernel, n_tok=n_tok, start0=start0),
        grid=(1,),
        in_specs=[full(a) for a in args],
        out_specs=[full(o) for o in outs],
        out_shape=outs,
        compiler_params=_params("arbitrary"),
        name="rglru_sample",
    )(*args)


def _compress_kernel(x_ref, pe_ref, w1_ref, w2_ref, pk_ref, o_ref, scr):
    x = x_ref[...]
    rows = x.shape[0]
    xb = (x.reshape(rows // SUBLANES, SUBLANES, CMP_ROW_W) + pe_ref[...][None]).reshape(rows, CMP_ROW_W)
    full = _dot(xb.astype(BF), w1_ref[...])
    hid = full[:, :KV_W] + pltpu.roll(full[:, KV_W:], rows - 1, 0)
    out = _dot(jax.nn.gelu(hid).astype(BF), w2_ref[...])
    nb = rows // 4
    ob = out.astype(BF)
    for k in range(KV_HEADS):
        scr[...] = _dot(ob, pk_ref[k])
        o_ref[k, 0] = scr[pl.ds(0, nb, stride=4), :].astype(BF)
        o_ref[k, 1] = scr[pl.ds(2, nb, stride=4), :].astype(BF)


def _compress_prompt(kv, pe2, w1, w2, pk):
    t = kv.shape[0]
    rows = 128
    x = kv.reshape(t // CMP_ROW, CMP_ROW_W)
    nb = rows // 4
    nbs = t // SEL_BLOCK
    const = lambda a: pl.BlockSpec(a.shape, lambda i: (0,) * a.ndim)
    return pl.pallas_call(
        _compress_kernel,
        grid=(x.shape[0] // rows,),
        in_specs=[pl.BlockSpec((rows, CMP_ROW_W), lambda i: (i, 0)),
                  const(pe2), const(w1), const(w2), const(pk)],
        out_specs=pl.BlockSpec((KV_HEADS, 2, nb, LANES), lambda i: (0, 0, i, 0)),
        out_shape=jax.ShapeDtypeStruct((KV_HEADS, 2, nbs, LANES), BF),
        scratch_shapes=[pltpu.VMEM((rows, LANES), F32)],
        compiler_params=_params("parallel"),
        name="compress_prompt",
    )(x, pe2, w1, w2, pk)


def _compress_paged_kernel(pt_ref, *refs, pages):
    del pt_ref
    x_refs = refs[:pages]
    pe_ref, w1_ref, w2_ref, o_ref, xs_scr, out_scr = refs[pages:]
    halves = KV_W // LANES
    for p in range(pages):
        xt = x_refs[p][0].T + pe_ref[...]
        for h in range(halves):
            xs_scr[h, p * PAGE_SIZE:(p + 1) * PAGE_SIZE, :] = xt[:, h * LANES:(h + 1) * LANES]
    nblk = pages * (PAGE_SIZE // CMP_BLOCK)
    acc = None
    for l2 in range(CMP_BLOCK // 2):
        parts = []
        for h in range(halves):
            a = xs_scr[h, pl.ds(2 * l2, nblk, stride=CMP_BLOCK), :]
            b = xs_scr[h, pl.ds(2 * l2 + 1, nblk, stride=CMP_BLOCK), :]
            parts.append(jnp.concatenate([a, b], axis=1))
        d = _dot(jnp.concatenate(parts, axis=0).astype(BF), w1_ref[l2])
        acc = d if acc is None else acc + d
    out_scr[...] = _dot(jax.nn.gelu(acc).astype(BF), w2_ref[...])
    for h in range(halves):
        for par in range(2):
            o_ref[0, par, :, h * LANES:(h + 1) * LANES] = (
                out_scr[pl.ds(h * nblk + par, nblk // 2, stride=2), :].astype(BF))


def _compress_paged(cache, page_table, pe_tok, w1p, w2p):
    b, n_pages = page_table.shape
    pages = 16
    x = jnp.transpose(cache, (0, 2, 3, 1)).reshape(cache.shape[0], KV_W, PAGE_SIZE)
    nblk = pages * (PAGE_SIZE // CMP_BLOCK)
    halves = KV_W // LANES
    nbs = n_pages * PAGE_SIZE // SEL_BLOCK
    const = lambda a: pl.BlockSpec(a.shape, lambda i, s, pt: (0,) * a.ndim)
    page_specs = [
        pl.BlockSpec((1, KV_W, PAGE_SIZE),
                     functools.partial(lambda i, s, pt, p: (pt[i, s * pages + p], 0, 0), p=p))
        for p in range(pages)]
    return pl.pallas_call(
        functools.partial(_compress_paged_kernel, pages=pages),
        grid_spec=pltpu.PrefetchScalarGridSpec(
            num_scalar_prefetch=1,
            grid=(b, n_pages // pages),
            in_specs=page_specs + [const(pe_tok), const(w1p), const(w2p)],
            out_specs=pl.BlockSpec((1, 2, nblk // 2, KV_W), lambda i, s, pt: (i, 0, s, 0)),
            scratch_shapes=[pltpu.VMEM((halves, pages * PAGE_SIZE, LANES), F32),
                            pltpu.VMEM((halves * nblk, LANES), F32)],
        ),
        out_shape=jax.ShapeDtypeStruct((b, 2, nbs, KV_W), BF),
        compiler_params=_params("parallel", "arbitrary"),
        name="compress_paged",
    )(page_table, *([x] * pages), pe_tok, w1p, w2p)


def _topk_columns(imp_t, n_pick):
    nblk = imp_t.shape[0]
    idx = lax.broadcasted_iota(jnp.int32, imp_t.shape, 0)
    sel = jnp.zeros(imp_t.shape, F32)
    for _ in range(n_pick):
        mx = jnp.max(imp_t, axis=0, keepdims=True)
        first = jnp.min(jnp.where(imp_t == mx, idx, nblk), axis=0, keepdims=True)
        hit = idx == first
        sel = jnp.where(hit, 1.0, sel)
        imp_t = jnp.where(hit, -jnp.inf, imp_t)
    return sel


def _softmax_cols(s):
    m = jnp.max(s, axis=0, keepdims=True)
    m = jnp.where(m == -jnp.inf, 0.0, m)
    e = jnp.exp(s - m)
    den = jnp.sum(e, axis=0, keepdims=True)
    return e * (1.0 / jnp.maximum(den, 1e-30))


def _nsa_prompt_kernel(*refs):
    (q_ref, kc_ref, vc_ref, ks_ref, vst_ref) = refs[:5]
    kw_refs = refs[5:10]
    vwt_refs = refs[10:15]
    (gn_ref, rp_ref, pa_ref, o_ref,
     bias_scr, gate_scr, sa_scr, sb_scr, m_scr, l_scr, acc_scr) = refs[15:]
    kvh = pl.program_id(0)
    i = pl.program_id(1)
    nq = Q_BLOCK
    r = GROUP * nq
    hd = HEAD_DIM
    q = q_ref[...].reshape(r, LANES)
    lane = lax.broadcasted_iota(jnp.int32, (1, r), 1)
    qpos = i * nq + lane % nq

    kc = kc_ref[0]
    n2 = kc.shape[0]
    nbs = n2 // 2
    rowc = lax.broadcasted_iota(jnp.int32, (n2, 1), 0)
    blk = 2 * (rowc % nbs) + rowc // nbs
    p_c = _softmax_cols(jnp.where(blk * CMP_BLOCK + (CMP_BLOCK - 1) <= qpos, _dot_nt(kc, q), -jnp.inf))
    vct = vc_ref[0].astype(F32).T[:hd].astype(BF)
    o_c = _dot(vct, p_c.astype(BF))

    imp = p_c[:, 0:nq]
    for g in range(1, GROUP):
        imp = imp + p_c[:, g * nq:(g + 1) * nq]
    imp = imp[:nbs] + imp[nbs:]
    qp = i * nq + lax.broadcasted_iota(jnp.int32, (1, nq), 1)
    jb = lax.broadcasted_iota(jnp.int32, (nbs, 1), 0)
    cur = qp // SEL_BLOCK
    valid = jb * SEL_BLOCK <= qp
    forced = (jb == 0) | (jb == cur) | (jb == cur - 1)
    imp = jnp.where(valid, jnp.where(forced, jnp.inf, imp), -jnp.inf)
    sel = _topk_columns(imp, min(N_SEL, nbs))
    bias = jnp.where(valid & (sel > 0.5), 0.0, NEG).T.astype(BF)
    bias_all = _dot(bias, pa_ref[...]).astype(BF)
    n_tiles = bias_scr.shape[0]
    for t in range(n_tiles):
        bias_scr[t] = bias_all[:, t * LANES:(t + 1) * LANES]

    def scores(t):
        k0 = pl.multiple_of(t * SEL_TILE, SEL_TILE)
        kt = ks_ref[0, pl.ds(k0, SEL_TILE), :]
        qa = q + jnp.concatenate([bias_scr[t]] * GROUP, axis=0)
        return _dot_nt(kt, qa)

    def update(t, s_ref, diagonal):
        vt = vst_ref[0, t]
        for g in range(GROUP):
            sl = slice(g * nq, (g + 1) * nq)
            sg = s_ref[:, sl]
            if diagonal:
                kpos = t * SEL_TILE + lax.broadcasted_iota(jnp.int32, (SEL_TILE, 1), 0)
                sg = jnp.where(kpos <= qpos[:, sl], sg, NEG)
            m_old = m_scr[:, sl]
            m_new = jnp.maximum(m_old, jnp.max(sg, axis=0, keepdims=True))
            scale = jnp.exp(m_old - m_new)
            p = jnp.exp(sg - m_new)
            l_scr[:, sl] = scale * l_scr[:, sl] + jnp.sum(p, axis=0, keepdims=True)
            acc_scr[:, sl] = scale * acc_scr[:, sl] + _dot(vt, p.astype(BF))
            m_scr[:, sl] = m_new

    t_diag = (i * nq) // SEL_TILE
    n_pairs = t_diag // 2
    m_scr[...] = jnp.full(m_scr.shape, NEG, F32)
    l_scr[...] = jnp.zeros_like(l_scr)
    acc_scr[...] = jnp.zeros_like(acc_scr)
    sa_scr[...] = scores(0)

    def pair(u, _):
        t0 = 2 * u
        sb_scr[...] = scores(t0 + 1)
        update(t0, sa_scr, False)
        sa_scr[...] = scores(t0 + 2)
        update(t0 + 1, sb_scr, False)
        return 0

    lax.fori_loop(0, n_pairs, pair, 0)

    @pl.when(t_diag % 2 == 1)
    def _():
        sb_scr[...] = scores(t_diag)
        update(t_diag - 1, sa_scr, False)
        update(t_diag, sb_scr, True)

    @pl.when(t_diag % 2 == 0)
    def _():
        update(t_diag, sa_scr, True)

    o_s = acc_scr[...] * (1.0 / l_scr[...])

    kw = jnp.concatenate([ref[0] for ref in kw_refs], axis=0)
    vwt = jnp.concatenate([ref[0] for ref in vwt_refs], axis=1)
    s_w = _dot_nt(kw, q)
    nwb = len(kw_refs)
    krow = lax.broadcasted_iota(jnp.int32, (nq, 1), 0)
    parts = []
    for b in range(nwb):
        blk_pos = (i - (nwb - 1) + b) * nq
        mask = blk_pos >= 0
        if b == 0:
            mask = mask & (qpos - (blk_pos + krow) <= WINDOW)
        if b == nwb - 1:
            mask = mask & (blk_pos + krow <= qpos)
        parts.append(jnp.where(mask, s_w[b * nq:(b + 1) * nq], -jnp.inf))
    p_w = _softmax_cols(jnp.concatenate(parts, axis=0))
    o_w = _dot(vwt, p_w.astype(BF))

    gate_scr[...] = jax.nn.sigmoid(gn_ref[...]).T

    def gate(br):
        return jnp.concatenate(
            [gate_scr[pl.ds(br * N_HEADS + kvh * GROUP + g, 1), :] for g in range(GROUP)], axis=1)

    comb = gate(0) * o_c + gate(1) * o_s + gate(2) * o_w
    comb = jnp.concatenate([comb, jnp.zeros_like(comb)], axis=0)
    out = None
    for g in range(GROUP):
        part = _dot(comb[:, g * nq:(g + 1) * nq].T.astype(BF), rp_ref[g])
        out = part if out is None else out + part
    o_ref[...] = out.astype(BF)


def _nsa_prompt(qa, kc, vc, ksa, vst, kwa, vwt, gn, rplace, pall):
    t = qa.shape[1]
    n_tiles = t // SEL_TILE
    wb = WINDOW // Q_BLOCK + 1
    kv_full = lambda a: pl.BlockSpec((1,) + a.shape[1:], lambda k, i: (k,) + (0,) * (a.ndim - 1))
    wblk = lambda i, m: jnp.maximum(i - (wb - 1) + m, 0)
    win_k = [pl.BlockSpec((1, Q_BLOCK, LANES), functools.partial(lambda k, i, m: (k, wblk(i, m), 0), m=m))
             for m in range(wb)]
    win_v = [pl.BlockSpec((1, HEAD_DIM, Q_BLOCK), functools.partial(lambda k, i, m: (k, 0, wblk(i, m)), m=m))
             for m in range(wb)]
    const = lambda a: pl.BlockSpec(a.shape, lambda k, i: (0,) * a.ndim)
    return pl.pallas_call(
        _nsa_prompt_kernel,
        grid=(KV_HEADS, t // Q_BLOCK),
        in_specs=[pl.BlockSpec((GROUP, Q_BLOCK, LANES), lambda k, i: (k, i, 0)),
                  kv_full(kc), kv_full(vc), kv_full(ksa), kv_full(vst)]
                 + win_k + win_v
                 + [pl.BlockSpec((Q_BLOCK, LANES), lambda k, i: (i, 0)), const(rplace), const(pall)],
        out_specs=pl.BlockSpec((Q_BLOCK, KV_W), lambda k, i: (i, k)),
        out_shape=jax.ShapeDtypeStruct((t, ATTN_W), BF),
        scratch_shapes=[pltpu.VMEM((n_tiles, Q_BLOCK, LANES), BF), pltpu.VMEM((LANES, Q_BLOCK), F32),
                        pltpu.VMEM((SEL_TILE, GROUP * Q_BLOCK), F32), pltpu.VMEM((SEL_TILE, GROUP * Q_BLOCK), F32),
                        pltpu.VMEM((1, GROUP * Q_BLOCK), F32), pltpu.VMEM((1, GROUP * Q_BLOCK), F32),
                        pltpu.VMEM((HEAD_DIM, GROUP * Q_BLOCK), F32)],
        compiler_params=_params("parallel", "arbitrary"),
        name="nsa_prompt",
    )(qa, kc, vc, ksa, vst, *([kwa] * wb), *([vwt] * wb), gn, rplace, pall)


def _nsa_sample_kernel(*refs, pages, past, n_tok):
    pt_ref = refs[0]
    q_ref, kc_ref, vc_ref = refs[1:4]
    ks_refs = refs[4:4 + pages]
    vs_refs = refs[4 + pages:4 + 2 * pages]
    (ksn_ref, vsn_ref, kwn_ref, vwn_ref, wk_ref, wv_ref, gn_ref,
     pg_ref, rp_ref, eg_ref, o_ref, q_scr, bias_scr, m_scr, l_scr, acc_scr, oc_scr) = refs[4 + 2 * pages:]
    del pt_ref
    s_id = pl.program_id(1)
    tp = SUBLANES
    r = N_HEADS * tp
    row = lax.broadcasted_iota(jnp.int32, (r, 1), 0)
    qpos = past + row % tp
    nbs = bias_scr.shape[1]

    @pl.when(s_id == 0)
    def _():
        qb = q_ref[0].astype(BF)
        for h in range(N_HEADS):
            k = h // GROUP
            q_scr[h * tp:(h + 1) * tp, :] = _dot(qb[:, k * KV_W:(k + 1) * KV_W], pg_ref[h]).astype(BF)
        q = q_scr[...]
        kc = kc_ref[0].reshape(2 * nbs, KV_W)
        vc = vc_ref[0].reshape(2 * nbs, KV_W)
        col = lax.broadcasted_iota(jnp.int32, (1, 2 * nbs), 1)
        blk = 2 * (col % nbs) + col // nbs
        p_c = _masked_softmax(_dot_nt(q, kc), blk * CMP_BLOCK + (CMP_BLOCK - 1) <= qpos)
        oc_scr[...] = _dot(p_c.astype(BF), vc)
        imps = []
        for k in range(KV_HEADS):
            base = k * GROUP * tp
            imp = p_c[base:base + tp]
            for g in range(1, GROUP):
                imp = imp + p_c[base + g * tp:base + (g + 1) * tp]
            imps.append(imp)
        imp = jnp.concatenate(imps, axis=0)
        imp = imp[:, :nbs] + imp[:, nbs:]
        jb = lax.broadcasted_iota(jnp.int32, (1, nbs), 1)
        imp = jnp.where((jb == 0) | (jb == nbs - 1), jnp.inf, imp)
        idx = lax.broadcasted_iota(jnp.int32, imp.shape, 1)
        sel = jnp.zeros(imp.shape, F32)
        for _ in range(min(N_SEL, nbs + 1) - 1):
            mx = jnp.max(imp, axis=1, keepdims=True)
            first = jnp.min(jnp.where(imp == mx, idx, nbs), axis=1, keepdims=True)
            hit = idx == first
            sel = jnp.where(hit, 1.0, sel)
            imp = jnp.where(hit, -jnp.inf, imp)
        bias = jnp.where(sel > 0.5, 0.0, NEG).astype(BF)
        for k in range(KV_HEADS):
            for g in range(GROUP):
                h = k * GROUP + g
                bias_scr[h * tp:(h + 1) * tp, :] = bias[k * tp:(k + 1) * tp]
        m_scr[...] = jnp.full(m_scr.shape, NEG, F32)
        l_scr[...] = jnp.zeros_like(l_scr)
        acc_scr[...] = jnp.zeros_like(acc_scr)

    def online(st, v, v_transposed):
        m = m_scr[...]
        m_new = jnp.maximum(m, jnp.max(st, axis=1, keepdims=True))
        scale = jnp.exp(m - m_new)
        p = jnp.exp(st - m_new)
        l_scr[...] = scale * l_scr[...] + jnp.sum(p, axis=1, keepdims=True)
        pv = _dot_nt(p.astype(BF), v) if v_transposed else _dot(p.astype(BF), v)
        acc_scr[...] = scale * acc_scr[...] + pv
        m_scr[...] = m_new

    q = q_scr[...]
    kt = jnp.concatenate([ref[0].astype(BF) for ref in ks_refs], axis=1)
    vt = jnp.concatenate([ref[0].astype(BF) for ref in vs_refs], axis=1)
    n_keys = kt.shape[1]
    jb = lax.broadcasted_iota(jnp.int32, (nbs, 1), 0)
    key_blk = s_id * (n_keys // SEL_BLOCK) + lax.broadcasted_iota(jnp.int32, (1, n_keys), 1) // SEL_BLOCK
    expand = jnp.where(jb == key_blk, 1.0, 0.0).astype(BF)
    online(_dot(q, kt) + _dot(bias_scr[...], expand), vt, True)

    @pl.when(s_id == pl.num_programs(1) - 1)
    def _():
        zeros = jnp.zeros((LANES - tp, KV_W), BF)
        tok = lax.broadcasted_iota(jnp.int32, (1, LANES), 1)
        ksn = jnp.concatenate([ksn_ref[0].astype(BF), zeros], axis=0)
        vsn = jnp.concatenate([vsn_ref[0].astype(BF), zeros], axis=0)
        online(jnp.where((past + tok <= qpos) & (tok < n_tok), _dot_nt(q, ksn), NEG), vsn, False)
        o_s = acc_scr[...] / l_scr[...]
        kw = jnp.concatenate([wk_ref[0].astype(BF), kwn_ref[0].astype(BF), zeros], axis=0)
        vw = jnp.concatenate([wv_ref[0].astype(BF), vwn_ref[0].astype(BF), zeros], axis=0)
        n_w = kw.shape[0]
        wcol = lax.broadcasted_iota(jnp.int32, (1, n_w), 1)
        kwpos = past - WINDOW + wcol
        dpos = qpos - kwpos
        mask_w = (dpos >= 0) & (dpos <= WINDOW) & (kwpos >= 0) & (wcol < WINDOW + n_tok)
        p_w = _masked_softmax(_dot_nt(q, kw), mask_w)
        o_w = _dot(p_w.astype(BF), vw)
        sg = jax.nn.sigmoid(gn_ref[0])
        gexp = _expand_f32(sg, eg_ref[...])

        def gate(br):
            return jnp.concatenate(
                [gexp[:, (br * N_HEADS + h) * KV_W:(br * N_HEADS + h + 1) * KV_W] for h in range(N_HEADS)],
                axis=0)

        comb = (gate(0) * oc_scr[...] + gate(1) * o_s + gate(2) * o_w).astype(BF)
        for k in range(KV_HEADS):
            out = None
            for g in range(GROUP):
                h = k * GROUP + g
                part = _dot(comb[h * tp:(h + 1) * tp], rp_ref[h])
                out = part if out is None else out + part
            o_ref[0, :, k * KV_W:(k + 1) * KV_W] = out


def _nsa_sample(page_table, q8, kc, vc, cache_ks, cache_vs, new8, win_k, win_v, gn8, pgk, rkg, egate, n_tok):
    b, n_pages = page_table.shape
    past = n_pages * PAGE_SIZE
    pages = 8
    tp = SUBLANES
    r = N_HEADS * tp
    nbs = past // SEL_BLOCK
    ck = jnp.transpose(cache_ks, (0, 2, 3, 1)).reshape(cache_ks.shape[0], KV_W, PAGE_SIZE)
    cv = jnp.transpose(cache_vs, (0, 2, 3, 1)).reshape(cache_vs.shape[0], KV_W, PAGE_SIZE)
    per_seq = lambda a: pl.BlockSpec((1,) + a.shape[1:], lambda i, s, pt: (i,) + (0,) * (a.ndim - 1))
    const = lambda a: pl.BlockSpec(a.shape, lambda i, s, pt: (0,) * a.ndim)
    page_specs = [
        pl.BlockSpec((1, KV_W, PAGE_SIZE),
                     functools.partial(lambda i, s, pt, p: (pt[i, s * pages + p], 0, 0), p=p))
        for p in range(pages)]
    return pl.pallas_call(
        functools.partial(_nsa_sample_kernel, pages=pages, past=past, n_tok=n_tok),
        grid_spec=pltpu.PrefetchScalarGridSpec(
            num_scalar_prefetch=1,
            grid=(b, n_pages // pages),
            in_specs=[per_seq(q8), per_seq(kc), per_seq(vc)] + page_specs + page_specs
                     + [per_seq(a) for a in new8] + [per_seq(win_k), per_seq(win_v), per_seq(gn8),
                                                     const(pgk), const(rkg), const(egate)],
            out_specs=pl.BlockSpec((1, tp, ATTN_W), lambda i, s, pt: (i, 0, 0)),
            scratch_shapes=[pltpu.VMEM((r, KV_W), BF), pltpu.VMEM((r, nbs), BF),
                            pltpu.VMEM((r, 1), F32), pltpu.VMEM((r, 1), F32),
                            pltpu.VMEM((r, KV_W), F32), pltpu.VMEM((r, KV_W), F32)],
        ),
        out_shape=jax.ShapeDtypeStruct((b, tp, ATTN_W), F32),
        compiler_params=_params("parallel", "arbitrary"),
        name="nsa_sample",
    )(page_table, q8, kc, vc, *([ck] * pages), *([cv] * pages), *new8, win_k, win_v, gn8, pgk, rkg, egate)


def _merge_kernel(x_ref, y_ref, o_ref, wgr_ref, wga_ref, wr_ref, wa_ref, wo_ref, g_ref, b_ref,
                  out_ref, xb_ref, acc_ref):
    j = pl.program_id(1)

    @pl.when(j == 0)
    def _():
        xb_ref[...] = x_ref[...].astype(BF)
        acc_ref[...] = jnp.zeros_like(acc_ref)

    xb = xb_ref[...]
    rec = _dot(y_ref[...], wr_ref[...])
    att = _dot(o_ref[...], wa_ref[...])
    u = jax.nn.sigmoid(_dot(xb, wgr_ref[...])) * rec + jax.nn.sigmoid(_dot(xb, wga_ref[...])) * att
    acc_ref[...] += _dot(u.astype(BF), wo_ref[...])

    @pl.when(j == pl.num_programs(1) - 1)
    def _():
        out_ref[...] = _layer_norm(ALPHA * x_ref[...] + acc_ref[...], g_ref[...], b_ref[...])


def _merge(x, y_rec, o_attn, w_grec, w_gatt, w_rec_o, w_attn_o, w_out, g, b):
    m, d = x.shape
    tm = min(512, m)
    tn = 512
    col = lambda a: pl.BlockSpec((a.shape[0], tn), lambda i, j: (0, j))
    rowb = lambda a: pl.BlockSpec((tm, a.shape[1]), lambda i, j: (i, 0))
    vec = pl.BlockSpec((1, d), lambda i, j: (0, 0))
    return pl.pallas_call(
        _merge_kernel,
        grid=(m // tm, d // tn),
        in_specs=[rowb(x), rowb(y_rec), rowb(o_attn), col(w_grec), col(w_gatt), col(w_rec_o), col(w_attn_o),
                  pl.BlockSpec((tn, d), lambda i, j: (j, 0)), vec, vec],
        out_specs=pl.BlockSpec((tm, d), lambda i, j: (i, 0)),
        out_shape=jax.ShapeDtypeStruct((m, d), F32),
        scratch_shapes=[pltpu.VMEM((tm, d), BF), pltpu.VMEM((tm, d), F32)],
        compiler_params=_params("parallel", "arbitrary"),
        name="merge_out",
    )(x, y_rec, o_attn, w_grec, w_gatt, w_rec_o, w_attn_o, w_out, g.reshape(1, d), b.reshape(1, d))


def _place(n_rows, n_cols, src0, dst0, width, value=1.0):
    m = np.zeros((n_rows, n_cols), np.float32)
    m[src0 + np.arange(width), dst0 + np.arange(width)] = value
    return m


def _layout_constants(n_sel_blocks_prompt):
    hd = HEAD_DIM
    scale = hd ** -0.5
    pq = np.stack([_place(LANES, LANES, e * hd, 0, hd, scale) for e in range(2)])
    pk = np.stack([_place(KV_W, LANES, k * hd, 0, hd) for k in range(KV_HEADS)])
    rplace = np.stack([_place(LANES, KV_W, 0, g * hd, hd) for g in range(GROUP)])
    pkt = np.transpose(pk, (0, 2, 1))[:, :hd]
    nbs = n_sel_blocks_prompt
    n_tiles = -(-nbs // BLOCKS_PER_TILE)
    pall = np.zeros((nbs, n_tiles * LANES), np.float32)
    j = np.arange(nbs)
    pall[j, (j // BLOCKS_PER_TILE) * LANES + hd + j % BLOCKS_PER_TILE] = 1.0
    pgk = np.stack([_place(KV_W, KV_W, (h % GROUP) * hd, (h // GROUP) * hd, hd, scale) for h in range(N_HEADS)])
    rkg = np.stack([_place(KV_W, KV_W, (h // GROUP) * hd, (h % GROUP) * hd, hd) for h in range(N_HEADS)])
    egs = np.zeros((LANES, N_NSA_BRANCH * N_HEADS * KV_W), np.float32)
    for c in range(N_NSA_BRANCH * N_HEADS):
        egs[c, c * KV_W:(c + 1) * KV_W] = 1.0
    as_bf = lambda a: jnp.asarray(a, BF)
    return dict(pq=as_bf(pq), pk=as_bf(pk), pkt=as_bf(pkt), rplace=as_bf(rplace), pall=as_bf(pall),
                pgk=as_bf(pgk), rkg=as_bf(rkg), egs=as_bf(egs))


def _block_diag(w, per_group):
    nb, c, _ = w.shape
    eye = jnp.eye(per_group, dtype=w.dtype)
    wg = w.reshape(nb // per_group, per_group, c, c)
    return jnp.einsum('gpcd,pq->gpcqd', wg, eye).reshape(nb // per_group, per_group * c, per_group * c)


def _compress_weights(w1, w2, pe):
    eye = jnp.eye(KV_HEADS, dtype=w1.dtype)
    big = jnp.einsum('lde,kq->lkdqe', w1, eye).reshape(CMP_BLOCK * KV_W, KV_HEADS * w1.shape[2])
    half = big.shape[0] // 2
    w1cat = jnp.concatenate([big[:half], big[half:]], axis=1).astype(BF)
    w2bd = _block_diag(jnp.broadcast_to(w2, (KV_HEADS,) + w2.shape), KV_HEADS)[0].astype(BF)
    pe_flat = jnp.broadcast_to(pe[:, None, :], (CMP_BLOCK, KV_HEADS, HEAD_DIM)).reshape(2, half)
    pe2 = jnp.tile(pe_flat, (SUBLANES // 2, 1))
    return w1cat, w2bd, pe2


def _compress_weights_paged(w1, w2, pe):
    per_half = LANES // HEAD_DIM
    eye = jnp.eye(per_half, dtype=w1.dtype)
    w1h = jnp.einsum('lde,kq->lkdqe', w1, eye).reshape(CMP_BLOCK, LANES, LANES)
    w1p = w1h.reshape(CMP_BLOCK // 2, 2 * LANES, LANES).astype(BF)
    w2p = jnp.einsum('ed,kq->keqd', w2, eye).reshape(LANES, LANES).astype(BF)
    pe_tok = jnp.tile(pe, (PAGE_SIZE // CMP_BLOCK, KV_HEADS))
    return w1p, w2p, pe_tok


def kernel(x_prompt, x_sample, cache_k_cmp, cache_v_cmp, cache_k_sel, cache_v_sel, page_table,
           state_win_k, state_win_v, state_conv, state_h,
           ln1_g, ln1_b, w_ffn1_up, w_ffn1_down, w_in, w_conv, b_conv, w_rg_a, b_rg_a, w_rg_x, b_rg_x,
           rg_lambda, cmp_pe, w_ck1, w_ck2, w_cv1, w_cv2, w_rec_o, w_attn_o, w_out,
           ln2_g, ln2_b, w_ffn2_up, w_ffn2_down, ln3_g, ln3_b):
    bp, t, d = x_prompt.shape
    bs, n_tok, _ = x_sample.shape
    d_rnn = w_conv.shape[1]
    assert bp == 1 and t % (128 * CMP_ROW) == 0 and t >= WINDOW
    past = page_table.shape[1] * PAGE_SIZE
    assert page_table.shape[1] % 16 == 0 and n_tok <= SUBLANES and n_tok >= CONV_W - 1
    assert state_win_k.shape[1] == WINDOW

    o_q = 2 * d_rnn
    o_kv = o_q + ATTN_W
    o_gn = o_kv + 6 * KV_W
    o_gr = o_gn + N_NSA_BRANCH * N_HEADS
    o_ga = o_gr + d
    w_in_b = w_in.astype(BF)
    w_rnn = w_in_b[:, :o_q]
    w_q = w_in_b[:, o_q:o_kv]
    w_kv = w_in_b[:, o_kv:o_gn]
    w_gn = jnp.pad(w_in_b[:, o_gn:o_gr], ((0, 0), (0, LANES - (o_gr - o_gn))))
    w_grec = w_in_b[:, o_gr:o_ga]
    w_gatt = w_in_b[:, o_ga:]
    w_attn_all = jnp.concatenate([w_in_b[:, :o_gn], jnp.pad(w_gn, ((0, 0), (0, 512 - LANES)))], axis=1)
    up1, down1 = w_ffn1_up.astype(BF), w_ffn1_down.astype(BF)
    up2, down2 = w_ffn2_up.astype(BF), w_ffn2_down.astype(BF)
    per_group = 2 * LANES // (d_rnn // RNN_BLOCKS)
    wa_bd = _block_diag(w_rg_a, per_group).astype(BF)
    wx_bd = _block_diag(w_rg_x, per_group).astype(BF)
    ck1, ck2, pe2 = _compress_weights(w_ck1, w_ck2, cmp_pe)
    cv1, cv2, _ = _compress_weights(w_cv1, w_cv2, cmp_pe)
    w_rec_b, w_attn_b, w_out_b = w_rec_o.astype(BF), w_attn_o.astype(BF), w_out.astype(BF)
    cst = _layout_constants(t // SEL_BLOCK)

    xp = _ffn_ln(x_prompt.reshape(t, d), up1, down1, ln1_g, ln1_b)
    xrg = _matmul(xp, w_rnn, 512)
    (p_kc, p_vc, p_ks, p_vs, p_kw, p_vw, p_gn, qa, ksa, vst, kwa, vwt) = _proj_attn(
        xp, w_q, w_kv, w_gn, cst['pq'], cst['pk'], cst['pkt'])
    y_rec, p_tail, p_h = _rglru_prompt(xrg, w_conv, b_conv, wa_bd, wx_bd, b_rg_a, b_rg_x, rg_lambda)
    nbs = t // SEL_BLOCK
    kc = _compress_prompt(p_kc, pe2, ck1, ck2, cst['pk']).reshape(KV_HEADS, 2 * nbs, LANES)
    vc = _compress_prompt(p_vc, pe2, cv1, cv2, cst['pk']).reshape(KV_HEADS, 2 * nbs, LANES)
    o_attn = _nsa_prompt(qa, kc, vc, ksa, vst, kwa, vwt, p_gn, cst['rplace'], cst['pall'])
    x2 = _merge(xp, y_rec, o_attn, w_grec, w_gatt, w_rec_b, w_attn_b, w_out_b, ln2_g, ln2_b)
    y_prompt = _ffn_ln(x2, up2, down2, ln3_g, ln3_b).reshape(bp, t, d)

    kvh = lambda a: a.reshape(bp, -1, KV_HEADS, HEAD_DIM)
    p_states = (kvh(p_kc), kvh(p_vc), kvh(p_ks), kvh(p_vs), kvh(p_kw[t - WINDOW:]), kvh(p_vw[t - WINDOW:]),
                p_tail[SUBLANES - (CONV_W - 1):].reshape(bp, CONV_W - 1, d_rnn), p_h.reshape(bp, d_rnn))

    m_s = bs * n_tok
    xs = _ffn_ln(x_sample.reshape(m_s, d), up1, down1, ln1_g, ln1_b)
    zs = _matmul(xs, w_attn_all, 512)
    s_xrg = zs[:, :o_q].reshape(bs, n_tok * o_q)
    seq = lambda a: a.reshape(bs, n_tok, a.shape[-1])
    s_q = seq(zs[:, o_q:o_kv])
    s_kv = [seq(zs[:, o_kv + n * KV_W:o_kv + (n + 1) * KV_W]) for n in range(6)]
    s_gn = seq(zs[:, o_gn:o_gn + LANES])
    pad8 = lambda a: jnp.pad(a, ((0, 0), (0, SUBLANES - n_tok), (0, 0)))
    ys_rec, s_conv, s_h = _rglru_sample(
        s_xrg, state_conv.reshape(bs, -1), state_h, w_conv, b_conv, wa_bd, wx_bd, b_rg_a, b_rg_x, rg_lambda,
        start0=(past == 0))
    nbs_s = past // SEL_BLOCK
    ck1p, ck2p, pe_tok = _compress_weights_paged(w_ck1, w_ck2, cmp_pe)
    cv1p, cv2p, _ = _compress_weights_paged(w_cv1, w_cv2, cmp_pe)
    kc_s = _compress_paged(cache_k_cmp, page_table, pe_tok, ck1p, ck2p)
    vc_s = _compress_paged(cache_v_cmp, page_table, pe_tok, cv1p, cv2p)
    o8 = _nsa_sample(page_table, pad8(s_q), kc_s, vc_s, cache_k_sel, cache_v_sel,
                     [pad8(s_kv[n]) for n in (2, 3, 4, 5)],
                     state_win_k.reshape(bs, WINDOW, KV_W), state_win_v.reshape(bs, WINDOW, KV_W),
                     pad8(s_gn), cst['pgk'], cst['rkg'], cst['egs'], n_tok)
    del nbs_s
    os_attn = o8[:, :n_tok].reshape(m_s, ATTN_W).astype(BF)
    x2s = _merge(xs, ys_rec.reshape(m_s, d_rnn), os_attn, w_grec, w_gatt, w_rec_b, w_attn_b, w_out_b, ln2_g, ln2_b)
    y_sample = _ffn_ln(x2s, up2, down2, ln3_g, ln3_b).reshape(bs, n_tok, d)

    kvs = lambda a: a.reshape(bs, n_tok, KV_HEADS, HEAD_DIM)
    win = lambda old, new: jnp.concatenate([old, kvs(new)], axis=1)[:, -WINDOW:]
    s_states = (kvs(s_kv[0]), kvs(s_kv[1]), kvs(s_kv[2]), kvs(s_kv[3]),
                win(state_win_k, s_kv[4]), win(state_win_v, s_kv[5]),
                s_conv.reshape(bs, CONV_W - 1, d_rnn), s_h)

    return (y_prompt, y_sample) + p_states + s_states
```

```python
import functools

import numpy as np
import jax
import jax.numpy as jnp
from jax import lax
from jax.experimental import pallas as pl
from jax.experimental.pallas import tpu as pltpu

F32 = jnp.float32
BF = jnp.bfloat16

DEPTH = 1
ALPHA = (2.0 * DEPTH) ** 0.25
N_HEADS = 16
HEAD_DIM = 64
KV_HEADS = 4
GROUP = N_HEADS // KV_HEADS
KV_W = KV_HEADS * HEAD_DIM
ATTN_W = N_HEADS * HEAD_DIM
N_NSA_BRANCH = 3
CMP_BLOCK = 32
SEL_BLOCK = 64
N_SEL = 16
WINDOW = 512
Q_BLOCK = 128
CONV_W = 4
LRU_C = 8.0
RNN_BLOCKS = 16
LN_EPS = 1e-5
PAGE_SIZE = 128

LANES = 128
SUBLANES = 8
VMEM_LIMIT = 56 * 1024 * 1024

NEG = -1e30
SEL_TILE = 512
BLOCKS_PER_TILE = SEL_TILE // SEL_BLOCK
SEL_V_ROWS = HEAD_DIM + 16
CMP_ROW = 16
CMP_ROW_W = CMP_ROW * KV_W

_NT = (((1,), (1,)), ((), ()))


def _dot(a, b):
    return jnp.dot(a, b, preferred_element_type=F32)


def _dot_nt(a, b):
    return lax.dot_general(a, b, _NT, preferred_element_type=F32)


def _params(*sem):
    return pltpu.CompilerParams(dimension_semantics=sem, vmem_limit_bytes=VMEM_LIMIT)


def _layer_norm(y, g, b):
    mu = jnp.mean(y, axis=-1, keepdims=True)
    d = y - mu
    var = jnp.mean(d * d, axis=-1, keepdims=True)
    return d * lax.rsqrt(var + LN_EPS) * g + b


def _masked_softmax(s, mask):
    s = jnp.where(mask, s, -jnp.inf)
    m = jnp.max(s, axis=-1, keepdims=True)
    m = jnp.where(m == -jnp.inf, 0.0, m)
    e = jnp.where(mask, jnp.exp(s - m), 0.0)
    den = jnp.sum(e, axis=-1, keepdims=True)
    return e / jnp.maximum(den, 1e-30)


def _split3(x):
    hi = x.astype(BF)
    r1 = x - hi.astype(F32)
    mid = r1.astype(BF)
    lo = (r1 - mid.astype(F32)).astype(BF)
    return hi, mid, lo


def _expand_f32(x, e):
    hi, mid, lo = _split3(x)
    return _dot(hi, e) + _dot(mid, e) + _dot(lo, e)


def _ffn_kernel(x_ref, wg_ref, wu_ref, wd_ref, g_ref, b_ref, o_ref, xb_ref, acc_ref):
    j = pl.program_id(1)

    @pl.when(j == 0)
    def _():
        xb_ref[...] = x_ref[...].astype(BF)
        acc_ref[...] = jnp.zeros_like(acc_ref)

    xb = xb_ref[...]
    gate = _dot(xb, wg_ref[...])
    up = _dot(xb, wu_ref[...])
    h = (gate * jax.nn.sigmoid(gate) * up).astype(BF)
    acc_ref[...] += _dot(h, wd_ref[...])

    @pl.when(j == pl.num_programs(1) - 1)
    def _():
        y = ALPHA * x_ref[...] + 0.5 * acc_ref[...]
        o_ref[...] = _layer_norm(y, g_ref[...], b_ref[...])


def _ffn_ln(x, w_up, w_down, g, b):
    m, d = x.shape
    f = w_down.shape[0]
    tm = min(512, m)
    tn = 512
    nj = f // tn
    return pl.pallas_call(
        _ffn_kernel,
        grid=(m // tm, nj),
        in_specs=[
            pl.BlockSpec((tm, d), lambda i, j: (i, 0)),
            pl.BlockSpec((d, tn), lambda i, j: (0, j)),
            pl.BlockSpec((d, tn), lambda i, j: (0, j + nj)),
            pl.BlockSpec((tn, d), lambda i, j: (j, 0)),
            pl.BlockSpec((1, d), lambda i, j: (0, 0)),
            pl.BlockSpec((1, d), lambda i, j: (0, 0)),
        ],
        out_specs=pl.BlockSpec((tm, d), lambda i, j: (i, 0)),
        out_shape=jax.ShapeDtypeStruct((m, d), F32),
        scratch_shapes=[pltpu.VMEM((tm, d), BF), pltpu.VMEM((tm, d), F32)],
        compiler_params=_params("parallel", "arbitrary"),
        name="ffn_ln",
    )(x, w_up, w_up, w_down, g.reshape(1, d), b.reshape(1, d))


def _mm_kernel(x_ref, w_ref, o_ref):
    o_ref[...] = _dot(x_ref[...].astype(BF), w_ref[...])


def _matmul(x, w, tn):
    m, d = x.shape
    n = w.shape[1]
    tm = min(512, m)
    return pl.pallas_call(
        _mm_kernel,
        grid=(m // tm, n // tn),
        in_specs=[pl.BlockSpec((tm, d), lambda i, j: (i, 0)),
                  pl.BlockSpec((d, tn), lambda i, j: (0, j))],
        out_specs=pl.BlockSpec((tm, tn), lambda i, j: (i, j)),
        out_shape=jax.ShapeDtypeStruct((m, n), F32),
        compiler_params=_params("parallel", "arbitrary"),
        name="proj",
    )(x, w)


def _proj_attn_kernel(x_ref, wq_ref, wkv_ref, wgn_ref, pq_ref, pk_ref, pkt_ref,
                      kc_ref, vc_ref, ks_ref, vs_ref, kw_ref, vw_ref, gn_ref,
                      qa_ref, ksa_ref, vst_ref, kwa_ref, vwt_ref):
    tm = x_ref.shape[0]
    xb = x_ref[...].astype(BF)
    zq = (_dot(xb, wq_ref[...]) * (HEAD_DIM ** -0.5 * np.log2(np.e))).astype(BF)
    for h in range(N_HEADS):
        pair = zq[:, (h // 2) * LANES:(h // 2 + 1) * LANES]
        qa_ref[h] = _dot(pair, pq_ref[h % 2]).astype(BF)
    zkv = _dot(xb, wkv_ref[...])
    parts = [zkv[:, n * KV_W:(n + 1) * KV_W] for n in range(6)]
    for ref, part in zip((kc_ref, vc_ref, ks_ref, vs_ref, kw_ref, vw_ref), parts):
        ref[...] = part
    gn_ref[...] = _dot(xb, wgn_ref[...])
    t = pl.program_id(0) * tm + lax.broadcasted_iota(jnp.int32, (tm, 1), 0)
    lane = lax.broadcasted_iota(jnp.int32, (1, LANES), 1)
    onehot = jnp.where(lane == HEAD_DIM + (t // SEL_BLOCK) % BLOCKS_PER_TILE, 1.0, 0.0)
    ksb = parts[2].astype(BF)
    kwb = parts[4].astype(BF)
    vsb = parts[3].astype(BF)
    vwb = parts[5].astype(BF)
    for k in range(KV_HEADS):
        ksa_ref[k] = (_dot(ksb, pk_ref[k]) + onehot).astype(BF)
        kwa_ref[k] = _dot(kwb, pk_ref[k]).astype(BF)
        ones_row = jnp.where(lax.broadcasted_iota(jnp.int32, (SEL_V_ROWS, 1), 0) == HEAD_DIM, 1.0, 0.0)
        vst_ref[k, 0] = (_dot_nt(pkt_ref[k], vsb) + ones_row).astype(BF)
        vwt_ref[k] = _dot_nt(pkt_ref[k, :HEAD_DIM], vwb).astype(BF)


def _proj_attn(x, wq, wkv, wgn, pq, pk, pkt):
    m, d = x.shape
    tm = 256
    per_tile = SEL_TILE // tm
    row = lambda w: pl.BlockSpec((tm, w), lambda i: (i, 0))
    const = lambda a: pl.BlockSpec(a.shape, lambda i: (0,) * a.ndim)
    head = lambda n: pl.BlockSpec((n, tm, LANES), lambda i: (0, i, 0))
    f32 = lambda w: jax.ShapeDtypeStruct((m, w), F32)
    aug = lambda n: jax.ShapeDtypeStruct((n, m, LANES), BF)
    return pl.pallas_call(
        _proj_attn_kernel,
        grid=(m // tm,),
        in_specs=[row(d), const(wq), const(wkv), const(wgn), const(pq), const(pk), const(pkt)],
        out_specs=[row(KV_W)] * 6 + [row(LANES), head(N_HEADS), head(KV_HEADS),
                                     pl.BlockSpec((KV_HEADS, 1, SEL_V_ROWS, tm),
                                                  lambda i: (0, i // per_tile, 0, i % per_tile)),
                                     head(KV_HEADS),
                                     pl.BlockSpec((KV_HEADS, HEAD_DIM, tm), lambda i: (0, 0, i))],
        out_shape=[f32(KV_W)] * 6 + [f32(LANES), aug(N_HEADS), aug(KV_HEADS),
                                     jax.ShapeDtypeStruct((KV_HEADS, m // SEL_TILE, SEL_V_ROWS, SEL_TILE), BF),
                                     aug(KV_HEADS),
                                     jax.ShapeDtypeStruct((KV_HEADS, HEAD_DIM, m), BF)],
        compiler_params=_params("parallel"),
        name="proj_attn",
    )(x, wq, wkv, wgn, pq, pk, pkt)


def _rglru_gates(xc, wa, wx, ba, bx, lam, is_start):
    xcb = xc.astype(BF)
    r = jax.nn.sigmoid(_dot(xcb, wa) + ba)
    ig = jax.nn.sigmoid(_dot(xcb, wx) + bx)
    log_a = -LRU_C * r * jax.nn.softplus(-lam)
    a = jnp.exp(log_a)
    if is_start is True:
        return a, ig * xc
    th = jnp.tanh(log_a)
    mult = jnp.sqrt(-2.0 * th / (1.0 - th))
    if is_start is not None:
        mult = jnp.where(is_start, 1.0, mult)
    return a, mult * (ig * xc)


def _rglru_prompt_kernel(x_ref, gate_ref, wc_ref, bc_ref, wa_ref, wx_ref, ba_ref, bx_ref, lam_ref,
                         y_ref, tail_ref, hl_ref, h_scr, tail_scr):
    c = pl.program_id(1)
    tc = x_ref.shape[0]

    @pl.when(c == 0)
    def _():
        h_scr[...] = jnp.zeros_like(h_scr)
        tail_scr[...] = jnp.zeros_like(tail_scr)

    x = x_ref[...]
    prev = tail_scr[...]
    row8 = lax.broadcasted_iota(jnp.int32, (SUBLANES, 1), 0)
    row = lax.broadcasted_iota(jnp.int32, (tc, 1), 0)

    def shifted(s):
        rolled = pltpu.roll(x, s, 0)
        top = jnp.where(row8 < s, pltpu.roll(prev, s, 0), rolled[:SUBLANES])
        return jnp.concatenate([top, rolled[SUBLANES:]], axis=0)

    wc = wc_ref[...]
    conv = wc[0:1] * shifted(3)
    conv = conv + wc[1:2] * shifted(2)
    conv = conv + wc[2:3] * shifted(1)
    conv = conv + wc[3:4] * x
    xc = bc_ref[...] + conv

    is_start = (row + c * tc) == 0
    a, u = _rglru_gates(xc, wa_ref[0], wx_ref[0], ba_ref[...], bx_ref[...], lam_ref[...], is_start)

    d = 1
    while d < tc:
        keep = row >= d
        a_sh = jnp.where(keep, pltpu.roll(a, d, 0), 1.0)
        u_sh = jnp.where(keep, pltpu.roll(u, d, 0), 0.0)
        u = a * u_sh + u
        a = a * a_sh
        d *= 2
    h = a * h_scr[...] + u

    y_ref[...] = (jax.nn.gelu(gate_ref[...]) * h).astype(BF)
    h_last = h[tc - 1:tc]
    h_scr[...] = h_last
    tail_scr[...] = x[tc - SUBLANES:]
    hl_ref[...] = h_last
    tail_ref[...] = x[tc - SUBLANES:]


def _rglru_prompt(xrg, w_conv, b_conv, wa_bd, wx_bd, b_a, b_x, lam):
    t = xrg.shape[0]
    d_rnn = w_conv.shape[1]
    gw = 2 * LANES
    ng = d_rnn // gw
    tc = min(512, t)
    vec = lambda: pl.BlockSpec((1, gw), lambda g, c: (0, g))
    return pl.pallas_call(
        _rglru_prompt_kernel,
        grid=(ng, t // tc),
        in_specs=[
            pl.BlockSpec((tc, gw), lambda g, c: (c, g)),
            pl.BlockSpec((tc, gw), lambda g, c: (c, g + ng)),
            pl.BlockSpec((CONV_W, gw), lambda g, c: (0, g)),
            vec(),
            pl.BlockSpec((1, gw, gw), lambda g, c: (g, 0, 0)),
            pl.BlockSpec((1, gw, gw), lambda g, c: (g, 0, 0)),
            vec(), vec(), vec(),
        ],
        out_specs=[
            pl.BlockSpec((tc, gw), lambda g, c: (c, g)),
            pl.BlockSpec((SUBLANES, gw), lambda g, c: (0, g)),
            pl.BlockSpec((1, gw), lambda g, c: (0, g)),
        ],
        out_shape=[
            jax.ShapeDtypeStruct((t, d_rnn), BF),
            jax.ShapeDtypeStruct((SUBLANES, d_rnn), F32),
            jax.ShapeDtypeStruct((1, d_rnn), F32),
        ],
        scratch_shapes=[pltpu.VMEM((1, gw), F32), pltpu.VMEM((SUBLANES, gw), F32)],
        compiler_params=_params("parallel", "arbitrary"),
        name="rglru_prompt",
    )(xrg, xrg, w_conv, b_conv.reshape(1, -1), wa_bd, wx_bd,
      b_a.reshape(1, -1), b_x.reshape(1, -1), lam.reshape(1, -1))


def _rglru_sample_kernel(xrg_ref, cp_ref, h0_ref, wc_ref, bc_ref, wa_ref, wx_ref, ba_ref, bx_ref, lam_ref,
                         y_ref, cn_ref, hl_ref, *, n_tok, start0):
    d_rnn = h0_ref.shape[1]
    gw = wa_ref.shape[1]
    wc = wc_ref[...]
    xp = [cp_ref[:, k * d_rnn:(k + 1) * d_rnn] for k in range(CONV_W - 1)]
    xp += [xrg_ref[:, t * 2 * d_rnn:t * 2 * d_rnn + d_rnn] for t in range(n_tok)]
    h = h0_ref[...]
    for t in range(n_tok):
        conv = wc[0:1] * xp[t]
        for k in range(1, CONV_W):
            conv = conv + wc[k:k + 1] * xp[t + k]
        xc = bc_ref[...] + conv
        a_parts, u_parts = [], []
        for g in range(d_rnn // gw):
            sl = slice(g * gw, (g + 1) * gw)
            a_g, u_g = _rglru_gates(xc[:, sl], wa_ref[g], wx_ref[g], ba_ref[:, sl], bx_ref[:, sl],
                                    lam_ref[:, sl], True if (start0 and t == 0) else None)
            a_parts.append(a_g)
            u_parts.append(u_g)
        a = jnp.concatenate(a_parts, axis=1)
        u = jnp.concatenate(u_parts, axis=1)
        h = a * h + u
        gate = xrg_ref[:, t * 2 * d_rnn + d_rnn:(t + 1) * 2 * d_rnn]
        y_ref[:, t * d_rnn:(t + 1) * d_rnn] = (jax.nn.gelu(gate) * h).astype(BF)
    hl_ref[...] = h
    tail = xp[-(CONV_W - 1):]
    for k in range(CONV_W - 1):
        cn_ref[:, k * d_rnn:(k + 1) * d_rnn] = tail[k]


def _rglru_sample(xrg, conv_prev, h0, w_conv, b_conv, wa_bd, wx_bd, b_a, b_x, lam, start0):
    b, d_rnn = h0.shape
    n_tok = xrg.shape[1] // (2 * d_rnn)
    args = (xrg, conv_prev, h0, w_conv, b_conv.reshape(1, -1), wa_bd, wx_bd,
            b_a.reshape(1, -1), b_x.reshape(1, -1), lam.reshape(1, -1))
    full = lambda a: pl.BlockSpec(a.shape, lambda i: (0,) * a.ndim)
    outs = [jax.ShapeDtypeStruct((b, n_tok * d_rnn), BF),
            jax.ShapeDtypeStruct((b, (CONV_W - 1) * d_rnn), F32),
            jax.ShapeDtypeStruct((b, d_rnn), F32)]
    return pl.pallas_call(
        functools.partial(_rglru_sample_kernel, n_tok=n_tok, start0=start0),
        grid=(1,),
        in_specs=[full(a) for a in args],
        out_specs=[full(o) for o in outs],
        out_shape=outs,
        compiler_params=_params("arbitrary"),
        name="rglru_sample",
    )(*args)


def _compress_kernel(x_ref, pe_ref, w1_ref, w2_ref, pk_ref, o_ref, scr):
    x = x_ref[...]
    rows = x.shape[0]
    xb = (x.reshape(rows // SUBLANES, SUBLANES, CMP_ROW_W) + pe_ref[...][None]).reshape(rows, CMP_ROW_W)
    full = _dot(xb.astype(BF), w1_ref[...])
    hid = full[:, :KV_W] + pltpu.roll(full[:, KV_W:], rows - 1, 0)
    out = _dot(jax.nn.gelu(hid).astype(BF), w2_ref[...])
    nb = rows // 4
    ob = out.astype(BF)
    for k in range(KV_HEADS):
        scr[...] = _dot(ob, pk_ref[k])
        o_ref[k, 0] = scr[pl.ds(0, nb, stride=4), :].astype(BF)
        o_ref[k, 1] = scr[pl.ds(2, nb, stride=4), :].astype(BF)


def _compress_prompt(kv, pe2, w1, w2, pk):
    t = kv.shape[0]
    rows = 128
    x = kv.reshape(t // CMP_ROW, CMP_ROW_W)
    nb = rows // 4
    nbs = t // SEL_BLOCK
    const = lambda a: pl.BlockSpec(a.shape, lambda i: (0,) * a.ndim)
    return pl.pallas_call(
        _compress_kernel,
        grid=(x.shape[0] // rows,),
        in_specs=[pl.BlockSpec((rows, CMP_ROW_W), lambda i: (i, 0)),
                  const(pe2), const(w1), const(w2), const(pk)],
        out_specs=pl.BlockSpec((KV_HEADS, 2, nb, LANES), lambda i: (0, 0, i, 0)),
        out_shape=jax.ShapeDtypeStruct((KV_HEADS, 2, nbs, LANES), BF),
        scratch_shapes=[pltpu.VMEM((rows, LANES), F32)],
        compiler_params=_params("parallel"),
        name="compress_prompt",
    )(x, pe2, w1, w2, pk)


def _compress_paged_kernel(pt_ref, *refs, pages):
    del pt_ref
    x_refs = refs[:pages]
    pe_ref, w1_ref, w2_ref, o_ref, xs_scr, out_scr = refs[pages:]
    halves = KV_W // LANES
    for p in range(pages):
        xt = x_refs[p][0].T + pe_ref[...]
        for h in range(halves):
            xs_scr[h, p * PAGE_SIZE:(p + 1) * PAGE_SIZE, :] = xt[:, h * LANES:(h + 1) * LANES]
    nblk = pages * (PAGE_SIZE // CMP_BLOCK)
    by_tok = [pltpu.einshape("mld->lmd", xs_scr[h].reshape(nblk, CMP_BLOCK, LANES)) for h in range(halves)]
    acc = None
    for l2 in range(CMP_BLOCK // 2):
        parts = [jnp.concatenate([by_tok[h][2 * l2], by_tok[h][2 * l2 + 1]], axis=1) for h in range(halves)]
        d = _dot(jnp.concatenate(parts, axis=0).astype(BF), w1_ref[l2])
        acc = d if acc is None else acc + d
    out_scr[...] = _dot(jax.nn.gelu(acc).astype(BF), w2_ref[...])
    for h in range(halves):
        for par in range(2):
            o_ref[0, par, :, h * LANES:(h + 1) * LANES] = (
                out_scr[pl.ds(h * nblk + par, nblk // 2, stride=2), :].astype(BF))


def _compress_paged(cache, page_table, pe_tok, w1p, w2p):
    b, n_pages = page_table.shape
    pages = 16
    x = jnp.transpose(cache, (0, 2, 3, 1)).reshape(cache.shape[0], KV_W, PAGE_SIZE)
    nblk = pages * (PAGE_SIZE // CMP_BLOCK)
    halves = KV_W // LANES
    nbs = n_pages * PAGE_SIZE // SEL_BLOCK
    const = lambda a: pl.BlockSpec(a.shape, lambda i, s, pt: (0,) * a.ndim)
    page_specs = [
        pl.BlockSpec((1, KV_W, PAGE_SIZE),
                     functools.partial(lambda i, s, pt, p: (pt[i, s * pages + p], 0, 0), p=p))
        for p in range(pages)]
    return pl.pallas_call(
        functools.partial(_compress_paged_kernel, pages=pages),
        grid_spec=pltpu.PrefetchScalarGridSpec(
            num_scalar_prefetch=1,
            grid=(b, n_pages // pages),
            in_specs=page_specs + [const(pe_tok), const(w1p), const(w2p)],
            out_specs=pl.BlockSpec((1, 2, nblk // 2, KV_W), lambda i, s, pt: (i, 0, s, 0)),
            scratch_shapes=[pltpu.VMEM((halves, pages * PAGE_SIZE, LANES), F32),
                            pltpu.VMEM((halves * nblk, LANES), F32)],
        ),
        out_shape=jax.ShapeDtypeStruct((b, 2, nbs, KV_W), BF),
        compiler_params=_params("parallel", "arbitrary"),
        name="compress_paged",
    )(page_table, *([x] * pages), pe_tok, w1p, w2p)


def _topk_columns(imp_t, n_pick):
    nblk = imp_t.shape[0]
    idx = lax.broadcasted_iota(jnp.int32, imp_t.shape, 0)
    sel = jnp.zeros(imp_t.shape, F32)
    for _ in range(n_pick):
        mx = jnp.max(imp_t, axis=0, keepdims=True)
        first = jnp.min(jnp.where(imp_t == mx, idx, nblk), axis=0, keepdims=True)
        hit = idx == first
        sel = jnp.where(hit, 1.0, sel)
        imp_t = jnp.where(hit, -jnp.inf, imp_t)
    return sel


def _softmax_cols(s):
    m = jnp.max(s, axis=0, keepdims=True)
    m = jnp.where(m == -jnp.inf, 0.0, m)
    e = jnp.exp2(s - m)
    den = jnp.sum(e, axis=0, keepdims=True)
    return e * (1.0 / jnp.maximum(den, 1e-30))


def _nsa_prompt_kernel(*refs):
    (q_ref, kc_ref, vc_ref, ks_ref, vst_ref) = refs[:5]
    kw_refs = refs[5:10]
    vwt_refs = refs[10:15]
    (gn_ref, rp_ref, pa_ref, o_ref,
     bias_scr, gate_scr, sa_scr, sb_scr, m_scr, acc_scr) = refs[15:]
    kvh = pl.program_id(0)
    i = pl.program_id(1)
    nq = Q_BLOCK
    r = GROUP * nq
    hd = HEAD_DIM
    q = q_ref[...].reshape(r, LANES)
    lane = lax.broadcasted_iota(jnp.int32, (1, r), 1)
    qpos = i * nq + lane % nq

    kc = kc_ref[0]
    n2 = kc.shape[0]
    nbs = n2 // 2
    rowc = lax.broadcasted_iota(jnp.int32, (n2, 1), 0)
    blk = 2 * (rowc % nbs) + rowc // nbs
    p_c = _softmax_cols(jnp.where(blk * CMP_BLOCK + (CMP_BLOCK - 1) <= qpos, _dot_nt(kc, q), -jnp.inf))
    vct = vc_ref[0].astype(F32).T[:hd].astype(BF)
    o_c = _dot(vct, p_c.astype(BF))

    imp = p_c[:, 0:nq]
    for g in range(1, GROUP):
        imp = imp + p_c[:, g * nq:(g + 1) * nq]
    imp = imp[:nbs] + imp[nbs:]
    qp = i * nq + lax.broadcasted_iota(jnp.int32, (1, nq), 1)
    jb = lax.broadcasted_iota(jnp.int32, (nbs, 1), 0)
    cur = qp // SEL_BLOCK
    valid = jb * SEL_BLOCK <= qp
    forced = (jb == 0) | (jb == cur) | (jb == cur - 1)
    imp = jnp.where(valid, jnp.where(forced, jnp.inf, imp), -jnp.inf)
    sel = _topk_columns(imp, min(N_SEL, nbs))
    bias = jnp.where(valid & (sel > 0.5), 0.0, NEG).T.astype(BF)
    bias_all = _dot(bias, pa_ref[...]).astype(BF)
    n_tiles = bias_scr.shape[0]
    for t in range(n_tiles):
        bias_scr[t] = bias_all[:, t * LANES:(t + 1) * LANES]

    def scores(t):
        k0 = pl.multiple_of(t * SEL_TILE, SEL_TILE)
        kt = ks_ref[0, pl.ds(k0, SEL_TILE), :]
        qa = q + jnp.concatenate([bias_scr[t]] * GROUP, axis=0)
        return _dot_nt(kt, qa)

    def update(t, s_ref, diagonal):
        vt = vst_ref[0, t]
        for g in range(GROUP):
            sl = slice(g * nq, (g + 1) * nq)
            sg = s_ref[:, sl]
            if diagonal:
                kpos = t * SEL_TILE + lax.broadcasted_iota(jnp.int32, (SEL_TILE, 1), 0)
                sg = jnp.where(kpos <= qpos[:, sl], sg, NEG)
            m_old = m_scr[:, sl]
            m_new = jnp.maximum(m_old, jnp.max(sg, axis=0, keepdims=True))
            p = jnp.exp2(sg - m_new)
            acc_scr[:, sl] = jnp.exp2(m_old - m_new) * acc_scr[:, sl] + _dot(vt, p.astype(BF))
            m_scr[:, sl] = m_new

    def pair(t0):
        sb_scr[...] = scores(t0 + 1)
        update(t0, sa_scr, False)
        sa_scr[...] = scores(t0 + 2)
        update(t0 + 1, sb_scr, False)

    def quad(u, _):
        pair(4 * u)
        pair(4 * u + 2)
        return 0

    t_diag = (i * nq) // SEL_TILE
    m_scr[...] = jnp.full(m_scr.shape, NEG, F32)
    acc_scr[...] = jnp.zeros_like(acc_scr)
    sa_scr[...] = scores(0)
    lax.fori_loop(0, t_diag // 4, quad, 0)

    @pl.when(t_diag % 4 >= 2)
    def _():
        pair(t_diag - t_diag % 4)

    @pl.when(t_diag % 2 == 1)
    def _():
        sb_scr[...] = scores(t_diag)
        update(t_diag - 1, sa_scr, False)
        update(t_diag, sb_scr, True)

    @pl.when(t_diag % 2 == 0)
    def _():
        update(t_diag, sa_scr, True)

    o_s = acc_scr[:hd] * (1.0 / acc_scr[hd:hd + 1])

    kw = jnp.concatenate([ref[0] for ref in kw_refs], axis=0)
    vwt = jnp.concatenate([ref[0] for ref in vwt_refs], axis=1)
    s_w = _dot_nt(kw, q)
    nwb = len(kw_refs)
    krow = lax.broadcasted_iota(jnp.int32, (nq, 1), 0)
    parts = []
    for b in range(nwb):
        blk_pos = (i - (nwb - 1) + b) * nq
        mask = blk_pos >= 0
        if b == 0:
            mask = mask & (qpos - (blk_pos + krow) <= WINDOW)
        if b == nwb - 1:
            mask = mask & (blk_pos + krow <= qpos)
        parts.append(jnp.where(mask, s_w[b * nq:(b + 1) * nq], -jnp.inf))
    p_w = _softmax_cols(jnp.concatenate(parts, axis=0))
    o_w = _dot(vwt, p_w.astype(BF))

    gate_scr[...] = jax.nn.sigmoid(gn_ref[...]).T

    def gate(br):
        return jnp.concatenate(
            [gate_scr[pl.ds(br * N_HEADS + kvh * GROUP + g, 1), :] for g in range(GROUP)], axis=1)

    comb = gate(0) * o_c + gate(1) * o_s + gate(2) * o_w
    comb = jnp.concatenate([comb, jnp.zeros_like(comb)], axis=0)
    out = None
    for g in range(GROUP):
        part = _dot(comb[:, g * nq:(g + 1) * nq].T.astype(BF), rp_ref[g])
        out = part if out is None else out + part
    o_ref[...] = out.astype(BF)


def _nsa_prompt(qa, kc, vc, ksa, vst, kwa, vwt, gn, rplace, pall):
    t = qa.shape[1]
    n_tiles = t // SEL_TILE
    wb = WINDOW // Q_BLOCK + 1
    kv_full = lambda a: pl.BlockSpec((1,) + a.shape[1:], lambda k, i: (k,) + (0,) * (a.ndim - 1))
    wblk = lambda i, m: jnp.maximum(i - (wb - 1) + m, 0)
    win_k = [pl.BlockSpec((1, Q_BLOCK, LANES), functools.partial(lambda k, i, m: (k, wblk(i, m), 0), m=m))
             for m in range(wb)]
    win_v = [pl.BlockSpec((1, HEAD_DIM, Q_BLOCK), functools.partial(lambda k, i, m: (k, 0, wblk(i, m)), m=m))
             for m in range(wb)]
    const = lambda a: pl.BlockSpec(a.shape, lambda k, i: (0,) * a.ndim)
    return pl.pallas_call(
        _nsa_prompt_kernel,
        grid=(KV_HEADS, t // Q_BLOCK),
        in_specs=[pl.BlockSpec((GROUP, Q_BLOCK, LANES), lambda k, i: (k, i, 0)),
                  kv_full(kc), kv_full(vc), kv_full(ksa), kv_full(vst)]
                 + win_k + win_v
                 + [pl.BlockSpec((Q_BLOCK, LANES), lambda k, i: (i, 0)), const(rplace), const(pall)],
        out_specs=pl.BlockSpec((Q_BLOCK, KV_W), lambda k, i: (i, k)),
        out_shape=jax.ShapeDtypeStruct((t, ATTN_W), BF),
        scratch_shapes=[pltpu.VMEM((n_tiles, Q_BLOCK, LANES), BF), pltpu.VMEM((LANES, Q_BLOCK), F32),
                        pltpu.VMEM((SEL_TILE, GROUP * Q_BLOCK), F32), pltpu.VMEM((SEL_TILE, GROUP * Q_BLOCK), F32),
                        pltpu.VMEM((1, GROUP * Q_BLOCK), F32),
                        pltpu.VMEM((SEL_V_ROWS, GROUP * Q_BLOCK), F32)],
        compiler_params=_params("parallel", "arbitrary"),
        name="nsa_prompt",
    )(qa, kc, vc, ksa, vst, *([kwa] * wb), *([vwt] * wb), gn, rplace, pall)


def _nsa_sample_kernel(*refs, pages, past, n_tok):
    pt_ref = refs[0]
    q_ref, kc_ref, vc_ref = refs[1:4]
    ks_refs = refs[4:4 + pages]
    vs_refs = refs[4 + pages:4 + 2 * pages]
    (ksn_ref, vsn_ref, kwn_ref, vwn_ref, wk_ref, wv_ref, gn_ref,
     pg_ref, rp_ref, eg_ref, o_ref, q_scr, bias_scr, m_scr, l_scr, acc_scr, oc_scr) = refs[4 + 2 * pages:]
    del pt_ref
    s_id = pl.program_id(1)
    tp = SUBLANES
    r = N_HEADS * tp
    row = lax.broadcasted_iota(jnp.int32, (r, 1), 0)
    qpos = past + row % tp
    nbs = bias_scr.shape[1]

    @pl.when(s_id == 0)
    def _():
        qb = q_ref[0].astype(BF)
        for h in range(N_HEADS):
            k = h // GROUP
            q_scr[h * tp:(h + 1) * tp, :] = _dot(qb[:, k * KV_W:(k + 1) * KV_W], pg_ref[h]).astype(BF)
        q = q_scr[...]
        kc = kc_ref[0].reshape(2 * nbs, KV_W)
        vc = vc_ref[0].reshape(2 * nbs, KV_W)
        col = lax.broadcasted_iota(jnp.int32, (1, 2 * nbs), 1)
        blk = 2 * (col % nbs) + col // nbs
        p_c = _masked_softmax(_dot_nt(q, kc), blk * CMP_BLOCK + (CMP_BLOCK - 1) <= qpos)
        oc_scr[...] = _dot(p_c.astype(BF), vc)
        imps = []
        for k in range(KV_HEADS):
            base = k * GROUP * tp
            imp = p_c[base:base + tp]
            for g in range(1, GROUP):
                imp = imp + p_c[base + g * tp:base + (g + 1) * tp]
            imps.append(imp)
        imp = jnp.concatenate(imps, axis=0)
        imp = imp[:, :nbs] + imp[:, nbs:]
        jb = lax.broadcasted_iota(jnp.int32, (1, nbs), 1)
        imp = jnp.where((jb == 0) | (jb == nbs - 1), jnp.inf, imp)
        idx = lax.broadcasted_iota(jnp.int32, imp.shape, 1)
        sel = jnp.zeros(imp.shape, F32)
        for _ in range(min(N_SEL, nbs + 1) - 1):
            mx = jnp.max(imp, axis=1, keepdims=True)
            first = jnp.min(jnp.where(imp == mx, idx, nbs), axis=1, keepdims=True)
            hit = idx == first
            sel = jnp.where(hit, 1.0, sel)
            imp = jnp.where(hit, -jnp.inf, imp)
        bias = jnp.where(sel > 0.5, 0.0, NEG).astype(BF)
        for k in range(KV_HEADS):
            for g in range(GROUP):
                h = k * GROUP + g
                bias_scr[h * tp:(h + 1) * tp, :] = bias[k * tp:(k + 1) * tp]
        m_scr[...] = jnp.full(m_scr.shape, NEG, F32)
        l_scr[...] = jnp.zeros_like(l_scr)
        acc_scr[...] = jnp.zeros_like(acc_scr)

    def online(st, v, v_transposed):
        m = m_scr[...]
        m_new = jnp.maximum(m, jnp.max(st, axis=1, keepdims=True))
        scale = jnp.exp(m - m_new)
        p = jnp.exp(st - m_new)
        l_scr[...] = scale * l_scr[...] + jnp.sum(p, axis=1, keepdims=True)
        pv = _dot_nt(p.astype(BF), v) if v_transposed else _dot(p.astype(BF), v)
        acc_scr[...] = scale * acc_scr[...] + pv
        m_scr[...] = m_new

    q = q_scr[...]
    kt = jnp.concatenate([ref[0].astype(BF) for ref in ks_refs], axis=1)
    vt = jnp.concatenate([ref[0].astype(BF) for ref in vs_refs], axis=1)
    n_keys = kt.shape[1]
    jb = lax.broadcasted_iota(jnp.int32, (nbs, 1), 0)
    key_blk = s_id * (n_keys // SEL_BLOCK) + lax.broadcasted_iota(jnp.int32, (1, n_keys), 1) // SEL_BLOCK
    expand = jnp.where(jb == key_blk, 1.0, 0.0).astype(BF)
    online(_dot(q, kt) + _dot(bias_scr[...], expand), vt, True)

    @pl.when(s_id == pl.num_programs(1) - 1)
    def _():
        zeros = jnp.zeros((LANES - tp, KV_W), BF)
        tok = lax.broadcasted_iota(jnp.int32, (1, LANES), 1)
        ksn = jnp.concatenate([ksn_ref[0].astype(BF), zeros], axis=0)
        vsn = jnp.concatenate([vsn_ref[0].astype(BF), zeros], axis=0)
        online(jnp.where((past + tok <= qpos) & (tok < n_tok), _dot_nt(q, ksn), NEG), vsn, False)
        o_s = acc_scr[...] / l_scr[...]
        kw = jnp.concatenate([wk_ref[0].astype(BF), kwn_ref[0].astype(BF), zeros], axis=0)
        vw = jnp.concatenate([wv_ref[0].astype(BF), vwn_ref[0].astype(BF), zeros], axis=0)
        n_w = kw.shape[0]
        wcol = lax.broadcasted_iota(jnp.int32, (1, n_w), 1)
        kwpos = past - WINDOW + wcol
        dpos = qpos - kwpos
        mask_w = (dpos >= 0) & (dpos <= WINDOW) & (kwpos >= 0) & (wcol < WINDOW + n_tok)
        p_w = _masked_softmax(_dot_nt(q, kw), mask_w)
        o_w = _dot(p_w.astype(BF), vw)
        sg = jax.nn.sigmoid(gn_ref[0])
        gexp = _expand_f32(sg, eg_ref[...])

        def gate(br):
            return jnp.concatenate(
                [gexp[:, (br * N_HEADS + h) * KV_W:(br * N_HEADS + h + 1) * KV_W] for h in range(N_HEADS)],
                axis=0)

        comb = (gate(0) * oc_scr[...] + gate(1) * o_s + gate(2) * o_w).astype(BF)
        for k in range(KV_HEADS):
            out = None
            for g in range(GROUP):
                h = k * GROUP + g
                part = _dot(comb[h * tp:(h + 1) * tp], rp_ref[h])
                out = part if out is None else out + part
            o_ref[0, :, k * KV_W:(k + 1) * KV_W] = out


def _nsa_sample(page_table, q8, kc, vc, cache_ks, cache_vs, new8, win_k, win_v, gn8, pgk, rkg, egate, n_tok):
    b, n_pages = page_table.shape
    past = n_pages * PAGE_SIZE
    pages = 8
    tp = SUBLANES
    r = N_HEADS * tp
    nbs = past // SEL_BLOCK
    ck = jnp.transpose(cache_ks, (0, 2, 3, 1)).reshape(cache_ks.shape[0], KV_W, PAGE_SIZE)
    cv = jnp.transpose(cache_vs, (0, 2, 3, 1)).reshape(cache_vs.shape[0], KV_W, PAGE_SIZE)
    per_seq = lambda a: pl.BlockSpec((1,) + a.shape[1:], lambda i, s, pt: (i,) + (0,) * (a.ndim - 1))
    const = lambda a: pl.BlockSpec(a.shape, lambda i, s, pt: (0,) * a.ndim)
    page_specs = [
        pl.BlockSpec((1, KV_W, PAGE_SIZE),
                     functools.partial(lambda i, s, pt, p: (pt[i, s * pages + p], 0, 0), p=p))
        for p in range(pages)]
    return pl.pallas_call(
        functools.partial(_nsa_sample_kernel, pages=pages, past=past, n_tok=n_tok),
        grid_spec=pltpu.PrefetchScalarGridSpec(
            num_scalar_prefetch=1,
            grid=(b, n_pages // pages),
            in_specs=[per_seq(q8), per_seq(kc), per_seq(vc)] + page_specs + page_specs
                     + [per_seq(a) for a in new8] + [per_seq(win_k), per_seq(win_v), per_seq(gn8),
                                                     const(pgk), const(rkg), const(egate)],
            out_specs=pl.BlockSpec((1, tp, ATTN_W), lambda i, s, pt: (i, 0, 0)),
            scratch_shapes=[pltpu.VMEM((r, KV_W), BF), pltpu.VMEM((r, nbs), BF),
                            pltpu.VMEM((r, 1), F32), pltpu.VMEM((r, 1), F32),
                            pltpu.VMEM((r, KV_W), F32), pltpu.VMEM((r, KV_W), F32)],
        ),
        out_shape=jax.ShapeDtypeStruct((b, tp, ATTN_W), F32),
        compiler_params=_params("parallel", "arbitrary"),
        name="nsa_sample",
    )(page_table, q8, kc, vc, *([ck] * pages), *([cv] * pages), *new8, win_k, win_v, gn8, pgk, rkg, egate)


def _merge_kernel(x_ref, y_ref, o_ref, wgr_ref, wga_ref, wr_ref, wa_ref, wo_ref, g_ref, b_ref,
                  out_ref, xb_ref, acc_ref):
    j = pl.program_id(1)

    @pl.when(j == 0)
    def _():
        xb_ref[...] = x_ref[...].astype(BF)
        acc_ref[...] = jnp.zeros_like(acc_ref)

    xb = xb_ref[...]
    rec = _dot(y_ref[...], wr_ref[...])
    att = _dot(o_ref[...], wa_ref[...])
    u = jax.nn.sigmoid(_dot(xb, wgr_ref[...])) * rec + jax.nn.sigmoid(_dot(xb, wga_ref[...])) * att
    acc_ref[...] += _dot(u.astype(BF), wo_ref[...])

    @pl.when(j == pl.num_programs(1) - 1)
    def _():
        out_ref[...] = _layer_norm(ALPHA * x_ref[...] + acc_ref[...], g_ref[...], b_ref[...])


def _merge(x, y_rec, o_attn, w_grec, w_gatt, w_rec_o, w_attn_o, w_out, g, b):
    m, d = x.shape
    tm = min(512, m)
    tn = 512
    col = lambda a: pl.BlockSpec((a.shape[0], tn), lambda i, j: (0, j))
    rowb = lambda a: pl.BlockSpec((tm, a.shape[1]), lambda i, j: (i, 0))
    vec = pl.BlockSpec((1, d), lambda i, j: (0, 0))
    return pl.pallas_call(
        _merge_kernel,
        grid=(m // tm, d // tn),
        in_specs=[rowb(x), rowb(y_rec), rowb(o_attn), col(w_grec), col(w_gatt), col(w_rec_o), col(w_attn_o),
                  pl.BlockSpec((tn, d), lambda i, j: (j, 0)), vec, vec],
        out_specs=pl.BlockSpec((tm, d), lambda i, j: (i, 0)),
        out_shape=jax.ShapeDtypeStruct((m, d), F32),
        scratch_shapes=[pltpu.VMEM((tm, d), BF), pltpu.VMEM((tm, d), F32)],
        compiler_params=_params("parallel", "arbitrary"),
        name="merge_out",
    )(x, y_rec, o_attn, w_grec, w_gatt, w_rec_o, w_attn_o, w_out, g.reshape(1, d), b.reshape(1, d))


def _place(n_rows, n_cols, src0, dst0, width, value=1.0):
    m = np.zeros((n_rows, n_cols), np.float32)
    m[src0 + np.arange(width), dst0 + np.arange(width)] = value
    return m


def _layout_constants(n_sel_blocks_prompt):
    hd = HEAD_DIM
    scale = hd ** -0.5
    pq = np.stack([_place(LANES, LANES, e * hd, 0, hd) for e in range(2)])
    pk = np.stack([_place(KV_W, LANES, k * hd, 0, hd) for k in range(KV_HEADS)])
    rplace = np.stack([_place(LANES, KV_W, 0, g * hd, hd) for g in range(GROUP)])
    pkt = np.transpose(pk, (0, 2, 1))[:, :SEL_V_ROWS]
    nbs = n_sel_blocks_prompt
    n_tiles = -(-nbs // BLOCKS_PER_TILE)
    pall = np.zeros((nbs, n_tiles * LANES), np.float32)
    j = np.arange(nbs)
    pall[j, (j // BLOCKS_PER_TILE) * LANES + hd + j % BLOCKS_PER_TILE] = 1.0
    pgk = np.stack([_place(KV_W, KV_W, (h % GROUP) * hd, (h // GROUP) * hd, hd, scale) for h in range(N_HEADS)])
    rkg = np.stack([_place(KV_W, KV_W, (h // GROUP) * hd, (h % GROUP) * hd, hd) for h in range(N_HEADS)])
    egs = np.zeros((LANES, N_NSA_BRANCH * N_HEADS * KV_W), np.float32)
    for c in range(N_NSA_BRANCH * N_HEADS):
        egs[c, c * KV_W:(c + 1) * KV_W] = 1.0
    as_bf = lambda a: jnp.asarray(a, BF)
    return dict(pq=as_bf(pq), pk=as_bf(pk), pkt=as_bf(pkt), rplace=as_bf(rplace), pall=as_bf(pall),
                pgk=as_bf(pgk), rkg=as_bf(rkg), egs=as_bf(egs))


def _block_diag(w, per_group):
    nb, c, _ = w.shape
    eye = jnp.eye(per_group, dtype=w.dtype)
    wg = w.reshape(nb // per_group, per_group, c, c)
    return jnp.einsum('gpcd,pq->gpcqd', wg, eye).reshape(nb // per_group, per_group * c, per_group * c)


def _compress_weights(w1, w2, pe):
    eye = jnp.eye(KV_HEADS, dtype=w1.dtype)
    big = jnp.einsum('lde,kq->lkdqe', w1, eye).reshape(CMP_BLOCK * KV_W, KV_HEADS * w1.shape[2])
    half = big.shape[0] // 2
    w1cat = jnp.concatenate([big[:half], big[half:]], axis=1).astype(BF)
    w2bd = _block_diag(jnp.broadcast_to(w2, (KV_HEADS,) + w2.shape), KV_HEADS)[0].astype(BF)
    pe_flat = jnp.broadcast_to(pe[:, None, :], (CMP_BLOCK, KV_HEADS, HEAD_DIM)).reshape(2, half)
    pe2 = jnp.tile(pe_flat, (SUBLANES // 2, 1))
    return w1cat, w2bd, pe2


def _compress_weights_paged(w1, w2, pe):
    per_half = LANES // HEAD_DIM
    eye = jnp.eye(per_half, dtype=w1.dtype)
    w1h = jnp.einsum('lde,kq->lkdqe', w1, eye).reshape(CMP_BLOCK, LANES, LANES)
    w1p = w1h.reshape(CMP_BLOCK // 2, 2 * LANES, LANES).astype(BF)
    w2p = jnp.einsum('ed,kq->keqd', w2, eye).reshape(LANES, LANES).astype(BF)
    pe_tok = jnp.tile(pe, (PAGE_SIZE // CMP_BLOCK, KV_HEADS))
    return w1p, w2p, pe_tok


def kernel(x_prompt, x_sample, cache_k_cmp, cache_v_cmp, cache_k_sel, cache_v_sel, page_table,
           state_win_k, state_win_v, state_conv, state_h,
           ln1_g, ln1_b, w_ffn1_up, w_ffn1_down, w_in, w_conv, b_conv, w_rg_a, b_rg_a, w_rg_x, b_rg_x,
           rg_lambda, cmp_pe, w_ck1, w_ck2, w_cv1, w_cv2, w_rec_o, w_attn_o, w_out,
           ln2_g, ln2_b, w_ffn2_up, w_ffn2_down, ln3_g, ln3_b):
    bp, t, d = x_prompt.shape
    bs, n_tok, _ = x_sample.shape
    d_rnn = w_conv.shape[1]
    assert bp == 1 and t % (128 * CMP_ROW) == 0 and t >= WINDOW
    past = page_table.shape[1] * PAGE_SIZE
    assert page_table.shape[1] % 16 == 0 and n_tok <= SUBLANES and n_tok >= CONV_W - 1
    assert state_win_k.shape[1] == WINDOW

    o_q = 2 * d_rnn
    o_kv = o_q + ATTN_W
    o_gn = o_kv + 6 * KV_W
    o_gr = o_gn + N_NSA_BRANCH * N_HEADS
    o_ga = o_gr + d
    w_in_b = w_in.astype(BF)
    w_rnn = w_in_b[:, :o_q]
    w_q = w_in_b[:, o_q:o_kv]
    w_kv = w_in_b[:, o_kv:o_gn]
    w_gn = jnp.pad(w_in_b[:, o_gn:o_gr], ((0, 0), (0, LANES - (o_gr - o_gn))))
    w_grec = w_in_b[:, o_gr:o_ga]
    w_gatt = w_in_b[:, o_ga:]
    w_attn_all = jnp.concatenate([w_in_b[:, :o_gn], jnp.pad(w_gn, ((0, 0), (0, 512 - LANES)))], axis=1)
    up1, down1 = w_ffn1_up.astype(BF), w_ffn1_down.astype(BF)
    up2, down2 = w_ffn2_up.astype(BF), w_ffn2_down.astype(BF)
    per_group = 2 * LANES // (d_rnn // RNN_BLOCKS)
    wa_bd = _block_diag(w_rg_a, per_group).astype(BF)
    wx_bd = _block_diag(w_rg_x, per_group).astype(BF)
    ck1, ck2, pe2 = _compress_weights(w_ck1, w_ck2, cmp_pe)
    cv1, cv2, _ = _compress_weights(w_cv1, w_cv2, cmp_pe)
    w_rec_b, w_attn_b, w_out_b = w_rec_o.astype(BF), w_attn_o.astype(BF), w_out.astype(BF)
    cst = _layout_constants(t // SEL_BLOCK)

    xp = _ffn_ln(x_prompt.reshape(t, d), up1, down1, ln1_g, ln1_b)
    xrg = _matmul(xp, w_rnn, 512)
    (p_kc, p_vc, p_ks, p_vs, p_kw, p_vw, p_gn, qa, ksa, vst, kwa, vwt) = _proj_attn(
        xp, w_q, w_kv, w_gn, cst['pq'], cst['pk'], cst['pkt'])
    y_rec, p_tail, p_h = _rglru_prompt(xrg, w_conv, b_conv, wa_bd, wx_bd, b_rg_a, b_rg_x, rg_lambda)
    nbs = t // SEL_BLOCK
    kc = _compress_prompt(p_kc, pe2, ck1, ck2, cst['pk']).reshape(KV_HEADS, 2 * nbs, LANES)
    vc = _compress_prompt(p_vc, pe2, cv1, cv2, cst['pk']).reshape(KV_HEADS, 2 * nbs, LANES)
    o_attn = _nsa_prompt(qa, kc, vc, ksa, vst, kwa, vwt, p_gn, cst['rplace'], cst['pall'])
    x2 = _merge(xp, y_rec, o_attn, w_grec, w_gatt, w_rec_b, w_attn_b, w_out_b, ln2_g, ln2_b)
    y_prompt = _ffn_ln(x2, up2, down2, ln3_g, ln3_b).reshape(bp, t, d)

    kvh = lambda a: a.reshape(bp, -1, KV_HEADS, HEAD_DIM)
    p_states = (kvh(p_kc), kvh(p_vc), kvh(p_ks), kvh(p_vs), kvh(p_kw[t - WINDOW:]), kvh(p_vw[t - WINDOW:]),
                p_tail[SUBLANES - (CONV_W - 1):].reshape(bp, CONV_W - 1, d_rnn), p_h.reshape(bp, d_rnn))

    m_s = bs * n_tok
    xs = _ffn_ln(x_sample.reshape(m_s, d), up1, down1, ln1_g, ln1_b)
    zs = _matmul(xs, w_attn_all, 512)
    s_xrg = zs[:, :o_q].reshape(bs, n_tok * o_q)
    seq = lambda a: a.reshape(bs, n_tok, a.shape[-1])
    s_q = seq(zs[:, o_q:o_kv])
    s_kv = [seq(zs[:, o_kv + n * KV_W:o_kv + (n + 1) * KV_W]) for n in range(6)]
    s_gn = seq(zs[:, o_gn:o_gn + LANES])
    pad8 = lambda a: jnp.pad(a, ((0, 0), (0, SUBLANES - n_tok), (0, 0)))
    ys_rec, s_conv, s_h = _rglru_sample(
        s_xrg, state_conv.reshape(bs, -1), state_h, w_conv, b_conv, wa_bd, wx_bd, b_rg_a, b_rg_x, rg_lambda,
        start0=(past == 0))
    nbs_s = past // SEL_BLOCK
    ck1p, ck2p, pe_tok = _compress_weights_paged(w_ck1, w_ck2, cmp_pe)
    cv1p, cv2p, _ = _compress_weights_paged(w_cv1, w_cv2, cmp_pe)
    kc_s = _compress_paged(cache_k_cmp, page_table, pe_tok, ck1p, ck2p)
    vc_s = _compress_paged(cache_v_cmp, page_table, pe_tok, cv1p, cv2p)
    o8 = _nsa_sample(page_table, pad8(s_q), kc_s, vc_s, cache_k_sel, cache_v_sel,
                     [pad8(s_kv[n]) for n in (2, 3, 4, 5)],
                     state_win_k.reshape(bs, WINDOW, KV_W), state_win_v.reshape(bs, WINDOW, KV_W),
                     pad8(s_gn), cst['pgk'], cst['rkg'], cst['egs'], n_tok)
    del nbs_s
    os_attn = o8[:, :n_tok].reshape(m_s, ATTN_W).astype(BF)
    x2s = _merge(xs, ys_rec.reshape(m_s, d_rnn), os_attn, w_grec, w_gatt, w_rec_b, w_attn_b, w_out_b, ln2_g, ln2_b)
    y_sample = _ffn_ln(x2s, up2, down2, ln3_g, ln3_b).reshape(bs, n_tok, d)

    kvs = lambda a: a.reshape(bs, n_tok, KV_HEADS, HEAD_DIM)
    win = lambda old, new: jnp.concatenate([old, kvs(new)], axis=1)[:, -WINDOW:]
    s_states = (kvs(s_kv[0]), kvs(s_kv[1]), kvs(s_kv[2]), kvs(s_kv[3]),
                win(state_win_k, s_kv[4]), win(state_win_v, s_kv[5]),
                s_conv.reshape(bs, CONV_W - 1, d_rnn), s_h)

    return (y_prompt, y_sample) + p_states + s_states
```

```python
import functools

import numpy as np
import jax
import jax.numpy as jnp
from jax import lax
from jax.experimental import pallas as pl
from jax.experimental.pallas import tpu as pltpu

F32 = jnp.float32
BF = jnp.bfloat16

DEPTH = 1
ALPHA = (2.0 * DEPTH) ** 0.25
N_HEADS = 16
HEAD_DIM = 64
KV_HEADS = 4
GROUP = N_HEADS // KV_HEADS
KV_W = KV_HEADS * HEAD_DIM
ATTN_W = N_HEADS * HEAD_DIM
N_NSA_BRANCH = 3
CMP_BLOCK = 32
SEL_BLOCK = 64
N_SEL = 16
WINDOW = 512
Q_BLOCK = 128
CONV_W = 4
LRU_C = 8.0
RNN_BLOCKS = 16
LN_EPS = 1e-5
PAGE_SIZE = 128

LANES = 128
SUBLANES = 8
VMEM_LIMIT = 56 * 1024 * 1024

NEG = -1e30
SEL_TILE = 512
BLOCKS_PER_TILE = SEL_TILE // SEL_BLOCK
SEL_V_ROWS = HEAD_DIM + 16
CMP_ROW = 16
CMP_ROW_W = CMP_ROW * KV_W

_NT = (((1,), (1,)), ((), ()))


def _dot(a, b):
    return jnp.dot(a, b, preferred_element_type=F32)


def _dot_nt(a, b):
    return lax.dot_general(a, b, _NT, preferred_element_type=F32)


def _params(*sem):
    return pltpu.CompilerParams(dimension_semantics=sem, vmem_limit_bytes=VMEM_LIMIT)


def _layer_norm(y, g, b):
    mu = jnp.mean(y, axis=-1, keepdims=True)
    d = y - mu
    var = jnp.mean(d * d, axis=-1, keepdims=True)
    return d * lax.rsqrt(var + LN_EPS) * g + b


def _masked_softmax(s, mask):
    s = jnp.where(mask, s, -jnp.inf)
    m = jnp.max(s, axis=-1, keepdims=True)
    m = jnp.where(m == -jnp.inf, 0.0, m)
    e = jnp.where(mask, jnp.exp(s - m), 0.0)
    den = jnp.sum(e, axis=-1, keepdims=True)
    return e / jnp.maximum(den, 1e-30)


def _split3(x):
    hi = x.astype(BF)
    r1 = x - hi.astype(F32)
    mid = r1.astype(BF)
    lo = (r1 - mid.astype(F32)).astype(BF)
    return hi, mid, lo


def _expand_f32(x, e):
    hi, mid, lo = _split3(x)
    return _dot(hi, e) + _dot(mid, e) + _dot(lo, e)


def _ffn_kernel(x_ref, wg_ref, wu_ref, wd_ref, g_ref, b_ref, o_ref, xb_ref, acc_ref):
    j = pl.program_id(1)

    @pl.when(j == 0)
    def _():
        xb_ref[...] = x_ref[...].astype(BF)
        acc_ref[...] = jnp.zeros_like(acc_ref)

    xb = xb_ref[...]
    gate = _dot(xb, wg_ref[...])
    up = _dot(xb, wu_ref[...])
    h = (gate * jax.nn.sigmoid(gate) * up).astype(BF)
    acc_ref[...] += _dot(h, wd_ref[...])

    @pl.when(j == pl.num_programs(1) - 1)
    def _():
        y = ALPHA * x_ref[...] + 0.5 * acc_ref[...]
        o_ref[...] = _layer_norm(y, g_ref[...], b_ref[...])


def _ffn_ln(x, w_up, w_down, g, b):
    m, d = x.shape
    f = w_down.shape[0]
    tm = min(512, m)
    tn = 512
    nj = f // tn
    return pl.pallas_call(
        _ffn_kernel,
        grid=(m // tm, nj),
        in_specs=[
            pl.BlockSpec((tm, d), lambda i, j: (i, 0)),
            pl.BlockSpec((d, tn), lambda i, j: (0, j)),
            pl.BlockSpec((d, tn), lambda i, j: (0, j + nj)),
            pl.BlockSpec((tn, d), lambda i, j: (j, 0)),
            pl.BlockSpec((1, d), lambda i, j: (0, 0)),
            pl.BlockSpec((1, d), lambda i, j: (0, 0)),
        ],
        out_specs=pl.BlockSpec((tm, d), lambda i, j: (i, 0)),
        out_shape=jax.ShapeDtypeStruct((m, d), F32),
        scratch_shapes=[pltpu.VMEM((tm, d), BF), pltpu.VMEM((tm, d), F32)],
        compiler_params=_params("parallel", "arbitrary"),
        name="ffn_ln",
    )(x, w_up, w_up, w_down, g.reshape(1, d), b.reshape(1, d))


def _mm_kernel(x_ref, w_ref, o_ref):
    o_ref[...] = _dot(x_ref[...].astype(BF), w_ref[...])


def _matmul(x, w, tn):
    m, d = x.shape
    n = w.shape[1]
    tm = min(512, m)
    return pl.pallas_call(
        _mm_kernel,
        grid=(m // tm, n // tn),
        in_specs=[pl.BlockSpec((tm, d), lambda i, j: (i, 0)),
                  pl.BlockSpec((d, tn), lambda i, j: (0, j))],
        out_specs=pl.BlockSpec((tm, tn), lambda i, j: (i, j)),
        out_shape=jax.ShapeDtypeStruct((m, n), F32),
        compiler_params=_params("parallel", "arbitrary"),
        name="proj",
    )(x, w)


def _proj_attn_kernel(x_ref, wq_ref, wkv_ref, wgn_ref, pq_ref, pk_ref, pkt_ref,
                      kc_ref, vc_ref, ks_ref, vs_ref, kw_ref, vw_ref, gn_ref,
                      qa_ref, ksa_ref, vst_ref, kwa_ref, vwt_ref):
    tm = x_ref.shape[0]
    xb = x_ref[...].astype(BF)
    zq = (_dot(xb, wq_ref[...]) * (HEAD_DIM ** -0.5 * np.log2(np.e))).astype(BF)
    for h in range(N_HEADS):
        pair = zq[:, (h // 2) * LANES:(h // 2 + 1) * LANES]
        qa_ref[h] = _dot(pair, pq_ref[h % 2]).astype(BF)
    zkv = _dot(xb, wkv_ref[...])
    parts = [zkv[:, n * KV_W:(n + 1) * KV_W] for n in range(6)]
    for ref, part in zip((kc_ref, vc_ref, ks_ref, vs_ref, kw_ref, vw_ref), parts):
        ref[...] = part
    gn_ref[...] = _dot(xb, wgn_ref[...])
    t = pl.program_id(0) * tm + lax.broadcasted_iota(jnp.int32, (tm, 1), 0)
    lane = lax.broadcasted_iota(jnp.int32, (1, LANES), 1)
    onehot = jnp.where(lane == HEAD_DIM + (t // SEL_BLOCK) % BLOCKS_PER_TILE, 1.0, 0.0)
    ksb = parts[2].astype(BF)
    kwb = parts[4].astype(BF)
    vsb = parts[3].astype(BF)
    vwb = parts[5].astype(BF)
    for k in range(KV_HEADS):
        ksa_ref[k] = (_dot(ksb, pk_ref[k]) + onehot).astype(BF)
        kwa_ref[k] = _dot(kwb, pk_ref[k]).astype(BF)
        ones_row = jnp.where(lax.broadcasted_iota(jnp.int32, (SEL_V_ROWS, 1), 0) == HEAD_DIM, 1.0, 0.0)
        vst_ref[k, 0] = (_dot_nt(pkt_ref[k], vsb) + ones_row).astype(BF)
        vwt_ref[k] = (_dot_nt(pkt_ref[k], vwb) + ones_row).astype(BF)


def _proj_attn(x, wq, wkv, wgn, pq, pk, pkt):
    m, d = x.shape
    tm = 256
    per_tile = SEL_TILE // tm
    row = lambda w: pl.BlockSpec((tm, w), lambda i: (i, 0))
    const = lambda a: pl.BlockSpec(a.shape, lambda i: (0,) * a.ndim)
    head = lambda n: pl.BlockSpec((n, tm, LANES), lambda i: (0, i, 0))
    f32 = lambda w: jax.ShapeDtypeStruct((m, w), F32)
    aug = lambda n: jax.ShapeDtypeStruct((n, m, LANES), BF)
    return pl.pallas_call(
        _proj_attn_kernel,
        grid=(m // tm,),
        in_specs=[row(d), const(wq), const(wkv), const(wgn), const(pq), const(pk), const(pkt)],
        out_specs=[row(KV_W)] * 6 + [row(LANES), head(N_HEADS), head(KV_HEADS),
                                     pl.BlockSpec((KV_HEADS, 1, SEL_V_ROWS, tm),
                                                  lambda i: (0, i // per_tile, 0, i % per_tile)),
                                     head(KV_HEADS),
                                     pl.BlockSpec((KV_HEADS, SEL_V_ROWS, tm), lambda i: (0, 0, i))],
        out_shape=[f32(KV_W)] * 6 + [f32(LANES), aug(N_HEADS), aug(KV_HEADS),
                                     jax.ShapeDtypeStruct((KV_HEADS, m // SEL_TILE, SEL_V_ROWS, SEL_TILE), BF),
                                     aug(KV_HEADS),
                                     jax.ShapeDtypeStruct((KV_HEADS, SEL_V_ROWS, m), BF)],
        compiler_params=_params("parallel"),
        name="proj_attn",
    )(x, wq, wkv, wgn, pq, pk, pkt)


def _rglru_gates(xc, wa, wx, ba, bx, lam, is_start):
    xcb = xc.astype(BF)
    r = jax.nn.sigmoid(_dot(xcb, wa) + ba)
    ig = jax.nn.sigmoid(_dot(xcb, wx) + bx)
    log_a = -LRU_C * r * jax.nn.softplus(-lam)
    a = jnp.exp(log_a)
    if is_start is True:
        return a, ig * xc
    th = jnp.tanh(log_a)
    mult = jnp.sqrt(-2.0 * th / (1.0 - th))
    if is_start is not None:
        mult = jnp.where(is_start, 1.0, mult)
    return a, mult * (ig * xc)


def _rglru_prompt_kernel(x_ref, gate_ref, wc_ref, bc_ref, wa_ref, wx_ref, ba_ref, bx_ref, lam_ref,
                         y_ref, tail_ref, hl_ref, h_scr, tail_scr):
    c = pl.program_id(1)
    tc = x_ref.shape[0]

    @pl.when(c == 0)
    def _():
        h_scr[...] = jnp.zeros_like(h_scr)
        tail_scr[...] = jnp.zeros_like(tail_scr)

    x = x_ref[...]
    prev = tail_scr[...]
    row8 = lax.broadcasted_iota(jnp.int32, (SUBLANES, 1), 0)
    row = lax.broadcasted_iota(jnp.int32, (tc, 1), 0)

    def shifted(s):
        rolled = pltpu.roll(x, s, 0)
        top = jnp.where(row8 < s, pltpu.roll(prev, s, 0), rolled[:SUBLANES])
        return jnp.concatenate([top, rolled[SUBLANES:]], axis=0)

    wc = wc_ref[...]
    conv = wc[0:1] * shifted(3)
    conv = conv + wc[1:2] * shifted(2)
    conv = conv + wc[2:3] * shifted(1)
    conv = conv + wc[3:4] * x
    xc = bc_ref[...] + conv

    is_start = (row + c * tc) == 0
    a, u = _rglru_gates(xc, wa_ref[0], wx_ref[0], ba_ref[...], bx_ref[...], lam_ref[...], is_start)

    d = 1
    while d < tc:
        keep = row >= d
        a_sh = jnp.where(keep, pltpu.roll(a, d, 0), 1.0)
        u_sh = jnp.where(keep, pltpu.roll(u, d, 0), 0.0)
        u = a * u_sh + u
        a = a * a_sh
        d *= 2
    h = a * h_scr[...] + u

    y_ref[...] = (jax.nn.gelu(gate_ref[...]) * h).astype(BF)
    h_last = h[tc - 1:tc]
    h_scr[...] = h_last
    tail_scr[...] = x[tc - SUBLANES:]
    hl_ref[...] = h_last
    tail_ref[...] = x[tc - SUBLANES:]


def _rglru_prompt(xrg, w_conv, b_conv, wa_bd, wx_bd, b_a, b_x, lam):
    t = xrg.shape[0]
    d_rnn = w_conv.shape[1]
    gw = 2 * LANES
    ng = d_rnn // gw
    tc = min(512, t)
    vec = lambda: pl.BlockSpec((1, gw), lambda g, c: (0, g))
    return pl.pallas_call(
        _rglru_prompt_kernel,
        grid=(ng, t // tc),
        in_specs=[
            pl.BlockSpec((tc, gw), lambda g, c: (c, g)),
            pl.BlockSpec((tc, gw), lambda g, c: (c, g + ng)),
            pl.BlockSpec((CONV_W, gw), lambda g, c: (0, g)),
            vec(),
            pl.BlockSpec((1, gw, gw), lambda g, c: (g, 0, 0)),
            pl.BlockSpec((1, gw, gw), lambda g, c: (g, 0, 0)),
            vec(), vec(), vec(),
        ],
        out_specs=[
            pl.BlockSpec((tc, gw), lambda g, c: (c, g)),
            pl.BlockSpec((SUBLANES, gw), lambda g, c: (0, g)),
            pl.BlockSpec((1, gw), lambda g, c: (0, g)),
        ],
        out_shape=[
            jax.ShapeDtypeStruct((t, d_rnn), BF),
            jax.ShapeDtypeStruct((SUBLANES, d_rnn), F32),
            jax.ShapeDtypeStruct((1, d_rnn), F32),
        ],
        scratch_shapes=[pltpu.VMEM((1, gw), F32), pltpu.VMEM((SUBLANES, gw), F32)],
        compiler_params=_params("parallel", "arbitrary"),
        name="rglru_prompt",
    )(xrg, xrg, w_conv, b_conv.reshape(1, -1), wa_bd, wx_bd,
      b_a.reshape(1, -1), b_x.reshape(1, -1), lam.reshape(1, -1))


def _rglru_sample_kernel(xrg_ref, cp_ref, h0_ref, wc_ref, bc_ref, wa_ref, wx_ref, ba_ref, bx_ref, lam_ref,
                         y_ref, cn_ref, hl_ref, *, n_tok, start0):
    d_rnn = h0_ref.shape[1]
    gw = wa_ref.shape[1]
    wc = wc_ref[...]
    xp = [cp_ref[:, k * d_rnn:(k + 1) * d_rnn] for k in range(CONV_W - 1)]
    xp += [xrg_ref[:, t * 2 * d_rnn:t * 2 * d_rnn + d_rnn] for t in range(n_tok)]
    h = h0_ref[...]
    for t in range(n_tok):
        conv = wc[0:1] * xp[t]
        for k in range(1, CONV_W):
            conv = conv + wc[k:k + 1] * xp[t + k]
        xc = bc_ref[...] + conv
        a_parts, u_parts = [], []
        for g in range(d_rnn // gw):
            sl = slice(g * gw, (g + 1) * gw)
            a_g, u_g = _rglru_gates(xc[:, sl], wa_ref[g], wx_ref[g], ba_ref[:, sl], bx_ref[:, sl],
                                    lam_ref[:, sl], True if (start0 and t == 0) else None)
            a_parts.append(a_g)
            u_parts.append(u_g)
        a = jnp.concatenate(a_parts, axis=1)
        u = jnp.concatenate(u_parts, axis=1)
        h = a * h + u
        gate = xrg_ref[:, t * 2 * d_rnn + d_rnn:(t + 1) * 2 * d_rnn]
        y_ref[:, t * d_rnn:(t + 1) * d_rnn] = (jax.nn.gelu(gate) * h).astype(BF)
    hl_ref[...] = h
    tail = xp[-(CONV_W - 1):]
    for k in range(CONV_W - 1):
        cn_ref[:, k * d_rnn:(k + 1) * d_rnn] = tail[k]


def _rglru_sample(xrg, conv_prev, h0, w_conv, b_conv, wa_bd, wx_bd, b_a, b_x, lam, start0):
    b, d_rnn = h0.shape
    n_tok = xrg.shape[1] // (2 * d_rnn)
    args = (xrg, conv_prev, h0, w_conv, b_conv.reshape(1, -1), wa_bd, wx_bd,
            b_a.reshape(1, -1), b_x.reshape(1, -1), lam.reshape(1, -1))
    full = lambda a: pl.BlockSpec(a.shape, lambda i: (0,) * a.ndim)
    outs = [jax.ShapeDtypeStruct((b, n_tok * d_rnn), BF),
            jax.ShapeDtypeStruct((b, (CONV_W - 1) * d_rnn), F32),
            jax.ShapeDtypeStruct((b, d_rnn), F32)]
    return pl.pallas_call(
        functools.partial(_rglru_sample_kernel, n_tok=n_tok, start0=start0),
        grid=(1,),
        in_specs=[full(a) for a in args],
        out_specs=[full(o) for o in outs],
        out_shape=outs,
        compiler_params=_params("arbitrary"),
        name="rglru_sample",
    )(*args)


def _compress_kernel(x_ref, pe_ref, w1_ref, w2_ref, pk_ref, o_ref, scr):
    x = x_ref[...]
    rows = x.shape[0]
    xb = (x.reshape(rows // SUBLANES, SUBLANES, CMP_ROW_W) + pe_ref[...][None]).reshape(rows, CMP_ROW_W)
    full = _dot(xb.astype(BF), w1_ref[...])
    hid = full[:, :KV_W] + pltpu.roll(full[:, KV_W:], rows - 1, 0)
    out = _dot(jax.nn.gelu(hid).astype(BF), w2_ref[...])
    nb = rows // 4
    ob = out.astype(BF)
    for k in range(KV_HEADS):
        scr[...] = _dot(ob, pk_ref[k])
        o_ref[k, 0] = scr[pl.ds(0, nb, stride=4), :].astype(BF)
        o_ref[k, 1] = scr[pl.ds(2, nb, stride=4), :].astype(BF)


def _compress_prompt(kv, pe2, w1, w2, pk):
    t = kv.shape[0]
    rows = 128
    x = kv.reshape(t // CMP_ROW, CMP_ROW_W)
    nb = rows // 4
    nbs = t // SEL_BLOCK
    const = lambda a: pl.BlockSpec(a.shape, lambda i: (0,) * a.ndim)
    return pl.pallas_call(
        _compress_kernel,
        grid=(x.shape[0] // rows,),
        in_specs=[pl.BlockSpec((rows, CMP_ROW_W), lambda i: (i, 0)),
                  const(pe2), const(w1), const(w2), const(pk)],
        out_specs=pl.BlockSpec((KV_HEADS, 2, nb, LANES), lambda i: (0, 0, i, 0)),
        out_shape=jax.ShapeDtypeStruct((KV_HEADS, 2, nbs, LANES), BF),
        scratch_shapes=[pltpu.VMEM((rows, LANES), F32)],
        compiler_params=_params("parallel"),
        name="compress_prompt",
    )(x, pe2, w1, w2, pk)


def _compress_paged_kernel(pt_ref, *refs, pages):
    del pt_ref
    x_refs = refs[:pages]
    pe_ref, w1_ref, w2_ref, o_ref, xs_scr, out_scr = refs[pages:]
    halves = KV_W // LANES
    for p in range(pages):
        xt = x_refs[p][0].T + pe_ref[...]
        for h in range(halves):
            xs_scr[h, p * PAGE_SIZE:(p + 1) * PAGE_SIZE, :] = xt[:, h * LANES:(h + 1) * LANES]
    nblk = pages * (PAGE_SIZE // CMP_BLOCK)
    by_tok = [pltpu.einshape("mld->lmd", xs_scr[h].reshape(nblk, CMP_BLOCK, LANES)) for h in range(halves)]
    acc = None
    for l2 in range(CMP_BLOCK // 2):
        parts = [jnp.concatenate([by_tok[h][2 * l2], by_tok[h][2 * l2 + 1]], axis=1) for h in range(halves)]
        d = _dot(jnp.concatenate(parts, axis=0).astype(BF), w1_ref[l2])
        acc = d if acc is None else acc + d
    out_scr[...] = _dot(jax.nn.gelu(acc).astype(BF), w2_ref[...])
    for h in range(halves):
        for par in range(2):
            o_ref[0, par, :, h * LANES:(h + 1) * LANES] = (
                out_scr[pl.ds(h * nblk + par, nblk // 2, stride=2), :].astype(BF))


def _compress_paged(cache, page_table, pe_tok, w1p, w2p):
    b, n_pages = page_table.shape
    pages = 16
    x = jnp.transpose(cache, (0, 2, 3, 1)).reshape(cache.shape[0], KV_W, PAGE_SIZE)
    nblk = pages * (PAGE_SIZE // CMP_BLOCK)
    halves = KV_W // LANES
    nbs = n_pages * PAGE_SIZE // SEL_BLOCK
    const = lambda a: pl.BlockSpec(a.shape, lambda i, s, pt: (0,) * a.ndim)
    page_specs = [
        pl.BlockSpec((1, KV_W, PAGE_SIZE),
                     functools.partial(lambda i, s, pt, p: (pt[i, s * pages + p], 0, 0), p=p))
        for p in range(pages)]
    return pl.pallas_call(
        functools.partial(_compress_paged_kernel, pages=pages),
        grid_spec=pltpu.PrefetchScalarGridSpec(
            num_scalar_prefetch=1,
            grid=(b, n_pages // pages),
            in_specs=page_specs + [const(pe_tok), const(w1p), const(w2p)],
            out_specs=pl.BlockSpec((1, 2, nblk // 2, KV_W), lambda i, s, pt: (i, 0, s, 0)),
            scratch_shapes=[pltpu.VMEM((halves, pages * PAGE_SIZE, LANES), F32),
                            pltpu.VMEM((halves * nblk, LANES), F32)],
        ),
        out_shape=jax.ShapeDtypeStruct((b, 2, nbs, KV_W), BF),
        compiler_params=_params("parallel", "arbitrary"),
        name="compress_paged",
    )(page_table, *([x] * pages), pe_tok, w1p, w2p)


def _topk_columns(imp_t, n_pick, quota=None):
    nblk = imp_t.shape[0]
    idx = lax.broadcasted_iota(jnp.int32, imp_t.shape, 0)
    sel = jnp.zeros(imp_t.shape, F32)
    for n in range(n_pick):
        mx = jnp.max(imp_t, axis=0, keepdims=True)
        first = jnp.min(jnp.where(imp_t == mx, idx, nblk), axis=0, keepdims=True)
        if quota is not None:
            first = jnp.where(n < quota, first, nblk)
        hit = idx == first
        sel = jnp.where(hit, 1.0, sel)
        imp_t = jnp.where(hit, -jnp.inf, imp_t)
    return sel


def _softmax_cols(s):
    m = jnp.max(s, axis=0, keepdims=True)
    m = jnp.where(m == -jnp.inf, 0.0, m)
    e = jnp.exp2(s - m)
    den = jnp.sum(e, axis=0, keepdims=True)
    return e * (1.0 / jnp.maximum(den, 1e-30))


def _nsa_prompt_kernel(*refs):
    (q_ref, kc_ref, vc_ref, ks_ref, vst_ref) = refs[:5]
    kw_refs = refs[5:10]
    vwt_refs = refs[10:15]
    (gn_ref, rp_ref, pa_ref, o_ref,
     bias_scr, gate_scr, sa_scr, sb_scr, m_scr, acc_scr, oc_scr) = refs[15:]
    kvh = pl.program_id(0)
    i = pl.program_id(1)
    nq = Q_BLOCK
    r = GROUP * nq
    hd = HEAD_DIM
    q = q_ref[...].reshape(r, LANES)
    lane = lax.broadcasted_iota(jnp.int32, (1, r), 1)
    qpos = i * nq + lane % nq

    nbs = kc_ref.shape[1] // 2

    def compressed_and_select(nv, first_quarter):
        kc = jnp.concatenate([kc_ref[0, 0:nv], kc_ref[0, nbs:nbs + nv]], axis=0)
        vc = jnp.concatenate([vc_ref[0, 0:nv], vc_ref[0, nbs:nbs + nv]], axis=0)
        rowc = lax.broadcasted_iota(jnp.int32, (2 * nv, 1), 0)
        blk = 2 * (rowc % nv) + rowc // nv
        p_c = _softmax_cols(jnp.where(blk * CMP_BLOCK + (CMP_BLOCK - 1) <= qpos, _dot_nt(kc, q), -jnp.inf))
        vct = vc.astype(F32).T[:hd].astype(BF)
        oc_scr[...] = _dot(vct, p_c.astype(BF))
        imp = p_c[:, 0:nq]
        for g in range(1, GROUP):
            imp = imp + p_c[:, g * nq:(g + 1) * nq]
        imp = imp[:nv] + imp[nv:]
        qp = i * nq + lax.broadcasted_iota(jnp.int32, (1, nq), 1)
        jb = lax.broadcasted_iota(jnp.int32, (nv, 1), 0)
        cur = qp // SEL_BLOCK
        valid = jb * SEL_BLOCK <= qp
        forced = (jb == 0) | (jb == cur) | (jb == cur - 1)
        n_forced = 1 + jnp.where(cur >= 1, 1, 0) + jnp.where(cur >= 2, 1, 0)
        n_sel = min(N_SEL, nbs)
        n_iter = n_sel - (1 if first_quarter else 3)
        picks = _topk_columns(jnp.where(valid & ~forced, imp, -jnp.inf), n_iter, n_sel - n_forced)
        sel = jnp.where(forced, 1.0, picks)
        bias = jnp.where(valid & (sel > 0.5), 0.0, NEG).T.astype(BF)
        nt = nv // BLOCKS_PER_TILE
        bias_all = _dot(bias, pa_ref[0:nv, 0:nt * LANES]).astype(BF)
        for t in range(nt):
            bias_scr[t] = bias_all[:, t * LANES:(t + 1) * LANES]

    n_var = 4
    variant = (i * n_var) // pl.num_programs(1)
    for v in range(n_var):
        pl.when(variant == v)(functools.partial(compressed_and_select, nbs * (v + 1) // n_var, v == 0))
    o_c = oc_scr[...]

    def scores(t):
        k0 = pl.multiple_of(t * SEL_TILE, SEL_TILE)
        kt = ks_ref[0, pl.ds(k0, SEL_TILE), :]
        qa = q + jnp.concatenate([bias_scr[t]] * GROUP, axis=0)
        return _dot_nt(kt, qa)

    def update(t, s_ref, diagonal):
        vt = vst_ref[0, t]
        for g in range(GROUP):
            sl = slice(g * nq, (g + 1) * nq)
            sg = s_ref[:, sl]
            if diagonal:
                kpos = t * SEL_TILE + lax.broadcasted_iota(jnp.int32, (SEL_TILE, 1), 0)
                sg = jnp.where(kpos <= qpos[:, sl], sg, NEG)
            m_old = m_scr[:, sl]
            m_new = jnp.maximum(m_old, jnp.max(sg, axis=0, keepdims=True))
            p = jnp.exp2(sg - m_new)
            acc_scr[:, sl] = jnp.exp2(m_old - m_new) * acc_scr[:, sl] + _dot(vt, p.astype(BF))
            m_scr[:, sl] = m_new

    def pair(t0):
        sb_scr[...] = scores(t0 + 1)
        update(t0, sa_scr, False)
        sa_scr[...] = scores(t0 + 2)
        update(t0 + 1, sb_scr, False)

    def quad(u, _):
        pair(4 * u)
        pair(4 * u + 2)
        return 0

    t_diag = (i * nq) // SEL_TILE
    m_scr[...] = jnp.full(m_scr.shape, NEG, F32)
    acc_scr[...] = jnp.zeros_like(acc_scr)
    sa_scr[...] = scores(0)
    lax.fori_loop(0, t_diag // 4, quad, 0)

    @pl.when(t_diag % 4 >= 2)
    def _():
        pair(t_diag - t_diag % 4)

    @pl.when(t_diag % 2 == 1)
    def _():
        sb_scr[...] = scores(t_diag)
        update(t_diag - 1, sa_scr, False)
        update(t_diag, sb_scr, True)

    @pl.when(t_diag % 2 == 0)
    def _():
        update(t_diag, sa_scr, True)

    o_s = acc_scr[:hd] * (1.0 / acc_scr[hd:hd + 1])

    kw = jnp.concatenate([ref[0] for ref in kw_refs], axis=0)
    vwt = jnp.concatenate([ref[0] for ref in vwt_refs], axis=1)
    s_w = _dot_nt(kw, q)
    nwb = len(kw_refs)
    krow = lax.broadcasted_iota(jnp.int32, (nq, 1), 0)
    parts = []
    for b in range(nwb):
        blk_pos = (i - (nwb - 1) + b) * nq
        mask = blk_pos >= 0
        if b == 0:
            mask = mask & (qpos - (blk_pos + krow) <= WINDOW)
        if b == nwb - 1:
            mask = mask & (blk_pos + krow <= qpos)
        parts.append(jnp.where(mask, s_w[b * nq:(b + 1) * nq], -jnp.inf))
    s_w = jnp.concatenate(parts, axis=0)
    e_w = jnp.exp2(s_w - jnp.max(s_w, axis=0, keepdims=True))
    acc_w = _dot(vwt, e_w.astype(BF))
    o_w = acc_w[:hd] * (1.0 / jnp.maximum(acc_w[hd:hd + 1], 1e-30))

    gate_scr[...] = jax.nn.sigmoid(gn_ref[...]).T

    def gate(br):
        return jnp.concatenate(
            [gate_scr[pl.ds(br * N_HEADS + kvh * GROUP + g, 1), :] for g in range(GROUP)], axis=1)

    comb = gate(0) * o_c + gate(1) * o_s + gate(2) * o_w
    comb = jnp.concatenate([comb, jnp.zeros_like(comb)], axis=0)
    out = None
    for g in range(GROUP):
        part = _dot(comb[:, g * nq:(g + 1) * nq].T.astype(BF), rp_ref[g])
        out = part if out is None else out + part
    o_ref[...] = out.astype(BF)


def _nsa_prompt(qa, kc, vc, ksa, vst, kwa, vwt, gn, rplace, pall):
    t = qa.shape[1]
    n_tiles = t // SEL_TILE
    wb = WINDOW // Q_BLOCK + 1
    kv_full = lambda a: pl.BlockSpec((1,) + a.shape[1:], lambda k, i: (k,) + (0,) * (a.ndim - 1))
    wblk = lambda i, m: jnp.maximum(i - (wb - 1) + m, 0)
    win_k = [pl.BlockSpec((1, Q_BLOCK, LANES), functools.partial(lambda k, i, m: (k, wblk(i, m), 0), m=m))
             for m in range(wb)]
    win_v = [pl.BlockSpec((1, SEL_V_ROWS, Q_BLOCK), functools.partial(lambda k, i, m: (k, 0, wblk(i, m)), m=m))
             for m in range(wb)]
    const = lambda a: pl.BlockSpec(a.shape, lambda k, i: (0,) * a.ndim)
    return pl.pallas_call(
        _nsa_prompt_kernel,
        grid=(KV_HEADS, t // Q_BLOCK),
        in_specs=[pl.BlockSpec((GROUP, Q_BLOCK, LANES), lambda k, i: (k, i, 0)),
                  kv_full(kc), kv_full(vc), kv_full(ksa), kv_full(vst)]
                 + win_k + win_v
                 + [pl.BlockSpec((Q_BLOCK, LANES), lambda k, i: (i, 0)), const(rplace), const(pall)],
        out_specs=pl.BlockSpec((Q_BLOCK, KV_W), lambda k, i: (i, k)),
        out_shape=jax.ShapeDtypeStruct((t, ATTN_W), BF),
        scratch_shapes=[pltpu.VMEM((n_tiles, Q_BLOCK, LANES), BF), pltpu.VMEM((LANES, Q_BLOCK), F32),
                        pltpu.VMEM((SEL_TILE, GROUP * Q_BLOCK), F32), pltpu.VMEM((SEL_TILE, GROUP * Q_BLOCK), F32),
                        pltpu.VMEM((1, GROUP * Q_BLOCK), F32),
                        pltpu.VMEM((SEL_V_ROWS, GROUP * Q_BLOCK), F32),
                        pltpu.VMEM((HEAD_DIM, GROUP * Q_BLOCK), F32)],
        compiler_params=_params("parallel", "arbitrary"),
        name="nsa_prompt",
    )(qa, kc, vc, ksa, vst, *([kwa] * wb), *([vwt] * wb), gn, rplace, pall)


def _nsa_sample_kernel(*refs, pages, past, n_tok):
    pt_ref = refs[0]
    q_ref, kc_ref, vc_ref = refs[1:4]
    ks_refs = refs[4:4 + pages]
    vs_refs = refs[4 + pages:4 + 2 * pages]
    (ksn_ref, vsn_ref, kwn_ref, vwn_ref, wk_ref, wv_ref, gn_ref,
     pg_ref, rp_ref, eg_ref, o_ref, q_scr, bias_scr, m_scr, l_scr, acc_scr, oc_scr) = refs[4 + 2 * pages:]
    del pt_ref
    s_id = pl.program_id(1)
    tp = SUBLANES
    r = N_HEADS * tp
    row = lax.broadcasted_iota(jnp.int32, (r, 1), 0)
    qpos = past + row % tp
    nbs = bias_scr.shape[1]

    @pl.when(s_id == 0)
    def _():
        qb = q_ref[0].astype(BF)
        for h in range(N_HEADS):
            k = h // GROUP
            q_scr[h * tp:(h + 1) * tp, :] = _dot(qb[:, k * KV_W:(k + 1) * KV_W], pg_ref[h]).astype(BF)
        q = q_scr[...]
        kc = kc_ref[0].reshape(2 * nbs, KV_W)
        vc = vc_ref[0].reshape(2 * nbs, KV_W)
        col = lax.broadcasted_iota(jnp.int32, (1, 2 * nbs), 1)
        blk = 2 * (col % nbs) + col // nbs
        p_c = _masked_softmax(_dot_nt(q, kc), blk * CMP_BLOCK + (CMP_BLOCK - 1) <= qpos)
        oc_scr[...] = _dot(p_c.astype(BF), vc)
        imps = []
        for k in range(KV_HEADS):
            base = k * GROUP * tp
            imp = p_c[base:base + tp]
            for g in range(1, GROUP):
                imp = imp + p_c[base + g * tp:base + (g + 1) * tp]
            imps.append(imp)
        imp = jnp.concatenate(imps, axis=0)
        imp = imp[:, :nbs] + imp[:, nbs:]
        jb = lax.broadcasted_iota(jnp.int32, (1, nbs), 1)
        imp = jnp.where((jb == 0) | (jb == nbs - 1), jnp.inf, imp)
        idx = lax.broadcasted_iota(jnp.int32, imp.shape, 1)
        sel = jnp.zeros(imp.shape, F32)
        for _ in range(min(N_SEL, nbs + 1) - 1):
            mx = jnp.max(imp, axis=1, keepdims=True)
            first = jnp.min(jnp.where(imp == mx, idx, nbs), axis=1, keepdims=True)
            hit = idx == first
            sel = jnp.where(hit, 1.0, sel)
            imp = jnp.where(hit, -jnp.inf, imp)
        bias = jnp.where(sel > 0.5, 0.0, NEG).astype(BF)
        for k in range(KV_HEADS):
            for g in range(GROUP):
                h = k * GROUP + g
                bias_scr[h * tp:(h + 1) * tp, :] = bias[k * tp:(k + 1) * tp]
        m_scr[...] = jnp.full(m_scr.shape, NEG, F32)
        l_scr[...] = jnp.zeros_like(l_scr)
        acc_scr[...] = jnp.zeros_like(acc_scr)

    def online(st, v, v_transposed):
        m = m_scr[...]
        m_new = jnp.maximum(m, jnp.max(st, axis=1, keepdims=True))
        scale = jnp.exp(m - m_new)
        p = jnp.exp(st - m_new)
        l_scr[...] = scale * l_scr[...] + jnp.sum(p, axis=1, keepdims=True)
        pv = _dot_nt(p.astype(BF), v) if v_transposed else _dot(p.astype(BF), v)
        acc_scr[...] = scale * acc_scr[...] + pv
        m_scr[...] = m_new

    q = q_scr[...]
    kt = jnp.concatenate([ref[0].astype(BF) for ref in ks_refs], axis=1)
    vt = jnp.concatenate([ref[0].astype(BF) for ref in vs_refs], axis=1)
    n_keys = kt.shape[1]
    jb = lax.broadcasted_iota(jnp.int32, (nbs, 1), 0)
    key_blk = s_id * (n_keys // SEL_BLOCK) + lax.broadcasted_iota(jnp.int32, (1, n_keys), 1) // SEL_BLOCK
    expand = jnp.where(jb == key_blk, 1.0, 0.0).astype(BF)
    online(_dot(q, kt) + _dot(bias_scr[...], expand), vt, True)

    @pl.when(s_id == pl.num_programs(1) - 1)
    def _():
        zeros = jnp.zeros((LANES - tp, KV_W), BF)
        tok = lax.broadcasted_iota(jnp.int32, (1, LANES), 1)
        ksn = jnp.concatenate([ksn_ref[0].astype(BF), zeros], axis=0)
        vsn = jnp.concatenate([vsn_ref[0].astype(BF), zeros], axis=0)
        online(jnp.where((past + tok <= qpos) & (tok < n_tok), _dot_nt(q, ksn), NEG), vsn, False)
        o_s = acc_scr[...] / l_scr[...]
        kw = jnp.concatenate([wk_ref[0].astype(BF), kwn_ref[0].astype(BF), zeros], axis=0)
        vw = jnp.concatenate([wv_ref[0].astype(BF), vwn_ref[0].astype(BF), zeros], axis=0)
        n_w = kw.shape[0]
        wcol = lax.broadcasted_iota(jnp.int32, (1, n_w), 1)
        kwpos = past - WINDOW + wcol
        dpos = qpos - kwpos
        mask_w = (dpos >= 0) & (dpos <= WINDOW) & (kwpos >= 0) & (wcol < WINDOW + n_tok)
        p_w = _masked_softmax(_dot_nt(q, kw), mask_w)
        o_w = _dot(p_w.astype(BF), vw)
        sg = jax.nn.sigmoid(gn_ref[0])
        gexp = _expand_f32(sg, eg_ref[...])

        def gate(br):
            return jnp.concatenate(
                [gexp[:, (br * N_HEADS + h) * KV_W:(br * N_HEADS + h + 1) * KV_W] for h in range(N_HEADS)],
                axis=0)

        comb = (gate(0) * oc_scr[...] + gate(1) * o_s + gate(2) * o_w).astype(BF)
        for k in range(KV_HEADS):
            out = None
            for g in range(GROUP):
                h = k * GROUP + g
                part = _dot(comb[h * tp:(h + 1) * tp], rp_ref[h])
                out = part if out is None else out + part
            o_ref[0, :, k * KV_W:(k + 1) * KV_W] = out


def _nsa_sample(page_table, q8, kc, vc, cache_ks, cache_vs, new8, win_k, win_v, gn8, pgk, rkg, egate, n_tok):
    b, n_pages = page_table.shape
    past = n_pages * PAGE_SIZE
    pages = 8
    tp = SUBLANES
    r = N_HEADS * tp
    nbs = past // SEL_BLOCK
    ck = jnp.transpose(cache_ks, (0, 2, 3, 1)).reshape(cache_ks.shape[0], KV_W, PAGE_SIZE)
    cv = jnp.transpose(cache_vs, (0, 2, 3, 1)).reshape(cache_vs.shape[0], KV_W, PAGE_SIZE)
    per_seq = lambda a: pl.BlockSpec((1,) + a.shape[1:], lambda i, s, pt: (i,) + (0,) * (a.ndim - 1))
    const = lambda a: pl.BlockSpec(a.shape, lambda i, s, pt: (0,) * a.ndim)
    page_specs = [
        pl.BlockSpec((1, KV_W, PAGE_SIZE),
                     functools.partial(lambda i, s, pt, p: (pt[i, s * pages + p], 0, 0), p=p))
        for p in range(pages)]
    return pl.pallas_call(
        functools.partial(_nsa_sample_kernel, pages=pages, past=past, n_tok=n_tok),
        grid_spec=pltpu.PrefetchScalarGridSpec(
            num_scalar_prefetch=1,
            grid=(b, n_pages // pages),
            in_specs=[per_seq(q8), per_seq(kc), per_seq(vc)] + page_specs + page_specs
                     + [per_seq(a) for a in new8] + [per_seq(win_k), per_seq(win_v), per_seq(gn8),
                                                     const(pgk), const(rkg), const(egate)],
            out_specs=pl.BlockSpec((1, tp, ATTN_W), lambda i, s, pt: (i, 0, 0)),
            scratch_shapes=[pltpu.VMEM((r, KV_W), BF), pltpu.VMEM((r, nbs), BF),
                            pltpu.VMEM((r, 1), F32), pltpu.VMEM((r, 1), F32),
                            pltpu.VMEM((r, KV_W), F32), pltpu.VMEM((r, KV_W), F32)],
        ),
        out_shape=jax.ShapeDtypeStruct((b, tp, ATTN_W), F32),
        compiler_params=_params("parallel", "arbitrary"),
        name="nsa_sample",
    )(page_table, q8, kc, vc, *([ck] * pages), *([cv] * pages), *new8, win_k, win_v, gn8, pgk, rkg, egate)


def _merge_kernel(x_ref, y_ref, o_ref, wgr_ref, wga_ref, wr_ref, wa_ref, wo_ref, g_ref, b_ref,
                  out_ref, xb_ref, acc_ref):
    j = pl.program_id(1)

    @pl.when(j == 0)
    def _():
        xb_ref[...] = x_ref[...].astype(BF)
        acc_ref[...] = jnp.zeros_like(acc_ref)

    xb = xb_ref[...]
    rec = _dot(y_ref[...], wr_ref[...])
    att = _dot(o_ref[...], wa_ref[...])
    u = jax.nn.sigmoid(_dot(xb, wgr_ref[...])) * rec + jax.nn.sigmoid(_dot(xb, wga_ref[...])) * att
    acc_ref[...] += _dot(u.astype(BF), wo_ref[...])

    @pl.when(j == pl.num_programs(1) - 1)
    def _():
        out_ref[...] = _layer_norm(ALPHA * x_ref[...] + acc_ref[...], g_ref[...], b_ref[...])


def _merge(x, y_rec, o_attn, w_grec, w_gatt, w_rec_o, w_attn_o, w_out, g, b):
    m, d = x.shape
    tm = min(512, m)
    tn = 512
    col = lambda a: pl.BlockSpec((a.shape[0], tn), lambda i, j: (0, j))
    rowb = lambda a: pl.BlockSpec((tm, a.shape[1]), lambda i, j: (i, 0))
    vec = pl.BlockSpec((1, d), lambda i, j: (0, 0))
    return pl.pallas_call(
        _merge_kernel,
        grid=(m // tm, d // tn),
        in_specs=[rowb(x), rowb(y_rec), rowb(o_attn), col(w_grec), col(w_gatt), col(w_rec_o), col(w_attn_o),
                  pl.BlockSpec((tn, d), lambda i, j: (j, 0)), vec, vec],
        out_specs=pl.BlockSpec((tm, d), lambda i, j: (i, 0)),
        out_shape=jax.ShapeDtypeStruct((m, d), F32),
        scratch_shapes=[pltpu.VMEM((tm, d), BF), pltpu.VMEM((tm, d), F32)],
        compiler_params=_params("parallel", "arbitrary"),
        name="merge_out",
    )(x, y_rec, o_attn, w_grec, w_gatt, w_rec_o, w_attn_o, w_out, g.reshape(1, d), b.reshape(1, d))


def _place(n_rows, n_cols, src0, dst0, width, value=1.0):
    m = np.zeros((n_rows, n_cols), np.float32)
    m[src0 + np.arange(width), dst0 + np.arange(width)] = value
    return m


def _layout_constants(n_sel_blocks_prompt):
    hd = HEAD_DIM
    scale = hd ** -0.5
    pq = np.stack([_place(LANES, LANES, e * hd, 0, hd) for e in range(2)])
    pk = np.stack([_place(KV_W, LANES, k * hd, 0, hd) for k in range(KV_HEADS)])
    rplace = np.stack([_place(LANES, KV_W, 0, g * hd, hd) for g in range(GROUP)])
    pkt = np.transpose(pk, (0, 2, 1))[:, :SEL_V_ROWS]
    nbs = n_sel_blocks_prompt
    n_tiles = -(-nbs // BLOCKS_PER_TILE)
    pall = np.zeros((nbs, n_tiles * LANES), np.float32)
    j = np.arange(nbs)
    pall[j, (j // BLOCKS_PER_TILE) * LANES + hd + j % BLOCKS_PER_TILE] = 1.0
    pgk = np.stack([_place(KV_W, KV_W, (h % GROUP) * hd, (h // GROUP) * hd, hd, scale) for h in range(N_HEADS)])
    rkg = np.stack([_place(KV_W, KV_W, (h // GROUP) * hd, (h % GROUP) * hd, hd) for h in range(N_HEADS)])
    egs = np.zeros((LANES, N_NSA_BRANCH * N_HEADS * KV_W), np.float32)
    for c in range(N_NSA_BRANCH * N_HEADS):
        egs[c, c * KV_W:(c + 1) * KV_W] = 1.0
    as_bf = lambda a: jnp.asarray(a, BF)
    return dict(pq=as_bf(pq), pk=as_bf(pk), pkt=as_bf(pkt), rplace=as_bf(rplace), pall=as_bf(pall),
                pgk=as_bf(pgk), rkg=as_bf(rkg), egs=as_bf(egs))


def _block_diag(w, per_group):
    nb, c, _ = w.shape
    eye = jnp.eye(per_group, dtype=w.dtype)
    wg = w.reshape(nb // per_group, per_group, c, c)
    return jnp.einsum('gpcd,pq->gpcqd', wg, eye).reshape(nb // per_group, per_group * c, per_group * c)


def _compress_weights(w1, w2, pe):
    eye = jnp.eye(KV_HEADS, dtype=w1.dtype)
    big = jnp.einsum('lde,kq->lkdqe', w1, eye).reshape(CMP_BLOCK * KV_W, KV_HEADS * w1.shape[2])
    half = big.shape[0] // 2
    w1cat = jnp.concatenate([big[:half], big[half:]], axis=1).astype(BF)
    w2bd = _block_diag(jnp.broadcast_to(w2, (KV_HEADS,) + w2.shape), KV_HEADS)[0].astype(BF)
    pe_flat = jnp.broadcast_to(pe[:, None, :], (CMP_BLOCK, KV_HEADS, HEAD_DIM)).reshape(2, half)
    pe2 = jnp.tile(pe_flat, (SUBLANES // 2, 1))
    return w1cat, w2bd, pe2


def _compress_weights_paged(w1, w2, pe):
    per_half = LANES // HEAD_DIM
    eye = jnp.eye(per_half, dtype=w1.dtype)
    w1h = jnp.einsum('lde,kq->lkdqe', w1, eye).reshape(CMP_BLOCK, LANES, LANES)
    w1p = w1h.reshape(CMP_BLOCK // 2, 2 * LANES, LANES).astype(BF)
    w2p = jnp.einsum('ed,kq->keqd', w2, eye).reshape(LANES, LANES).astype(BF)
    pe_tok = jnp.tile(pe, (PAGE_SIZE // CMP_BLOCK, KV_HEADS))
    return w1p, w2p, pe_tok


def kernel(x_prompt, x_sample, cache_k_cmp, cache_v_cmp, cache_k_sel, cache_v_sel, page_table,
           state_win_k, state_win_v, state_conv, state_h,
           ln1_g, ln1_b, w_ffn1_up, w_ffn1_down, w_in, w_conv, b_conv, w_rg_a, b_rg_a, w_rg_x, b_rg_x,
           rg_lambda, cmp_pe, w_ck1, w_ck2, w_cv1, w_cv2, w_rec_o, w_attn_o, w_out,
           ln2_g, ln2_b, w_ffn2_up, w_ffn2_down, ln3_g, ln3_b):
    bp, t, d = x_prompt.shape
    bs, n_tok, _ = x_sample.shape
    d_rnn = w_conv.shape[1]
    assert bp == 1 and t % (128 * CMP_ROW) == 0 and t >= WINDOW
    past = page_table.shape[1] * PAGE_SIZE
    assert page_table.shape[1] % 16 == 0 and n_tok <= SUBLANES and n_tok >= CONV_W - 1
    assert state_win_k.shape[1] == WINDOW

    o_q = 2 * d_rnn
    o_kv = o_q + ATTN_W
    o_gn = o_kv + 6 * KV_W
    o_gr = o_gn + N_NSA_BRANCH * N_HEADS
    o_ga = o_gr + d
    w_in_b = w_in.astype(BF)
    w_rnn = w_in_b[:, :o_q]
    w_q = w_in_b[:, o_q:o_kv]
    w_kv = w_in_b[:, o_kv:o_gn]
    w_gn = jnp.pad(w_in_b[:, o_gn:o_gr], ((0, 0), (0, LANES - (o_gr - o_gn))))
    w_grec = w_in_b[:, o_gr:o_ga]
    w_gatt = w_in_b[:, o_ga:]
    w_attn_all = jnp.concatenate([w_in_b[:, :o_gn], jnp.pad(w_gn, ((0, 0), (0, 512 - LANES)))], axis=1)
    up1, down1 = w_ffn1_up.astype(BF), w_ffn1_down.astype(BF)
    up2, down2 = w_ffn2_up.astype(BF), w_ffn2_down.astype(BF)
    per_group = 2 * LANES // (d_rnn // RNN_BLOCKS)
    wa_bd = _block_diag(w_rg_a, per_group).astype(BF)
    wx_bd = _block_diag(w_rg_x, per_group).astype(BF)
    ck1, ck2, pe2 = _compress_weights(w_ck1, w_ck2, cmp_pe)
    cv1, cv2, _ = _compress_weights(w_cv1, w_cv2, cmp_pe)
    w_rec_b, w_attn_b, w_out_b = w_rec_o.astype(BF), w_attn_o.astype(BF), w_out.astype(BF)
    cst = _layout_constants(t // SEL_BLOCK)

    xp = _ffn_ln(x_prompt.reshape(t, d), up1, down1, ln1_g, ln1_b)
    xrg = _matmul(xp, w_rnn, 512)
    (p_kc, p_vc, p_ks, p_vs, p_kw, p_vw, p_gn, qa, ksa, vst, kwa, vwt) = _proj_attn(
        xp, w_q, w_kv, w_gn, cst['pq'], cst['pk'], cst['pkt'])
    y_rec, p_tail, p_h = _rglru_prompt(xrg, w_conv, b_conv, wa_bd, wx_bd, b_rg_a, b_rg_x, rg_lambda)
    nbs = t // SEL_BLOCK
    kc = _compress_prompt(p_kc, pe2, ck1, ck2, cst['pk']).reshape(KV_HEADS, 2 * nbs, LANES)
    vc = _compress_prompt(p_vc, pe2, cv1, cv2, cst['pk']).reshape(KV_HEADS, 2 * nbs, LANES)
    o_attn = _nsa_prompt(qa, kc, vc, ksa, vst, kwa, vwt, p_gn, cst['rplace'], cst['pall'])
    x2 = _merge(xp, y_rec, o_attn, w_grec, w_gatt, w_rec_b, w_attn_b, w_out_b, ln2_g, ln2_b)
    y_prompt = _ffn_ln(x2, up2, down2, ln3_g, ln3_b).reshape(bp, t, d)

    kvh = lambda a: a.reshape(bp, -1, KV_HEADS, HEAD_DIM)
    p_states = (kvh(p_kc), kvh(p_vc), kvh(p_ks), kvh(p_vs), kvh(p_kw[t - WINDOW:]), kvh(p_vw[t - WINDOW:]),
                p_tail[SUBLANES - (CONV_W - 1):].reshape(bp, CONV_W - 1, d_rnn), p_h.reshape(bp, d_rnn))

    m_s = bs * n_tok
    xs = _ffn_ln(x_sample.reshape(m_s, d), up1, down1, ln1_g, ln1_b)
    zs = _matmul(xs, w_attn_all, 512)
    s_xrg = zs[:, :o_q].reshape(bs, n_tok * o_q)
    seq = lambda a: a.reshape(bs, n_tok, a.shape[-1])
    s_q = seq(zs[:, o_q:o_kv])
    s_kv = [seq(zs[:, o_kv + n * KV_W:o_kv + (n + 1) * KV_W]) for n in range(6)]
    s_gn = seq(zs[:, o_gn:o_gn + LANES])
    pad8 = lambda a: jnp.pad(a, ((0, 0), (0, SUBLANES - n_tok), (0, 0)))
    ys_rec, s_conv, s_h = _rglru_sample(
        s_xrg, state_conv.reshape(bs, -1), state_h, w_conv, b_conv, wa_bd, wx_bd, b_rg_a, b_rg_x, rg_lambda,
        start0=(past == 0))
    nbs_s = past // SEL_BLOCK
    ck1p, ck2p, pe_tok = _compress_weights_paged(w_ck1, w_ck2, cmp_pe)
    cv1p, cv2p, _ = _compress_weights_paged(w_cv1, w_cv2, cmp_pe)
    kc_s = _compress_paged(cache_k_cmp, page_table, pe_tok, ck1p, ck2p)
    vc_s = _compress_paged(cache_v_cmp, page_table, pe_tok, cv1p, cv2p)
    o8 = _nsa_sample(page_table, pad8(s_q), kc_s, vc_s, cache_k_sel, cache_v_sel,
                     [pad8(s_kv[n]) for n in (2, 3, 4, 5)],
                     state_win_k.reshape(bs, WINDOW, KV_W), state_win_v.reshape(bs, WINDOW, KV_W),
                     pad8(s_gn), cst['pgk'], cst['rkg'], cst['egs'], n_tok)
    del nbs_s
    os_attn = o8[:, :n_tok].reshape(m_s, ATTN_W).astype(BF)
    x2s = _merge(xs, ys_rec.reshape(m_s, d_rnn), os_attn, w_grec, w_gatt, w_rec_b, w_attn_b, w_out_b, ln2_g, ln2_b)
    y_sample = _ffn_ln(x2s, up2, down2, ln3_g, ln3_b).reshape(bs, n_tok, d)

    kvs = lambda a: a.reshape(bs, n_tok, KV_HEADS, HEAD_DIM)
    win = lambda old, new: jnp.concatenate([old, kvs(new)], axis=1)[:, -WINDOW:]
    s_states = (kvs(s_kv[0]), kvs(s_kv[1]), kvs(s_kv[2]), kvs(s_kv[3]),
                win(state_win_k, s_kv[4]), win(state_win_v, s_kv[5]),
                s_conv.reshape(bs, CONV_W - 1, d_rnn), s_h)

    return (y_prompt, y_sample) + p_states + s_states
```

```python
import functools

import numpy as np
import jax
import jax.numpy as jnp
from jax import lax
from jax.experimental import pallas as pl
from jax.experimental.pallas import tpu as pltpu

F32 = jnp.float32
BF = jnp.bfloat16

DEPTH = 1
ALPHA = (2.0 * DEPTH) ** 0.25
N_HEADS = 16
HEAD_DIM = 64
KV_HEADS = 4
GROUP = N_HEADS // KV_HEADS
KV_W = KV_HEADS * HEAD_DIM
ATTN_W = N_HEADS * HEAD_DIM
N_NSA_BRANCH = 3
CMP_BLOCK = 32
SEL_BLOCK = 64
N_SEL = 16
WINDOW = 512
Q_BLOCK = 128
CONV_W = 4
LRU_C = 8.0
RNN_BLOCKS = 16
LN_EPS = 1e-5
PAGE_SIZE = 128

LANES = 128
SUBLANES = 8
VMEM_LIMIT = 56 * 1024 * 1024

NEG = -1e30
SEL_TILE = 512
BLOCKS_PER_TILE = SEL_TILE // SEL_BLOCK
SEL_V_ROWS = HEAD_DIM + 16
CMP_ROW = 16
CMP_ROW_W = CMP_ROW * KV_W

_NT = (((1,), (1,)), ((), ()))


def _dot(a, b):
    return jnp.dot(a, b, preferred_element_type=F32)


def _dot_nt(a, b):
    return lax.dot_general(a, b, _NT, preferred_element_type=F32)


def _params(*sem):
    return pltpu.CompilerParams(dimension_semantics=sem, vmem_limit_bytes=VMEM_LIMIT)


def _layer_norm(y, g, b):
    mu = jnp.mean(y, axis=-1, keepdims=True)
    d = y - mu
    var = jnp.mean(d * d, axis=-1, keepdims=True)
    return d * lax.rsqrt(var + LN_EPS) * g + b


def _masked_softmax(s, mask):
    s = jnp.where(mask, s, -jnp.inf)
    m = jnp.max(s, axis=-1, keepdims=True)
    m = jnp.where(m == -jnp.inf, 0.0, m)
    e = jnp.where(mask, jnp.exp(s - m), 0.0)
    den = jnp.sum(e, axis=-1, keepdims=True)
    return e / jnp.maximum(den, 1e-30)


def _split3(x):
    hi = x.astype(BF)
    r1 = x - hi.astype(F32)
    mid = r1.astype(BF)
    lo = (r1 - mid.astype(F32)).astype(BF)
    return hi, mid, lo


def _expand_f32(x, e):
    hi, mid, lo = _split3(x)
    return _dot(hi, e) + _dot(mid, e) + _dot(lo, e)


def _ffn_kernel(x_ref, wg_ref, wu_ref, wd_ref, g_ref, b_ref, o_ref, xb_ref, acc_ref):
    j = pl.program_id(1)

    @pl.when(j == 0)
    def _():
        xb_ref[...] = x_ref[...].astype(BF)
        acc_ref[...] = jnp.zeros_like(acc_ref)

    xb = xb_ref[...]
    gate = _dot(xb, wg_ref[...])
    up = _dot(xb, wu_ref[...])
    h = (gate * jax.nn.sigmoid(gate) * up).astype(BF)
    acc_ref[...] += _dot(h, wd_ref[...])

    @pl.when(j == pl.num_programs(1) - 1)
    def _():
        y = ALPHA * x_ref[...] + 0.5 * acc_ref[...]
        o_ref[...] = _layer_norm(y, g_ref[...], b_ref[...])


def _ffn_ln(x, w_up, w_down, g, b):
    m, d = x.shape
    f = w_down.shape[0]
    tm = min(512, m)
    tn = 512
    nj = f // tn
    return pl.pallas_call(
        _ffn_kernel,
        grid=(m // tm, nj),
        in_specs=[
            pl.BlockSpec((tm, d), lambda i, j: (i, 0)),
            pl.BlockSpec((d, tn), lambda i, j: (0, j)),
            pl.BlockSpec((d, tn), lambda i, j: (0, j + nj)),
            pl.BlockSpec((tn, d), lambda i, j: (j, 0)),
            pl.BlockSpec((1, d), lambda i, j: (0, 0)),
            pl.BlockSpec((1, d), lambda i, j: (0, 0)),
        ],
        out_specs=pl.BlockSpec((tm, d), lambda i, j: (i, 0)),
        out_shape=jax.ShapeDtypeStruct((m, d), F32),
        scratch_shapes=[pltpu.VMEM((tm, d), BF), pltpu.VMEM((tm, d), F32)],
        compiler_params=_params("parallel", "arbitrary"),
        name="ffn_ln",
    )(x, w_up, w_up, w_down, g.reshape(1, d), b.reshape(1, d))


def _mm_kernel(x_ref, w_ref, o_ref):
    o_ref[...] = _dot(x_ref[...].astype(BF), w_ref[...])


def _matmul(x, w, tn):
    m, d = x.shape
    n = w.shape[1]
    tm = min(512, m)
    return pl.pallas_call(
        _mm_kernel,
        grid=(m // tm, n // tn),
        in_specs=[pl.BlockSpec((tm, d), lambda i, j: (i, 0)),
                  pl.BlockSpec((d, tn), lambda i, j: (0, j))],
        out_specs=pl.BlockSpec((tm, tn), lambda i, j: (i, j)),
        out_shape=jax.ShapeDtypeStruct((m, n), F32),
        compiler_params=_params("parallel", "arbitrary"),
        name="proj",
    )(x, w)


def _proj_attn_kernel(x_ref, wq_ref, wkv_ref, wgn_ref, pq_ref, pk_ref, pkt_ref,
                      kc_ref, vc_ref, kct_ref, vct_ref, kst_ref, vst32_ref, kwt_ref, vwt32_ref, gn_ref,
                      qa_ref, ksa_ref, vst_ref, kwa_ref, vwt_ref):
    tm = x_ref.shape[0]
    xb = x_ref[...].astype(BF)
    zq = (_dot(xb, wq_ref[...]) * (HEAD_DIM ** -0.5 * np.log2(np.e))).astype(BF)
    for h in range(N_HEADS):
        pair = zq[:, (h // 2) * LANES:(h // 2 + 1) * LANES]
        qa_ref[h] = _dot(pair, pq_ref[h % 2]).astype(BF)
    zkv = _dot(xb, wkv_ref[...])
    parts = [zkv[:, n * KV_W:(n + 1) * KV_W] for n in range(6)]
    kc_ref[...] = parts[0]
    vc_ref[...] = parts[1]
    for ref, part in zip((kct_ref, vct_ref, kst_ref, vst32_ref, kwt_ref, vwt32_ref), parts):
        ref[...] = part.T
    gn_ref[...] = _dot(xb, wgn_ref[...])
    t = pl.program_id(0) * tm + lax.broadcasted_iota(jnp.int32, (tm, 1), 0)
    lane = lax.broadcasted_iota(jnp.int32, (1, LANES), 1)
    onehot = jnp.where(lane == HEAD_DIM + (t // SEL_BLOCK) % BLOCKS_PER_TILE, 1.0, 0.0)
    ksb = parts[2].astype(BF)
    kwb = parts[4].astype(BF)
    vsb = parts[3].astype(BF)
    vwb = parts[5].astype(BF)
    for k in range(KV_HEADS):
        ksa_ref[k] = (_dot(ksb, pk_ref[k]) + onehot).astype(BF)
        kwa_ref[k] = _dot(kwb, pk_ref[k]).astype(BF)
        ones_row = jnp.where(lax.broadcasted_iota(jnp.int32, (SEL_V_ROWS, 1), 0) == HEAD_DIM, 1.0, 0.0)
        vst_ref[k, 0] = (_dot_nt(pkt_ref[k], vsb) + ones_row).astype(BF)
        vwt_ref[k] = (_dot_nt(pkt_ref[k], vwb) + ones_row).astype(BF)


def _proj_attn(x, wq, wkv, wgn, pq, pk, pkt):
    m, d = x.shape
    tm = 256
    per_tile = SEL_TILE // tm
    row = lambda w: pl.BlockSpec((tm, w), lambda i: (i, 0))
    const = lambda a: pl.BlockSpec(a.shape, lambda i: (0,) * a.ndim)
    head = lambda n: pl.BlockSpec((n, tm, LANES), lambda i: (0, i, 0))
    f32 = lambda w: jax.ShapeDtypeStruct((m, w), F32)
    aug = lambda n: jax.ShapeDtypeStruct((n, m, LANES), BF)
    return pl.pallas_call(
        _proj_attn_kernel,
        grid=(m // tm,),
        in_specs=[row(d), const(wq), const(wkv), const(wgn), const(pq), const(pk), const(pkt)],
        out_specs=[row(KV_W)] * 2 + [pl.BlockSpec((KV_W, tm), lambda i: (0, i))] * 6
                  + [row(LANES), head(N_HEADS), head(KV_HEADS),
                                     pl.BlockSpec((KV_HEADS, 1, SEL_V_ROWS, tm),
                                                  lambda i: (0, i // per_tile, 0, i % per_tile)),
                                     head(KV_HEADS),
                                     pl.BlockSpec((KV_HEADS, SEL_V_ROWS, tm), lambda i: (0, 0, i))],
        out_shape=[f32(KV_W)] * 2 + [jax.ShapeDtypeStruct((KV_W, m), F32)] * 6
                  + [f32(LANES), aug(N_HEADS), aug(KV_HEADS),
                                     jax.ShapeDtypeStruct((KV_HEADS, m // SEL_TILE, SEL_V_ROWS, SEL_TILE), BF),
                                     aug(KV_HEADS),
                                     jax.ShapeDtypeStruct((KV_HEADS, SEL_V_ROWS, m), BF)],
        compiler_params=_params("parallel"),
        name="proj_attn",
    )(x, wq, wkv, wgn, pq, pk, pkt)


def _rglru_gates(xc, wa, wx, ba, bx, lam, is_start):
    xcb = xc.astype(BF)
    r = jax.nn.sigmoid(_dot(xcb, wa) + ba)
    ig = jax.nn.sigmoid(_dot(xcb, wx) + bx)
    log_a = -LRU_C * r * jax.nn.softplus(-lam)
    a = jnp.exp(log_a)
    if is_start is True:
        return a, ig * xc
    th = jnp.tanh(log_a)
    mult = jnp.sqrt(-2.0 * th / (1.0 - th))
    if is_start is not None:
        mult = jnp.where(is_start, 1.0, mult)
    return a, mult * (ig * xc)


def _rglru_prompt_kernel(x_ref, gate_ref, wc_ref, bc_ref, wa_ref, wx_ref, ba_ref, bx_ref, lam_ref,
                         y_ref, tail_ref, hl_ref, h_scr, tail_scr):
    c = pl.program_id(1)
    tc = x_ref.shape[0]

    @pl.when(c == 0)
    def _():
        h_scr[...] = jnp.zeros_like(h_scr)
        tail_scr[...] = jnp.zeros_like(tail_scr)

    x = x_ref[...]
    prev = tail_scr[...]
    row8 = lax.broadcasted_iota(jnp.int32, (SUBLANES, 1), 0)
    row = lax.broadcasted_iota(jnp.int32, (tc, 1), 0)

    def shifted(s):
        rolled = pltpu.roll(x, s, 0)
        top = jnp.where(row8 < s, pltpu.roll(prev, s, 0), rolled[:SUBLANES])
        return jnp.concatenate([top, rolled[SUBLANES:]], axis=0)

    wc = wc_ref[...]
    conv = wc[0:1] * shifted(3)
    conv = conv + wc[1:2] * shifted(2)
    conv = conv + wc[2:3] * shifted(1)
    conv = conv + wc[3:4] * x
    xc = bc_ref[...] + conv

    is_start = (row + c * tc) == 0
    a, u = _rglru_gates(xc, wa_ref[0], wx_ref[0], ba_ref[...], bx_ref[...], lam_ref[...], is_start)

    d = 1
    while d < tc:
        keep = row >= d
        a_sh = jnp.where(keep, pltpu.roll(a, d, 0), 1.0)
        u_sh = jnp.where(keep, pltpu.roll(u, d, 0), 0.0)
        u = a * u_sh + u
        a = a * a_sh
        d *= 2
    h = a * h_scr[...] + u

    y_ref[...] = (jax.nn.gelu(gate_ref[...]) * h).astype(BF)
    h_last = h[tc - 1:tc]
    h_scr[...] = h_last
    tail_scr[...] = x[tc - SUBLANES:]
    hl_ref[...] = h_last
    tail_ref[...] = x[tc - SUBLANES:]


def _rglru_prompt(xrg, w_conv, b_conv, wa_bd, wx_bd, b_a, b_x, lam):
    t = xrg.shape[0]
    d_rnn = w_conv.shape[1]
    gw = 2 * LANES
    ng = d_rnn // gw
    tc = min(512, t)
    vec = lambda: pl.BlockSpec((1, gw), lambda g, c: (0, g))
    return pl.pallas_call(
        _rglru_prompt_kernel,
        grid=(ng, t // tc),
        in_specs=[
            pl.BlockSpec((tc, gw), lambda g, c: (c, g)),
            pl.BlockSpec((tc, gw), lambda g, c: (c, g + ng)),
            pl.BlockSpec((CONV_W, gw), lambda g, c: (0, g)),
            vec(),
            pl.BlockSpec((1, gw, gw), lambda g, c: (g, 0, 0)),
            pl.BlockSpec((1, gw, gw), lambda g, c: (g, 0, 0)),
            vec(), vec(), vec(),
        ],
        out_specs=[
            pl.BlockSpec((tc, gw), lambda g, c: (c, g)),
            pl.BlockSpec((SUBLANES, gw), lambda g, c: (0, g)),
            pl.BlockSpec((1, gw), lambda g, c: (0, g)),
        ],
        out_shape=[
            jax.ShapeDtypeStruct((t, d_rnn), BF),
            jax.ShapeDtypeStruct((SUBLANES, d_rnn), F32),
            jax.ShapeDtypeStruct((1, d_rnn), F32),
        ],
        scratch_shapes=[pltpu.VMEM((1, gw), F32), pltpu.VMEM((SUBLANES, gw), F32)],
        compiler_params=_params("parallel", "arbitrary"),
        name="rglru_prompt",
    )(xrg, xrg, w_conv, b_conv.reshape(1, -1), wa_bd, wx_bd,
      b_a.reshape(1, -1), b_x.reshape(1, -1), lam.reshape(1, -1))


def _rglru_sample_kernel(xrg_ref, cp_ref, h0_ref, wc_ref, bc_ref, wa_ref, wx_ref, ba_ref, bx_ref, lam_ref,
                         y_ref, cn_ref, hl_ref, *, n_tok, start0):
    d_rnn = h0_ref.shape[1]
    gw = wa_ref.shape[1]
    wc = wc_ref[...]
    xp = [cp_ref[:, k * d_rnn:(k + 1) * d_rnn] for k in range(CONV_W - 1)]
    xp += [xrg_ref[:, t * 2 * d_rnn:t * 2 * d_rnn + d_rnn] for t in range(n_tok)]
    h = h0_ref[...]
    for t in range(n_tok):
        conv = wc[0:1] * xp[t]
        for k in range(1, CONV_W):
            conv = conv + wc[k:k + 1] * xp[t + k]
        xc = bc_ref[...] + conv
        a_parts, u_parts = [], []
        for g in range(d_rnn // gw):
            sl = slice(g * gw, (g + 1) * gw)
            a_g, u_g = _rglru_gates(xc[:, sl], wa_ref[g], wx_ref[g], ba_ref[:, sl], bx_ref[:, sl],
                                    lam_ref[:, sl], True if (start0 and t == 0) else None)
            a_parts.append(a_g)
            u_parts.append(u_g)
        a = jnp.concatenate(a_parts, axis=1)
        u = jnp.concatenate(u_parts, axis=1)
        h = a * h + u
        gate = xrg_ref[:, t * 2 * d_rnn + d_rnn:(t + 1) * 2 * d_rnn]
        y_ref[:, t * d_rnn:(t + 1) * d_rnn] = (jax.nn.gelu(gate) * h).astype(BF)
    hl_ref[...] = h
    tail = xp[-(CONV_W - 1):]
    for k in range(CONV_W - 1):
        cn_ref[:, k * d_rnn:(k + 1) * d_rnn] = tail[k]


def _rglru_sample(xrg, conv_prev, h0, w_conv, b_conv, wa_bd, wx_bd, b_a, b_x, lam, start0):
    b, d_rnn = h0.shape
    n_tok = xrg.shape[1] // (2 * d_rnn)
    args = (xrg, conv_prev, h0, w_conv, b_conv.reshape(1, -1), wa_bd, wx_bd,
            b_a.reshape(1, -1), b_x.reshape(1, -1), lam.reshape(1, -1))
    full = lambda a: pl.BlockSpec(a.shape, lambda i: (0,) * a.ndim)
    outs = [jax.ShapeDtypeStruct((b, n_tok * d_rnn), BF),
            jax.ShapeDtypeStruct((b, (CONV_W - 1) * d_rnn), F32),
            jax.ShapeDtypeStruct((b, d_rnn), F32)]
    return pl.pallas_call(
        functools.partial(_rglru_sample_kernel, n_tok=n_tok, start0=start0),
        grid=(1,),
        in_specs=[full(a) for a in args],
        out_specs=[full(o) for o in outs],
        out_shape=outs,
        compiler_params=_params("arbitrary"),
        name="rglru_sample",
    )(*args)


def _compress_kernel(x_ref, pe_ref, w1_ref, w2_ref, pk_ref, o_ref, scr):
    x = x_ref[...]
    rows = x.shape[0]
    xb = (x.reshape(rows // SUBLANES, SUBLANES, CMP_ROW_W) + pe_ref[...][None]).reshape(rows, CMP_ROW_W)
    full = _dot(xb.astype(BF), w1_ref[...])
    hid = full[:, :KV_W] + pltpu.roll(full[:, KV_W:], rows - 1, 0)
    out = _dot(jax.nn.gelu(hid).astype(BF), w2_ref[...])
    nb = rows // 4
    ob = out.astype(BF)
    for k in range(KV_HEADS):
        scr[...] = _dot(ob, pk_ref[k])
        o_ref[k, 0] = scr[pl.ds(0, nb, stride=4), :].astype(BF)
        o_ref[k, 1] = scr[pl.ds(2, nb, stride=4), :].astype(BF)


def _compress_prompt(kv, pe2, w1, w2, pk):
    t = kv.shape[0]
    rows = 128
    x = kv.reshape(t // CMP_ROW, CMP_ROW_W)
    nb = rows // 4
    nbs = t // SEL_BLOCK
    const = lambda a: pl.BlockSpec(a.shape, lambda i: (0,) * a.ndim)
    return pl.pallas_call(
        _compress_kernel,
        grid=(x.shape[0] // rows,),
        in_specs=[pl.BlockSpec((rows, CMP_ROW_W), lambda i: (i, 0)),
                  const(pe2), const(w1), const(w2), const(pk)],
        out_specs=pl.BlockSpec((KV_HEADS, 2, nb, LANES), lambda i: (0, 0, i, 0)),
        out_shape=jax.ShapeDtypeStruct((KV_HEADS, 2, nbs, LANES), BF),
        scratch_shapes=[pltpu.VMEM((rows, LANES), F32)],
        compiler_params=_params("parallel"),
        name="compress_prompt",
    )(x, pe2, w1, w2, pk)


def _compress_paged_kernel(pt_ref, *refs, pages):
    del pt_ref
    pe_ref = refs[2 * pages]
    wk1_ref, wk2_ref, wv1_ref, wv2_ref, ok_ref, ov_ref, xk_scr, xv_scr, outk_scr, outv_scr = refs[2 * pages + 1:]
    _compress_pages(refs[:pages], pe_ref, wk1_ref, wk2_ref, ok_ref, xk_scr, outk_scr)
    _compress_pages(refs[pages:2 * pages], pe_ref, wv1_ref, wv2_ref, ov_ref, xv_scr, outv_scr)


def _compress_pages(x_refs, pe_ref, w1_ref, w2_ref, o_ref, xs_scr, out_scr):
    pages = len(x_refs)
    halves = KV_W // LANES
    for p in range(pages):
        xt = x_refs[p][0].T + pe_ref[...]
        for h in range(halves):
            xs_scr[h, p * PAGE_SIZE:(p + 1) * PAGE_SIZE, :] = xt[:, h * LANES:(h + 1) * LANES]
    nblk = pages * (PAGE_SIZE // CMP_BLOCK)
    by_tok = [pltpu.einshape("mld->lmd", xs_scr[h].reshape(nblk, CMP_BLOCK, LANES)) for h in range(halves)]
    acc = None
    for l2 in range(CMP_BLOCK // 2):
        parts = [jnp.concatenate([by_tok[h][2 * l2], by_tok[h][2 * l2 + 1]], axis=1) for h in range(halves)]
        d = _dot(jnp.concatenate(parts, axis=0).astype(BF), w1_ref[l2])
        acc = d if acc is None else acc + d
    out_scr[...] = _dot(jax.nn.gelu(acc).astype(BF), w2_ref[...])
    for h in range(halves):
        for par in range(2):
            o_ref[0, par, :, h * LANES:(h + 1) * LANES] = (
                out_scr[pl.ds(h * nblk + par, nblk // 2, stride=2), :].astype(BF))


def _compress_paged(cache_k, cache_v, page_table, pe_tok, wk1, wk2, wv1, wv2):
    b, n_pages = page_table.shape
    pages = 16
    view = lambda c: jnp.transpose(c, (0, 2, 3, 1)).reshape(c.shape[0], KV_W, PAGE_SIZE)
    xk, xv = view(cache_k), view(cache_v)
    nblk = pages * (PAGE_SIZE // CMP_BLOCK)
    halves = KV_W // LANES
    nbs = n_pages * PAGE_SIZE // SEL_BLOCK
    const = lambda a: pl.BlockSpec(a.shape, lambda i, s, pt: (0,) * a.ndim)
    page_specs = [
        pl.BlockSpec((1, KV_W, PAGE_SIZE),
                     functools.partial(lambda i, s, pt, p: (pt[i, s * pages + p], 0, 0), p=p))
        for p in range(pages)]
    return pl.pallas_call(
        functools.partial(_compress_paged_kernel, pages=pages),
        grid_spec=pltpu.PrefetchScalarGridSpec(
            num_scalar_prefetch=1,
            grid=(b, n_pages // pages),
            in_specs=page_specs + page_specs + [const(a) for a in (pe_tok, wk1, wk2, wv1, wv2)],
            out_specs=[pl.BlockSpec((1, 2, nblk // 2, KV_W), lambda i, s, pt: (i, 0, s, 0))] * 2,
            scratch_shapes=[pltpu.VMEM((halves, pages * PAGE_SIZE, LANES), F32)] * 2
                           + [pltpu.VMEM((halves * nblk, LANES), F32)] * 2,
        ),
        out_shape=[jax.ShapeDtypeStruct((b, 2, nbs, KV_W), BF)] * 2,
        compiler_params=_params("parallel", "arbitrary"),
        name="compress_paged",
    )(page_table, *([xk] * pages), *([xv] * pages), pe_tok, wk1, wk2, wv1, wv2)


def _topk_columns(imp_t, n_pick, quota=None):
    nblk = imp_t.shape[0]
    idx = lax.broadcasted_iota(jnp.int32, imp_t.shape, 0)
    sel = jnp.zeros(imp_t.shape, F32)
    for n in range(n_pick):
        mx = jnp.max(imp_t, axis=0, keepdims=True)
        first = jnp.min(jnp.where(imp_t == mx, idx, nblk), axis=0, keepdims=True)
        if quota is not None:
            first = jnp.where(n < quota, first, nblk)
        hit = idx == first
        sel = jnp.where(hit, 1.0, sel)
        imp_t = jnp.where(hit, -jnp.inf, imp_t)
    return sel


def _softmax_cols(s):
    m = jnp.max(s, axis=0, keepdims=True)
    m = jnp.where(m == -jnp.inf, 0.0, m)
    e = jnp.exp2(s - m)
    den = jnp.sum(e, axis=0, keepdims=True)
    return e * (1.0 / jnp.maximum(den, 1e-30))


def _nsa_prompt_kernel(*refs):
    (q_ref, kc_ref, vc_ref, ks_ref, vst_ref) = refs[:5]
    kw_refs = refs[5:10]
    vwt_refs = refs[10:15]
    (gn_ref, pa_ref, o_ref,
     bias_scr, gate_scr, sa_scr, sb_scr, m_scr, acc_scr, oc_scr, ow_scr) = refs[15:]
    kvh = pl.program_id(0)
    i = pl.program_id(1)
    nq = Q_BLOCK
    r = GROUP * nq
    hd = HEAD_DIM
    q = q_ref[...].reshape(r, LANES)
    lane = lax.broadcasted_iota(jnp.int32, (1, r), 1)
    qpos = i * nq + lane % nq

    nbs = kc_ref.shape[1] // 2

    def window_and_gates():
        kw = jnp.concatenate([ref[0] for ref in kw_refs], axis=0)
        vwt = jnp.concatenate([ref[0] for ref in vwt_refs], axis=1)
        s_w = _dot_nt(kw, q)
        nwb = len(kw_refs)
        krow = lax.broadcasted_iota(jnp.int32, (nq, 1), 0)
        parts = []
        for b in range(nwb):
            blk_pos = (i - (nwb - 1) + b) * nq
            mask = blk_pos >= 0
            if b == 0:
                mask = mask & (qpos - (blk_pos + krow) <= WINDOW)
            if b == nwb - 1:
                mask = mask & (blk_pos + krow <= qpos)
            parts.append(jnp.where(mask, s_w[b * nq:(b + 1) * nq], -jnp.inf))
        s_w = jnp.concatenate(parts, axis=0)
        e_w = jnp.exp2(s_w - jnp.max(s_w, axis=0, keepdims=True))
        acc_w = _dot(vwt, e_w.astype(BF))
        ow_scr[...] = acc_w[:hd] * (1.0 / jnp.maximum(acc_w[hd:hd + 1], 1e-30))
        gate_scr[...] = jax.nn.sigmoid(gn_ref[...]).T

    def compressed_and_select(nv, first_quarter):
        window_and_gates()
        kc = jnp.concatenate([kc_ref[0, 0:nv], kc_ref[0, nbs:nbs + nv]], axis=0)
        vc = jnp.concatenate([vc_ref[0, 0:nv], vc_ref[0, nbs:nbs + nv]], axis=0)
        rowc = lax.broadcasted_iota(jnp.int32, (2 * nv, 1), 0)
        blk = 2 * (rowc % nv) + rowc // nv
        p_c = _softmax_cols(jnp.where(blk * CMP_BLOCK + (CMP_BLOCK - 1) <= qpos, _dot_nt(kc, q), -jnp.inf))
        vct = vc.astype(F32).T[:hd].astype(BF)
        oc_scr[...] = _dot(vct, p_c.astype(BF))
        imp = p_c[:, 0:nq]
        for g in range(1, GROUP):
            imp = imp + p_c[:, g * nq:(g + 1) * nq]
        imp = imp[:nv] + imp[nv:]
        qp = i * nq + lax.broadcasted_iota(jnp.int32, (1, nq), 1)
        jb = lax.broadcasted_iota(jnp.int32, (nv, 1), 0)
        cur = qp // SEL_BLOCK
        valid = jb * SEL_BLOCK <= qp
        forced = (jb == 0) | (jb == cur) | (jb == cur - 1)
        n_forced = 1 + jnp.where(cur >= 1, 1, 0) + jnp.where(cur >= 2, 1, 0)
        n_sel = min(N_SEL, nbs)
        n_iter = n_sel - (1 if first_quarter else 3)
        picks = _topk_columns(jnp.where(valid & ~forced, imp, -jnp.inf), n_iter, n_sel - n_forced)
        sel = jnp.where(forced, 1.0, picks)
        bias = jnp.where(valid & (sel > 0.5), 0.0, NEG).T.astype(BF)
        nt = nv // BLOCKS_PER_TILE
        bias_all = _dot(bias, pa_ref[0:nv, 0:nt * LANES]).astype(BF)
        for t in range(nt):
            bias_scr[t] = bias_all[:, t * LANES:(t + 1) * LANES]

    n_var = 4
    variant = (i * n_var) // pl.num_programs(1)
    for v in range(n_var):
        pl.when(variant == v)(functools.partial(compressed_and_select, nbs * (v + 1) // n_var, v == 0))
    o_c = oc_scr[...]

    def scores(t):
        k0 = pl.multiple_of(t * SEL_TILE, SEL_TILE)
        kt = ks_ref[0, pl.ds(k0, SEL_TILE), :]
        qa = q + jnp.concatenate([bias_scr[t]] * GROUP, axis=0)
        return _dot_nt(kt, qa)

    def update(t, s_ref, diagonal):
        vt = vst_ref[0, t]
        for g in range(GROUP):
            sl = slice(g * nq, (g + 1) * nq)
            sg = s_ref[:, sl]
            if diagonal:
                kpos = t * SEL_TILE + lax.broadcasted_iota(jnp.int32, (SEL_TILE, 1), 0)
                sg = jnp.where(kpos <= qpos[:, sl], sg, NEG)
            m_old = m_scr[:, sl]
            m_new = jnp.maximum(m_old, jnp.max(sg, axis=0, keepdims=True))
            p = jnp.exp2(sg - m_new)
            acc_scr[:, sl] = jnp.exp2(m_old - m_new) * acc_scr[:, sl] + _dot(vt, p.astype(BF))
            m_scr[:, sl] = m_new

    def pair(t0):
        sb_scr[...] = scores(t0 + 1)
        update(t0, sa_scr, False)
        sa_scr[...] = scores(t0 + 2)
        update(t0 + 1, sb_scr, False)

    def quad(u, _):
        pair(4 * u)
        pair(4 * u + 2)
        return 0

    t_diag = (i * nq) // SEL_TILE
    m_scr[...] = jnp.full(m_scr.shape, NEG, F32)
    acc_scr[...] = jnp.zeros_like(acc_scr)
    sa_scr[...] = scores(0)
    lax.fori_loop(0, t_diag // 4, quad, 0)

    @pl.when(t_diag % 4 >= 2)
    def _():
        pair(t_diag - t_diag % 4)

    @pl.when(t_diag % 2 == 1)
    def _():
        sb_scr[...] = scores(t_diag)
        update(t_diag - 1, sa_scr, False)
        update(t_diag, sb_scr, True)

    @pl.when(t_diag % 2 == 0)
    def _():
        update(t_diag, sa_scr, True)

    o_s = acc_scr[:hd] * (1.0 / acc_scr[hd:hd + 1])

    o_w = ow_scr[...]

    def gate(br):
        return jnp.concatenate(
            [gate_scr[pl.ds(br * N_HEADS + kvh * GROUP + g, 1), :] for g in range(GROUP)], axis=1)

    comb = (gate(0) * o_c + gate(1) * o_s + gate(2) * o_w).astype(BF)
    o_ref[...] = jnp.concatenate([comb[:, g * nq:(g + 1) * nq] for g in range(GROUP)], axis=0)


def _nsa_prompt(qa, kc, vc, ksa, vst, kwa, vwt, gn, pall):
    t = qa.shape[1]
    n_tiles = t // SEL_TILE
    wb = WINDOW // Q_BLOCK + 1
    kv_full = lambda a: pl.BlockSpec((1,) + a.shape[1:], lambda k, i: (k,) + (0,) * (a.ndim - 1))
    wblk = lambda i, m: jnp.maximum(i - (wb - 1) + m, 0)
    win_k = [pl.BlockSpec((1, Q_BLOCK, LANES), functools.partial(lambda k, i, m: (k, wblk(i, m), 0), m=m))
             for m in range(wb)]
    win_v = [pl.BlockSpec((1, SEL_V_ROWS, Q_BLOCK), functools.partial(lambda k, i, m: (k, 0, wblk(i, m)), m=m))
             for m in range(wb)]
    const = lambda a: pl.BlockSpec(a.shape, lambda k, i: (0,) * a.ndim)
    return pl.pallas_call(
        _nsa_prompt_kernel,
        grid=(KV_HEADS, t // Q_BLOCK),
        in_specs=[pl.BlockSpec((GROUP, Q_BLOCK, LANES), lambda k, i: (k, i, 0)),
                  kv_full(kc), kv_full(vc), kv_full(ksa), kv_full(vst)]
                 + win_k + win_v
                 + [pl.BlockSpec((Q_BLOCK, LANES), lambda k, i: (i, 0)), const(pall)],
        out_specs=pl.BlockSpec((KV_W, Q_BLOCK), lambda k, i: (k, i)),
        out_shape=jax.ShapeDtypeStruct((ATTN_W, t), BF),
        scratch_shapes=[pltpu.VMEM((n_tiles, Q_BLOCK, LANES), BF), pltpu.VMEM((LANES, Q_BLOCK), F32),
                        pltpu.VMEM((SEL_TILE, GROUP * Q_BLOCK), F32), pltpu.VMEM((SEL_TILE, GROUP * Q_BLOCK), F32),
                        pltpu.VMEM((1, GROUP * Q_BLOCK), F32),
                        pltpu.VMEM((SEL_V_ROWS, GROUP * Q_BLOCK), F32),
                        pltpu.VMEM((HEAD_DIM, GROUP * Q_BLOCK), F32), pltpu.VMEM((HEAD_DIM, GROUP * Q_BLOCK), F32)],
        compiler_params=_params("parallel", "arbitrary"),
        name="nsa_prompt",
    )(qa, kc, vc, ksa, vst, *([kwa] * wb), *([vwt] * wb), gn, pall)


def _nsa_sample_kernel(*refs, pages, past, n_tok):
    pt_ref = refs[0]
    q_ref, kc_ref, vc_ref = refs[1:4]
    ks_refs = refs[4:4 + pages]
    vs_refs = refs[4 + pages:4 + 2 * pages]
    (ksn_ref, vsn_ref, kwn_ref, vwn_ref, wk_ref, wv_ref, gn_ref,
     pg_ref, rp_ref, eg_ref, o_ref, q_scr, bias_scr, m_scr, l_scr, acc_scr, oc_scr) = refs[4 + 2 * pages:]
    del pt_ref
    s_id = pl.program_id(1)
    tp = SUBLANES
    r = N_HEADS * tp
    row = lax.broadcasted_iota(jnp.int32, (r, 1), 0)
    qpos = past + row % tp
    nbs = bias_scr.shape[1]

    @pl.when(s_id == 0)
    def _():
        qb = q_ref[0].astype(BF)
        for h in range(N_HEADS):
            k = h // GROUP
            q_scr[h * tp:(h + 1) * tp, :] = _dot(qb[:, k * KV_W:(k + 1) * KV_W], pg_ref[h]).astype(BF)
        q = q_scr[...]
        kc = kc_ref[0].reshape(2 * nbs, KV_W)
        vc = vc_ref[0].reshape(2 * nbs, KV_W)
        col = lax.broadcasted_iota(jnp.int32, (1, 2 * nbs), 1)
        blk = 2 * (col % nbs) + col // nbs
        p_c = _masked_softmax(_dot_nt(q, kc), blk * CMP_BLOCK + (CMP_BLOCK - 1) <= qpos)
        oc_scr[...] = _dot(p_c.astype(BF), vc)
        imps = []
        for k in range(KV_HEADS):
            base = k * GROUP * tp
            imp = p_c[base:base + tp]
            for g in range(1, GROUP):
                imp = imp + p_c[base + g * tp:base + (g + 1) * tp]
            imps.append(imp)
        imp = jnp.concatenate(imps, axis=0)
        imp = imp[:, :nbs] + imp[:, nbs:]
        jb = lax.broadcasted_iota(jnp.int32, (1, nbs), 1)
        imp = jnp.where((jb == 0) | (jb == nbs - 1), jnp.inf, imp)
        idx = lax.broadcasted_iota(jnp.int32, imp.shape, 1)
        sel = jnp.zeros(imp.shape, F32)
        for _ in range(min(N_SEL, nbs + 1) - 1):
            mx = jnp.max(imp, axis=1, keepdims=True)
            first = jnp.min(jnp.where(imp == mx, idx, nbs), axis=1, keepdims=True)
            hit = idx == first
            sel = jnp.where(hit, 1.0, sel)
            imp = jnp.where(hit, -jnp.inf, imp)
        bias = jnp.where(sel > 0.5, 0.0, NEG).astype(BF)
        for k in range(KV_HEADS):
            for g in range(GROUP):
                h = k * GROUP + g
                bias_scr[h * tp:(h + 1) * tp, :] = bias[k * tp:(k + 1) * tp]
        m_scr[...] = jnp.full(m_scr.shape, NEG, F32)
        l_scr[...] = jnp.zeros_like(l_scr)
        acc_scr[...] = jnp.zeros_like(acc_scr)

    def online(st, v, v_transposed):
        m = m_scr[...]
        m_new = jnp.maximum(m, jnp.max(st, axis=1, keepdims=True))
        scale = jnp.exp(m - m_new)
        p = jnp.exp(st - m_new)
        l_scr[...] = scale * l_scr[...] + jnp.sum(p, axis=1, keepdims=True)
        pv = _dot_nt(p.astype(BF), v) if v_transposed else _dot(p.astype(BF), v)
        acc_scr[...] = scale * acc_scr[...] + pv
        m_scr[...] = m_new

    q = q_scr[...]
    kt = jnp.concatenate([ref[0].astype(BF) for ref in ks_refs], axis=1)
    vt = jnp.concatenate([ref[0].astype(BF) for ref in vs_refs], axis=1)
    n_keys = kt.shape[1]
    jb = lax.broadcasted_iota(jnp.int32, (nbs, 1), 0)
    key_blk = s_id * (n_keys // SEL_BLOCK) + lax.broadcasted_iota(jnp.int32, (1, n_keys), 1) // SEL_BLOCK
    expand = jnp.where(jb == key_blk, 1.0, 0.0).astype(BF)
    online(_dot(q, kt) + _dot(bias_scr[...], expand), vt, True)

    @pl.when(s_id == pl.num_programs(1) - 1)
    def _():
        zeros = jnp.zeros((LANES - tp, KV_W), BF)
        tok = lax.broadcasted_iota(jnp.int32, (1, LANES), 1)
        ksn = jnp.concatenate([ksn_ref[0].astype(BF), zeros], axis=0)
        vsn = jnp.concatenate([vsn_ref[0].astype(BF), zeros], axis=0)
        online(jnp.where((past + tok <= qpos) & (tok < n_tok), _dot_nt(q, ksn), NEG), vsn, False)
        o_s = acc_scr[...] / l_scr[...]
        kw = jnp.concatenate([wk_ref[0].astype(BF), kwn_ref[0].astype(BF), zeros], axis=0)
        vw = jnp.concatenate([wv_ref[0].astype(BF), vwn_ref[0].astype(BF), zeros], axis=0)
        n_w = kw.shape[0]
        wcol = lax.broadcasted_iota(jnp.int32, (1, n_w), 1)
        kwpos = past - WINDOW + wcol
        dpos = qpos - kwpos
        mask_w = (dpos >= 0) & (dpos <= WINDOW) & (kwpos >= 0) & (wcol < WINDOW + n_tok)
        p_w = _masked_softmax(_dot_nt(q, kw), mask_w)
        o_w = _dot(p_w.astype(BF), vw)
        sg = jax.nn.sigmoid(gn_ref[0])
        gexp = _expand_f32(sg, eg_ref[...])

        def gate(br):
            return jnp.concatenate(
                [gexp[:, (br * N_HEADS + h) * KV_W:(br * N_HEADS + h + 1) * KV_W] for h in range(N_HEADS)],
                axis=0)

        comb = (gate(0) * oc_scr[...] + gate(1) * o_s + gate(2) * o_w).astype(BF)
        for k in range(KV_HEADS):
            out = None
            for g in range(GROUP):
                h = k * GROUP + g
                part = _dot(comb[h * tp:(h + 1) * tp], rp_ref[h])
                out = part if out is None else out + part
            o_ref[0, :, k * KV_W:(k + 1) * KV_W] = out


def _nsa_sample(page_table, q8, kc, vc, cache_ks, cache_vs, new8, win_k, win_v, gn8, pgk, rkg, egate, n_tok):
    b, n_pages = page_table.shape
    past = n_pages * PAGE_SIZE
    pages = 8
    tp = SUBLANES
    r = N_HEADS * tp
    nbs = past // SEL_BLOCK
    ck = jnp.transpose(cache_ks, (0, 2, 3, 1)).reshape(cache_ks.shape[0], KV_W, PAGE_SIZE)
    cv = jnp.transpose(cache_vs, (0, 2, 3, 1)).reshape(cache_vs.shape[0], KV_W, PAGE_SIZE)
    per_seq = lambda a: pl.BlockSpec((1,) + a.shape[1:], lambda i, s, pt: (i,) + (0,) * (a.ndim - 1))
    const = lambda a: pl.BlockSpec(a.shape, lambda i, s, pt: (0,) * a.ndim)
    page_specs = [
        pl.BlockSpec((1, KV_W, PAGE_SIZE),
                     functools.partial(lambda i, s, pt, p: (pt[i, s * pages + p], 0, 0), p=p))
        for p in range(pages)]
    return pl.pallas_call(
        functools.partial(_nsa_sample_kernel, pages=pages, past=past, n_tok=n_tok),
        grid_spec=pltpu.PrefetchScalarGridSpec(
            num_scalar_prefetch=1,
            grid=(b, n_pages // pages),
            in_specs=[per_seq(q8), per_seq(kc), per_seq(vc)] + page_specs + page_specs
                     + [per_seq(a) for a in new8] + [per_seq(win_k), per_seq(win_v), per_seq(gn8),
                                                     const(pgk), const(rkg), const(egate)],
            out_specs=pl.BlockSpec((1, tp, ATTN_W), lambda i, s, pt: (i, 0, 0)),
            scratch_shapes=[pltpu.VMEM((r, KV_W), BF), pltpu.VMEM((r, nbs), BF),
                            pltpu.VMEM((r, 1), F32), pltpu.VMEM((r, 1), F32),
                            pltpu.VMEM((r, KV_W), F32), pltpu.VMEM((r, KV_W), F32)],
        ),
        out_shape=jax.ShapeDtypeStruct((b, tp, ATTN_W), F32),
        compiler_params=_params("parallel", "arbitrary"),
        name="nsa_sample",
    )(page_table, q8, kc, vc, *([ck] * pages), *([cv] * pages), *new8, win_k, win_v, gn8, pgk, rkg, egate)


def _merge_kernel(x_ref, y_ref, o_ref, wgr_ref, wga_ref, wr_ref, wa_ref, wo_ref, g_ref, b_ref,
                  out_ref, xb_ref, acc_ref, *ob_ref):
    j = pl.program_id(1)

    @pl.when(j == 0)
    def _():
        xb_ref[...] = x_ref[...].astype(BF)
        acc_ref[...] = jnp.zeros_like(acc_ref)
        if ob_ref:
            ob_ref[0][...] = o_ref[...].astype(F32).T.astype(BF)

    xb = xb_ref[...]
    rec = _dot(y_ref[...], wr_ref[...])
    att = _dot(ob_ref[0][...] if ob_ref else o_ref[...], wa_ref[...])
    u = jax.nn.sigmoid(_dot(xb, wgr_ref[...])) * rec + jax.nn.sigmoid(_dot(xb, wga_ref[...])) * att
    acc_ref[...] += _dot(u.astype(BF), wo_ref[...])

    @pl.when(j == pl.num_programs(1) - 1)
    def _():
        out_ref[...] = _layer_norm(ALPHA * x_ref[...] + acc_ref[...], g_ref[...], b_ref[...])


def _merge(x, y_rec, o_attn, w_grec, w_gatt, w_rec_o, w_attn_o, w_out, g, b, o_transposed):
    m, d = x.shape
    tm = min(512, m)
    tn = 512
    col = lambda a: pl.BlockSpec((a.shape[0], tn), lambda i, j: (0, j))
    rowb = lambda a: pl.BlockSpec((tm, a.shape[1]), lambda i, j: (i, 0))
    vec = pl.BlockSpec((1, d), lambda i, j: (0, 0))
    if o_transposed:
        o_spec = pl.BlockSpec((o_attn.shape[0], tm), lambda i, j: (0, i))
        o_scratch = [pltpu.VMEM((tm, o_attn.shape[0]), BF)]
    else:
        o_spec, o_scratch = rowb(o_attn), []
    return pl.pallas_call(
        _merge_kernel,
        grid=(m // tm, d // tn),
        in_specs=[rowb(x), rowb(y_rec), o_spec, col(w_grec), col(w_gatt), col(w_rec_o), col(w_attn_o),
                  pl.BlockSpec((tn, d), lambda i, j: (j, 0)), vec, vec],
        out_specs=pl.BlockSpec((tm, d), lambda i, j: (i, 0)),
        out_shape=jax.ShapeDtypeStruct((m, d), F32),
        scratch_shapes=[pltpu.VMEM((tm, d), BF), pltpu.VMEM((tm, d), F32)] + o_scratch,
        compiler_params=_params("parallel", "arbitrary"),
        name="merge_out",
    )(x, y_rec, o_attn, w_grec, w_gatt, w_rec_o, w_attn_o, w_out, g.reshape(1, d), b.reshape(1, d))


def _place(n_rows, n_cols, src0, dst0, width, value=1.0):
    m = np.zeros((n_rows, n_cols), np.float32)
    m[src0 + np.arange(width), dst0 + np.arange(width)] = value
    return m


def _layout_constants(n_sel_blocks_prompt):
    hd = HEAD_DIM
    scale = hd ** -0.5
    pq = np.stack([_place(LANES, LANES, e * hd, 0, hd) for e in range(2)])
    pk = np.stack([_place(KV_W, LANES, k * hd, 0, hd) for k in range(KV_HEADS)])
    pkt = np.transpose(pk, (0, 2, 1))[:, :SEL_V_ROWS]
    nbs = n_sel_blocks_prompt
    n_tiles = -(-nbs // BLOCKS_PER_TILE)
    pall = np.zeros((nbs, n_tiles * LANES), np.float32)
    j = np.arange(nbs)
    pall[j, (j // BLOCKS_PER_TILE) * LANES + hd + j % BLOCKS_PER_TILE] = 1.0
    pgk = np.stack([_place(KV_W, KV_W, (h % GROUP) * hd, (h // GROUP) * hd, hd, scale) for h in range(N_HEADS)])
    rkg = np.stack([_place(KV_W, KV_W, (h // GROUP) * hd, (h % GROUP) * hd, hd) for h in range(N_HEADS)])
    egs = np.zeros((LANES, N_NSA_BRANCH * N_HEADS * KV_W), np.float32)
    for c in range(N_NSA_BRANCH * N_HEADS):
        egs[c, c * KV_W:(c + 1) * KV_W] = 1.0
    as_bf = lambda a: jnp.asarray(a, BF)
    return dict(pq=as_bf(pq), pk=as_bf(pk), pkt=as_bf(pkt), pall=as_bf(pall),
                pgk=as_bf(pgk), rkg=as_bf(rkg), egs=as_bf(egs))


def _block_diag(w, per_group):
    nb, c, _ = w.shape
    eye = jnp.eye(per_group, dtype=w.dtype)
    wg = w.reshape(nb // per_group, per_group, c, c)
    return jnp.einsum('gpcd,pq->gpcqd', wg, eye).reshape(nb // per_group, per_group * c, per_group * c)


def _compress_weights(w1, w2, pe):
    eye = jnp.eye(KV_HEADS, dtype=w1.dtype)
    big = jnp.einsum('lde,kq->lkdqe', w1, eye).reshape(CMP_BLOCK * KV_W, KV_HEADS * w1.shape[2])
    half = big.shape[0] // 2
    w1cat = jnp.concatenate([big[:half], big[half:]], axis=1).astype(BF)
    w2bd = _block_diag(jnp.broadcast_to(w2, (KV_HEADS,) + w2.shape), KV_HEADS)[0].astype(BF)
    pe_flat = jnp.broadcast_to(pe[:, None, :], (CMP_BLOCK, KV_HEADS, HEAD_DIM)).reshape(2, half)
    pe2 = jnp.tile(pe_flat, (SUBLANES // 2, 1))
    return w1cat, w2bd, pe2


def _compress_weights_paged(w1, w2, pe):
    per_half = LANES // HEAD_DIM
    eye = jnp.eye(per_half, dtype=w1.dtype)
    w1h = jnp.einsum('lde,kq->lkdqe', w1, eye).reshape(CMP_BLOCK, LANES, LANES)
    w1p = w1h.reshape(CMP_BLOCK // 2, 2 * LANES, LANES).astype(BF)
    w2p = jnp.einsum('ed,kq->keqd', w2, eye).reshape(LANES, LANES).astype(BF)
    pe_tok = jnp.tile(pe, (PAGE_SIZE // CMP_BLOCK, KV_HEADS))
    return w1p, w2p, pe_tok


def kernel(x_prompt, x_sample, cache_k_cmp, cache_v_cmp, cache_k_sel, cache_v_sel, page_table,
           state_win_k, state_win_v, state_conv, state_h,
           ln1_g, ln1_b, w_ffn1_up, w_ffn1_down, w_in, w_conv, b_conv, w_rg_a, b_rg_a, w_rg_x, b_rg_x,
           rg_lambda, cmp_pe, w_ck1, w_ck2, w_cv1, w_cv2, w_rec_o, w_attn_o, w_out,
           ln2_g, ln2_b, w_ffn2_up, w_ffn2_down, ln3_g, ln3_b):
    bp, t, d = x_prompt.shape
    bs, n_tok, _ = x_sample.shape
    d_rnn = w_conv.shape[1]
    assert bp == 1 and t % (128 * CMP_ROW) == 0 and t >= WINDOW
    past = page_table.shape[1] * PAGE_SIZE
    assert page_table.shape[1] % 16 == 0 and n_tok <= SUBLANES and n_tok >= CONV_W - 1
    assert state_win_k.shape[1] == WINDOW

    o_q = 2 * d_rnn
    o_kv = o_q + ATTN_W
    o_gn = o_kv + 6 * KV_W
    o_gr = o_gn + N_NSA_BRANCH * N_HEADS
    o_ga = o_gr + d
    w_in_b = w_in.astype(BF)
    w_rnn = w_in_b[:, :o_q]
    w_q = w_in_b[:, o_q:o_kv]
    w_kv = w_in_b[:, o_kv:o_gn]
    w_gn = jnp.pad(w_in_b[:, o_gn:o_gr], ((0, 0), (0, LANES - (o_gr - o_gn))))
    w_grec = w_in_b[:, o_gr:o_ga]
    w_gatt = w_in_b[:, o_ga:]
    w_attn_all = jnp.concatenate([w_in_b[:, :o_gn], jnp.pad(w_gn, ((0, 0), (0, 512 - LANES)))], axis=1)
    up1, down1 = w_ffn1_up.astype(BF), w_ffn1_down.astype(BF)
    up2, down2 = w_ffn2_up.astype(BF), w_ffn2_down.astype(BF)
    per_group = 2 * LANES // (d_rnn // RNN_BLOCKS)
    wa_bd = _block_diag(w_rg_a, per_group).astype(BF)
    wx_bd = _block_diag(w_rg_x, per_group).astype(BF)
    ck1, ck2, pe2 = _compress_weights(w_ck1, w_ck2, cmp_pe)
    cv1, cv2, _ = _compress_weights(w_cv1, w_cv2, cmp_pe)
    w_rec_b, w_attn_b, w_out_b = w_rec_o.astype(BF), w_attn_o.astype(BF), w_out.astype(BF)
    cst = _layout_constants(t // SEL_BLOCK)

    xp = _ffn_ln(x_prompt.reshape(t, d), up1, down1, ln1_g, ln1_b)
    xrg = _matmul(xp, w_rnn, 512)
    (p_kc, p_vc, kct, vct, kst, vst32, kwt, vwt32, p_gn, qa, ksa, vst, kwa, vwt) = _proj_attn(
        xp, w_q, w_kv, w_gn, cst['pq'], cst['pk'], cst['pkt'])
    y_rec, p_tail, p_h = _rglru_prompt(xrg, w_conv, b_conv, wa_bd, wx_bd, b_rg_a, b_rg_x, rg_lambda)
    nbs = t // SEL_BLOCK
    kc = _compress_prompt(p_kc, pe2, ck1, ck2, cst['pk']).reshape(KV_HEADS, 2 * nbs, LANES)
    vc = _compress_prompt(p_vc, pe2, cv1, cv2, cst['pk']).reshape(KV_HEADS, 2 * nbs, LANES)
    o_attn_t = _nsa_prompt(qa, kc, vc, ksa, vst, kwa, vwt, p_gn, cst['pall'])
    x2 = _merge(xp, y_rec, o_attn_t, w_grec, w_gatt, w_rec_b, w_attn_b, w_out_b, ln2_g, ln2_b, True)
    y_prompt = _ffn_ln(x2, up2, down2, ln3_g, ln3_b).reshape(bp, t, d)

    kvh = lambda a: jnp.transpose(a.reshape(bp, KV_HEADS, HEAD_DIM, -1), (0, 3, 1, 2))
    p_states = (kvh(kct), kvh(vct), kvh(kst), kvh(vst32), kvh(kwt[:, t - WINDOW:]), kvh(vwt32[:, t - WINDOW:]),
                p_tail[SUBLANES - (CONV_W - 1):].reshape(bp, CONV_W - 1, d_rnn), p_h.reshape(bp, d_rnn))

    m_s = bs * n_tok
    xs = _ffn_ln(x_sample.reshape(m_s, d), up1, down1, ln1_g, ln1_b)
    zs = _matmul(xs, w_attn_all, 512)
    s_xrg = zs[:, :o_q].reshape(bs, n_tok * o_q)
    seq = lambda a: a.reshape(bs, n_tok, a.shape[-1])
    s_q = seq(zs[:, o_q:o_kv])
    s_kv = [seq(zs[:, o_kv + n * KV_W:o_kv + (n + 1) * KV_W]) for n in range(6)]
    s_gn = seq(zs[:, o_gn:o_gn + LANES])
    pad8 = lambda a: jnp.pad(a, ((0, 0), (0, SUBLANES - n_tok), (0, 0)))
    ys_rec, s_conv, s_h = _rglru_sample(
        s_xrg, state_conv.reshape(bs, -1), state_h, w_conv, b_conv, wa_bd, wx_bd, b_rg_a, b_rg_x, rg_lambda,
        start0=(past == 0))
    nbs_s = past // SEL_BLOCK
    ck1p, ck2p, pe_tok = _compress_weights_paged(w_ck1, w_ck2, cmp_pe)
    cv1p, cv2p, _ = _compress_weights_paged(w_cv1, w_cv2, cmp_pe)
    kc_s, vc_s = _compress_paged(cache_k_cmp, cache_v_cmp, page_table, pe_tok, ck1p, ck2p, cv1p, cv2p)
    o8 = _nsa_sample(page_table, pad8(s_q), kc_s, vc_s, cache_k_sel, cache_v_sel,
                     [pad8(s_kv[n]) for n in (2, 3, 4, 5)],
                     state_win_k.reshape(bs, WINDOW, KV_W), state_win_v.reshape(bs, WINDOW, KV_W),
                     pad8(s_gn), cst['pgk'], cst['rkg'], cst['egs'], n_tok)
    del nbs_s
    os_attn = o8[:, :n_tok].reshape(m_s, ATTN_W).astype(BF)
    x2s = _merge(xs, ys_rec.reshape(m_s, d_rnn), os_attn, w_grec, w_gatt, w_rec_b, w_attn_b, w_out_b, ln2_g, ln2_b,
                 False)
    y_sample = _ffn_ln(x2s, up2, down2, ln3_g, ln3_b).reshape(bs, n_tok, d)

    kvs = lambda a: a.reshape(bs, n_tok, KV_HEADS, HEAD_DIM)
    win = lambda old, new: jnp.concatenate([old, kvs(new)], axis=1)[:, -WINDOW:]
    s_states = (kvs(s_kv[0]), kvs(s_kv[1]), kvs(s_kv[2]), kvs(s_kv[3]),
                win(state_win_k, s_kv[4]), win(state_win_v, s_kv[5]),
                s_conv.reshape(bs, CONV_W - 1, d_rnn), s_h)

    return (y_prompt, y_sample) + p_states + s_states
```

```python
import functools

import numpy as np
import jax
import jax.numpy as jnp
from jax import lax
from jax.experimental import pallas as pl
from jax.experimental.pallas import tpu as pltpu

F32 = jnp.float32
BF = jnp.bfloat16

DEPTH = 1
ALPHA = (2.0 * DEPTH) ** 0.25
N_HEADS = 16
HEAD_DIM = 64
KV_HEADS = 4
GROUP = N_HEADS // KV_HEADS
KV_W = KV_HEADS * HEAD_DIM
ATTN_W = N_HEADS * HEAD_DIM
N_NSA_BRANCH = 3
CMP_BLOCK = 32
SEL_BLOCK = 64
N_SEL = 16
WINDOW = 512
Q_BLOCK = 128
CONV_W = 4
LRU_C = 8.0
RNN_BLOCKS = 16
LN_EPS = 1e-5
PAGE_SIZE = 128

LANES = 128
SUBLANES = 8
VMEM_LIMIT = 56 * 1024 * 1024

NEG = -1e30
SEL_TILE = 512
BLOCKS_PER_TILE = SEL_TILE // SEL_BLOCK
SEL_V_ROWS = HEAD_DIM + 16
CMP_ROW = 16
CMP_ROW_W = CMP_ROW * KV_W

_NT = (((1,), (1,)), ((), ()))


def _dot(a, b):
    return jnp.dot(a, b, preferred_element_type=F32)


def _dot_nt(a, b):
    return lax.dot_general(a, b, _NT, preferred_element_type=F32)


def _params(*sem):
    return pltpu.CompilerParams(dimension_semantics=sem, vmem_limit_bytes=VMEM_LIMIT)


def _layer_norm(y, g, b):
    mu = jnp.mean(y, axis=-1, keepdims=True)
    d = y - mu
    var = jnp.mean(d * d, axis=-1, keepdims=True)
    return d * lax.rsqrt(var + LN_EPS) * g + b


def _masked_softmax(s, mask):
    s = jnp.where(mask, s, -jnp.inf)
    m = jnp.max(s, axis=-1, keepdims=True)
    m = jnp.where(m == -jnp.inf, 0.0, m)
    e = jnp.where(mask, jnp.exp(s - m), 0.0)
    den = jnp.sum(e, axis=-1, keepdims=True)
    return e / jnp.maximum(den, 1e-30)


def _split3(x):
    hi = x.astype(BF)
    r1 = x - hi.astype(F32)
    mid = r1.astype(BF)
    lo = (r1 - mid.astype(F32)).astype(BF)
    return hi, mid, lo


def _expand_f32(x, e):
    hi, mid, lo = _split3(x)
    return _dot(hi, e) + _dot(mid, e) + _dot(lo, e)


def _ffn_kernel(x_ref, wg_ref, wu_ref, wd_ref, g_ref, b_ref, o_ref, xb_ref, acc_ref):
    j = pl.program_id(1)

    @pl.when(j == 0)
    def _():
        xb_ref[...] = x_ref[...].astype(BF)
        acc_ref[...] = jnp.zeros_like(acc_ref)

    xb = xb_ref[...]
    gate = _dot(xb, wg_ref[...])
    up = _dot(xb, wu_ref[...])
    h = (gate * jax.nn.sigmoid(gate) * up).astype(BF)
    acc_ref[...] += _dot(h, wd_ref[...])

    @pl.when(j == pl.num_programs(1) - 1)
    def _():
        y = ALPHA * x_ref[...] + 0.5 * acc_ref[...]
        o_ref[...] = _layer_norm(y, g_ref[...], b_ref[...])


def _ffn_ln(x, w_up, w_down, g, b):
    m, d = x.shape
    f = w_down.shape[0]
    tm = min(512, m)
    tn = 512
    nj = f // tn
    return pl.pallas_call(
        _ffn_kernel,
        grid=(m // tm, nj),
        in_specs=[
            pl.BlockSpec((tm, d), lambda i, j: (i, 0)),
            pl.BlockSpec((d, tn), lambda i, j: (0, j)),
            pl.BlockSpec((d, tn), lambda i, j: (0, j + nj)),
            pl.BlockSpec((tn, d), lambda i, j: (j, 0)),
            pl.BlockSpec((1, d), lambda i, j: (0, 0)),
            pl.BlockSpec((1, d), lambda i, j: (0, 0)),
        ],
        out_specs=pl.BlockSpec((tm, d), lambda i, j: (i, 0)),
        out_shape=jax.ShapeDtypeStruct((m, d), F32),
        scratch_shapes=[pltpu.VMEM((tm, d), BF), pltpu.VMEM((tm, d), F32)],
        compiler_params=_params("parallel", "arbitrary"),
        name="ffn_ln",
    )(x, w_up, w_up, w_down, g.reshape(1, d), b.reshape(1, d))


def _mm_kernel(x_ref, w_ref, o_ref):
    o_ref[...] = _dot(x_ref[...].astype(BF), w_ref[...])


def _matmul(x, w, tn):
    m, d = x.shape
    n = w.shape[1]
    tm = min(512, m)
    return pl.pallas_call(
        _mm_kernel,
        grid=(m // tm, n // tn),
        in_specs=[pl.BlockSpec((tm, d), lambda i, j: (i, 0)),
                  pl.BlockSpec((d, tn), lambda i, j: (0, j))],
        out_specs=pl.BlockSpec((tm, tn), lambda i, j: (i, j)),
        out_shape=jax.ShapeDtypeStruct((m, n), F32),
        compiler_params=_params("parallel", "arbitrary"),
        name="proj",
    )(x, w)


def _proj_attn_kernel(x_ref, wq_ref, wkv_ref, wgn_ref, pq_ref, pk_ref, pkt_ref,
                      kc_ref, vc_ref, kct_ref, vct_ref, kst_ref, vst32_ref, kwt_ref, vwt32_ref, gn_ref,
                      qa_ref, ksa_ref, vst_ref, kwa_ref, vwt_ref):
    tm = x_ref.shape[0]
    xb = x_ref[...].astype(BF)
    zq = (_dot(xb, wq_ref[...]) * (HEAD_DIM ** -0.5 * np.log2(np.e))).astype(BF)
    for h in range(N_HEADS):
        pair = zq[:, (h // 2) * LANES:(h // 2 + 1) * LANES]
        qa_ref[h] = _dot(pair, pq_ref[h % 2]).astype(BF)
    zkv = _dot(xb, wkv_ref[...])
    parts = [zkv[:, n * KV_W:(n + 1) * KV_W] for n in range(6)]
    kc_ref[...] = parts[0]
    vc_ref[...] = parts[1]
    for ref, part in zip((kct_ref, vct_ref, kst_ref, vst32_ref, kwt_ref, vwt32_ref), parts):
        ref[...] = part.T
    gn_ref[...] = _dot(xb, wgn_ref[...])
    t = pl.program_id(0) * tm + lax.broadcasted_iota(jnp.int32, (tm, 1), 0)
    lane = lax.broadcasted_iota(jnp.int32, (1, LANES), 1)
    onehot = jnp.where(lane == HEAD_DIM + (t // SEL_BLOCK) % BLOCKS_PER_TILE, 1.0, 0.0)
    ksb = parts[2].astype(BF)
    kwb = parts[4].astype(BF)
    vsb = parts[3].astype(BF)
    vwb = parts[5].astype(BF)
    for k in range(KV_HEADS):
        ksa_ref[k] = (_dot(ksb, pk_ref[k]) + onehot).astype(BF)
        kwa_ref[k] = _dot(kwb, pk_ref[k]).astype(BF)
        ones_row = jnp.where(lax.broadcasted_iota(jnp.int32, (SEL_V_ROWS, 1), 0) == HEAD_DIM, 1.0, 0.0)
        vst_ref[k, 0] = (_dot_nt(pkt_ref[k], vsb) + ones_row).astype(BF)
        vwt_ref[k] = (_dot_nt(pkt_ref[k], vwb) + ones_row).astype(BF)


def _proj_attn(x, wq, wkv, wgn, pq, pk, pkt):
    m, d = x.shape
    tm = 256
    per_tile = SEL_TILE // tm
    row = lambda w: pl.BlockSpec((tm, w), lambda i: (i, 0))
    const = lambda a: pl.BlockSpec(a.shape, lambda i: (0,) * a.ndim)
    head = lambda n: pl.BlockSpec((n, tm, LANES), lambda i: (0, i, 0))
    f32 = lambda w: jax.ShapeDtypeStruct((m, w), F32)
    aug = lambda n: jax.ShapeDtypeStruct((n, m, LANES), BF)
    return pl.pallas_call(
        _proj_attn_kernel,
        grid=(m // tm,),
        in_specs=[row(d), const(wq), const(wkv), const(wgn), const(pq), const(pk), const(pkt)],
        out_specs=[row(KV_W)] * 2 + [pl.BlockSpec((KV_W, tm), lambda i: (0, i))] * 6
                  + [row(LANES), head(N_HEADS), head(KV_HEADS),
                                     pl.BlockSpec((KV_HEADS, 1, SEL_V_ROWS, tm),
                                                  lambda i: (0, i // per_tile, 0, i % per_tile)),
                                     head(KV_HEADS),
                                     pl.BlockSpec((KV_HEADS, SEL_V_ROWS, tm), lambda i: (0, 0, i))],
        out_shape=[f32(KV_W)] * 2 + [jax.ShapeDtypeStruct((KV_W, m), F32)] * 6
                  + [f32(LANES), aug(N_HEADS), aug(KV_HEADS),
                                     jax.ShapeDtypeStruct((KV_HEADS, m // SEL_TILE, SEL_V_ROWS, SEL_TILE), BF),
                                     aug(KV_HEADS),
                                     jax.ShapeDtypeStruct((KV_HEADS, SEL_V_ROWS, m), BF)],
        compiler_params=_params("parallel"),
        name="proj_attn",
    )(x, wq, wkv, wgn, pq, pk, pkt)


def _rglru_gates(xc, wa, wx, ba, bx, lam, is_start):
    xcb = xc.astype(BF)
    r = jax.nn.sigmoid(_dot(xcb, wa) + ba)
    ig = jax.nn.sigmoid(_dot(xcb, wx) + bx)
    log_a = -LRU_C * r * jax.nn.softplus(-lam)
    a = jnp.exp(log_a)
    if is_start is True:
        return a, ig * xc
    th = jnp.tanh(log_a)
    mult = jnp.sqrt(-2.0 * th / (1.0 - th))
    if is_start is not None:
        mult = jnp.where(is_start, 1.0, mult)
    return a, mult * (ig * xc)


def _rglru_prompt_kernel(x_ref, gate_ref, wc_ref, bc_ref, wa_ref, wx_ref, ba_ref, bx_ref, lam_ref,
                         y_ref, tail_ref, hl_ref, h_scr, tail_scr):
    c = pl.program_id(1)
    tc = x_ref.shape[0]

    @pl.when(c == 0)
    def _():
        h_scr[...] = jnp.zeros_like(h_scr)
        tail_scr[...] = jnp.zeros_like(tail_scr)

    x = x_ref[...]
    prev = tail_scr[...]
    row8 = lax.broadcasted_iota(jnp.int32, (SUBLANES, 1), 0)
    row = lax.broadcasted_iota(jnp.int32, (tc, 1), 0)

    def shifted(s):
        rolled = pltpu.roll(x, s, 0)
        top = jnp.where(row8 < s, pltpu.roll(prev, s, 0), rolled[:SUBLANES])
        return jnp.concatenate([top, rolled[SUBLANES:]], axis=0)

    wc = wc_ref[...]
    conv = wc[0:1] * shifted(3)
    conv = conv + wc[1:2] * shifted(2)
    conv = conv + wc[2:3] * shifted(1)
    conv = conv + wc[3:4] * x
    xc = bc_ref[...] + conv

    is_start = (row + c * tc) == 0
    a, u = _rglru_gates(xc, wa_ref[0], wx_ref[0], ba_ref[...], bx_ref[...], lam_ref[...], is_start)

    d = 1
    while d < tc:
        keep = row >= d
        a_sh = jnp.where(keep, pltpu.roll(a, d, 0), 1.0)
        u_sh = jnp.where(keep, pltpu.roll(u, d, 0), 0.0)
        u = a * u_sh + u
        a = a * a_sh
        d *= 2
    h = a * h_scr[...] + u

    y_ref[...] = (jax.nn.gelu(gate_ref[...]) * h).astype(BF)
    h_last = h[tc - 1:tc]
    h_scr[...] = h_last
    tail_scr[...] = x[tc - SUBLANES:]
    hl_ref[...] = h_last
    tail_ref[...] = x[tc - SUBLANES:]


def _rglru_prompt(xrg, w_conv, b_conv, wa_bd, wx_bd, b_a, b_x, lam):
    t = xrg.shape[0]
    d_rnn = w_conv.shape[1]
    gw = 2 * LANES
    ng = d_rnn // gw
    tc = min(512, t)
    vec = lambda: pl.BlockSpec((1, gw), lambda g, c: (0, g))
    return pl.pallas_call(
        _rglru_prompt_kernel,
        grid=(ng, t // tc),
        in_specs=[
            pl.BlockSpec((tc, gw), lambda g, c: (c, g)),
            pl.BlockSpec((tc, gw), lambda g, c: (c, g + ng)),
            pl.BlockSpec((CONV_W, gw), lambda g, c: (0, g)),
            vec(),
            pl.BlockSpec((1, gw, gw), lambda g, c: (g, 0, 0)),
            pl.BlockSpec((1, gw, gw), lambda g, c: (g, 0, 0)),
            vec(), vec(), vec(),
        ],
        out_specs=[
            pl.BlockSpec((tc, gw), lambda g, c: (c, g)),
            pl.BlockSpec((SUBLANES, gw), lambda g, c: (0, g)),
            pl.BlockSpec((1, gw), lambda g, c: (0, g)),
        ],
        out_shape=[
            jax.ShapeDtypeStruct((t, d_rnn), BF),
            jax.ShapeDtypeStruct((SUBLANES, d_rnn), F32),
            jax.ShapeDtypeStruct((1, d_rnn), F32),
        ],
        scratch_shapes=[pltpu.VMEM((1, gw), F32), pltpu.VMEM((SUBLANES, gw), F32)],
        compiler_params=_params("parallel", "arbitrary"),
        name="rglru_prompt",
    )(xrg, xrg, w_conv, b_conv.reshape(1, -1), wa_bd, wx_bd,
      b_a.reshape(1, -1), b_x.reshape(1, -1), lam.reshape(1, -1))


def _rglru_sample_kernel(xrg_ref, cp_ref, h0_ref, wc_ref, bc_ref, wa_ref, wx_ref, ba_ref, bx_ref, lam_ref,
                         y_ref, cn_ref, hl_ref, *, n_tok, start0):
    d_rnn = h0_ref.shape[1]
    gw = wa_ref.shape[1]
    wc = wc_ref[...]
    xp = [cp_ref[:, k * d_rnn:(k + 1) * d_rnn] for k in range(CONV_W - 1)]
    xp += [xrg_ref[:, t * 2 * d_rnn:t * 2 * d_rnn + d_rnn] for t in range(n_tok)]
    h = h0_ref[...]
    for t in range(n_tok):
        conv = wc[0:1] * xp[t]
        for k in range(1, CONV_W):
            conv = conv + wc[k:k + 1] * xp[t + k]
        xc = bc_ref[...] + conv
        a_parts, u_parts = [], []
        for g in range(d_rnn // gw):
            sl = slice(g * gw, (g + 1) * gw)
            a_g, u_g = _rglru_gates(xc[:, sl], wa_ref[g], wx_ref[g], ba_ref[:, sl], bx_ref[:, sl],
                                    lam_ref[:, sl], True if (start0 and t == 0) else None)
            a_parts.append(a_g)
            u_parts.append(u_g)
        a = jnp.concatenate(a_parts, axis=1)
        u = jnp.concatenate(u_parts, axis=1)
        h = a * h + u
        gate = xrg_ref[:, t * 2 * d_rnn + d_rnn:(t + 1) * 2 * d_rnn]
        y_ref[:, t * d_rnn:(t + 1) * d_rnn] = (jax.nn.gelu(gate) * h).astype(BF)
    hl_ref[...] = h
    tail = xp[-(CONV_W - 1):]
    for k in range(CONV_W - 1):
        cn_ref[:, k * d_rnn:(k + 1) * d_rnn] = tail[k]


def _rglru_sample(xrg, conv_prev, h0, w_conv, b_conv, wa_bd, wx_bd, b_a, b_x, lam, start0):
    b, d_rnn = h0.shape
    n_tok = xrg.shape[1] // (2 * d_rnn)
    args = (xrg, conv_prev, h0, w_conv, b_conv.reshape(1, -1), wa_bd, wx_bd,
            b_a.reshape(1, -1), b_x.reshape(1, -1), lam.reshape(1, -1))
    full = lambda a: pl.BlockSpec(a.shape, lambda i: (0,) * a.ndim)
    outs = [jax.ShapeDtypeStruct((b, n_tok * d_rnn), BF),
            jax.ShapeDtypeStruct((b, (CONV_W - 1) * d_rnn), F32),
            jax.ShapeDtypeStruct((b, d_rnn), F32)]
    return pl.pallas_call(
        functools.partial(_rglru_sample_kernel, n_tok=n_tok, start0=start0),
        grid=(1,),
        in_specs=[full(a) for a in args],
        out_specs=[full(o) for o in outs],
        out_shape=outs,
        compiler_params=_params("arbitrary"),
        name="rglru_sample",
    )(*args)


def _compress_kernel(x_ref, pe_ref, w1_ref, w2_ref, pk_ref, o_ref, scr):
    x = x_ref[...]
    rows = x.shape[0]
    xb = (x.reshape(rows // SUBLANES, SUBLANES, CMP_ROW_W) + pe_ref[...][None]).reshape(rows, CMP_ROW_W)
    full = _dot(xb.astype(BF), w1_ref[...])
    hid = full[:, :KV_W] + pltpu.roll(full[:, KV_W:], rows - 1, 0)
    out = _dot(jax.nn.gelu(hid).astype(BF), w2_ref[...])
    nb = rows // 4
    ob = out.astype(BF)
    for k in range(KV_HEADS):
        scr[...] = _dot(ob, pk_ref[k])
        o_ref[k, 0] = scr[pl.ds(0, nb, stride=4), :].astype(BF)
        o_ref[k, 1] = scr[pl.ds(2, nb, stride=4), :].astype(BF)


def _compress_prompt(kv, pe2, w1, w2, pk):
    t = kv.shape[0]
    rows = 128
    x = kv.reshape(t // CMP_ROW, CMP_ROW_W)
    nb = rows // 4
    nbs = t // SEL_BLOCK
    const = lambda a: pl.BlockSpec(a.shape, lambda i: (0,) * a.ndim)
    return pl.pallas_call(
        _compress_kernel,
        grid=(x.shape[0] // rows,),
        in_specs=[pl.BlockSpec((rows, CMP_ROW_W), lambda i: (i, 0)),
                  const(pe2), const(w1), const(w2), const(pk)],
        out_specs=pl.BlockSpec((KV_HEADS, 2, nb, LANES), lambda i: (0, 0, i, 0)),
        out_shape=jax.ShapeDtypeStruct((KV_HEADS, 2, nbs, LANES), BF),
        scratch_shapes=[pltpu.VMEM((rows, LANES), F32)],
        compiler_params=_params("parallel"),
        name="compress_prompt",
    )(x, pe2, w1, w2, pk)


def _compress_paged_kernel(pt_ref, *refs, pages):
    del pt_ref
    pe_ref = refs[2 * pages]
    wk1_ref, wk2_ref, wv1_ref, wv2_ref, ok_ref, ov_ref, xk_scr, xv_scr, outk_scr, outv_scr = refs[2 * pages + 1:]
    _compress_pages(refs[:pages], pe_ref, wk1_ref, wk2_ref, ok_ref, xk_scr, outk_scr)
    _compress_pages(refs[pages:2 * pages], pe_ref, wv1_ref, wv2_ref, ov_ref, xv_scr, outv_scr)


def _compress_pages(x_refs, pe_ref, w1_ref, w2_ref, o_ref, xs_scr, out_scr):
    pages = len(x_refs)
    halves = KV_W // LANES
    for p in range(pages):
        xt = x_refs[p][0].T + pe_ref[...]
        for h in range(halves):
            xs_scr[h, p * PAGE_SIZE:(p + 1) * PAGE_SIZE, :] = xt[:, h * LANES:(h + 1) * LANES]
    nblk = pages * (PAGE_SIZE // CMP_BLOCK)
    by_tok = [pltpu.einshape("mld->lmd", xs_scr[h].reshape(nblk, CMP_BLOCK, LANES)) for h in range(halves)]
    acc = None
    for l2 in range(CMP_BLOCK // 2):
        parts = [jnp.concatenate([by_tok[h][2 * l2], by_tok[h][2 * l2 + 1]], axis=1) for h in range(halves)]
        d = _dot(jnp.concatenate(parts, axis=0).astype(BF), w1_ref[l2])
        acc = d if acc is None else acc + d
    out_scr[...] = _dot(jax.nn.gelu(acc).astype(BF), w2_ref[...])
    for h in range(halves):
        for par in range(2):
            o_ref[0, par, :, h * LANES:(h + 1) * LANES] = (
                out_scr[pl.ds(h * nblk + par, nblk // 2, stride=2), :].astype(BF))


def _compress_paged(cache_k, cache_v, page_table, pe_tok, wk1, wk2, wv1, wv2):
    b, n_pages = page_table.shape
    pages = 16
    view = lambda c: jnp.transpose(c, (0, 2, 3, 1)).reshape(c.shape[0], KV_W, PAGE_SIZE)
    xk, xv = view(cache_k), view(cache_v)
    nblk = pages * (PAGE_SIZE // CMP_BLOCK)
    halves = KV_W // LANES
    nbs = n_pages * PAGE_SIZE // SEL_BLOCK
    const = lambda a: pl.BlockSpec(a.shape, lambda i, s, pt: (0,) * a.ndim)
    page_specs = [
        pl.BlockSpec((1, KV_W, PAGE_SIZE),
                     functools.partial(lambda i, s, pt, p: (pt[i, s * pages + p], 0, 0), p=p))
        for p in range(pages)]
    return pl.pallas_call(
        functools.partial(_compress_paged_kernel, pages=pages),
        grid_spec=pltpu.PrefetchScalarGridSpec(
            num_scalar_prefetch=1,
            grid=(b, n_pages // pages),
            in_specs=page_specs + page_specs + [const(a) for a in (pe_tok, wk1, wk2, wv1, wv2)],
            out_specs=[pl.BlockSpec((1, 2, nblk // 2, KV_W), lambda i, s, pt: (i, 0, s, 0))] * 2,
            scratch_shapes=[pltpu.VMEM((halves, pages * PAGE_SIZE, LANES), F32)] * 2
                           + [pltpu.VMEM((halves * nblk, LANES), F32)] * 2,
        ),
        out_shape=[jax.ShapeDtypeStruct((b, 2, nbs, KV_W), BF)] * 2,
        compiler_params=_params("parallel", "arbitrary"),
        name="compress_paged",
    )(page_table, *([xk] * pages), *([xv] * pages), pe_tok, wk1, wk2, wv1, wv2)


def _topk_columns(imp_t, n_pick, quota=None):
    nblk = imp_t.shape[0]
    idx = lax.broadcasted_iota(jnp.int32, imp_t.shape, 0)
    sel = jnp.zeros(imp_t.shape, F32)
    for n in range(n_pick):
        mx = jnp.max(imp_t, axis=0, keepdims=True)
        first = jnp.min(jnp.where(imp_t == mx, idx, nblk), axis=0, keepdims=True)
        if quota is not None:
            first = jnp.where(n < quota, first, nblk)
        hit = idx == first
        sel = jnp.where(hit, 1.0, sel)
        imp_t = jnp.where(hit, -jnp.inf, imp_t)
    return sel


def _softmax_cols(s):
    m = jnp.max(s, axis=0, keepdims=True)
    m = jnp.where(m == -jnp.inf, 0.0, m)
    e = jnp.exp2(s - m)
    den = jnp.sum(e, axis=0, keepdims=True)
    return e * (1.0 / jnp.maximum(den, 1e-30))


def _nsa_prompt_kernel(*refs):
    (q_ref, kc_ref, vc_ref, ks_ref, vst_ref) = refs[:5]
    kw_refs = refs[5:10]
    vwt_refs = refs[10:15]
    (gn_ref, pa_ref, o_ref,
     bias_scr, gate_scr, sa_scr, sb_scr, m_scr, acc_scr, oc_scr, ow_scr) = refs[15:]
    kvh = pl.program_id(0)
    i = pl.program_id(1)
    nq = Q_BLOCK
    r = GROUP * nq
    hd = HEAD_DIM
    q = q_ref[...].reshape(r, LANES)
    lane = lax.broadcasted_iota(jnp.int32, (1, r), 1)
    qpos = i * nq + lane % nq

    nbs = kc_ref.shape[1] // 2

    def window_and_gates():
        kw = jnp.concatenate([ref[0] for ref in kw_refs], axis=0)
        vwt = jnp.concatenate([ref[0] for ref in vwt_refs], axis=1)
        s_w = _dot_nt(kw, q)
        nwb = len(kw_refs)
        krow = lax.broadcasted_iota(jnp.int32, (nq, 1), 0)
        parts = []
        for b in range(nwb):
            blk_pos = (i - (nwb - 1) + b) * nq
            mask = blk_pos >= 0
            if b == 0:
                mask = mask & (qpos - (blk_pos + krow) <= WINDOW)
            if b == nwb - 1:
                mask = mask & (blk_pos + krow <= qpos)
            parts.append(jnp.where(mask, s_w[b * nq:(b + 1) * nq], -jnp.inf))
        s_w = jnp.concatenate(parts, axis=0)
        e_w = jnp.exp2(s_w - jnp.max(s_w, axis=0, keepdims=True))
        acc_w = _dot(vwt, e_w.astype(BF))
        ow_scr[...] = acc_w[:hd] * (1.0 / jnp.maximum(acc_w[hd:hd + 1], 1e-30))
        gate_scr[...] = jax.nn.sigmoid(gn_ref[...]).T

    def compressed_and_select(nv, first_quarter):
        window_and_gates()
        kc = jnp.concatenate([kc_ref[0, 0:nv], kc_ref[0, nbs:nbs + nv]], axis=0)
        vc = jnp.concatenate([vc_ref[0, 0:nv], vc_ref[0, nbs:nbs + nv]], axis=0)
        rowc = lax.broadcasted_iota(jnp.int32, (2 * nv, 1), 0)
        blk = 2 * (rowc % nv) + rowc // nv
        p_c = _softmax_cols(jnp.where(blk * CMP_BLOCK + (CMP_BLOCK - 1) <= qpos, _dot_nt(kc, q), -jnp.inf))
        vct = vc.astype(F32).T[:hd].astype(BF)
        oc_scr[...] = _dot(vct, p_c.astype(BF))
        imp = p_c[:, 0:nq]
        for g in range(1, GROUP):
            imp = imp + p_c[:, g * nq:(g + 1) * nq]
        imp = imp[:nv] + imp[nv:]
        qp = i * nq + lax.broadcasted_iota(jnp.int32, (1, nq), 1)
        jb = lax.broadcasted_iota(jnp.int32, (nv, 1), 0)
        cur = qp // SEL_BLOCK
        valid = jb * SEL_BLOCK <= qp
        forced = (jb == 0) | (jb == cur) | (jb == cur - 1)
        n_forced = 1 + jnp.where(cur >= 1, 1, 0) + jnp.where(cur >= 2, 1, 0)
        n_sel = min(N_SEL, nbs)
        n_iter = n_sel - (1 if first_quarter else 3)
        picks = _topk_columns(jnp.where(valid & ~forced, imp, -jnp.inf), n_iter, n_sel - n_forced)
        sel = jnp.where(forced, 1.0, picks)
        bias = jnp.where(valid & (sel > 0.5), 0.0, NEG).T.astype(BF)
        nt = nv // BLOCKS_PER_TILE
        bias_all = _dot(bias, pa_ref[0:nv, 0:nt * LANES]).astype(BF)
        for t in range(nt):
            bias_scr[t] = bias_all[:, t * LANES:(t + 1) * LANES]

    n_var = 4
    variant = (i * n_var) // pl.num_programs(1)
    for v in range(n_var):
        pl.when(variant == v)(functools.partial(compressed_and_select, nbs * (v + 1) // n_var, v == 0))
    o_c = oc_scr[...]

    def scores(t):
        k0 = pl.multiple_of(t * SEL_TILE, SEL_TILE)
        kt = ks_ref[0, pl.ds(k0, SEL_TILE), :]
        qa = q + jnp.concatenate([bias_scr[t]] * GROUP, axis=0)
        return _dot_nt(kt, qa)

    def update(t, s_ref, diagonal):
        vt = vst_ref[0, t]
        cw = 2 * LANES
        for c in range(r // cw):
            sl = slice(c * cw, (c + 1) * cw)
            sg = s_ref[:, sl]
            if diagonal:
                kpos = t * SEL_TILE + lax.broadcasted_iota(jnp.int32, (SEL_TILE, 1), 0)
                sg = jnp.where(kpos <= qpos[:, sl], sg, NEG)
            m_old = m_scr[:, sl]
            m_new = jnp.maximum(m_old, jnp.max(sg, axis=0, keepdims=True))
            p = jnp.exp2(sg - m_new)
            acc_scr[:, sl] = jnp.exp2(m_old - m_new) * acc_scr[:, sl] + _dot(vt, p.astype(BF))
            m_scr[:, sl] = m_new

    def pair(t0):
        sb_scr[...] = scores(t0 + 1)
        update(t0, sa_scr, False)
        sa_scr[...] = scores(t0 + 2)
        update(t0 + 1, sb_scr, False)

    def octet(u, _):
        for n in range(4):
            pair(8 * u + 2 * n)
        return 0

    t_diag = (i * nq) // SEL_TILE
    m_scr[...] = jnp.full(m_scr.shape, NEG, F32)
    acc_scr[...] = jnp.zeros_like(acc_scr)
    sa_scr[...] = scores(0)
    lax.fori_loop(0, t_diag // 8, octet, 0)

    @pl.when(t_diag % 8 >= 4)
    def _():
        pair(t_diag - t_diag % 8)
        pair(t_diag - t_diag % 8 + 2)

    @pl.when(t_diag % 4 >= 2)
    def _():
        pair(t_diag - t_diag % 4)

    @pl.when(t_diag % 2 == 1)
    def _():
        sb_scr[...] = scores(t_diag)
        update(t_diag - 1, sa_scr, False)
        update(t_diag, sb_scr, True)

    @pl.when(t_diag % 2 == 0)
    def _():
        update(t_diag, sa_scr, True)

    o_s = acc_scr[:hd] * (1.0 / acc_scr[hd:hd + 1])

    o_w = ow_scr[...]

    def gate(br):
        return jnp.concatenate(
            [gate_scr[pl.ds(br * N_HEADS + kvh * GROUP + g, 1), :] for g in range(GROUP)], axis=1)

    comb = (gate(0) * o_c + gate(1) * o_s + gate(2) * o_w).astype(BF)
    o_ref[...] = jnp.concatenate([comb[:, g * nq:(g + 1) * nq] for g in range(GROUP)], axis=0)


def _nsa_prompt(qa, kc, vc, ksa, vst, kwa, vwt, gn, pall):
    t = qa.shape[1]
    n_tiles = t // SEL_TILE
    wb = WINDOW // Q_BLOCK + 1
    kv_full = lambda a: pl.BlockSpec((1,) + a.shape[1:], lambda k, i: (k,) + (0,) * (a.ndim - 1))
    wblk = lambda i, m: jnp.maximum(i - (wb - 1) + m, 0)
    win_k = [pl.BlockSpec((1, Q_BLOCK, LANES), functools.partial(lambda k, i, m: (k, wblk(i, m), 0), m=m))
             for m in range(wb)]
    win_v = [pl.BlockSpec((1, SEL_V_ROWS, Q_BLOCK), functools.partial(lambda k, i, m: (k, 0, wblk(i, m)), m=m))
             for m in range(wb)]
    const = lambda a: pl.BlockSpec(a.shape, lambda k, i: (0,) * a.ndim)
    return pl.pallas_call(
        _nsa_prompt_kernel,
        grid=(KV_HEADS, t // Q_BLOCK),
        in_specs=[pl.BlockSpec((GROUP, Q_BLOCK, LANES), lambda k, i: (k, i, 0)),
                  kv_full(kc), kv_full(vc), kv_full(ksa), kv_full(vst)]
                 + win_k + win_v
                 + [pl.BlockSpec((Q_BLOCK, LANES), lambda k, i: (i, 0)), const(pall)],
        out_specs=pl.BlockSpec((KV_W, Q_BLOCK), lambda k, i: (k, i)),
        out_shape=jax.ShapeDtypeStruct((ATTN_W, t), BF),
        scratch_shapes=[pltpu.VMEM((n_tiles, Q_BLOCK, LANES), BF), pltpu.VMEM((LANES, Q_BLOCK), F32),
                        pltpu.VMEM((SEL_TILE, GROUP * Q_BLOCK), F32), pltpu.VMEM((SEL_TILE, GROUP * Q_BLOCK), F32),
                        pltpu.VMEM((1, GROUP * Q_BLOCK), F32),
                        pltpu.VMEM((SEL_V_ROWS, GROUP * Q_BLOCK), F32),
                        pltpu.VMEM((HEAD_DIM, GROUP * Q_BLOCK), F32), pltpu.VMEM((HEAD_DIM, GROUP * Q_BLOCK), F32)],
        compiler_params=_params("parallel", "arbitrary"),
        name="nsa_prompt",
    )(qa, kc, vc, ksa, vst, *([kwa] * wb), *([vwt] * wb), gn, pall)


def _nsa_sample_kernel(*refs, pages, past, n_tok):
    pt_ref = refs[0]
    q_ref, kc_ref, vc_ref = refs[1:4]
    ks_refs = refs[4:4 + pages]
    vs_refs = refs[4 + pages:4 + 2 * pages]
    (ksn_ref, vsn_ref, kwn_ref, vwn_ref, wk_ref, wv_ref, gn_ref,
     pg_ref, rp_ref, eg_ref, o_ref, q_scr, bias_scr, m_scr, l_scr, acc_scr, oc_scr) = refs[4 + 2 * pages:]
    del pt_ref
    s_id = pl.program_id(1)
    tp = SUBLANES
    r = N_HEADS * tp
    row = lax.broadcasted_iota(jnp.int32, (r, 1), 0)
    qpos = past + row % tp
    nbs = bias_scr.shape[1]

    @pl.when(s_id == 0)
    def _():
        qb = q_ref[0].astype(BF)
        for h in range(N_HEADS):
            k = h // GROUP
            q_scr[h * tp:(h + 1) * tp, :] = _dot(qb[:, k * KV_W:(k + 1) * KV_W], pg_ref[h]).astype(BF)
        q = q_scr[...]
        kc = kc_ref[0].reshape(2 * nbs, KV_W)
        vc = vc_ref[0].reshape(2 * nbs, KV_W)
        col = lax.broadcasted_iota(jnp.int32, (1, 2 * nbs), 1)
        blk = 2 * (col % nbs) + col // nbs
        p_c = _masked_softmax(_dot_nt(q, kc), blk * CMP_BLOCK + (CMP_BLOCK - 1) <= qpos)
        oc_scr[...] = _dot(p_c.astype(BF), vc)
        imps = []
        for k in range(KV_HEADS):
            base = k * GROUP * tp
            imp = p_c[base:base + tp]
            for g in range(1, GROUP):
                imp = imp + p_c[base + g * tp:base + (g + 1) * tp]
            imps.append(imp)
        imp = jnp.concatenate(imps, axis=0)
        imp = imp[:, :nbs] + imp[:, nbs:]
        jb = lax.broadcasted_iota(jnp.int32, (1, nbs), 1)
        imp = jnp.where((jb == 0) | (jb == nbs - 1), jnp.inf, imp)
        idx = lax.broadcasted_iota(jnp.int32, imp.shape, 1)
        sel = jnp.zeros(imp.shape, F32)
        for _ in range(min(N_SEL, nbs + 1) - 1):
            mx = jnp.max(imp, axis=1, keepdims=True)
            first = jnp.min(jnp.where(imp == mx, idx, nbs), axis=1, keepdims=True)
            hit = idx == first
            sel = jnp.where(hit, 1.0, sel)
            imp = jnp.where(hit, -jnp.inf, imp)
        bias = jnp.where(sel > 0.5, 0.0, NEG).astype(BF)
        for k in range(KV_HEADS):
            for g in range(GROUP):
                h = k * GROUP + g
                bias_scr[h * tp:(h + 1) * tp, :] = bias[k * tp:(k + 1) * tp]
        m_scr[...] = jnp.full(m_scr.shape, NEG, F32)
        l_scr[...] = jnp.zeros_like(l_scr)
        acc_scr[...] = jnp.zeros_like(acc_scr)

    def online(st, v, v_transposed):
        m = m_scr[...]
        m_new = jnp.maximum(m, jnp.max(st, axis=1, keepdims=True))
        scale = jnp.exp(m - m_new)
        p = jnp.exp(st - m_new)
        l_scr[...] = scale * l_scr[...] + jnp.sum(p, axis=1, keepdims=True)
        pv = _dot_nt(p.astype(BF), v) if v_transposed else _dot(p.astype(BF), v)
        acc_scr[...] = scale * acc_scr[...] + pv
        m_scr[...] = m_new

    q = q_scr[...]
    kt = jnp.concatenate([ref[0].astype(BF) for ref in ks_refs], axis=1)
    vt = jnp.concatenate([ref[0].astype(BF) for ref in vs_refs], axis=1)
    n_keys = kt.shape[1]
    jb = lax.broadcasted_iota(jnp.int32, (nbs, 1), 0)
    key_blk = s_id * (n_keys // SEL_BLOCK) + lax.broadcasted_iota(jnp.int32, (1, n_keys), 1) // SEL_BLOCK
    expand = jnp.where(jb == key_blk, 1.0, 0.0).astype(BF)
    online(_dot(q, kt) + _dot(bias_scr[...], expand), vt, True)

    @pl.when(s_id == pl.num_programs(1) - 1)
    def _():
        zeros = jnp.zeros((LANES - tp, KV_W), BF)
        tok = lax.broadcasted_iota(jnp.int32, (1, LANES), 1)
        ksn = jnp.concatenate([ksn_ref[0].astype(BF), zeros], axis=0)
        vsn = jnp.concatenate([vsn_ref[0].astype(BF), zeros], axis=0)
        online(jnp.where((past + tok <= qpos) & (tok < n_tok), _dot_nt(q, ksn), NEG), vsn, False)
        o_s = acc_scr[...] / l_scr[...]
        kw = jnp.concatenate([wk_ref[0].astype(BF), kwn_ref[0].astype(BF), zeros], axis=0)
        vw = jnp.concatenate([wv_ref[0].astype(BF), vwn_ref[0].astype(BF), zeros], axis=0)
        n_w = kw.shape[0]
        wcol = lax.broadcasted_iota(jnp.int32, (1, n_w), 1)
        kwpos = past - WINDOW + wcol
        dpos = qpos - kwpos
        mask_w = (dpos >= 0) & (dpos <= WINDOW) & (kwpos >= 0) & (wcol < WINDOW + n_tok)
        p_w = _masked_softmax(_dot_nt(q, kw), mask_w)
        o_w = _dot(p_w.astype(BF), vw)
        sg = jax.nn.sigmoid(gn_ref[0])
        gexp = _expand_f32(sg, eg_ref[...])

        def gate(br):
            return jnp.concatenate(
                [gexp[:, (br * N_HEADS + h) * KV_W:(br * N_HEADS + h + 1) * KV_W] for h in range(N_HEADS)],
                axis=0)

        comb = (gate(0) * oc_scr[...] + gate(1) * o_s + gate(2) * o_w).astype(BF)
        for k in range(KV_HEADS):
            out = None
            for g in range(GROUP):
                h = k * GROUP + g
                part = _dot(comb[h * tp:(h + 1) * tp], rp_ref[h])
                out = part if out is None else out + part
            o_ref[0, :, k * KV_W:(k + 1) * KV_W] = out


def _nsa_sample(page_table, q8, kc, vc, cache_ks, cache_vs, new8, win_k, win_v, gn8, pgk, rkg, egate, n_tok):
    b, n_pages = page_table.shape
    past = n_pages * PAGE_SIZE
    pages = 8
    tp = SUBLANES
    r = N_HEADS * tp
    nbs = past // SEL_BLOCK
    ck = jnp.transpose(cache_ks, (0, 2, 3, 1)).reshape(cache_ks.shape[0], KV_W, PAGE_SIZE)
    cv = jnp.transpose(cache_vs, (0, 2, 3, 1)).reshape(cache_vs.shape[0], KV_W, PAGE_SIZE)
    per_seq = lambda a: pl.BlockSpec((1,) + a.shape[1:], lambda i, s, pt: (i,) + (0,) * (a.ndim - 1))
    const = lambda a: pl.BlockSpec(a.shape, lambda i, s, pt: (0,) * a.ndim)
    page_specs = [
        pl.BlockSpec((1, KV_W, PAGE_SIZE),
                     functools.partial(lambda i, s, pt, p: (pt[i, s * pages + p], 0, 0), p=p))
        for p in range(pages)]
    return pl.pallas_call(
        functools.partial(_nsa_sample_kernel, pages=pages, past=past, n_tok=n_tok),
        grid_spec=pltpu.PrefetchScalarGridSpec(
            num_scalar_prefetch=1,
            grid=(b, n_pages // pages),
            in_specs=[per_seq(q8), per_seq(kc), per_seq(vc)] + page_specs + page_specs
                     + [per_seq(a) for a in new8] + [per_seq(win_k), per_seq(win_v), per_seq(gn8),
                                                     const(pgk), const(rkg), const(egate)],
            out_specs=pl.BlockSpec((1, tp, ATTN_W), lambda i, s, pt: (i, 0, 0)),
            scratch_shapes=[pltpu.VMEM((r, KV_W), BF), pltpu.VMEM((r, nbs), BF),
                            pltpu.VMEM((r, 1), F32), pltpu.VMEM((r, 1), F32),
                            pltpu.VMEM((r, KV_W), F32), pltpu.VMEM((r, KV_W), F32)],
        ),
        out_shape=jax.ShapeDtypeStruct((b, tp, ATTN_W), F32),
        compiler_params=_params("parallel", "arbitrary"),
        name="nsa_sample",
    )(page_table, q8, kc, vc, *([ck] * pages), *([cv] * pages), *new8, win_k, win_v, gn8, pgk, rkg, egate)


def _merge_kernel(x_ref, y_ref, o_ref, wgr_ref, wga_ref, wr_ref, wa_ref, wo_ref, g_ref, b_ref,
                  out_ref, xb_ref, acc_ref, *ob_ref):
    j = pl.program_id(1)

    @pl.when(j == 0)
    def _():
        xb_ref[...] = x_ref[...].astype(BF)
        acc_ref[...] = jnp.zeros_like(acc_ref)
        if ob_ref:
            ob_ref[0][...] = o_ref[...].astype(F32).T.astype(BF)

    xb = xb_ref[...]
    rec = _dot(y_ref[...], wr_ref[...])
    att = _dot(ob_ref[0][...] if ob_ref else o_ref[...], wa_ref[...])
    u = jax.nn.sigmoid(_dot(xb, wgr_ref[...])) * rec + jax.nn.sigmoid(_dot(xb, wga_ref[...])) * att
    acc_ref[...] += _dot(u.astype(BF), wo_ref[...])

    @pl.when(j == pl.num_programs(1) - 1)
    def _():
        out_ref[...] = _layer_norm(ALPHA * x_ref[...] + acc_ref[...], g_ref[...], b_ref[...])


def _merge(x, y_rec, o_attn, w_grec, w_gatt, w_rec_o, w_attn_o, w_out, g, b, o_transposed):
    m, d = x.shape
    tm = min(512, m)
    tn = 512
    col = lambda a: pl.BlockSpec((a.shape[0], tn), lambda i, j: (0, j))
    rowb = lambda a: pl.BlockSpec((tm, a.shape[1]), lambda i, j: (i, 0))
    vec = pl.BlockSpec((1, d), lambda i, j: (0, 0))
    if o_transposed:
        o_spec = pl.BlockSpec((o_attn.shape[0], tm), lambda i, j: (0, i))
        o_scratch = [pltpu.VMEM((tm, o_attn.shape[0]), BF)]
    else:
        o_spec, o_scratch = rowb(o_attn), []
    return pl.pallas_call(
        _merge_kernel,
        grid=(m // tm, d // tn),
        in_specs=[rowb(x), rowb(y_rec), o_spec, col(w_grec), col(w_gatt), col(w_rec_o), col(w_attn_o),
                  pl.BlockSpec((tn, d), lambda i, j: (j, 0)), vec, vec],
        out_specs=pl.BlockSpec((tm, d), lambda i, j: (i, 0)),
        out_shape=jax.ShapeDtypeStruct((m, d), F32),
        scratch_shapes=[pltpu.VMEM((tm, d), BF), pltpu.VMEM((tm, d), F32)] + o_scratch,
        compiler_params=_params("parallel", "arbitrary"),
        name="merge_out",
    )(x, y_rec, o_attn, w_grec, w_gatt, w_rec_o, w_attn_o, w_out, g.reshape(1, d), b.reshape(1, d))


def _place(n_rows, n_cols, src0, dst0, width, value=1.0):
    m = np.zeros((n_rows, n_cols), np.float32)
    m[src0 + np.arange(width), dst0 + np.arange(width)] = value
    return m


def _layout_constants(n_sel_blocks_prompt):
    hd = HEAD_DIM
    scale = hd ** -0.5
    pq = np.stack([_place(LANES, LANES, e * hd, 0, hd) for e in range(2)])
    pk = np.stack([_place(KV_W, LANES, k * hd, 0, hd) for k in range(KV_HEADS)])
    pkt = np.transpose(pk, (0, 2, 1))[:, :SEL_V_ROWS]
    nbs = n_sel_blocks_prompt
    n_tiles = -(-nbs // BLOCKS_PER_TILE)
    pall = np.zeros((nbs, n_tiles * LANES), np.float32)
    j = np.arange(nbs)
    pall[j, (j // BLOCKS_PER_TILE) * LANES + hd + j % BLOCKS_PER_TILE] = 1.0
    pgk = np.stack([_place(KV_W, KV_W, (h % GROUP) * hd, (h // GROUP) * hd, hd, scale) for h in range(N_HEADS)])
    rkg = np.stack([_place(KV_W, KV_W, (h // GROUP) * hd, (h % GROUP) * hd, hd) for h in range(N_HEADS)])
    egs = np.zeros((LANES, N_NSA_BRANCH * N_HEADS * KV_W), np.float32)
    for c in range(N_NSA_BRANCH * N_HEADS):
        egs[c, c * KV_W:(c + 1) * KV_W] = 1.0
    as_bf = lambda a: jnp.asarray(a, BF)
    return dict(pq=as_bf(pq), pk=as_bf(pk), pkt=as_bf(pkt), pall=as_bf(pall),
                pgk=as_bf(pgk), rkg=as_bf(rkg), egs=as_bf(egs))


def _block_diag(w, per_group):
    nb, c, _ = w.shape
    eye = jnp.eye(per_group, dtype=w.dtype)
    wg = w.reshape(nb // per_group, per_group, c, c)
    return jnp.einsum('gpcd,pq->gpcqd', wg, eye).reshape(nb // per_group, per_group * c, per_group * c)


def _compress_weights(w1, w2, pe):
    eye = jnp.eye(KV_HEADS, dtype=w1.dtype)
    big = jnp.einsum('lde,kq->lkdqe', w1, eye).reshape(CMP_BLOCK * KV_W, KV_HEADS * w1.shape[2])
    half = big.shape[0] // 2
    w1cat = jnp.concatenate([big[:half], big[half:]], axis=1).astype(BF)
    w2bd = _block_diag(jnp.broadcast_to(w2, (KV_HEADS,) + w2.shape), KV_HEADS)[0].astype(BF)
    pe_flat = jnp.broadcast_to(pe[:, None, :], (CMP_BLOCK, KV_HEADS, HEAD_DIM)).reshape(2, half)
    pe2 = jnp.tile(pe_flat, (SUBLANES // 2, 1))
    return w1cat, w2bd, pe2


def _compress_weights_paged(w1, w2, pe):
    per_half = LANES // HEAD_DIM
    eye = jnp.eye(per_half, dtype=w1.dtype)
    w1h = jnp.einsum('lde,kq->lkdqe', w1, eye).reshape(CMP_BLOCK, LANES, LANES)
    w1p = w1h.reshape(CMP_BLOCK // 2, 2 * LANES, LANES).astype(BF)
    w2p = jnp.einsum('ed,kq->keqd', w2, eye).reshape(LANES, LANES).astype(BF)
    pe_tok = jnp.tile(pe, (PAGE_SIZE // CMP_BLOCK, KV_HEADS))
    return w1p, w2p, pe_tok


def kernel(x_prompt, x_sample, cache_k_cmp, cache_v_cmp, cache_k_sel, cache_v_sel, page_table,
           state_win_k, state_win_v, state_conv, state_h,
           ln1_g, ln1_b, w_ffn1_up, w_ffn1_down, w_in, w_conv, b_conv, w_rg_a, b_rg_a, w_rg_x, b_rg_x,
           rg_lambda, cmp_pe, w_ck1, w_ck2, w_cv1, w_cv2, w_rec_o, w_attn_o, w_out,
           ln2_g, ln2_b, w_ffn2_up, w_ffn2_down, ln3_g, ln3_b):
    bp, t, d = x_prompt.shape
    bs, n_tok, _ = x_sample.shape
    d_rnn = w_conv.shape[1]
    assert bp == 1 and t % (128 * CMP_ROW) == 0 and t >= WINDOW
    past = page_table.shape[1] * PAGE_SIZE
    assert page_table.shape[1] % 16 == 0 and n_tok <= SUBLANES and n_tok >= CONV_W - 1
    assert state_win_k.shape[1] == WINDOW

    o_q = 2 * d_rnn
    o_kv = o_q + ATTN_W
    o_gn = o_kv + 6 * KV_W
    o_gr = o_gn + N_NSA_BRANCH * N_HEADS
    o_ga = o_gr + d
    w_in_b = w_in.astype(BF)
    w_rnn = w_in_b[:, :o_q]
    w_q = w_in_b[:, o_q:o_kv]
    w_kv = w_in_b[:, o_kv:o_gn]
    w_gn = jnp.pad(w_in_b[:, o_gn:o_gr], ((0, 0), (0, LANES - (o_gr - o_gn))))
    w_grec = w_in_b[:, o_gr:o_ga]
    w_gatt = w_in_b[:, o_ga:]
    w_attn_all = jnp.concatenate([w_in_b[:, :o_gn], jnp.pad(w_gn, ((0, 0), (0, 512 - LANES)))], axis=1)
    up1, down1 = w_ffn1_up.astype(BF), w_ffn1_down.astype(BF)
    up2, down2 = w_ffn2_up.astype(BF), w_ffn2_down.astype(BF)
    per_group = 2 * LANES // (d_rnn // RNN_BLOCKS)
    wa_bd = _block_diag(w_rg_a, per_group).astype(BF)
    wx_bd = _block_diag(w_rg_x, per_group).astype(BF)
    ck1, ck2, pe2 = _compress_weights(w_ck1, w_ck2, cmp_pe)
    cv1, cv2, _ = _compress_weights(w_cv1, w_cv2, cmp_pe)
    w_rec_b, w_attn_b, w_out_b = w_rec_o.astype(BF), w_attn_o.astype(BF), w_out.astype(BF)
    cst = _layout_constants(t // SEL_BLOCK)

    xp = _ffn_ln(x_prompt.reshape(t, d), up1, down1, ln1_g, ln1_b)
    xrg = _matmul(xp, w_rnn, 512)
    (p_kc, p_vc, kct, vct, kst, vst32, kwt, vwt32, p_gn, qa, ksa, vst, kwa, vwt) = _proj_attn(
        xp, w_q, w_kv, w_gn, cst['pq'], cst['pk'], cst['pkt'])
    y_rec, p_tail, p_h = _rglru_prompt(xrg, w_conv, b_conv, wa_bd, wx_bd, b_rg_a, b_rg_x, rg_lambda)
    nbs = t // SEL_BLOCK
    kc = _compress_prompt(p_kc, pe2, ck1, ck2, cst['pk']).reshape(KV_HEADS, 2 * nbs, LANES)
    vc = _compress_prompt(p_vc, pe2, cv1, cv2, cst['pk']).reshape(KV_HEADS, 2 * nbs, LANES)
    o_attn_t = _nsa_prompt(qa, kc, vc, ksa, vst, kwa, vwt, p_gn, cst['pall'])
    x2 = _merge(xp, y_rec, o_attn_t, w_grec, w_gatt, w_rec_b, w_attn_b, w_out_b, ln2_g, ln2_b, True)
    y_prompt = _ffn_ln(x2, up2, down2, ln3_g, ln3_b).reshape(bp, t, d)

    kvh = lambda a: jnp.transpose(a.reshape(bp, KV_HEADS, HEAD_DIM, -1), (0, 3, 1, 2))
    p_states = (kvh(kct), kvh(vct), kvh(kst), kvh(vst32), kvh(kwt[:, t - WINDOW:]), kvh(vwt32[:, t - WINDOW:]),
                p_tail[SUBLANES - (CONV_W - 1):].reshape(bp, CONV_W - 1, d_rnn), p_h.reshape(bp, d_rnn))

    m_s = bs * n_tok
    xs = _ffn_ln(x_sample.reshape(m_s, d), up1, down1, ln1_g, ln1_b)
    zs = _matmul(xs, w_attn_all, 512)
    s_xrg = zs[:, :o_q].reshape(bs, n_tok * o_q)
    seq = lambda a: a.reshape(bs, n_tok, a.shape[-1])
    s_q = seq(zs[:, o_q:o_kv])
    s_kv = [seq(zs[:, o_kv + n * KV_W:o_kv + (n + 1) * KV_W]) for n in range(6)]
    s_gn = seq(zs[:, o_gn:o_gn + LANES])
    pad8 = lambda a: jnp.pad(a, ((0, 0), (0, SUBLANES - n_tok), (0, 0)))
    ys_rec, s_conv, s_h = _rglru_sample(
        s_xrg, state_conv.reshape(bs, -1), state_h, w_conv, b_conv, wa_bd, wx_bd, b_rg_a, b_rg_x, rg_lambda,
        start0=(past == 0))
    nbs_s = past // SEL_BLOCK
    ck1p, ck2p, pe_tok = _compress_weights_paged(w_ck1, w_ck2, cmp_pe)
    cv1p, cv2p, _ = _compress_weights_paged(w_cv1, w_cv2, cmp_pe)
    kc_s, vc_s = _compress_paged(cache_k_cmp, cache_v_cmp, page_table, pe_tok, ck1p, ck2p, cv1p, cv2p)
    o8 = _nsa_sample(page_table, pad8(s_q), kc_s, vc_s, cache_k_sel, cache_v_sel,
                     [pad8(s_kv[n]) for n in (2, 3, 4, 5)],
                     state_win_k.reshape(bs, WINDOW, KV_W), state_win_v.reshape(bs, WINDOW, KV_W),
                     pad8(s_gn), cst['pgk'], cst['rkg'], cst['egs'], n_tok)
    del nbs_s
    os_attn = o8[:, :n_tok].reshape(m_s, ATTN_W).astype(BF)
    x2s = _merge(xs, ys_rec.reshape(m_s, d_rnn), os_attn, w_grec, w_gatt, w_rec_b, w_attn_b, w_out_b, ln2_g, ln2_b,
                 False)
    y_sample = _ffn_ln(x2s, up2, down2, ln3_g, ln3_b).reshape(bs, n_tok, d)

    kvs = lambda a: a.reshape(bs, n_tok, KV_HEADS, HEAD_DIM)
    win = lambda old, new: jnp.concatenate([old, kvs(new)], axis=1)[:, -WINDOW:]
    s_states = (kvs(s_kv[0]), kvs(s_kv[1]), kvs(s_kv[2]), kvs(s_kv[3]),
                win(state_win_k, s_kv[4]), win(state_win_v, s_kv[5]),
                s_conv.reshape(bs, CONV_W - 1, d_rnn), s_h)

    return (y_prompt, y_sample) + p_states + s_states
```

```python
import functools

import numpy as np
import jax
import jax.numpy as jnp
from jax import lax
from jax.experimental import pallas as pl
from jax.experimental.pallas import tpu as pltpu

F32 = jnp.float32
BF = jnp.bfloat16

DEPTH = 1
ALPHA = (2.0 * DEPTH) ** 0.25
N_HEADS = 16
HEAD_DIM = 64
KV_HEADS = 4
GROUP = N_HEADS // KV_HEADS
KV_W = KV_HEADS * HEAD_DIM
ATTN_W = N_HEADS * HEAD_DIM
N_NSA_BRANCH = 3
CMP_BLOCK = 32
SEL_BLOCK = 64
N_SEL = 16
WINDOW = 512
Q_BLOCK = 256
CONV_W = 4
LRU_C = 8.0
RNN_BLOCKS = 16
LN_EPS = 1e-5
PAGE_SIZE = 128

LANES = 128
SUBLANES = 8
VMEM_LIMIT = 56 * 1024 * 1024

NEG = -1e30
SEL_TILE = 512
BLOCKS_PER_TILE = SEL_TILE // SEL_BLOCK
SEL_V_ROWS = HEAD_DIM + 16
CMP_ROW = 16
CMP_ROW_W = CMP_ROW * KV_W

_NT = (((1,), (1,)), ((), ()))


def _dot(a, b):
    return jnp.dot(a, b, preferred_element_type=F32)


def _dot_nt(a, b):
    return lax.dot_general(a, b, _NT, preferred_element_type=F32)


def _params(*sem):
    return pltpu.CompilerParams(dimension_semantics=sem, vmem_limit_bytes=VMEM_LIMIT)


def _layer_norm(y, g, b):
    mu = jnp.mean(y, axis=-1, keepdims=True)
    d = y - mu
    var = jnp.mean(d * d, axis=-1, keepdims=True)
    return d * lax.rsqrt(var + LN_EPS) * g + b


def _masked_softmax(s, mask):
    s = jnp.where(mask, s, -jnp.inf)
    m = jnp.max(s, axis=-1, keepdims=True)
    m = jnp.where(m == -jnp.inf, 0.0, m)
    e = jnp.where(mask, jnp.exp(s - m), 0.0)
    den = jnp.sum(e, axis=-1, keepdims=True)
    return e / jnp.maximum(den, 1e-30)


def _split3(x):
    hi = x.astype(BF)
    r1 = x - hi.astype(F32)
    mid = r1.astype(BF)
    lo = (r1 - mid.astype(F32)).astype(BF)
    return hi, mid, lo


def _expand_f32(x, e):
    hi, mid, lo = _split3(x)
    return _dot(hi, e) + _dot(mid, e) + _dot(lo, e)


def _ffn_kernel(x_ref, wg_ref, wu_ref, wd_ref, g_ref, b_ref, o_ref, xb_ref, acc_ref):
    j = pl.program_id(1)

    @pl.when(j == 0)
    def _():
        xb_ref[...] = x_ref[...].astype(BF)
        acc_ref[...] = jnp.zeros_like(acc_ref)

    xb = xb_ref[...]
    gate = _dot(xb, wg_ref[...])
    up = _dot(xb, wu_ref[...])
    h = (gate * jax.nn.sigmoid(gate) * up).astype(BF)
    acc_ref[...] += _dot(h, wd_ref[...])

    @pl.when(j == pl.num_programs(1) - 1)
    def _():
        y = ALPHA * x_ref[...] + 0.5 * acc_ref[...]
        o_ref[...] = _layer_norm(y, g_ref[...], b_ref[...])


def _ffn_ln(x, w_up, w_down, g, b):
    m, d = x.shape
    f = w_down.shape[0]
    tm = min(512, m)
    tn = 512
    nj = f // tn
    return pl.pallas_call(
        _ffn_kernel,
        grid=(m // tm, nj),
        in_specs=[
            pl.BlockSpec((tm, d), lambda i, j: (i, 0)),
            pl.BlockSpec((d, tn), lambda i, j: (0, j)),
            pl.BlockSpec((d, tn), lambda i, j: (0, j + nj)),
            pl.BlockSpec((tn, d), lambda i, j: (j, 0)),
            pl.BlockSpec((1, d), lambda i, j: (0, 0)),
            pl.BlockSpec((1, d), lambda i, j: (0, 0)),
        ],
        out_specs=pl.BlockSpec((tm, d), lambda i, j: (i, 0)),
        out_shape=jax.ShapeDtypeStruct((m, d), F32),
        scratch_shapes=[pltpu.VMEM((tm, d), BF), pltpu.VMEM((tm, d), F32)],
        compiler_params=_params("parallel", "arbitrary"),
        name="ffn_ln",
    )(x, w_up, w_up, w_down, g.reshape(1, d), b.reshape(1, d))


def _mm_kernel(x_ref, w_ref, o_ref):
    o_ref[...] = _dot(x_ref[...].astype(BF), w_ref[...])


def _matmul(x, w, tn):
    m, d = x.shape
    n = w.shape[1]
    tm = min(512, m)
    return pl.pallas_call(
        _mm_kernel,
        grid=(m // tm, n // tn),
        in_specs=[pl.BlockSpec((tm, d), lambda i, j: (i, 0)),
                  pl.BlockSpec((d, tn), lambda i, j: (0, j))],
        out_specs=pl.BlockSpec((tm, tn), lambda i, j: (i, j)),
        out_shape=jax.ShapeDtypeStruct((m, n), F32),
        compiler_params=_params("parallel", "arbitrary"),
        name="proj",
    )(x, w)


def _proj_attn_kernel(x_ref, wq_ref, wkv_ref, wgn_ref, pq_ref, pk_ref, pkt_ref,
                      kc_ref, vc_ref, kct_ref, vct_ref, kst_ref, vst32_ref, kwt_ref, vwt32_ref, gn_ref,
                      qa_ref, ksa_ref, vst_ref, kwa_ref, vwt_ref):
    tm = x_ref.shape[0]
    xb = x_ref[...].astype(BF)
    zq = (_dot(xb, wq_ref[...]) * (HEAD_DIM ** -0.5 * np.log2(np.e))).astype(BF)
    for h in range(N_HEADS):
        pair = zq[:, (h // 2) * LANES:(h // 2 + 1) * LANES]
        qa_ref[h] = _dot(pair, pq_ref[h % 2]).astype(BF)
    zkv = _dot(xb, wkv_ref[...])
    parts = [zkv[:, n * KV_W:(n + 1) * KV_W] for n in range(6)]
    kc_ref[...] = parts[0]
    vc_ref[...] = parts[1]
    for ref, part in zip((kct_ref, vct_ref, kst_ref, vst32_ref, kwt_ref, vwt32_ref), parts):
        ref[...] = part.T
    gn_ref[...] = _dot(xb, wgn_ref[...])
    t = pl.program_id(0) * tm + lax.broadcasted_iota(jnp.int32, (tm, 1), 0)
    lane = lax.broadcasted_iota(jnp.int32, (1, LANES), 1)
    onehot = jnp.where(lane == HEAD_DIM + (t // SEL_BLOCK) % BLOCKS_PER_TILE, 1.0, 0.0)
    ksb = parts[2].astype(BF)
    kwb = parts[4].astype(BF)
    vsb = parts[3].astype(BF)
    vwb = parts[5].astype(BF)
    for k in range(KV_HEADS):
        ksa_ref[k] = (_dot(ksb, pk_ref[k]) + onehot).astype(BF)
        kwa_ref[k] = _dot(kwb, pk_ref[k]).astype(BF)
        ones_row = jnp.where(lax.broadcasted_iota(jnp.int32, (SEL_V_ROWS, 1), 0) == HEAD_DIM, 1.0, 0.0)
        vst_ref[k, 0] = (_dot_nt(pkt_ref[k], vsb) + ones_row).astype(BF)
        vwt_ref[k] = (_dot_nt(pkt_ref[k], vwb) + ones_row).astype(BF)


def _proj_attn(x, wq, wkv, wgn, pq, pk, pkt):
    m, d = x.shape
    tm = 256
    per_tile = SEL_TILE // tm
    row = lambda w: pl.BlockSpec((tm, w), lambda i: (i, 0))
    const = lambda a: pl.BlockSpec(a.shape, lambda i: (0,) * a.ndim)
    head = lambda n: pl.BlockSpec((n, tm, LANES), lambda i: (0, i, 0))
    f32 = lambda w: jax.ShapeDtypeStruct((m, w), F32)
    aug = lambda n: jax.ShapeDtypeStruct((n, m, LANES), BF)
    return pl.pallas_call(
        _proj_attn_kernel,
        grid=(m // tm,),
        in_specs=[row(d), const(wq), const(wkv), const(wgn), const(pq), const(pk), const(pkt)],
        out_specs=[row(KV_W)] * 2 + [pl.BlockSpec((KV_W, tm), lambda i: (0, i))] * 6
                  + [row(LANES), head(N_HEADS), head(KV_HEADS),
                                     pl.BlockSpec((KV_HEADS, 1, SEL_V_ROWS, tm),
                                                  lambda i: (0, i // per_tile, 0, i % per_tile)),
                                     head(KV_HEADS),
                                     pl.BlockSpec((KV_HEADS, SEL_V_ROWS, tm), lambda i: (0, 0, i))],
        out_shape=[f32(KV_W)] * 2 + [jax.ShapeDtypeStruct((KV_W, m), F32)] * 6
                  + [f32(LANES), aug(N_HEADS), aug(KV_HEADS),
                                     jax.ShapeDtypeStruct((KV_HEADS, m // SEL_TILE, SEL_V_ROWS, SEL_TILE), BF),
                                     aug(KV_HEADS),
                                     jax.ShapeDtypeStruct((KV_HEADS, SEL_V_ROWS, m), BF)],
        compiler_params=_params("parallel"),
        name="proj_attn",
    )(x, wq, wkv, wgn, pq, pk, pkt)


def _rglru_gates(xc, wa, wx, ba, bx, lam, is_start):
    xcb = xc.astype(BF)
    r = jax.nn.sigmoid(_dot(xcb, wa) + ba)
    ig = jax.nn.sigmoid(_dot(xcb, wx) + bx)
    log_a = -LRU_C * r * jax.nn.softplus(-lam)
    a = jnp.exp(log_a)
    if is_start is True:
        return a, ig * xc
    th = jnp.tanh(log_a)
    mult = jnp.sqrt(-2.0 * th / (1.0 - th))
    if is_start is not None:
        mult = jnp.where(is_start, 1.0, mult)
    return a, mult * (ig * xc)


def _rglru_prompt_kernel(x_ref, gate_ref, wc_ref, bc_ref, wa_ref, wx_ref, ba_ref, bx_ref, lam_ref,
                         y_ref, tail_ref, hl_ref, h_scr, tail_scr):
    c = pl.program_id(1)
    tc = x_ref.shape[0]

    @pl.when(c == 0)
    def _():
        h_scr[...] = jnp.zeros_like(h_scr)
        tail_scr[...] = jnp.zeros_like(tail_scr)

    x = x_ref[...]
    prev = tail_scr[...]
    row8 = lax.broadcasted_iota(jnp.int32, (SUBLANES, 1), 0)
    row = lax.broadcasted_iota(jnp.int32, (tc, 1), 0)

    def shifted(s):
        rolled = pltpu.roll(x, s, 0)
        top = jnp.where(row8 < s, pltpu.roll(prev, s, 0), rolled[:SUBLANES])
        return jnp.concatenate([top, rolled[SUBLANES:]], axis=0)

    wc = wc_ref[...]
    conv = wc[0:1] * shifted(3)
    conv = conv + wc[1:2] * shifted(2)
    conv = conv + wc[2:3] * shifted(1)
    conv = conv + wc[3:4] * x
    xc = bc_ref[...] + conv

    is_start = (row + c * tc) == 0
    a, u = _rglru_gates(xc, wa_ref[0], wx_ref[0], ba_ref[...], bx_ref[...], lam_ref[...], is_start)

    d = 1
    while d < tc:
        keep = row >= d
        a_sh = jnp.where(keep, pltpu.roll(a, d, 0), 1.0)
        u_sh = jnp.where(keep, pltpu.roll(u, d, 0), 0.0)
        u = a * u_sh + u
        a = a * a_sh
        d *= 2
    h = a * h_scr[...] + u

    y_ref[...] = (jax.nn.gelu(gate_ref[...]) * h).astype(BF)
    h_last = h[tc - 1:tc]
    h_scr[...] = h_last
    tail_scr[...] = x[tc - SUBLANES:]
    hl_ref[...] = h_last
    tail_ref[...] = x[tc - SUBLANES:]


def _rglru_prompt(xrg, w_conv, b_conv, wa_bd, wx_bd, b_a, b_x, lam):
    t = xrg.shape[0]
    d_rnn = w_conv.shape[1]
    gw = 2 * LANES
    ng = d_rnn // gw
    tc = min(512, t)
    vec = lambda: pl.BlockSpec((1, gw), lambda g, c: (0, g))
    return pl.pallas_call(
        _rglru_prompt_kernel,
        grid=(ng, t // tc),
        in_specs=[
            pl.BlockSpec((tc, gw), lambda g, c: (c, g)),
            pl.BlockSpec((tc, gw), lambda g, c: (c, g + ng)),
            pl.BlockSpec((CONV_W, gw), lambda g, c: (0, g)),
            vec(),
            pl.BlockSpec((1, gw, gw), lambda g, c: (g, 0, 0)),
            pl.BlockSpec((1, gw, gw), lambda g, c: (g, 0, 0)),
            vec(), vec(), vec(),
        ],
        out_specs=[
            pl.BlockSpec((tc, gw), lambda g, c: (c, g)),
            pl.BlockSpec((SUBLANES, gw), lambda g, c: (0, g)),
            pl.BlockSpec((1, gw), lambda g, c: (0, g)),
        ],
        out_shape=[
            jax.ShapeDtypeStruct((t, d_rnn), BF),
            jax.ShapeDtypeStruct((SUBLANES, d_rnn), F32),
            jax.ShapeDtypeStruct((1, d_rnn), F32),
        ],
        scratch_shapes=[pltpu.VMEM((1, gw), F32), pltpu.VMEM((SUBLANES, gw), F32)],
        compiler_params=_params("parallel", "arbitrary"),
        name="rglru_prompt",
    )(xrg, xrg, w_conv, b_conv.reshape(1, -1), wa_bd, wx_bd,
      b_a.reshape(1, -1), b_x.reshape(1, -1), lam.reshape(1, -1))


def _rglru_sample_kernel(xrg_ref, cp_ref, h0_ref, wc_ref, bc_ref, wa_ref, wx_ref, ba_ref, bx_ref, lam_ref,
                         y_ref, cn_ref, hl_ref, *, n_tok, start0):
    d_rnn = h0_ref.shape[1]
    gw = wa_ref.shape[1]
    wc = wc_ref[...]
    xp = [cp_ref[:, k * d_rnn:(k + 1) * d_rnn] for k in range(CONV_W - 1)]
    xp += [xrg_ref[:, t * 2 * d_rnn:t * 2 * d_rnn + d_rnn] for t in range(n_tok)]
    h = h0_ref[...]
    for t in range(n_tok):
        conv = wc[0:1] * xp[t]
        for k in range(1, CONV_W):
            conv = conv + wc[k:k + 1] * xp[t + k]
        xc = bc_ref[...] + conv
        a_parts, u_parts = [], []
        for g in range(d_rnn // gw):
            sl = slice(g * gw, (g + 1) * gw)
            a_g, u_g = _rglru_gates(xc[:, sl], wa_ref[g], wx_ref[g], ba_ref[:, sl], bx_ref[:, sl],
                                    lam_ref[:, sl], True if (start0 and t == 0) else None)
            a_parts.append(a_g)
            u_parts.append(u_g)
        a = jnp.concatenate(a_parts, axis=1)
        u = jnp.concatenate(u_parts, axis=1)
        h = a * h + u
        gate = xrg_ref[:, t * 2 * d_rnn + d_rnn:(t + 1) * 2 * d_rnn]
        y_ref[:, t * d_rnn:(t + 1) * d_rnn] = (jax.nn.gelu(gate) * h).astype(BF)
    hl_ref[...] = h
    tail = xp[-(CONV_W - 1):]
    for k in range(CONV_W - 1):
        cn_ref[:, k * d_rnn:(k + 1) * d_rnn] = tail[k]


def _rglru_sample(xrg, conv_prev, h0, w_conv, b_conv, wa_bd, wx_bd, b_a, b_x, lam, start0):
    b, d_rnn = h0.shape
    n_tok = xrg.shape[1] // (2 * d_rnn)
    args = (xrg, conv_prev, h0, w_conv, b_conv.reshape(1, -1), wa_bd, wx_bd,
            b_a.reshape(1, -1), b_x.reshape(1, -1), lam.reshape(1, -1))
    full = lambda a: pl.BlockSpec(a.shape, lambda i: (0,) * a.ndim)
    outs = [jax.ShapeDtypeStruct((b, n_tok * d_rnn), BF),
            jax.ShapeDtypeStruct((b, (CONV_W - 1) * d_rnn), F32),
            jax.ShapeDtypeStruct((b, d_rnn), F32)]
    return pl.pallas_call(
        functools.partial(_rglru_sample_kernel, n_tok=n_tok, start0=start0),
        grid=(1,),
        in_specs=[full(a) for a in args],
        out_specs=[full(o) for o in outs],
        out_shape=outs,
        compiler_params=_params("arbitrary"),
        name="rglru_sample",
    )(*args)


def _compress_kernel(x_ref, pe_ref, w1_ref, w2_ref, pk_ref, o_ref, scr):
    x = x_ref[...]
    rows = x.shape[0]
    xb = (x.reshape(rows // SUBLANES, SUBLANES, CMP_ROW_W) + pe_ref[...][None]).reshape(rows, CMP_ROW_W)
    full = _dot(xb.astype(BF), w1_ref[...])
    hid = full[:, :KV_W] + pltpu.roll(full[:, KV_W:], rows - 1, 0)
    out = _dot(jax.nn.gelu(hid).astype(BF), w2_ref[...])
    nb = rows // 4
    ob = out.astype(BF)
    for k in range(KV_HEADS):
        scr[...] = _dot(ob, pk_ref[k])
        o_ref[k, 0] = scr[pl.ds(0, nb, stride=4), :].astype(BF)
        o_ref[k, 1] = scr[pl.ds(2, nb, stride=4), :].astype(BF)


def _compress_prompt(kv, pe2, w1, w2, pk):
    t = kv.shape[0]
    rows = 128
    x = kv.reshape(t // CMP_ROW, CMP_ROW_W)
    nb = rows // 4
    nbs = t // SEL_BLOCK
    const = lambda a: pl.BlockSpec(a.shape, lambda i: (0,) * a.ndim)
    return pl.pallas_call(
        _compress_kernel,
        grid=(x.shape[0] // rows,),
        in_specs=[pl.BlockSpec((rows, CMP_ROW_W), lambda i: (i, 0)),
                  const(pe2), const(w1), const(w2), const(pk)],
        out_specs=pl.BlockSpec((KV_HEADS, 2, nb, LANES), lambda i: (0, 0, i, 0)),
        out_shape=jax.ShapeDtypeStruct((KV_HEADS, 2, nbs, LANES), BF),
        scratch_shapes=[pltpu.VMEM((rows, LANES), F32)],
        compiler_params=_params("parallel"),
        name="compress_prompt",
    )(x, pe2, w1, w2, pk)


def _compress_paged_kernel(pt_ref, *refs, pages):
    del pt_ref
    pe_ref = refs[2 * pages]
    wk1_ref, wk2_ref, wv1_ref, wv2_ref, ok_ref, ov_ref, xk_scr, xv_scr, outk_scr, outv_scr = refs[2 * pages + 1:]
    _compress_pages(refs[:pages], pe_ref, wk1_ref, wk2_ref, ok_ref, xk_scr, outk_scr)
    _compress_pages(refs[pages:2 * pages], pe_ref, wv1_ref, wv2_ref, ov_ref, xv_scr, outv_scr)


def _compress_pages(x_refs, pe_ref, w1_ref, w2_ref, o_ref, xs_scr, out_scr):
    pages = len(x_refs)
    halves = KV_W // LANES
    for p in range(pages):
        xt = x_refs[p][0].T + pe_ref[...]
        for h in range(halves):
            xs_scr[h, p * PAGE_SIZE:(p + 1) * PAGE_SIZE, :] = xt[:, h * LANES:(h + 1) * LANES]
    nblk = pages * (PAGE_SIZE // CMP_BLOCK)
    by_tok = [pltpu.einshape("mld->lmd", xs_scr[h].reshape(nblk, CMP_BLOCK, LANES)) for h in range(halves)]
    acc = None
    for l2 in range(CMP_BLOCK // 2):
        parts = [jnp.concatenate([by_tok[h][2 * l2], by_tok[h][2 * l2 + 1]], axis=1) for h in range(halves)]
        d = _dot(jnp.concatenate(parts, axis=0).astype(BF), w1_ref[l2])
        acc = d if acc is None else acc + d
    out_scr[...] = _dot(jax.nn.gelu(acc).astype(BF), w2_ref[...])
    for h in range(halves):
        for par in range(2):
            o_ref[0, par, :, h * LANES:(h + 1) * LANES] = (
                out_scr[pl.ds(h * nblk + par, nblk // 2, stride=2), :].astype(BF))


def _compress_paged(cache_k, cache_v, page_table, pe_tok, wk1, wk2, wv1, wv2):
    b, n_pages = page_table.shape
    pages = 16
    view = lambda c: jnp.transpose(c, (0, 2, 3, 1)).reshape(c.shape[0], KV_W, PAGE_SIZE)
    xk, xv = view(cache_k), view(cache_v)
    nblk = pages * (PAGE_SIZE // CMP_BLOCK)
    halves = KV_W // LANES
    nbs = n_pages * PAGE_SIZE // SEL_BLOCK
    const = lambda a: pl.BlockSpec(a.shape, lambda i, s, pt: (0,) * a.ndim)
    page_specs = [
        pl.BlockSpec((1, KV_W, PAGE_SIZE),
                     functools.partial(lambda i, s, pt, p: (pt[i, s * pages + p], 0, 0), p=p))
        for p in range(pages)]
    return pl.pallas_call(
        functools.partial(_compress_paged_kernel, pages=pages),
        grid_spec=pltpu.PrefetchScalarGridSpec(
            num_scalar_prefetch=1,
            grid=(b, n_pages // pages),
            in_specs=page_specs + page_specs + [const(a) for a in (pe_tok, wk1, wk2, wv1, wv2)],
            out_specs=[pl.BlockSpec((1, 2, nblk // 2, KV_W), lambda i, s, pt: (i, 0, s, 0))] * 2,
            scratch_shapes=[pltpu.VMEM((halves, pages * PAGE_SIZE, LANES), F32)] * 2
                           + [pltpu.VMEM((halves * nblk, LANES), F32)] * 2,
        ),
        out_shape=[jax.ShapeDtypeStruct((b, 2, nbs, KV_W), BF)] * 2,
        compiler_params=_params("parallel", "arbitrary"),
        name="compress_paged",
    )(page_table, *([xk] * pages), *([xv] * pages), pe_tok, wk1, wk2, wv1, wv2)


def _topk_columns(imp_t, n_pick, quota=None):
    nblk = imp_t.shape[0]
    idx = lax.broadcasted_iota(jnp.int32, imp_t.shape, 0)
    sel = jnp.zeros(imp_t.shape, F32)
    for n in range(n_pick):
        mx = jnp.max(imp_t, axis=0, keepdims=True)
        first = jnp.min(jnp.where(imp_t == mx, idx, nblk), axis=0, keepdims=True)
        if quota is not None:
            first = jnp.where(n < quota, first, nblk)
        hit = idx == first
        sel = jnp.where(hit, 1.0, sel)
        imp_t = jnp.where(hit, -jnp.inf, imp_t)
    return sel


def _softmax_cols(s):
    m = jnp.max(s, axis=0, keepdims=True)
    m = jnp.where(m == -jnp.inf, 0.0, m)
    e = jnp.exp2(s - m)
    den = jnp.sum(e, axis=0, keepdims=True)
    return e * (1.0 / jnp.maximum(den, 1e-30))


def _nsa_prompt_kernel(*refs):
    (q_ref, kc_ref, vc_ref, ks_ref, vst_ref) = refs[:5]
    wb = WINDOW // Q_BLOCK + 1
    kw_refs = refs[5:5 + wb]
    vwt_refs = refs[5 + wb:5 + 2 * wb]
    (gn_ref, pa_ref, o_ref,
     bias_scr, gate_scr, sa_scr, sb_scr, m_scr, acc_scr, oc_scr, ow_scr) = refs[5 + 2 * wb:]
    kvh = pl.program_id(0)
    i = pl.program_id(1)
    nq = Q_BLOCK
    r = GROUP * nq
    hd = HEAD_DIM
    q = q_ref[...].reshape(r, LANES)
    lane = lax.broadcasted_iota(jnp.int32, (1, r), 1)
    qpos = i * nq + lane % nq

    nbs = kc_ref.shape[1] // 2

    def window_and_gates():
        kw = jnp.concatenate([ref[0] for ref in kw_refs], axis=0)
        vwt = jnp.concatenate([ref[0] for ref in vwt_refs], axis=1)
        s_w = _dot_nt(kw, q)
        nwb = len(kw_refs)
        krow = lax.broadcasted_iota(jnp.int32, (nq, 1), 0)
        parts = []
        for b in range(nwb):
            blk_pos = (i - (nwb - 1) + b) * nq
            mask = blk_pos >= 0
            if b == 0:
                mask = mask & (qpos - (blk_pos + krow) <= WINDOW)
            if b == nwb - 1:
                mask = mask & (blk_pos + krow <= qpos)
            parts.append(jnp.where(mask, s_w[b * nq:(b + 1) * nq], -jnp.inf))
        s_w = jnp.concatenate(parts, axis=0)
        e_w = jnp.exp2(s_w - jnp.max(s_w, axis=0, keepdims=True))
        acc_w = _dot(vwt, e_w.astype(BF))
        ow_scr[...] = acc_w[:hd] * (1.0 / jnp.maximum(acc_w[hd:hd + 1], 1e-30))
        gate_scr[...] = jax.nn.sigmoid(gn_ref[...]).T

    def compressed_and_select(nv, first_quarter):
        window_and_gates()
        kc = jnp.concatenate([kc_ref[0, 0:nv], kc_ref[0, nbs:nbs + nv]], axis=0)
        vc = jnp.concatenate([vc_ref[0, 0:nv], vc_ref[0, nbs:nbs + nv]], axis=0)
        rowc = lax.broadcasted_iota(jnp.int32, (2 * nv, 1), 0)
        blk = 2 * (rowc % nv) + rowc // nv
        p_c = _softmax_cols(jnp.where(blk * CMP_BLOCK + (CMP_BLOCK - 1) <= qpos, _dot_nt(kc, q), -jnp.inf))
        vct = vc.astype(F32).T[:hd].astype(BF)
        oc_scr[...] = _dot(vct, p_c.astype(BF))
        imp = p_c[:, 0:nq]
        for g in range(1, GROUP):
            imp = imp + p_c[:, g * nq:(g + 1) * nq]
        imp = imp[:nv] + imp[nv:]
        qp = i * nq + lax.broadcasted_iota(jnp.int32, (1, nq), 1)
        jb = lax.broadcasted_iota(jnp.int32, (nv, 1), 0)
        cur = qp // SEL_BLOCK
        valid = jb * SEL_BLOCK <= qp
        forced = (jb == 0) | (jb == cur) | (jb == cur - 1)
        n_forced = 1 + jnp.where(cur >= 1, 1, 0) + jnp.where(cur >= 2, 1, 0)
        n_sel = min(N_SEL, nbs)
        n_iter = n_sel - (1 if first_quarter else 3)
        picks = _topk_columns(jnp.where(valid & ~forced, imp, -jnp.inf), n_iter, n_sel - n_forced)
        sel = jnp.where(forced, 1.0, picks)
        bias = jnp.where(valid & (sel > 0.5), 0.0, NEG).T.astype(BF)
        nt = nv // BLOCKS_PER_TILE
        bias_all = _dot(bias, pa_ref[0:nv, 0:nt * LANES]).astype(BF)
        for t in range(nt):
            bias_scr[t] = bias_all[:, t * LANES:(t + 1) * LANES]

    n_var = 4
    variant = (i * n_var) // pl.num_programs(1)
    for v in range(n_var):
        pl.when(variant == v)(functools.partial(compressed_and_select, nbs * (v + 1) // n_var, v == 0))
    o_c = oc_scr[...]

    def scores(t):
        k0 = pl.multiple_of(t * SEL_TILE, SEL_TILE)
        kt = ks_ref[0, pl.ds(k0, SEL_TILE), :]
        qa = q + jnp.concatenate([bias_scr[t]] * GROUP, axis=0)
        return _dot_nt(kt, qa)

    def update(t, s_ref, diagonal):
        vt = vst_ref[0, t]
        cw = 2 * LANES
        for c in range(r // cw):
            sl = slice(c * cw, (c + 1) * cw)
            sg = s_ref[:, sl]
            if diagonal:
                kpos = t * SEL_TILE + lax.broadcasted_iota(jnp.int32, (SEL_TILE, 1), 0)
                sg = jnp.where(kpos <= qpos[:, sl], sg, NEG)
            m_old = m_scr[:, sl]
            m_new = jnp.maximum(m_old, jnp.max(sg, axis=0, keepdims=True))
            p = jnp.exp2(sg - m_new)
            acc_scr[:, sl] = jnp.exp2(m_old - m_new) * acc_scr[:, sl] + _dot(vt, p.astype(BF))
            m_scr[:, sl] = m_new

    def pair(t0):
        sb_scr[...] = scores(t0 + 1)
        update(t0, sa_scr, False)
        sa_scr[...] = scores(t0 + 2)
        update(t0 + 1, sb_scr, False)

    def octet(u, _):
        for n in range(4):
            pair(8 * u + 2 * n)
        return 0

    t_diag = (i * nq) // SEL_TILE
    m_scr[...] = jnp.full(m_scr.shape, NEG, F32)
    acc_scr[...] = jnp.zeros_like(acc_scr)
    sa_scr[...] = scores(0)
    lax.fori_loop(0, t_diag // 8, octet, 0)

    @pl.when(t_diag % 8 >= 4)
    def _():
        pair(t_diag - t_diag % 8)
        pair(t_diag - t_diag % 8 + 2)

    @pl.when(t_diag % 4 >= 2)
    def _():
        pair(t_diag - t_diag % 4)

    @pl.when(t_diag % 2 == 1)
    def _():
        sb_scr[...] = scores(t_diag)
        update(t_diag - 1, sa_scr, False)
        update(t_diag, sb_scr, True)

    @pl.when(t_diag % 2 == 0)
    def _():
        update(t_diag, sa_scr, True)

    o_s = acc_scr[:hd] * (1.0 / acc_scr[hd:hd + 1])

    o_w = ow_scr[...]

    def gate(br):
        return jnp.concatenate(
            [gate_scr[pl.ds(br * N_HEADS + kvh * GROUP + g, 1), :] for g in range(GROUP)], axis=1)

    comb = (gate(0) * o_c + gate(1) * o_s + gate(2) * o_w).astype(BF)
    o_ref[...] = jnp.concatenate([comb[:, g * nq:(g + 1) * nq] for g in range(GROUP)], axis=0)


def _nsa_prompt(qa, kc, vc, ksa, vst, kwa, vwt, gn, pall):
    t = qa.shape[1]
    n_tiles = t // SEL_TILE
    wb = WINDOW // Q_BLOCK + 1
    kv_full = lambda a: pl.BlockSpec((1,) + a.shape[1:], lambda k, i: (k,) + (0,) * (a.ndim - 1))
    wblk = lambda i, m: jnp.maximum(i - (wb - 1) + m, 0)
    win_k = [pl.BlockSpec((1, Q_BLOCK, LANES), functools.partial(lambda k, i, m: (k, wblk(i, m), 0), m=m))
             for m in range(wb)]
    win_v = [pl.BlockSpec((1, SEL_V_ROWS, Q_BLOCK), functools.partial(lambda k, i, m: (k, 0, wblk(i, m)), m=m))
             for m in range(wb)]
    const = lambda a: pl.BlockSpec(a.shape, lambda k, i: (0,) * a.ndim)
    return pl.pallas_call(
        _nsa_prompt_kernel,
        grid=(KV_HEADS, t // Q_BLOCK),
        in_specs=[pl.BlockSpec((GROUP, Q_BLOCK, LANES), lambda k, i: (k, i, 0)),
                  kv_full(kc), kv_full(vc), kv_full(ksa), kv_full(vst)]
                 + win_k + win_v
                 + [pl.BlockSpec((Q_BLOCK, LANES), lambda k, i: (i, 0)), const(pall)],
        out_specs=pl.BlockSpec((KV_W, Q_BLOCK), lambda k, i: (k, i)),
        out_shape=jax.ShapeDtypeStruct((ATTN_W, t), BF),
        scratch_shapes=[pltpu.VMEM((n_tiles, Q_BLOCK, LANES), BF), pltpu.VMEM((LANES, Q_BLOCK), F32),
                        pltpu.VMEM((SEL_TILE, GROUP * Q_BLOCK), F32), pltpu.VMEM((SEL_TILE, GROUP * Q_BLOCK), F32),
                        pltpu.VMEM((1, GROUP * Q_BLOCK), F32),
                        pltpu.VMEM((SEL_V_ROWS, GROUP * Q_BLOCK), F32),
                        pltpu.VMEM((HEAD_DIM, GROUP * Q_BLOCK), F32), pltpu.VMEM((HEAD_DIM, GROUP * Q_BLOCK), F32)],
        compiler_params=_params("parallel", "arbitrary"),
        name="nsa_prompt",
    )(qa, kc, vc, ksa, vst, *([kwa] * wb), *([vwt] * wb), gn, pall)


def _nsa_sample_kernel(*refs, pages, past, n_tok):
    pt_ref = refs[0]
    q_ref, kc_ref, vc_ref = refs[1:4]
    ks_refs = refs[4:4 + pages]
    vs_refs = refs[4 + pages:4 + 2 * pages]
    (ksn_ref, vsn_ref, kwn_ref, vwn_ref, wk_ref, wv_ref, gn_ref,
     pg_ref, rp_ref, eg_ref, o_ref, q_scr, bias_scr, m_scr, l_scr, acc_scr, oc_scr) = refs[4 + 2 * pages:]
    del pt_ref
    s_id = pl.program_id(1)
    tp = SUBLANES
    r = N_HEADS * tp
    row = lax.broadcasted_iota(jnp.int32, (r, 1), 0)
    qpos = past + row % tp
    nbs = bias_scr.shape[1]

    @pl.when(s_id == 0)
    def _():
        qb = q_ref[0].astype(BF)
        for h in range(N_HEADS):
            k = h // GROUP
            q_scr[h * tp:(h + 1) * tp, :] = _dot(qb[:, k * KV_W:(k + 1) * KV_W], pg_ref[h]).astype(BF)
        q = q_scr[...]
        kc = kc_ref[0].reshape(2 * nbs, KV_W)
        vc = vc_ref[0].reshape(2 * nbs, KV_W)
        col = lax.broadcasted_iota(jnp.int32, (1, 2 * nbs), 1)
        blk = 2 * (col % nbs) + col // nbs
        p_c = _masked_softmax(_dot_nt(q, kc), blk * CMP_BLOCK + (CMP_BLOCK - 1) <= qpos)
        oc_scr[...] = _dot(p_c.astype(BF), vc)
        imps = []
        for k in range(KV_HEADS):
            base = k * GROUP * tp
            imp = p_c[base:base + tp]
            for g in range(1, GROUP):
                imp = imp + p_c[base + g * tp:base + (g + 1) * tp]
            imps.append(imp)
        imp = jnp.concatenate(imps, axis=0)
        imp = imp[:, :nbs] + imp[:, nbs:]
        jb = lax.broadcasted_iota(jnp.int32, (1, nbs), 1)
        imp = jnp.where((jb == 0) | (jb == nbs - 1), jnp.inf, imp)
        idx = lax.broadcasted_iota(jnp.int32, imp.shape, 1)
        sel = jnp.zeros(imp.shape, F32)
        for _ in range(min(N_SEL, nbs + 1) - 1):
            mx = jnp.max(imp, axis=1, keepdims=True)
            first = jnp.min(jnp.where(imp == mx, idx, nbs), axis=1, keepdims=True)
            hit = idx == first
            sel = jnp.where(hit, 1.0, sel)
            imp = jnp.where(hit, -jnp.inf, imp)
        bias = jnp.where(sel > 0.5, 0.0, NEG).astype(BF)
        for k in range(KV_HEADS):
            for g in range(GROUP):
                h = k * GROUP + g
                bias_scr[h * tp:(h + 1) * tp, :] = bias[k * tp:(k + 1) * tp]
        m_scr[...] = jnp.full(m_scr.shape, NEG, F32)
        l_scr[...] = jnp.zeros_like(l_scr)
        acc_scr[...] = jnp.zeros_like(acc_scr)

    def online(st, v, v_transposed):
        m = m_scr[...]
        m_new = jnp.maximum(m, jnp.max(st, axis=1, keepdims=True))
        scale = jnp.exp(m - m_new)
        p = jnp.exp(st - m_new)
        l_scr[...] = scale * l_scr[...] + jnp.sum(p, axis=1, keepdims=True)
        pv = _dot_nt(p.astype(BF), v) if v_transposed else _dot(p.astype(BF), v)
        acc_scr[...] = scale * acc_scr[...] + pv
        m_scr[...] = m_new

    q = q_scr[...]
    kt = jnp.concatenate([ref[0].astype(BF) for ref in ks_refs], axis=1)
    vt = jnp.concatenate([ref[0].astype(BF) for ref in vs_refs], axis=1)
    n_keys = kt.shape[1]
    jb = lax.broadcasted_iota(jnp.int32, (nbs, 1), 0)
    key_blk = s_id * (n_keys // SEL_BLOCK) + lax.broadcasted_iota(jnp.int32, (1, n_keys), 1) // SEL_BLOCK
    expand = jnp.where(jb == key_blk, 1.0, 0.0).astype(BF)
    online(_dot(q, kt) + _dot(bias_scr[...], expand), vt, True)

    @pl.when(s_id == pl.num_programs(1) - 1)
    def _():
        zeros = jnp.zeros((LANES - tp, KV_W), BF)
        tok = lax.broadcasted_iota(jnp.int32, (1, LANES), 1)
        ksn = jnp.concatenate([ksn_ref[0].astype(BF), zeros], axis=0)
        vsn = jnp.concatenate([vsn_ref[0].astype(BF), zeros], axis=0)
        online(jnp.where((past + tok <= qpos) & (tok < n_tok), _dot_nt(q, ksn), NEG), vsn, False)
        o_s = acc_scr[...] / l_scr[...]
        kw = jnp.concatenate([wk_ref[0].astype(BF), kwn_ref[0].astype(BF), zeros], axis=0)
        vw = jnp.concatenate([wv_ref[0].astype(BF), vwn_ref[0].astype(BF), zeros], axis=0)
        n_w = kw.shape[0]
        wcol = lax.broadcasted_iota(jnp.int32, (1, n_w), 1)
        kwpos = past - WINDOW + wcol
        dpos = qpos - kwpos
        mask_w = (dpos >= 0) & (dpos <= WINDOW) & (kwpos >= 0) & (wcol < WINDOW + n_tok)
        p_w = _masked_softmax(_dot_nt(q, kw), mask_w)
        o_w = _dot(p_w.astype(BF), vw)
        sg = jax.nn.sigmoid(gn_ref[0])
        gexp = _expand_f32(sg, eg_ref[...])

        def gate(br):
            return jnp.concatenate(
                [gexp[:, (br * N_HEADS + h) * KV_W:(br * N_HEADS + h + 1) * KV_W] for h in range(N_HEADS)],
                axis=0)

        comb = (gate(0) * oc_scr[...] + gate(1) * o_s + gate(2) * o_w).astype(BF)
        for k in range(KV_HEADS):
            out = None
            for g in range(GROUP):
                h = k * GROUP + g
                part = _dot(comb[h * tp:(h + 1) * tp], rp_ref[h])
                out = part if out is None else out + part
            o_ref[0, :, k * KV_W:(k + 1) * KV_W] = out


def _nsa_sample(page_table, q8, kc, vc, cache_ks, cache_vs, new8, win_k, win_v, gn8, pgk, rkg, egate, n_tok):
    b, n_pages = page_table.shape
    past = n_pages * PAGE_SIZE
    pages = 16
    tp = SUBLANES
    r = N_HEADS * tp
    nbs = past // SEL_BLOCK
    ck = jnp.transpose(cache_ks, (0, 2, 3, 1)).reshape(cache_ks.shape[0], KV_W, PAGE_SIZE)
    cv = jnp.transpose(cache_vs, (0, 2, 3, 1)).reshape(cache_vs.shape[0], KV_W, PAGE_SIZE)
    per_seq = lambda a: pl.BlockSpec((1,) + a.shape[1:], lambda i, s, pt: (i,) + (0,) * (a.ndim - 1))
    const = lambda a: pl.BlockSpec(a.shape, lambda i, s, pt: (0,) * a.ndim)
    page_specs = [
        pl.BlockSpec((1, KV_W, PAGE_SIZE),
                     functools.partial(lambda i, s, pt, p: (pt[i, s * pages + p], 0, 0), p=p))
        for p in range(pages)]
    return pl.pallas_call(
        functools.partial(_nsa_sample_kernel, pages=pages, past=past, n_tok=n_tok),
        grid_spec=pltpu.PrefetchScalarGridSpec(
            num_scalar_prefetch=1,
            grid=(b, n_pages // pages),
            in_specs=[per_seq(q8), per_seq(kc), per_seq(vc)] + page_specs + page_specs
                     + [per_seq(a) for a in new8] + [per_seq(win_k), per_seq(win_v), per_seq(gn8),
                                                     const(pgk), const(rkg), const(egate)],
            out_specs=pl.BlockSpec((1, tp, ATTN_W), lambda i, s, pt: (i, 0, 0)),
            scratch_shapes=[pltpu.VMEM((r, KV_W), BF), pltpu.VMEM((r, nbs), BF),
                            pltpu.VMEM((r, 1), F32), pltpu.VMEM((r, 1), F32),
                            pltpu.VMEM((r, KV_W), F32), pltpu.VMEM((r, KV_W), F32)],
        ),
        out_shape=jax.ShapeDtypeStruct((b, tp, ATTN_W), F32),
        compiler_params=_params("parallel", "arbitrary"),
        name="nsa_sample",
    )(page_table, q8, kc, vc, *([ck] * pages), *([cv] * pages), *new8, win_k, win_v, gn8, pgk, rkg, egate)


def _merge_kernel(x_ref, y_ref, o_ref, wgr_ref, wga_ref, wr_ref, wa_ref, wo_ref, g_ref, b_ref,
                  out_ref, xb_ref, acc_ref, *ob_ref):
    j = pl.program_id(1)

    @pl.when(j == 0)
    def _():
        xb_ref[...] = x_ref[...].astype(BF)
        acc_ref[...] = jnp.zeros_like(acc_ref)
        if ob_ref:
            ob_ref[0][...] = o_ref[...].astype(F32).T.astype(BF)

    xb = xb_ref[...]
    rec = _dot(y_ref[...], wr_ref[...])
    att = _dot(ob_ref[0][...] if ob_ref else o_ref[...], wa_ref[...])
    u = jax.nn.sigmoid(_dot(xb, wgr_ref[...])) * rec + jax.nn.sigmoid(_dot(xb, wga_ref[...])) * att
    acc_ref[...] += _dot(u.astype(BF), wo_ref[...])

    @pl.when(j == pl.num_programs(1) - 1)
    def _():
        out_ref[...] = _layer_norm(ALPHA * x_ref[...] + acc_ref[...], g_ref[...], b_ref[...])


def _merge(x, y_rec, o_attn, w_grec, w_gatt, w_rec_o, w_attn_o, w_out, g, b, o_transposed):
    m, d = x.shape
    tm = min(512, m)
    tn = 512
    col = lambda a: pl.BlockSpec((a.shape[0], tn), lambda i, j: (0, j))
    rowb = lambda a: pl.BlockSpec((tm, a.shape[1]), lambda i, j: (i, 0))
    vec = pl.BlockSpec((1, d), lambda i, j: (0, 0))
    if o_transposed:
        o_spec = pl.BlockSpec((o_attn.shape[0], tm), lambda i, j: (0, i))
        o_scratch = [pltpu.VMEM((tm, o_attn.shape[0]), BF)]
    else:
        o_spec, o_scratch = rowb(o_attn), []
    return pl.pallas_call(
        _merge_kernel,
        grid=(m // tm, d // tn),
        in_specs=[rowb(x), rowb(y_rec), o_spec, col(w_grec), col(w_gatt), col(w_rec_o), col(w_attn_o),
                  pl.BlockSpec((tn, d), lambda i, j: (j, 0)), vec, vec],
        out_specs=pl.BlockSpec((tm, d), lambda i, j: (i, 0)),
        out_shape=jax.ShapeDtypeStruct((m, d), F32),
        scratch_shapes=[pltpu.VMEM((tm, d), BF), pltpu.VMEM((tm, d), F32)] + o_scratch,
        compiler_params=_params("parallel", "arbitrary"),
        name="merge_out",
    )(x, y_rec, o_attn, w_grec, w_gatt, w_rec_o, w_attn_o, w_out, g.reshape(1, d), b.reshape(1, d))


def _place(n_rows, n_cols, src0, dst0, width, value=1.0):
    m = np.zeros((n_rows, n_cols), np.float32)
    m[src0 + np.arange(width), dst0 + np.arange(width)] = value
    return m


def _layout_constants(n_sel_blocks_prompt):
    hd = HEAD_DIM
    scale = hd ** -0.5
    pq = np.stack([_place(LANES, LANES, e * hd, 0, hd) for e in range(2)])
    pk = np.stack([_place(KV_W, LANES, k * hd, 0, hd) for k in range(KV_HEADS)])
    pkt = np.transpose(pk, (0, 2, 1))[:, :SEL_V_ROWS]
    nbs = n_sel_blocks_prompt
    n_tiles = -(-nbs // BLOCKS_PER_TILE)
    pall = np.zeros((nbs, n_tiles * LANES), np.float32)
    j = np.arange(nbs)
    pall[j, (j // BLOCKS_PER_TILE) * LANES + hd + j % BLOCKS_PER_TILE] = 1.0
    pgk = np.stack([_place(KV_W, KV_W, (h % GROUP) * hd, (h // GROUP) * hd, hd, scale) for h in range(N_HEADS)])
    rkg = np.stack([_place(KV_W, KV_W, (h // GROUP) * hd, (h % GROUP) * hd, hd) for h in range(N_HEADS)])
    egs = np.zeros((LANES, N_NSA_BRANCH * N_HEADS * KV_W), np.float32)
    for c in range(N_NSA_BRANCH * N_HEADS):
        egs[c, c * KV_W:(c + 1) * KV_W] = 1.0
    as_bf = lambda a: jnp.asarray(a, BF)
    return dict(pq=as_bf(pq), pk=as_bf(pk), pkt=as_bf(pkt), pall=as_bf(pall),
                pgk=as_bf(pgk), rkg=as_bf(rkg), egs=as_bf(egs))


def _block_diag(w, per_group):
    nb, c, _ = w.shape
    eye = jnp.eye(per_group, dtype=w.dtype)
    wg = w.reshape(nb // per_group, per_group, c, c)
    return jnp.einsum('gpcd,pq->gpcqd', wg, eye).reshape(nb // per_group, per_group * c, per_group * c)


def _compress_weights(w1, w2, pe):
    eye = jnp.eye(KV_HEADS, dtype=w1.dtype)
    big = jnp.einsum('lde,kq->lkdqe', w1, eye).reshape(CMP_BLOCK * KV_W, KV_HEADS * w1.shape[2])
    half = big.shape[0] // 2
    w1cat = jnp.concatenate([big[:half], big[half:]], axis=1).astype(BF)
    w2bd = _block_diag(jnp.broadcast_to(w2, (KV_HEADS,) + w2.shape), KV_HEADS)[0].astype(BF)
    pe_flat = jnp.broadcast_to(pe[:, None, :], (CMP_BLOCK, KV_HEADS, HEAD_DIM)).reshape(2, half)
    pe2 = jnp.tile(pe_flat, (SUBLANES // 2, 1))
    return w1cat, w2bd, pe2


def _compress_weights_paged(w1, w2, pe):
    per_half = LANES // HEAD_DIM
    eye = jnp.eye(per_half, dtype=w1.dtype)
    w1h = jnp.einsum('lde,kq->lkdqe', w1, eye).reshape(CMP_BLOCK, LANES, LANES)
    w1p = w1h.reshape(CMP_BLOCK // 2, 2 * LANES, LANES).astype(BF)
    w2p = jnp.einsum('ed,kq->keqd', w2, eye).reshape(LANES, LANES).astype(BF)
    pe_tok = jnp.tile(pe, (PAGE_SIZE // CMP_BLOCK, KV_HEADS))
    return w1p, w2p, pe_tok


def kernel(x_prompt, x_sample, cache_k_cmp, cache_v_cmp, cache_k_sel, cache_v_sel, page_table,
           state_win_k, state_win_v, state_conv, state_h,
           ln1_g, ln1_b, w_ffn1_up, w_ffn1_down, w_in, w_conv, b_conv, w_rg_a, b_rg_a, w_rg_x, b_rg_x,
           rg_lambda, cmp_pe, w_ck1, w_ck2, w_cv1, w_cv2, w_rec_o, w_attn_o, w_out,
           ln2_g, ln2_b, w_ffn2_up, w_ffn2_down, ln3_g, ln3_b):
    bp, t, d = x_prompt.shape
    bs, n_tok, _ = x_sample.shape
    d_rnn = w_conv.shape[1]
    assert bp == 1 and t % (128 * CMP_ROW) == 0 and t >= WINDOW
    past = page_table.shape[1] * PAGE_SIZE
    assert page_table.shape[1] % 16 == 0 and n_tok <= SUBLANES and n_tok >= CONV_W - 1
    assert state_win_k.shape[1] == WINDOW

    o_q = 2 * d_rnn
    o_kv = o_q + ATTN_W
    o_gn = o_kv + 6 * KV_W
    o_gr = o_gn + N_NSA_BRANCH * N_HEADS
    o_ga = o_gr + d
    w_in_b = w_in.astype(BF)
    w_rnn = w_in_b[:, :o_q]
    w_q = w_in_b[:, o_q:o_kv]
    w_kv = w_in_b[:, o_kv:o_gn]
    w_gn = jnp.pad(w_in_b[:, o_gn:o_gr], ((0, 0), (0, LANES - (o_gr - o_gn))))
    w_grec = w_in_b[:, o_gr:o_ga]
    w_gatt = w_in_b[:, o_ga:]
    w_attn_all = jnp.concatenate([w_in_b[:, :o_gn], jnp.pad(w_gn, ((0, 0), (0, 512 - LANES)))], axis=1)
    up1, down1 = w_ffn1_up.astype(BF), w_ffn1_down.astype(BF)
    up2, down2 = w_ffn2_up.astype(BF), w_ffn2_down.astype(BF)
    per_group = 2 * LANES // (d_rnn // RNN_BLOCKS)
    wa_bd = _block_diag(w_rg_a, per_group).astype(BF)
    wx_bd = _block_diag(w_rg_x, per_group).astype(BF)
    ck1, ck2, pe2 = _compress_weights(w_ck1, w_ck2, cmp_pe)
    cv1, cv2, _ = _compress_weights(w_cv1, w_cv2, cmp_pe)
    w_rec_b, w_attn_b, w_out_b = w_rec_o.astype(BF), w_attn_o.astype(BF), w_out.astype(BF)
    cst = _layout_constants(t // SEL_BLOCK)

    xp = _ffn_ln(x_prompt.reshape(t, d), up1, down1, ln1_g, ln1_b)
    xrg = _matmul(xp, w_rnn, 512)
    (p_kc, p_vc, kct, vct, kst, vst32, kwt, vwt32, p_gn, qa, ksa, vst, kwa, vwt) = _proj_attn(
        xp, w_q, w_kv, w_gn, cst['pq'], cst['pk'], cst['pkt'])
    y_rec, p_tail, p_h = _rglru_prompt(xrg, w_conv, b_conv, wa_bd, wx_bd, b_rg_a, b_rg_x, rg_lambda)
    nbs = t // SEL_BLOCK
    kc = _compress_prompt(p_kc, pe2, ck1, ck2, cst['pk']).reshape(KV_HEADS, 2 * nbs, LANES)
    vc = _compress_prompt(p_vc, pe2, cv1, cv2, cst['pk']).reshape(KV_HEADS, 2 * nbs, LANES)
    o_attn_t = _nsa_prompt(qa, kc, vc, ksa, vst, kwa, vwt, p_gn, cst['pall'])
    x2 = _merge(xp, y_rec, o_attn_t, w_grec, w_gatt, w_rec_b, w_attn_b, w_out_b, ln2_g, ln2_b, True)
    y_prompt = _ffn_ln(x2, up2, down2, ln3_g, ln3_b).reshape(bp, t, d)

    kvh = lambda a: jnp.transpose(a.reshape(bp, KV_HEADS, HEAD_DIM, -1), (0, 3, 1, 2))
    p_states = (kvh(kct), kvh(vct), kvh(kst), kvh(vst32), kvh(kwt[:, t - WINDOW:]), kvh(vwt32[:, t - WINDOW:]),
                p_tail[SUBLANES - (CONV_W - 1):].reshape(bp, CONV_W - 1, d_rnn), p_h.reshape(bp, d_rnn))

    m_s = bs * n_tok
    xs = _ffn_ln(x_sample.reshape(m_s, d), up1, down1, ln1_g, ln1_b)
    zs = _matmul(xs, w_attn_all, 512)
    s_xrg = zs[:, :o_q].reshape(bs, n_tok * o_q)
    seq = lambda a: a.reshape(bs, n_tok, a.shape[-1])
    s_q = seq(zs[:, o_q:o_kv])
    s_kv = [seq(zs[:, o_kv + n * KV_W:o_kv + (n + 1) * KV_W]) for n in range(6)]
    s_gn = seq(zs[:, o_gn:o_gn + LANES])
    pad8 = lambda a: jnp.pad(a, ((0, 0), (0, SUBLANES - n_tok), (0, 0)))
    ys_rec, s_conv, s_h = _rglru_sample(
        s_xrg, state_conv.reshape(bs, -1), state_h, w_conv, b_conv, wa_bd, wx_bd, b_rg_a, b_rg_x, rg_lambda,
        start0=(past == 0))
    nbs_s = past // SEL_BLOCK
    ck1p, ck2p, pe_tok = _compress_weights_paged(w_ck1, w_ck2, cmp_pe)
    cv1p, cv2p, _ = _compress_weights_paged(w_cv1, w_cv2, cmp_pe)
    kc_s, vc_s = _compress_paged(cache_k_cmp, cache_v_cmp, page_table, pe_tok, ck1p, ck2p, cv1p, cv2p)
    o8 = _nsa_sample(page_table, pad8(s_q), kc_s, vc_s, cache_k_sel, cache_v_sel,
                     [pad8(s_kv[n]) for n in (2, 3, 4, 5)],
                     state_win_k.reshape(bs, WINDOW, KV_W), state_win_v.reshape(bs, WINDOW, KV_W),
                     pad8(s_gn), cst['pgk'], cst['rkg'], cst['egs'], n_tok)
    del nbs_s
    os_attn = o8[:, :n_tok].reshape(m_s, ATTN_W).astype(BF)
    x2s = _merge(xs, ys_rec.reshape(m_s, d_rnn), os_attn, w_grec, w_gatt, w_rec_b, w_attn_b, w_out_b, ln2_g, ln2_b,
                 False)
    y_sample = _ffn_ln(x2s, up2, down2, ln3_g, ln3_b).reshape(bs, n_tok, d)

    kvs = lambda a: a.reshape(bs, n_tok, KV_HEADS, HEAD_DIM)
    win = lambda old, new: jnp.concatenate([old, kvs(new)], axis=1)[:, -WINDOW:]
    s_states = (kvs(s_kv[0]), kvs(s_kv[1]), kvs(s_kv[2]), kvs(s_kv[3]),
                win(state_win_k, s_kv[4]), win(state_win_v, s_kv[5]),
                s_conv.reshape(bs, CONV_W - 1, d_rnn), s_h)

    return (y_prompt, y_sample) + p_states + s_states
```

```python
import functools

import numpy as np
import jax
import jax.numpy as jnp
from jax import lax
from jax.experimental import pallas as pl
from jax.experimental.pallas import tpu as pltpu

F32 = jnp.float32
BF = jnp.bfloat16

DEPTH = 1
ALPHA = (2.0 * DEPTH) ** 0.25
N_HEADS = 16
HEAD_DIM = 64
KV_HEADS = 4
GROUP = N_HEADS // KV_HEADS
KV_W = KV_HEADS * HEAD_DIM
ATTN_W = N_HEADS * HEAD_DIM
N_NSA_BRANCH = 3
CMP_BLOCK = 32
SEL_BLOCK = 64
N_SEL = 16
WINDOW = 512
Q_BLOCK = 256
CONV_W = 4
LRU_C = 8.0
RNN_BLOCKS = 16
LN_EPS = 1e-5
PAGE_SIZE = 128

LANES = 128
SUBLANES = 8
VMEM_LIMIT = 56 * 1024 * 1024

NEG = -1e30
SEL_TILE = 512
BLOCKS_PER_TILE = SEL_TILE // SEL_BLOCK
SEL_V_ROWS = HEAD_DIM + 16
CMP_ROW = 16
CMP_PAGES = 16
SEL_PAGES = 32
CMP_ROW_W = CMP_ROW * KV_W

_NT = (((1,), (1,)), ((), ()))


def _dot(a, b):
    return jnp.dot(a, b, preferred_element_type=F32)


def _dot_nt(a, b):
    return lax.dot_general(a, b, _NT, preferred_element_type=F32)


def _params(*sem):
    return pltpu.CompilerParams(dimension_semantics=sem, vmem_limit_bytes=VMEM_LIMIT)


def _layer_norm(y, g, b):
    mu = jnp.mean(y, axis=-1, keepdims=True)
    d = y - mu
    var = jnp.mean(d * d, axis=-1, keepdims=True)
    return d * lax.rsqrt(var + LN_EPS) * g + b


def _masked_softmax(s, mask):
    s = jnp.where(mask, s, -jnp.inf)
    m = jnp.max(s, axis=-1, keepdims=True)
    m = jnp.where(m == -jnp.inf, 0.0, m)
    e = jnp.where(mask, jnp.exp(s - m), 0.0)
    den = jnp.sum(e, axis=-1, keepdims=True)
    return e / jnp.maximum(den, 1e-30)


def _split3(x):
    hi = x.astype(BF)
    r1 = x - hi.astype(F32)
    mid = r1.astype(BF)
    lo = (r1 - mid.astype(F32)).astype(BF)
    return hi, mid, lo


def _expand_f32(x, e):
    hi, mid, lo = _split3(x)
    return _dot(hi, e) + _dot(mid, e) + _dot(lo, e)


def _ffn_kernel(x_ref, wg_ref, wu_ref, wd_ref, g_ref, b_ref, o_ref, xb_ref, acc_ref):
    j = pl.program_id(1)

    @pl.when(j == 0)
    def _():
        xb_ref[...] = x_ref[...].astype(BF)
        acc_ref[...] = jnp.zeros_like(acc_ref)

    xb = xb_ref[...]
    gate = _dot(xb, wg_ref[...])
    up = _dot(xb, wu_ref[...])
    h = (gate * jax.nn.sigmoid(gate) * up).astype(BF)
    acc_ref[...] += _dot(h, wd_ref[...])

    @pl.when(j == pl.num_programs(1) - 1)
    def _():
        y = ALPHA * x_ref[...] + 0.5 * acc_ref[...]
        o_ref[...] = _layer_norm(y, g_ref[...], b_ref[...])


def _ffn_ln(x, w_up, w_down, g, b):
    m, d = x.shape
    f = w_down.shape[0]
    tm = min(512, m)
    tn = 512
    nj = f // tn
    return pl.pallas_call(
        _ffn_kernel,
        grid=(m // tm, nj),
        in_specs=[
            pl.BlockSpec((tm, d), lambda i, j: (i, 0)),
            pl.BlockSpec((d, tn), lambda i, j: (0, j)),
            pl.BlockSpec((d, tn), lambda i, j: (0, j + nj)),
            pl.BlockSpec((tn, d), lambda i, j: (j, 0)),
            pl.BlockSpec((1, d), lambda i, j: (0, 0)),
            pl.BlockSpec((1, d), lambda i, j: (0, 0)),
        ],
        out_specs=pl.BlockSpec((tm, d), lambda i, j: (i, 0)),
        out_shape=jax.ShapeDtypeStruct((m, d), F32),
        scratch_shapes=[pltpu.VMEM((tm, d), BF), pltpu.VMEM((tm, d), F32)],
        compiler_params=_params("parallel", "arbitrary"),
        name="ffn_ln",
    )(x, w_up, w_up, w_down, g.reshape(1, d), b.reshape(1, d))


def _mm_kernel(x_ref, w_ref, o_ref):
    o_ref[...] = _dot(x_ref[...].astype(BF), w_ref[...])


def _matmul(x, w, tn):
    m, d = x.shape
    n = w.shape[1]
    tm = min(512, m)
    return pl.pallas_call(
        _mm_kernel,
        grid=(m // tm, n // tn),
        in_specs=[pl.BlockSpec((tm, d), lambda i, j: (i, 0)),
                  pl.BlockSpec((d, tn), lambda i, j: (0, j))],
        out_specs=pl.BlockSpec((tm, tn), lambda i, j: (i, j)),
        out_shape=jax.ShapeDtypeStruct((m, n), F32),
        compiler_params=_params("parallel", "arbitrary"),
        name="proj",
    )(x, w)


def _proj_attn_kernel(x_ref, wq_ref, wkv_ref, wgn_ref, pq_ref, pk_ref, pkt_ref,
                      kc_ref, vc_ref, kct_ref, vct_ref, kst_ref, vst32_ref, kwt_ref, vwt32_ref, gn_ref,
                      qa_ref, ksa_ref, vst_ref, kwa_ref, vwt_ref):
    tm = x_ref.shape[0]
    xb = x_ref[...].astype(BF)
    zq = (_dot(xb, wq_ref[...]) * (HEAD_DIM ** -0.5 * np.log2(np.e))).astype(BF)
    for h in range(N_HEADS):
        pair = zq[:, (h // 2) * LANES:(h // 2 + 1) * LANES]
        qa_ref[h] = _dot(pair, pq_ref[h % 2]).astype(BF)
    zkv = _dot(xb, wkv_ref[...])
    parts = [zkv[:, n * KV_W:(n + 1) * KV_W] for n in range(6)]
    kc_ref[...] = parts[0]
    vc_ref[...] = parts[1]
    for ref, part in zip((kct_ref, vct_ref, kst_ref, vst32_ref, kwt_ref, vwt32_ref), parts):
        ref[...] = part.T
    gn_ref[...] = _dot(xb, wgn_ref[...])
    t = pl.program_id(0) * tm + lax.broadcasted_iota(jnp.int32, (tm, 1), 0)
    lane = lax.broadcasted_iota(jnp.int32, (1, LANES), 1)
    onehot = jnp.where(lane == HEAD_DIM + (t // SEL_BLOCK) % BLOCKS_PER_TILE, 1.0, 0.0)
    ksb = parts[2].astype(BF)
    kwb = parts[4].astype(BF)
    vsb = parts[3].astype(BF)
    vwb = parts[5].astype(BF)
    for k in range(KV_HEADS):
        ksa_ref[k] = (_dot(ksb, pk_ref[k]) + onehot).astype(BF)
        kwa_ref[k] = _dot(kwb, pk_ref[k]).astype(BF)
        ones_row = jnp.where(lax.broadcasted_iota(jnp.int32, (SEL_V_ROWS, 1), 0) == HEAD_DIM, 1.0, 0.0)
        vst_ref[k, 0] = (_dot_nt(pkt_ref[k], vsb) + ones_row).astype(BF)
        vwt_ref[k] = (_dot_nt(pkt_ref[k], vwb) + ones_row).astype(BF)


def _proj_attn(x, wq, wkv, wgn, pq, pk, pkt):
    m, d = x.shape
    tm = 256
    per_tile = SEL_TILE // tm
    row = lambda w: pl.BlockSpec((tm, w), lambda i: (i, 0))
    const = lambda a: pl.BlockSpec(a.shape, lambda i: (0,) * a.ndim)
    head = lambda n: pl.BlockSpec((n, tm, LANES), lambda i: (0, i, 0))
    f32 = lambda w: jax.ShapeDtypeStruct((m, w), F32)
    aug = lambda n: jax.ShapeDtypeStruct((n, m, LANES), BF)
    return pl.pallas_call(
        _proj_attn_kernel,
        grid=(m // tm,),
        in_specs=[row(d), const(wq), const(wkv), const(wgn), const(pq), const(pk), const(pkt)],
        out_specs=[row(KV_W)] * 2 + [pl.BlockSpec((KV_W, tm), lambda i: (0, i))] * 6
                  + [row(LANES), head(N_HEADS), head(KV_HEADS),
                                     pl.BlockSpec((KV_HEADS, 1, SEL_V_ROWS, tm),
                                                  lambda i: (0, i // per_tile, 0, i % per_tile)),
                                     head(KV_HEADS),
                                     pl.BlockSpec((KV_HEADS, SEL_V_ROWS, tm), lambda i: (0, 0, i))],
        out_shape=[f32(KV_W)] * 2 + [jax.ShapeDtypeStruct((KV_W, m), F32)] * 6
                  + [f32(LANES), aug(N_HEADS), aug(KV_HEADS),
                                     jax.ShapeDtypeStruct((KV_HEADS, m // SEL_TILE, SEL_V_ROWS, SEL_TILE), BF),
                                     aug(KV_HEADS),
                                     jax.ShapeDtypeStruct((KV_HEADS, SEL_V_ROWS, m), BF)],
        compiler_params=_params("parallel"),
        name="proj_attn",
    )(x, wq, wkv, wgn, pq, pk, pkt)


def _rglru_gates(xc, wa, wx, ba, bx, lam, is_start):
    xcb = xc.astype(BF)
    r = jax.nn.sigmoid(_dot(xcb, wa) + ba)
    ig = jax.nn.sigmoid(_dot(xcb, wx) + bx)
    log_a = -LRU_C * r * jax.nn.softplus(-lam)
    a = jnp.exp(log_a)
    if is_start is True:
        return a, ig * xc
    th = jnp.tanh(log_a)
    mult = jnp.sqrt(-2.0 * th / (1.0 - th))
    if is_start is not None:
        mult = jnp.where(is_start, 1.0, mult)
    return a, mult * (ig * xc)


def _rglru_prompt_kernel(x_ref, gate_ref, wc_ref, bc_ref, wa_ref, wx_ref, ba_ref, bx_ref, lam_ref,
                         y_ref, tail_ref, hl_ref, h_scr, tail_scr):
    c = pl.program_id(1)
    tc = x_ref.shape[0]

    @pl.when(c == 0)
    def _():
        h_scr[...] = jnp.zeros_like(h_scr)
        tail_scr[...] = jnp.zeros_like(tail_scr)

    x = x_ref[...]
    prev = tail_scr[...]
    row8 = lax.broadcasted_iota(jnp.int32, (SUBLANES, 1), 0)
    row = lax.broadcasted_iota(jnp.int32, (tc, 1), 0)

    def shifted(s):
        rolled = pltpu.roll(x, s, 0)
        top = jnp.where(row8 < s, pltpu.roll(prev, s, 0), rolled[:SUBLANES])
        return jnp.concatenate([top, rolled[SUBLANES:]], axis=0)

    wc = wc_ref[...]
    conv = wc[0:1] * shifted(3)
    conv = conv + wc[1:2] * shifted(2)
    conv = conv + wc[2:3] * shifted(1)
    conv = conv + wc[3:4] * x
    xc = bc_ref[...] + conv

    is_start = (row + c * tc) == 0
    a, u = _rglru_gates(xc, wa_ref[0], wx_ref[0], ba_ref[...], bx_ref[...], lam_ref[...], is_start)

    d = 1
    while d < tc:
        keep = row >= d
        a_sh = jnp.where(keep, pltpu.roll(a, d, 0), 1.0)
        u_sh = jnp.where(keep, pltpu.roll(u, d, 0), 0.0)
        u = a * u_sh + u
        a = a * a_sh
        d *= 2
    h = a * h_scr[...] + u

    y_ref[...] = (jax.nn.gelu(gate_ref[...]) * h).astype(BF)
    h_last = h[tc - 1:tc]
    h_scr[...] = h_last
    tail_scr[...] = x[tc - SUBLANES:]
    hl_ref[...] = h_last
    tail_ref[...] = x[tc - SUBLANES:]


def _rglru_prompt(xrg, w_conv, b_conv, wa_bd, wx_bd, b_a, b_x, lam):
    t = xrg.shape[0]
    d_rnn = w_conv.shape[1]
    gw = 2 * LANES
    ng = d_rnn // gw
    tc = min(512, t)
    vec = lambda: pl.BlockSpec((1, gw), lambda g, c: (0, g))
    return pl.pallas_call(
        _rglru_prompt_kernel,
        grid=(ng, t // tc),
        in_specs=[
            pl.BlockSpec((tc, gw), lambda g, c: (c, g)),
            pl.BlockSpec((tc, gw), lambda g, c: (c, g + ng)),
            pl.BlockSpec((CONV_W, gw), lambda g, c: (0, g)),
            vec(),
            pl.BlockSpec((1, gw, gw), lambda g, c: (g, 0, 0)),
            pl.BlockSpec((1, gw, gw), lambda g, c: (g, 0, 0)),
            vec(), vec(), vec(),
        ],
        out_specs=[
            pl.BlockSpec((tc, gw), lambda g, c: (c, g)),
            pl.BlockSpec((SUBLANES, gw), lambda g, c: (0, g)),
            pl.BlockSpec((1, gw), lambda g, c: (0, g)),
        ],
        out_shape=[
            jax.ShapeDtypeStruct((t, d_rnn), BF),
            jax.ShapeDtypeStruct((SUBLANES, d_rnn), F32),
            jax.ShapeDtypeStruct((1, d_rnn), F32),
        ],
        scratch_shapes=[pltpu.VMEM((1, gw), F32), pltpu.VMEM((SUBLANES, gw), F32)],
        compiler_params=_params("parallel", "arbitrary"),
        name="rglru_prompt",
    )(xrg, xrg, w_conv, b_conv.reshape(1, -1), wa_bd, wx_bd,
      b_a.reshape(1, -1), b_x.reshape(1, -1), lam.reshape(1, -1))


def _rglru_sample_kernel(xrg_ref, cp_ref, h0_ref, wc_ref, bc_ref, wa_ref, wx_ref, ba_ref, bx_ref, lam_ref,
                         y_ref, cn_ref, hl_ref, *, n_tok, start0):
    d_rnn = h0_ref.shape[1]
    gw = wa_ref.shape[1]
    wc = wc_ref[...]
    xp = [cp_ref[:, k * d_rnn:(k + 1) * d_rnn] for k in range(CONV_W - 1)]
    xp += [xrg_ref[:, t * 2 * d_rnn:t * 2 * d_rnn + d_rnn] for t in range(n_tok)]
    h = h0_ref[...]
    for t in range(n_tok):
        conv = wc[0:1] * xp[t]
        for k in range(1, CONV_W):
            conv = conv + wc[k:k + 1] * xp[t + k]
        xc = bc_ref[...] + conv
        a_parts, u_parts = [], []
        for g in range(d_rnn // gw):
            sl = slice(g * gw, (g + 1) * gw)
            a_g, u_g = _rglru_gates(xc[:, sl], wa_ref[g], wx_ref[g], ba_ref[:, sl], bx_ref[:, sl],
                                    lam_ref[:, sl], True if (start0 and t == 0) else None)
            a_parts.append(a_g)
            u_parts.append(u_g)
        a = jnp.concatenate(a_parts, axis=1)
        u = jnp.concatenate(u_parts, axis=1)
        h = a * h + u
        gate = xrg_ref[:, t * 2 * d_rnn + d_rnn:(t + 1) * 2 * d_rnn]
        y_ref[:, t * d_rnn:(t + 1) * d_rnn] = (jax.nn.gelu(gate) * h).astype(BF)
    hl_ref[...] = h
    tail = xp[-(CONV_W - 1):]
    for k in range(CONV_W - 1):
        cn_ref[:, k * d_rnn:(k + 1) * d_rnn] = tail[k]


def _rglru_sample(xrg, conv_prev, h0, w_conv, b_conv, wa_bd, wx_bd, b_a, b_x, lam, start0):
    b, d_rnn = h0.shape
    n_tok = xrg.shape[1] // (2 * d_rnn)
    args = (xrg, conv_prev, h0, w_conv, b_conv.reshape(1, -1), wa_bd, wx_bd,
            b_a.reshape(1, -1), b_x.reshape(1, -1), lam.reshape(1, -1))
    full = lambda a: pl.BlockSpec(a.shape, lambda i: (0,) * a.ndim)
    outs = [jax.ShapeDtypeStruct((b, n_tok * d_rnn), BF),
            jax.ShapeDtypeStruct((b, (CONV_W - 1) * d_rnn), F32),
            jax.ShapeDtypeStruct((b, d_rnn), F32)]
    return pl.pallas_call(
        functools.partial(_rglru_sample_kernel, n_tok=n_tok, start0=start0),
        grid=(1,),
        in_specs=[full(a) for a in args],
        out_specs=[full(o) for o in outs],
        out_shape=outs,
        compiler_params=_params("arbitrary"),
        name="rglru_sample",
    )(*args)


def _compress_kernel(x_ref, pe_ref, w1_ref, w2_ref, pk_ref, o_ref, scr):
    x = x_ref[...]
    rows = x.shape[0]
    xb = (x.reshape(rows // SUBLANES, SUBLANES, CMP_ROW_W) + pe_ref[...][None]).reshape(rows, CMP_ROW_W)
    full = _dot(xb.astype(BF), w1_ref[...])
    hid = full[:, :KV_W] + pltpu.roll(full[:, KV_W:], rows - 1, 0)
    out = _dot(jax.nn.gelu(hid).astype(BF), w2_ref[...])
    nb = rows // 4
    ob = out.astype(BF)
    for k in range(KV_HEADS):
        scr[...] = _dot(ob, pk_ref[k])
        o_ref[k, 0] = scr[pl.ds(0, nb, stride=4), :].astype(BF)
        o_ref[k, 1] = scr[pl.ds(2, nb, stride=4), :].astype(BF)


def _compress_prompt(kv, pe2, w1, w2, pk):
    t = kv.shape[0]
    rows = 128
    x = kv.reshape(t // CMP_ROW, CMP_ROW_W)
    nb = rows // 4
    nbs = t // SEL_BLOCK
    const = lambda a: pl.BlockSpec(a.shape, lambda i: (0,) * a.ndim)
    return pl.pallas_call(
        _compress_kernel,
        grid=(x.shape[0] // rows,),
        in_specs=[pl.BlockSpec((rows, CMP_ROW_W), lambda i: (i, 0)),
                  const(pe2), const(w1), const(w2), const(pk)],
        out_specs=pl.BlockSpec((KV_HEADS, 2, nb, LANES), lambda i: (0, 0, i, 0)),
        out_shape=jax.ShapeDtypeStruct((KV_HEADS, 2, nbs, LANES), BF),
        scratch_shapes=[pltpu.VMEM((rows, LANES), F32)],
        compiler_params=_params("parallel"),
        name="compress_prompt",
    )(x, pe2, w1, w2, pk)


def _compress_paged_kernel(pt_ref, *refs, pages):
    del pt_ref
    pe_ref = refs[2 * pages]
    wk1_ref, wk2_ref, wv1_ref, wv2_ref, ok_ref, ov_ref, xk_scr, xv_scr, outk_scr, outv_scr = refs[2 * pages + 1:]
    _compress_pages(refs[:pages], pe_ref, wk1_ref, wk2_ref, ok_ref, xk_scr, outk_scr)
    _compress_pages(refs[pages:2 * pages], pe_ref, wv1_ref, wv2_ref, ov_ref, xv_scr, outv_scr)


def _compress_pages(x_refs, pe_ref, w1_ref, w2_ref, o_ref, xs_scr, out_scr):
    pages = len(x_refs)
    halves = KV_W // LANES
    for p in range(pages):
        xt = x_refs[p][0].T + pe_ref[...]
        for h in range(halves):
            xs_scr[h, p * PAGE_SIZE:(p + 1) * PAGE_SIZE, :] = xt[:, h * LANES:(h + 1) * LANES]
    nblk = pages * (PAGE_SIZE // CMP_BLOCK)
    by_tok = [pltpu.einshape("mld->lmd", xs_scr[h].reshape(nblk, CMP_BLOCK, LANES)) for h in range(halves)]
    acc = None
    for l2 in range(CMP_BLOCK // 2):
        parts = [jnp.concatenate([by_tok[h][2 * l2], by_tok[h][2 * l2 + 1]], axis=1) for h in range(halves)]
        d = _dot(jnp.concatenate(parts, axis=0).astype(BF), w1_ref[l2])
        acc = d if acc is None else acc + d
    out_scr[...] = _dot(jax.nn.gelu(acc).astype(BF), w2_ref[...])
    for h in range(halves):
        for par in range(2):
            o_ref[0, par, :, h * LANES:(h + 1) * LANES] = (
                out_scr[pl.ds(h * nblk + par, nblk // 2, stride=2), :].astype(BF))


def _compress_paged(cache_k, cache_v, page_table, pe_tok, wk1, wk2, wv1, wv2):
    b, n_pages = page_table.shape
    pages = CMP_PAGES
    view = lambda c: jnp.transpose(c, (0, 2, 3, 1)).reshape(c.shape[0], KV_W, PAGE_SIZE)
    xk, xv = view(cache_k), view(cache_v)
    nblk = pages * (PAGE_SIZE // CMP_BLOCK)
    halves = KV_W // LANES
    nbs = n_pages * PAGE_SIZE // SEL_BLOCK
    const = lambda a: pl.BlockSpec(a.shape, lambda i, s, pt: (0,) * a.ndim)
    page_specs = [
        pl.BlockSpec((1, KV_W, PAGE_SIZE),
                     functools.partial(lambda i, s, pt, p: (pt[i, s * pages + p], 0, 0), p=p))
        for p in range(pages)]
    return pl.pallas_call(
        functools.partial(_compress_paged_kernel, pages=pages),
        grid_spec=pltpu.PrefetchScalarGridSpec(
            num_scalar_prefetch=1,
            grid=(b, n_pages // pages),
            in_specs=page_specs + page_specs + [const(a) for a in (pe_tok, wk1, wk2, wv1, wv2)],
            out_specs=[pl.BlockSpec((1, 2, nblk // 2, KV_W), lambda i, s, pt: (i, 0, s, 0))] * 2,
            scratch_shapes=[pltpu.VMEM((halves, pages * PAGE_SIZE, LANES), F32)] * 2
                           + [pltpu.VMEM((halves * nblk, LANES), F32)] * 2,
        ),
        out_shape=[jax.ShapeDtypeStruct((b, 2, nbs, KV_W), BF)] * 2,
        compiler_params=_params("parallel", "arbitrary"),
        name="compress_paged",
    )(page_table, *([xk] * pages), *([xv] * pages), pe_tok, wk1, wk2, wv1, wv2)


def _topk_columns(imp_t, n_pick, quota=None):
    nblk = imp_t.shape[0]
    idx = lax.broadcasted_iota(jnp.int32, imp_t.shape, 0)
    left = imp_t
    for n in range(n_pick):
        mx = jnp.max(left, axis=0, keepdims=True)
        first = jnp.min(jnp.where(left == mx, idx, nblk), axis=0, keepdims=True)
        if quota is not None:
            first = jnp.where(n < quota, first, nblk)
        left = jnp.where(idx == first, -jnp.inf, left)
    return jnp.where((left == -jnp.inf) & (imp_t != -jnp.inf), 1.0, 0.0)


def _softmax_cols(s):
    m = jnp.max(s, axis=0, keepdims=True)
    m = jnp.where(m == -jnp.inf, 0.0, m)
    e = jnp.exp2(s - m)
    den = jnp.sum(e, axis=0, keepdims=True)
    return e * (1.0 / jnp.maximum(den, 1e-30))


def _nsa_prompt_kernel(*refs):
    (q_ref, kc_ref, vc_ref, ks_ref, vst_ref) = refs[:5]
    wb = WINDOW // Q_BLOCK + 1
    kw_refs = refs[5:5 + wb]
    vwt_refs = refs[5 + wb:5 + 2 * wb]
    (gn_ref, pa_ref, o_ref,
     bias_scr, gate_scr, sa_scr, sb_scr, m_scr, acc_scr, oc_scr, ow_scr) = refs[5 + 2 * wb:]
    kvh = pl.program_id(0)
    i = pl.program_id(1)
    nq = Q_BLOCK
    r = GROUP * nq
    hd = HEAD_DIM
    q = q_ref[...].reshape(r, LANES)
    lane = lax.broadcasted_iota(jnp.int32, (1, r), 1)
    qpos = i * nq + lane % nq

    nbs = kc_ref.shape[1] // 2

    def window_and_gates():
        kw = jnp.concatenate([ref[0] for ref in kw_refs], axis=0)
        vwt = jnp.concatenate([ref[0] for ref in vwt_refs], axis=1)
        s_w = _dot_nt(kw, q)
        nwb = len(kw_refs)
        krow = lax.broadcasted_iota(jnp.int32, (nq, 1), 0)
        parts = []
        for b in range(nwb):
            blk_pos = (i - (nwb - 1) + b) * nq
            mask = blk_pos >= 0
            if b == 0:
                mask = mask & (qpos - (blk_pos + krow) <= WINDOW)
            if b == nwb - 1:
                mask = mask & (blk_pos + krow <= qpos)
            parts.append(jnp.where(mask, s_w[b * nq:(b + 1) * nq], -jnp.inf))
        s_w = jnp.concatenate(parts, axis=0)
        e_w = jnp.exp2(s_w - jnp.max(s_w, axis=0, keepdims=True))
        acc_w = _dot(vwt, e_w.astype(BF))
        ow_scr[...] = acc_w[:hd] * (1.0 / jnp.maximum(acc_w[hd:hd + 1], 1e-30))
        gate_scr[...] = jax.nn.sigmoid(gn_ref[...]).T

    def compressed_and_select(nv, first_quarter):
        window_and_gates()
        kc = jnp.concatenate([kc_ref[0, 0:nv], kc_ref[0, nbs:nbs + nv]], axis=0)
        vc = jnp.concatenate([vc_ref[0, 0:nv], vc_ref[0, nbs:nbs + nv]], axis=0)
        rowc = lax.broadcasted_iota(jnp.int32, (2 * nv, 1), 0)
        blk = 2 * (rowc % nv) + rowc // nv
        p_c = _softmax_cols(jnp.where(blk * CMP_BLOCK + (CMP_BLOCK - 1) <= qpos, _dot_nt(kc, q), -jnp.inf))
        vct = vc.astype(F32).T[:hd].astype(BF)
        oc_scr[...] = _dot(vct, p_c.astype(BF))
        imp = p_c[:, 0:nq]
        for g in range(1, GROUP):
            imp = imp + p_c[:, g * nq:(g + 1) * nq]
        imp = imp[:nv] + imp[nv:]
        qp = i * nq + lax.broadcasted_iota(jnp.int32, (1, nq), 1)
        jb = lax.broadcasted_iota(jnp.int32, (nv, 1), 0)
        cur = qp // SEL_BLOCK
        valid = jb * SEL_BLOCK <= qp
        forced = (jb == 0) | (jb == cur) | (jb == cur - 1)
        n_forced = 1 + jnp.where(cur >= 1, 1, 0) + jnp.where(cur >= 2, 1, 0)
        n_sel = min(N_SEL, nbs)
        n_iter = n_sel - (1 if first_quarter else 3)
        picks = _topk_columns(jnp.where(valid & ~forced, imp, -jnp.inf), n_iter, n_sel - n_forced)
        sel = jnp.where(forced, 1.0, picks)
        bias = jnp.where(valid & (sel > 0.5), 0.0, NEG).T.astype(BF)
        nt = nv // BLOCKS_PER_TILE
        bias_all = _dot(bias, pa_ref[0:nv, 0:nt * LANES]).astype(BF)
        for t in range(nt):
            bias_scr[t] = bias_all[:, t * LANES:(t + 1) * LANES]

    n_var = 4
    variant = (i * n_var) // pl.num_programs(1)
    for v in range(n_var):
        pl.when(variant == v)(functools.partial(compressed_and_select, nbs * (v + 1) // n_var, v == 0))
    o_c = oc_scr[...]

    def scores(t):
        k0 = pl.multiple_of(t * SEL_TILE, SEL_TILE)
        kt = ks_ref[0, pl.ds(k0, SEL_TILE), :]
        qa = q + jnp.concatenate([bias_scr[t]] * GROUP, axis=0)
        return _dot_nt(kt, qa)

    def update(t, s_ref, diagonal):
        vt = vst_ref[0, t]
        cw = 2 * LANES
        for c in range(r // cw):
            sl = slice(c * cw, (c + 1) * cw)
            sg = s_ref[:, sl]
            if diagonal:
                kpos = t * SEL_TILE + lax.broadcasted_iota(jnp.int32, (SEL_TILE, 1), 0)
                sg = jnp.where(kpos <= qpos[:, sl], sg, NEG)
            m_old = m_scr[:, sl]
            m_new = jnp.maximum(m_old, jnp.max(sg, axis=0, keepdims=True))
            p = jnp.exp2(sg - m_new)
            acc_scr[:, sl] = jnp.exp2(m_old - m_new) * acc_scr[:, sl] + _dot(vt, p.astype(BF))
            m_scr[:, sl] = m_new

    def pair(t0):
        sb_scr[...] = scores(t0 + 1)
        update(t0, sa_scr, False)
        sa_scr[...] = scores(t0 + 2)
        update(t0 + 1, sb_scr, False)

    def octet(u, _):
        for n in range(4):
            pair(8 * u + 2 * n)
        return 0

    t_diag = (i * nq) // SEL_TILE
    m_scr[...] = jnp.full(m_scr.shape, NEG, F32)
    acc_scr[...] = jnp.zeros_like(acc_scr)
    sa_scr[...] = scores(0)
    lax.fori_loop(0, t_diag // 8, octet, 0)

    @pl.when(t_diag % 8 >= 4)
    def _():
        pair(t_diag - t_diag % 8)
        pair(t_diag - t_diag % 8 + 2)

    @pl.when(t_diag % 4 >= 2)
    def _():
        pair(t_diag - t_diag % 4)

    @pl.when(t_diag % 2 == 1)
    def _():
        sb_scr[...] = scores(t_diag)
        update(t_diag - 1, sa_scr, False)
        update(t_diag, sb_scr, True)

    @pl.when(t_diag % 2 == 0)
    def _():
        update(t_diag, sa_scr, True)

    o_s = acc_scr[:hd] * (1.0 / acc_scr[hd:hd + 1])

    o_w = ow_scr[...]

    def gate(br):
        return jnp.concatenate(
            [gate_scr[pl.ds(br * N_HEADS + kvh * GROUP + g, 1), :] for g in range(GROUP)], axis=1)

    comb = (gate(0) * o_c + gate(1) * o_s + gate(2) * o_w).astype(BF)
    o_ref[...] = jnp.concatenate([comb[:, g * nq:(g + 1) * nq] for g in range(GROUP)], axis=0)


def _nsa_prompt(qa, kc, vc, ksa, vst, kwa, vwt, gn, pall):
    t = qa.shape[1]
    n_tiles = t // SEL_TILE
    wb = WINDOW // Q_BLOCK + 1
    kv_full = lambda a: pl.BlockSpec((1,) + a.shape[1:], lambda k, i: (k,) + (0,) * (a.ndim - 1))
    wblk = lambda i, m: jnp.maximum(i - (wb - 1) + m, 0)
    win_k = [pl.BlockSpec((1, Q_BLOCK, LANES), functools.partial(lambda k, i, m: (k, wblk(i, m), 0), m=m))
             for m in range(wb)]
    win_v = [pl.BlockSpec((1, SEL_V_ROWS, Q_BLOCK), functools.partial(lambda k, i, m: (k, 0, wblk(i, m)), m=m))
             for m in range(wb)]
    const = lambda a: pl.BlockSpec(a.shape, lambda k, i: (0,) * a.ndim)
    return pl.pallas_call(
        _nsa_prompt_kernel,
        grid=(KV_HEADS, t // Q_BLOCK),
        in_specs=[pl.BlockSpec((GROUP, Q_BLOCK, LANES), lambda k, i: (k, i, 0)),
                  kv_full(kc), kv_full(vc), kv_full(ksa), kv_full(vst)]
                 + win_k + win_v
                 + [pl.BlockSpec((Q_BLOCK, LANES), lambda k, i: (i, 0)), const(pall)],
        out_specs=pl.BlockSpec((KV_W, Q_BLOCK), lambda k, i: (k, i)),
        out_shape=jax.ShapeDtypeStruct((ATTN_W, t), BF),
        scratch_shapes=[pltpu.VMEM((n_tiles, Q_BLOCK, LANES), BF), pltpu.VMEM((LANES, Q_BLOCK), F32),
                        pltpu.VMEM((SEL_TILE, GROUP * Q_BLOCK), F32), pltpu.VMEM((SEL_TILE, GROUP * Q_BLOCK), F32),
                        pltpu.VMEM((1, GROUP * Q_BLOCK), F32),
                        pltpu.VMEM((SEL_V_ROWS, GROUP * Q_BLOCK), F32),
                        pltpu.VMEM((HEAD_DIM, GROUP * Q_BLOCK), F32), pltpu.VMEM((HEAD_DIM, GROUP * Q_BLOCK), F32)],
        compiler_params=_params("parallel", "arbitrary"),
        name="nsa_prompt",
    )(qa, kc, vc, ksa, vst, *([kwa] * wb), *([vwt] * wb), gn, pall)


def _nsa_sample_kernel(*refs, pages, past, n_tok):
    pt_ref = refs[0]
    q_ref, kc_ref, vc_ref = refs[1:4]
    ks_refs = refs[4:4 + pages]
    vs_refs = refs[4 + pages:4 + 2 * pages]
    (ksn_ref, vsn_ref, kwn_ref, vwn_ref, wk_ref, wv_ref, gn_ref,
     pg_ref, rp_ref, eg_ref, o_ref, q_scr, bias_scr, m_scr, l_scr, acc_scr, oc_scr) = refs[4 + 2 * pages:]
    del pt_ref
    s_id = pl.program_id(1)
    tp = SUBLANES
    r = N_HEADS * tp
    row = lax.broadcasted_iota(jnp.int32, (r, 1), 0)
    qpos = past + row % tp
    nbs = bias_scr.shape[1]

    @pl.when(s_id == 0)
    def _():
        qb = q_ref[0].astype(BF)
        for h in range(N_HEADS):
            k = h // GROUP
            q_scr[h * tp:(h + 1) * tp, :] = _dot(qb[:, k * KV_W:(k + 1) * KV_W], pg_ref[h]).astype(BF)
        q = q_scr[...]
        kc = kc_ref[0].reshape(2 * nbs, KV_W)
        vc = vc_ref[0].reshape(2 * nbs, KV_W)
        col = lax.broadcasted_iota(jnp.int32, (1, 2 * nbs), 1)
        blk = 2 * (col % nbs) + col // nbs
        p_c = _masked_softmax(_dot_nt(q, kc), blk * CMP_BLOCK + (CMP_BLOCK - 1) <= qpos)
        oc_scr[...] = _dot(p_c.astype(BF), vc)
        imps = []
        for k in range(KV_HEADS):
            base = k * GROUP * tp
            imp = p_c[base:base + tp]
            for g in range(1, GROUP):
                imp = imp + p_c[base + g * tp:base + (g + 1) * tp]
            imps.append(imp)
        imp = jnp.concatenate(imps, axis=0)
        imp = imp[:, :nbs] + imp[:, nbs:]
        jb = lax.broadcasted_iota(jnp.int32, (1, nbs), 1)
        imp = jnp.where((jb == 0) | (jb == nbs - 1), jnp.inf, imp)
        idx = lax.broadcasted_iota(jnp.int32, imp.shape, 1)
        sel = jnp.zeros(imp.shape, F32)
        for _ in range(min(N_SEL, nbs + 1) - 1):
            mx = jnp.max(imp, axis=1, keepdims=True)
            first = jnp.min(jnp.where(imp == mx, idx, nbs), axis=1, keepdims=True)
            hit = idx == first
            sel = jnp.where(hit, 1.0, sel)
            imp = jnp.where(hit, -jnp.inf, imp)
        bias = jnp.where(sel > 0.5, 0.0, NEG).astype(BF)
        for k in range(KV_HEADS):
            for g in range(GROUP):
                h = k * GROUP + g
                bias_scr[h * tp:(h + 1) * tp, :] = bias[k * tp:(k + 1) * tp]
        m_scr[...] = jnp.full(m_scr.shape, NEG, F32)
        l_scr[...] = jnp.zeros_like(l_scr)
        acc_scr[...] = jnp.zeros_like(acc_scr)

    def online(st, v, v_transposed):
        m = m_scr[...]
        m_new = jnp.maximum(m, jnp.max(st, axis=1, keepdims=True))
        scale = jnp.exp(m - m_new)
        p = jnp.exp(st - m_new)
        l_scr[...] = scale * l_scr[...] + jnp.sum(p, axis=1, keepdims=True)
        pv = _dot_nt(p.astype(BF), v) if v_transposed else _dot(p.astype(BF), v)
        acc_scr[...] = scale * acc_scr[...] + pv
        m_scr[...] = m_new

    q = q_scr[...]
    kt = jnp.concatenate([ref[0].astype(BF) for ref in ks_refs], axis=1)
    vt = jnp.concatenate([ref[0].astype(BF) for ref in vs_refs], axis=1)
    n_keys = kt.shape[1]
    jb = lax.broadcasted_iota(jnp.int32, (nbs, 1), 0)
    key_blk = s_id * (n_keys // SEL_BLOCK) + lax.broadcasted_iota(jnp.int32, (1, n_keys), 1) // SEL_BLOCK
    expand = jnp.where(jb == key_blk, 1.0, 0.0).astype(BF)
    online(_dot(q, kt) + _dot(bias_scr[...], expand), vt, True)

    @pl.when(s_id == pl.num_programs(1) - 1)
    def _():
        zeros = jnp.zeros((LANES - tp, KV_W), BF)
        tok = lax.broadcasted_iota(jnp.int32, (1, LANES), 1)
        ksn = jnp.concatenate([ksn_ref[0].astype(BF), zeros], axis=0)
        vsn = jnp.concatenate([vsn_ref[0].astype(BF), zeros], axis=0)
        online(jnp.where((past + tok <= qpos) & (tok < n_tok), _dot_nt(q, ksn), NEG), vsn, False)
        o_s = acc_scr[...] / l_scr[...]
        kw = jnp.concatenate([wk_ref[0].astype(BF), kwn_ref[0].astype(BF), zeros], axis=0)
        vw = jnp.concatenate([wv_ref[0].astype(BF), vwn_ref[0].astype(BF), zeros], axis=0)
        n_w = kw.shape[0]
        wcol = lax.broadcasted_iota(jnp.int32, (1, n_w), 1)
        kwpos = past - WINDOW + wcol
        dpos = qpos - kwpos
        mask_w = (dpos >= 0) & (dpos <= WINDOW) & (kwpos >= 0) & (wcol < WINDOW + n_tok)
        p_w = _masked_softmax(_dot_nt(q, kw), mask_w)
        o_w = _dot(p_w.astype(BF), vw)
        sg = jax.nn.sigmoid(gn_ref[0])
        gexp = _expand_f32(sg, eg_ref[...])

        def gate(br):
            return jnp.concatenate(
                [gexp[:, (br * N_HEADS + h) * KV_W:(br * N_HEADS + h + 1) * KV_W] for h in range(N_HEADS)],
                axis=0)

        comb = (gate(0) * oc_scr[...] + gate(1) * o_s + gate(2) * o_w).astype(BF)
        for k in range(KV_HEADS):
            out = None
            for g in range(GROUP):
                h = k * GROUP + g
                part = _dot(comb[h * tp:(h + 1) * tp], rp_ref[h])
                out = part if out is None else out + part
            o_ref[0, :, k * KV_W:(k + 1) * KV_W] = out


def _nsa_sample(page_table, q8, kc, vc, cache_ks, cache_vs, new8, win_k, win_v, gn8, pgk, rkg, egate, n_tok):
    b, n_pages = page_table.shape
    past = n_pages * PAGE_SIZE
    pages = SEL_PAGES
    tp = SUBLANES
    r = N_HEADS * tp
    nbs = past // SEL_BLOCK
    ck = jnp.transpose(cache_ks, (0, 2, 3, 1)).reshape(cache_ks.shape[0], KV_W, PAGE_SIZE)
    cv = jnp.transpose(cache_vs, (0, 2, 3, 1)).reshape(cache_vs.shape[0], KV_W, PAGE_SIZE)
    per_seq = lambda a: pl.BlockSpec((1,) + a.shape[1:], lambda i, s, pt: (i,) + (0,) * (a.ndim - 1))
    const = lambda a: pl.BlockSpec(a.shape, lambda i, s, pt: (0,) * a.ndim)
    page_specs = [
        pl.BlockSpec((1, KV_W, PAGE_SIZE),
                     functools.partial(lambda i, s, pt, p: (pt[i, s * pages + p], 0, 0), p=p))
        for p in range(pages)]
    return pl.pallas_call(
        functools.partial(_nsa_sample_kernel, pages=pages, past=past, n_tok=n_tok),
        grid_spec=pltpu.PrefetchScalarGridSpec(
            num_scalar_prefetch=1,
            grid=(b, n_pages // pages),
            in_specs=[per_seq(q8), per_seq(kc), per_seq(vc)] + page_specs + page_specs
                     + [per_seq(a) for a in new8] + [per_seq(win_k), per_seq(win_v), per_seq(gn8),
                                                     const(pgk), const(rkg), const(egate)],
            out_specs=pl.BlockSpec((1, tp, ATTN_W), lambda i, s, pt: (i, 0, 0)),
            scratch_shapes=[pltpu.VMEM((r, KV_W), BF), pltpu.VMEM((r, nbs), BF),
                            pltpu.VMEM((r, 1), F32), pltpu.VMEM((r, 1), F32),
                            pltpu.VMEM((r, KV_W), F32), pltpu.VMEM((r, KV_W), F32)],
        ),
        out_shape=jax.ShapeDtypeStruct((b, tp, ATTN_W), F32),
        compiler_params=_params("parallel", "arbitrary"),
        name="nsa_sample",
    )(page_table, q8, kc, vc, *([ck] * pages), *([cv] * pages), *new8, win_k, win_v, gn8, pgk, rkg, egate)


def _merge_kernel(x_ref, y_ref, o_ref, wgr_ref, wga_ref, wr_ref, wa_ref, wo_ref, g_ref, b_ref,
                  out_ref, xb_ref, acc_ref, *ob_ref):
    j = pl.program_id(1)

    @pl.when(j == 0)
    def _():
        xb_ref[...] = x_ref[...].astype(BF)
        acc_ref[...] = jnp.zeros_like(acc_ref)
        if ob_ref:
            ob_ref[0][...] = o_ref[...].astype(F32).T.astype(BF)

    xb = xb_ref[...]
    rec = _dot(y_ref[...], wr_ref[...])
    att = _dot(ob_ref[0][...] if ob_ref else o_ref[...], wa_ref[...])
    u = jax.nn.sigmoid(_dot(xb, wgr_ref[...])) * rec + jax.nn.sigmoid(_dot(xb, wga_ref[...])) * att
    acc_ref[...] += _dot(u.astype(BF), wo_ref[...])

    @pl.when(j == pl.num_programs(1) - 1)
    def _():
        out_ref[...] = _layer_norm(ALPHA * x_ref[...] + acc_ref[...], g_ref[...], b_ref[...])


def _merge(x, y_rec, o_attn, w_grec, w_gatt, w_rec_o, w_attn_o, w_out, g, b, o_transposed):
    m, d = x.shape
    tm = min(512, m)
    tn = 512
    col = lambda a: pl.BlockSpec((a.shape[0], tn), lambda i, j: (0, j))
    rowb = lambda a: pl.BlockSpec((tm, a.shape[1]), lambda i, j: (i, 0))
    vec = pl.BlockSpec((1, d), lambda i, j: (0, 0))
    if o_transposed:
        o_spec = pl.BlockSpec((o_attn.shape[0], tm), lambda i, j: (0, i))
        o_scratch = [pltpu.VMEM((tm, o_attn.shape[0]), BF)]
    else:
        o_spec, o_scratch = rowb(o_attn), []
    return pl.pallas_call(
        _merge_kernel,
        grid=(m // tm, d // tn),
        in_specs=[rowb(x), rowb(y_rec), o_spec, col(w_grec), col(w_gatt), col(w_rec_o), col(w_attn_o),
                  pl.BlockSpec((tn, d), lambda i, j: (j, 0)), vec, vec],
        out_specs=pl.BlockSpec((tm, d), lambda i, j: (i, 0)),
        out_shape=jax.ShapeDtypeStruct((m, d), F32),
        scratch_shapes=[pltpu.VMEM((tm, d), BF), pltpu.VMEM((tm, d), F32)] + o_scratch,
        compiler_params=_params("parallel", "arbitrary"),
        name="merge_out",
    )(x, y_rec, o_attn, w_grec, w_gatt, w_rec_o, w_attn_o, w_out, g.reshape(1, d), b.reshape(1, d))


def _place(n_rows, n_cols, src0, dst0, width, value=1.0):
    m = np.zeros((n_rows, n_cols), np.float32)
    m[src0 + np.arange(width), dst0 + np.arange(width)] = value
    return m


def _layout_constants(n_sel_blocks_prompt):
    hd = HEAD_DIM
    scale = hd ** -0.5
    pq = np.stack([_place(LANES, LANES, e * hd, 0, hd) for e in range(2)])
    pk = np.stack([_place(KV_W, LANES, k * hd, 0, hd) for k in range(KV_HEADS)])
    pkt = np.transpose(pk, (0, 2, 1))[:, :SEL_V_ROWS]
    nbs = n_sel_blocks_prompt
    n_tiles = -(-nbs // BLOCKS_PER_TILE)
    pall = np.zeros((nbs, n_tiles * LANES), np.float32)
    j = np.arange(nbs)
    pall[j, (j // BLOCKS_PER_TILE) * LANES + hd + j % BLOCKS_PER_TILE] = 1.0
    pgk = np.stack([_place(KV_W, KV_W, (h % GROUP) * hd, (h // GROUP) * hd, hd, scale) for h in range(N_HEADS)])
    rkg = np.stack([_place(KV_W, KV_W, (h // GROUP) * hd, (h % GROUP) * hd, hd) for h in range(N_HEADS)])
    egs = np.zeros((LANES, N_NSA_BRANCH * N_HEADS * KV_W), np.float32)
    for c in range(N_NSA_BRANCH * N_HEADS):
        egs[c, c * KV_W:(c + 1) * KV_W] = 1.0
    as_bf = lambda a: jnp.asarray(a, BF)
    return dict(pq=as_bf(pq), pk=as_bf(pk), pkt=as_bf(pkt), pall=as_bf(pall),
                pgk=as_bf(pgk), rkg=as_bf(rkg), egs=as_bf(egs))


def _block_diag(w, per_group):
    nb, c, _ = w.shape
    eye = jnp.eye(per_group, dtype=w.dtype)
    wg = w.reshape(nb // per_group, per_group, c, c)
    return jnp.einsum('gpcd,pq->gpcqd', wg, eye).reshape(nb // per_group, per_group * c, per_group * c)


def _compress_weights(w1, w2, pe):
    eye = jnp.eye(KV_HEADS, dtype=w1.dtype)
    big = jnp.einsum('lde,kq->lkdqe', w1, eye).reshape(CMP_BLOCK * KV_W, KV_HEADS * w1.shape[2])
    half = big.shape[0] // 2
    w1cat = jnp.concatenate([big[:half], big[half:]], axis=1).astype(BF)
    w2bd = _block_diag(jnp.broadcast_to(w2, (KV_HEADS,) + w2.shape), KV_HEADS)[0].astype(BF)
    pe_flat = jnp.broadcast_to(pe[:, None, :], (CMP_BLOCK, KV_HEADS, HEAD_DIM)).reshape(2, half)
    pe2 = jnp.tile(pe_flat, (SUBLANES // 2, 1))
    return w1cat, w2bd, pe2


def _compress_weights_paged(w1, w2, pe):
    per_half = LANES // HEAD_DIM
    eye = jnp.eye(per_half, dtype=w1.dtype)
    w1h = jnp.einsum('lde,kq->lkdqe', w1, eye).reshape(CMP_BLOCK, LANES, LANES)
    w1p = w1h.reshape(CMP_BLOCK // 2, 2 * LANES, LANES).astype(BF)
    w2p = jnp.einsum('ed,kq->keqd', w2, eye).reshape(LANES, LANES).astype(BF)
    pe_tok = jnp.tile(pe, (PAGE_SIZE // CMP_BLOCK, KV_HEADS))
    return w1p, w2p, pe_tok


def kernel(x_prompt, x_sample, cache_k_cmp, cache_v_cmp, cache_k_sel, cache_v_sel, page_table,
           state_win_k, state_win_v, state_conv, state_h,
           ln1_g, ln1_b, w_ffn1_up, w_ffn1_down, w_in, w_conv, b_conv, w_rg_a, b_rg_a, w_rg_x, b_rg_x,
           rg_lambda, cmp_pe, w_ck1, w_ck2, w_cv1, w_cv2, w_rec_o, w_attn_o, w_out,
           ln2_g, ln2_b, w_ffn2_up, w_ffn2_down, ln3_g, ln3_b):
    bp, t, d = x_prompt.shape
    bs, n_tok, _ = x_sample.shape
    d_rnn = w_conv.shape[1]
    assert bp == 1 and t % (128 * CMP_ROW) == 0 and t >= WINDOW
    past = page_table.shape[1] * PAGE_SIZE
    assert page_table.shape[1] % max(CMP_PAGES, SEL_PAGES) == 0 and CONV_W - 1 <= n_tok <= SUBLANES
    assert state_win_k.shape[1] == WINDOW

    o_q = 2 * d_rnn
    o_kv = o_q + ATTN_W
    o_gn = o_kv + 6 * KV_W
    o_gr = o_gn + N_NSA_BRANCH * N_HEADS
    o_ga = o_gr + d
    w_in_b = w_in.astype(BF)
    w_rnn = w_in_b[:, :o_q]
    w_q = w_in_b[:, o_q:o_kv]
    w_kv = w_in_b[:, o_kv:o_gn]
    w_gn = jnp.pad(w_in_b[:, o_gn:o_gr], ((0, 0), (0, LANES - (o_gr - o_gn))))
    w_grec = w_in_b[:, o_gr:o_ga]
    w_gatt = w_in_b[:, o_ga:]
    w_attn_all = jnp.concatenate([w_in_b[:, :o_gn], jnp.pad(w_gn, ((0, 0), (0, 512 - LANES)))], axis=1)
    up1, down1 = w_ffn1_up.astype(BF), w_ffn1_down.astype(BF)
    up2, down2 = w_ffn2_up.astype(BF), w_ffn2_down.astype(BF)
    per_group = 2 * LANES // (d_rnn // RNN_BLOCKS)
    wa_bd = _block_diag(w_rg_a, per_group).astype(BF)
    wx_bd = _block_diag(w_rg_x, per_group).astype(BF)
    ck1, ck2, pe2 = _compress_weights(w_ck1, w_ck2, cmp_pe)
    cv1, cv2, _ = _compress_weights(w_cv1, w_cv2, cmp_pe)
    w_rec_b, w_attn_b, w_out_b = w_rec_o.astype(BF), w_attn_o.astype(BF), w_out.astype(BF)
    cst = _layout_constants(t // SEL_BLOCK)

    xp = _ffn_ln(x_prompt.reshape(t, d), up1, down1, ln1_g, ln1_b)
    xrg = _matmul(xp, w_rnn, 512)
    (p_kc, p_vc, kct, vct, kst, vst32, kwt, vwt32, p_gn, qa, ksa, vst, kwa, vwt) = _proj_attn(
        xp, w_q, w_kv, w_gn, cst['pq'], cst['pk'], cst['pkt'])
    y_rec, p_tail, p_h = _rglru_prompt(xrg, w_conv, b_conv, wa_bd, wx_bd, b_rg_a, b_rg_x, rg_lambda)
    nbs = t // SEL_BLOCK
    kc = _compress_prompt(p_kc, pe2, ck1, ck2, cst['pk']).reshape(KV_HEADS, 2 * nbs, LANES)
    vc = _compress_prompt(p_vc, pe2, cv1, cv2, cst['pk']).reshape(KV_HEADS, 2 * nbs, LANES)
    o_attn_t = _nsa_prompt(qa, kc, vc, ksa, vst, kwa, vwt, p_gn, cst['pall'])
    x2 = _merge(xp, y_rec, o_attn_t, w_grec, w_gatt, w_rec_b, w_attn_b, w_out_b, ln2_g, ln2_b, True)
    y_prompt = _ffn_ln(x2, up2, down2, ln3_g, ln3_b).reshape(bp, t, d)

    kvh = lambda a: jnp.transpose(a.reshape(bp, KV_HEADS, HEAD_DIM, -1), (0, 3, 1, 2))
    p_states = (kvh(kct), kvh(vct), kvh(kst), kvh(vst32), kvh(kwt[:, t - WINDOW:]), kvh(vwt32[:, t - WINDOW:]),
                p_tail[SUBLANES - (CONV_W - 1):].reshape(bp, CONV_W - 1, d_rnn), p_h.reshape(bp, d_rnn))

    m_s = bs * n_tok
    xs = _ffn_ln(x_sample.reshape(m_s, d), up1, down1, ln1_g, ln1_b)
    zs = _matmul(xs, w_attn_all, 512)
    s_xrg = zs[:, :o_q].reshape(bs, n_tok * o_q)
    seq = lambda a: a.reshape(bs, n_tok, a.shape[-1])
    s_q = seq(zs[:, o_q:o_kv])
    s_kv = [seq(zs[:, o_kv + n * KV_W:o_kv + (n + 1) * KV_W]) for n in range(6)]
    s_gn = seq(zs[:, o_gn:o_gn + LANES])
    pad8 = lambda a: jnp.pad(a, ((0, 0), (0, SUBLANES - n_tok), (0, 0)))
    ys_rec, s_conv, s_h = _rglru_sample(
        s_xrg, state_conv.reshape(bs, -1), state_h, w_conv, b_conv, wa_bd, wx_bd, b_rg_a, b_rg_x, rg_lambda,
        start0=(past == 0))
    ck1p, ck2p, pe_tok = _compress_weights_paged(w_ck1, w_ck2, cmp_pe)
    cv1p, cv2p, _ = _compress_weights_paged(w_cv1, w_cv2, cmp_pe)
    kc_s, vc_s = _compress_paged(cache_k_cmp, cache_v_cmp, page_table, pe_tok, ck1p, ck2p, cv1p, cv2p)
    o8 = _nsa_sample(page_table, pad8(s_q), kc_s, vc_s, cache_k_sel, cache_v_sel,
                     [pad8(s_kv[n]) for n in (2, 3, 4, 5)],
                     state_win_k.reshape(bs, WINDOW, KV_W), state_win_v.reshape(bs, WINDOW, KV_W),
                     pad8(s_gn), cst['pgk'], cst['rkg'], cst['egs'], n_tok)
    os_attn = o8[:, :n_tok].reshape(m_s, ATTN_W).astype(BF)
    x2s = _merge(xs, ys_rec.reshape(m_s, d_rnn), os_attn, w_grec, w_gatt, w_rec_b, w_attn_b, w_out_b, ln2_g, ln2_b,
                 False)
    y_sample = _ffn_ln(x2s, up2, down2, ln3_g, ln3_b).reshape(bs, n_tok, d)

    kvs = lambda a: a.reshape(bs, n_tok, KV_HEADS, HEAD_DIM)
    win = lambda old, new: jnp.concatenate([old, kvs(new)], axis=1)[:, -WINDOW:]
    s_states = (kvs(s_kv[0]), kvs(s_kv[1]), kvs(s_kv[2]), kvs(s_kv[3]),
                win(state_win_k, s_kv[4]), win(state_win_v, s_kv[5]),
                s_conv.reshape(bs, CONV_W - 1, d_rnn), s_h)

    return (y_prompt, y_sample) + p_states + s_states
```

```python
import functools

import numpy as np
import jax
import jax.numpy as jnp
from jax import lax
from jax.experimental import pallas as pl
from jax.experimental.pallas import tpu as pltpu

F32 = jnp.float32
BF = jnp.bfloat16

DEPTH = 1
ALPHA = (2.0 * DEPTH) ** 0.25
N_HEADS = 16
HEAD_DIM = 64
KV_HEADS = 4
GROUP = N_HEADS // KV_HEADS
KV_W = KV_HEADS * HEAD_DIM
ATTN_W = N_HEADS * HEAD_DIM
N_NSA_BRANCH = 3
CMP_BLOCK = 32
SEL_BLOCK = 64
N_SEL = 16
WINDOW = 512
Q_BLOCK = 256
CONV_W = 4
LRU_C = 8.0
RNN_BLOCKS = 16
LN_EPS = 1e-5
PAGE_SIZE = 128

LANES = 128
SUBLANES = 8
VMEM_LIMIT = 56 * 1024 * 1024

NEG = -1e30
SEL_TILE = 512
BLOCKS_PER_TILE = SEL_TILE // SEL_BLOCK
SEL_V_ROWS = HEAD_DIM + 16
CMP_ROW = 16
CMP_PAGES = 16
SEL_PAGES = 32
CMP_ROW_W = CMP_ROW * KV_W

_NT = (((1,), (1,)), ((), ()))


def _dot(a, b):
    return jnp.dot(a, b, preferred_element_type=F32)


def _dot_nt(a, b):
    return lax.dot_general(a, b, _NT, preferred_element_type=F32)


def _params(*sem):
    return pltpu.CompilerParams(dimension_semantics=sem, vmem_limit_bytes=VMEM_LIMIT)


def _layer_norm(y, g, b):
    mu = jnp.mean(y, axis=-1, keepdims=True)
    d = y - mu
    var = jnp.mean(d * d, axis=-1, keepdims=True)
    return d * lax.rsqrt(var + LN_EPS) * g + b


def _masked_softmax(s, mask):
    s = jnp.where(mask, s, -jnp.inf)
    m = jnp.max(s, axis=-1, keepdims=True)
    m = jnp.where(m == -jnp.inf, 0.0, m)
    e = jnp.where(mask, jnp.exp(s - m), 0.0)
    den = jnp.sum(e, axis=-1, keepdims=True)
    return e / jnp.maximum(den, 1e-30)


def _split3(x):
    hi = x.astype(BF)
    r1 = x - hi.astype(F32)
    mid = r1.astype(BF)
    lo = (r1 - mid.astype(F32)).astype(BF)
    return hi, mid, lo


def _expand_f32(x, e):
    hi, mid, lo = _split3(x)
    return _dot(hi, e) + _dot(mid, e) + _dot(lo, e)


def _ffn_kernel(x_ref, wg_ref, wu_ref, wd_ref, g_ref, b_ref, o_ref, xb_ref, acc_ref):
    j = pl.program_id(1)

    @pl.when(j == 0)
    def _():
        xb_ref[...] = x_ref[...].astype(BF)
        acc_ref[...] = jnp.zeros_like(acc_ref)

    xb = xb_ref[...]
    gate = _dot(xb, wg_ref[...])
    up = _dot(xb, wu_ref[...])
    h = (gate * jax.nn.sigmoid(gate) * up).astype(BF)
    acc_ref[...] += _dot(h, wd_ref[...])

    @pl.when(j == pl.num_programs(1) - 1)
    def _():
        y = ALPHA * x_ref[...] + 0.5 * acc_ref[...]
        o_ref[...] = _layer_norm(y, g_ref[...], b_ref[...])


def _ffn_ln(x, w_up, w_down, g, b):
    m, d = x.shape
    f = w_down.shape[0]
    tm = min(512, m)
    tn = 512
    nj = f // tn
    return pl.pallas_call(
        _ffn_kernel,
        grid=(m // tm, nj),
        in_specs=[
            pl.BlockSpec((tm, d), lambda i, j: (i, 0)),
            pl.BlockSpec((d, tn), lambda i, j: (0, j)),
            pl.BlockSpec((d, tn), lambda i, j: (0, j + nj)),
            pl.BlockSpec((tn, d), lambda i, j: (j, 0)),
            pl.BlockSpec((1, d), lambda i, j: (0, 0)),
            pl.BlockSpec((1, d), lambda i, j: (0, 0)),
        ],
        out_specs=pl.BlockSpec((tm, d), lambda i, j: (i, 0)),
        out_shape=jax.ShapeDtypeStruct((m, d), F32),
        scratch_shapes=[pltpu.VMEM((tm, d), BF), pltpu.VMEM((tm, d), F32)],
        compiler_params=_params("parallel", "arbitrary"),
        name="ffn_ln",
    )(x, w_up, w_up, w_down, g.reshape(1, d), b.reshape(1, d))


def _mm_kernel(x_ref, w_ref, o_ref):
    o_ref[...] = _dot(x_ref[...].astype(BF), w_ref[...])


def _matmul(x, w, tn):
    m, d = x.shape
    n = w.shape[1]
    tm = min(512, m)
    return pl.pallas_call(
        _mm_kernel,
        grid=(m // tm, n // tn),
        in_specs=[pl.BlockSpec((tm, d), lambda i, j: (i, 0)),
                  pl.BlockSpec((d, tn), lambda i, j: (0, j))],
        out_specs=pl.BlockSpec((tm, tn), lambda i, j: (i, j)),
        out_shape=jax.ShapeDtypeStruct((m, n), F32),
        compiler_params=_params("parallel", "arbitrary"),
        name="proj",
    )(x, w)


def _proj_attn_kernel(x_ref, wq_ref, wkv_ref, wgn_ref, pq_ref, pk_ref, pkt_ref,
                      kc_ref, vc_ref, kct_ref, vct_ref, kst_ref, vst32_ref, kwt_ref, vwt32_ref, gn_ref,
                      qa_ref, ksa_ref, vst_ref, kwa_ref, vwt_ref):
    tm = x_ref.shape[0]
    xb = x_ref[...].astype(BF)
    zq = (_dot(xb, wq_ref[...]) * (HEAD_DIM ** -0.5 * np.log2(np.e))).astype(BF)
    for h in range(N_HEADS):
        pair = zq[:, (h // 2) * LANES:(h // 2 + 1) * LANES]
        qa_ref[h] = _dot(pair, pq_ref[h % 2]).astype(BF)
    zkv = _dot(xb, wkv_ref[...])
    parts = [zkv[:, n * KV_W:(n + 1) * KV_W] for n in range(6)]
    kc_ref[...] = parts[0]
    vc_ref[...] = parts[1]
    for ref, part in zip((kct_ref, vct_ref, kst_ref, vst32_ref, kwt_ref, vwt32_ref), parts):
        ref[...] = part.T
    gn_ref[...] = _dot(xb, wgn_ref[...])
    t = pl.program_id(0) * tm + lax.broadcasted_iota(jnp.int32, (tm, 1), 0)
    lane = lax.broadcasted_iota(jnp.int32, (1, LANES), 1)
    onehot = jnp.where(lane == HEAD_DIM + (t // SEL_BLOCK) % BLOCKS_PER_TILE, 1.0, 0.0)
    ksb = parts[2].astype(BF)
    kwb = parts[4].astype(BF)
    vsb = parts[3].astype(BF)
    vwb = parts[5].astype(BF)
    for k in range(KV_HEADS):
        ksa_ref[k] = (_dot(ksb, pk_ref[k]) + onehot).astype(BF)
        kwa_ref[k] = _dot(kwb, pk_ref[k]).astype(BF)
        ones_row = jnp.where(lax.broadcasted_iota(jnp.int32, (SEL_V_ROWS, 1), 0) == HEAD_DIM, 1.0, 0.0)
        vst_ref[k, 0] = (_dot_nt(pkt_ref[k], vsb) + ones_row).astype(BF)
        vwt_ref[k] = (_dot_nt(pkt_ref[k], vwb) + ones_row).astype(BF)


def _proj_attn(x, wq, wkv, wgn, pq, pk, pkt):
    m, d = x.shape
    tm = 256
    per_tile = SEL_TILE // tm
    row = lambda w: pl.BlockSpec((tm, w), lambda i: (i, 0))
    const = lambda a: pl.BlockSpec(a.shape, lambda i: (0,) * a.ndim)
    head = lambda n: pl.BlockSpec((n, tm, LANES), lambda i: (0, i, 0))
    f32 = lambda w: jax.ShapeDtypeStruct((m, w), F32)
    aug = lambda n: jax.ShapeDtypeStruct((n, m, LANES), BF)
    return pl.pallas_call(
        _proj_attn_kernel,
        grid=(m // tm,),
        in_specs=[row(d), const(wq), const(wkv), const(wgn), const(pq), const(pk), const(pkt)],
        out_specs=[row(KV_W)] * 2 + [pl.BlockSpec((KV_W, tm), lambda i: (0, i))] * 6
                  + [row(LANES), head(N_HEADS), head(KV_HEADS),
                                     pl.BlockSpec((KV_HEADS, 1, SEL_V_ROWS, tm),
                                                  lambda i: (0, i // per_tile, 0, i % per_tile)),
                                     head(KV_HEADS),
                                     pl.BlockSpec((KV_HEADS, SEL_V_ROWS, tm), lambda i: (0, 0, i))],
        out_shape=[f32(KV_W)] * 2 + [jax.ShapeDtypeStruct((KV_W, m), F32)] * 6
                  + [f32(LANES), aug(N_HEADS), aug(KV_HEADS),
                                     jax.ShapeDtypeStruct((KV_HEADS, m // SEL_TILE, SEL_V_ROWS, SEL_TILE), BF),
                                     aug(KV_HEADS),
                                     jax.ShapeDtypeStruct((KV_HEADS, SEL_V_ROWS, m), BF)],
        compiler_params=_params("parallel"),
        name="proj_attn",
    )(x, wq, wkv, wgn, pq, pk, pkt)


def _rglru_gates(xc, wa, wx, ba, bx, lam, is_start):
    xcb = xc.astype(BF)
    r = jax.nn.sigmoid(_dot(xcb, wa) + ba)
    ig = jax.nn.sigmoid(_dot(xcb, wx) + bx)
    log_a = -LRU_C * r * jax.nn.softplus(-lam)
    a = jnp.exp(log_a)
    if is_start is True:
        return a, ig * xc
    th = jnp.tanh(log_a)
    mult = jnp.sqrt(-2.0 * th / (1.0 - th))
    if is_start is not None:
        mult = jnp.where(is_start, 1.0, mult)
    return a, mult * (ig * xc)


def _rglru_prompt_kernel(x_ref, gate_ref, wc_ref, bc_ref, wa_ref, wx_ref, ba_ref, bx_ref, lam_ref,
                         y_ref, tail_ref, hl_ref, h_scr, tail_scr):
    c = pl.program_id(1)
    tc = x_ref.shape[0]

    @pl.when(c == 0)
    def _():
        h_scr[...] = jnp.zeros_like(h_scr)
        tail_scr[...] = jnp.zeros_like(tail_scr)

    x = x_ref[...]
    prev = tail_scr[...]
    row8 = lax.broadcasted_iota(jnp.int32, (SUBLANES, 1), 0)
    row = lax.broadcasted_iota(jnp.int32, (tc, 1), 0)

    def shifted(s):
        rolled = pltpu.roll(x, s, 0)
        top = jnp.where(row8 < s, pltpu.roll(prev, s, 0), rolled[:SUBLANES])
        return jnp.concatenate([top, rolled[SUBLANES:]], axis=0)

    wc = wc_ref[...]
    conv = wc[0:1] * shifted(3)
    conv = conv + wc[1:2] * shifted(2)
    conv = conv + wc[2:3] * shifted(1)
    conv = conv + wc[3:4] * x
    xc = bc_ref[...] + conv

    is_start = (row + c * tc) == 0
    a, u = _rglru_gates(xc, wa_ref[0], wx_ref[0], ba_ref[...], bx_ref[...], lam_ref[...], is_start)

    d = 1
    while d < SUBLANES:
        keep = row % SUBLANES >= d
        a_sh = jnp.where(keep, pltpu.roll(a, d, 0), 1.0)
        u_sh = jnp.where(keep, pltpu.roll(u, d, 0), 0.0)
        u = a * u_sh + u
        a = a * a_sh
        d *= 2
    carry = h_scr[...]
    groups = []
    for g in range(tc // SUBLANES):
        rows = slice(g * SUBLANES, (g + 1) * SUBLANES)
        hg = a[rows] * carry + u[rows]
        carry = hg[SUBLANES - 1:SUBLANES]
        groups.append(hg)
    h = jnp.concatenate(groups, axis=0)

    y_ref[...] = (jax.nn.gelu(gate_ref[...]) * h).astype(BF)
    h_last = h[tc - 1:tc]
    h_scr[...] = h_last
    tail_scr[...] = x[tc - SUBLANES:]
    hl_ref[...] = h_last
    tail_ref[...] = x[tc - SUBLANES:]


def _rglru_prompt(xrg, w_conv, b_conv, wa_bd, wx_bd, b_a, b_x, lam):
    t = xrg.shape[0]
    d_rnn = w_conv.shape[1]
    gw = 2 * LANES
    ng = d_rnn // gw
    tc = min(512, t)
    vec = lambda: pl.BlockSpec((1, gw), lambda g, c: (0, g))
    return pl.pallas_call(
        _rglru_prompt_kernel,
        grid=(ng, t // tc),
        in_specs=[
            pl.BlockSpec((tc, gw), lambda g, c: (c, g)),
            pl.BlockSpec((tc, gw), lambda g, c: (c, g + ng)),
            pl.BlockSpec((CONV_W, gw), lambda g, c: (0, g)),
            vec(),
            pl.BlockSpec((1, gw, gw), lambda g, c: (g, 0, 0)),
            pl.BlockSpec((1, gw, gw), lambda g, c: (g, 0, 0)),
            vec(), vec(), vec(),
        ],
        out_specs=[
            pl.BlockSpec((tc, gw), lambda g, c: (c, g)),
            pl.BlockSpec((SUBLANES, gw), lambda g, c: (0, g)),
            pl.BlockSpec((1, gw), lambda g, c: (0, g)),
        ],
        out_shape=[
            jax.ShapeDtypeStruct((t, d_rnn), BF),
            jax.ShapeDtypeStruct((SUBLANES, d_rnn), F32),
            jax.ShapeDtypeStruct((1, d_rnn), F32),
        ],
        scratch_shapes=[pltpu.VMEM((1, gw), F32), pltpu.VMEM((SUBLANES, gw), F32)],
        compiler_params=_params("parallel", "arbitrary"),
        name="rglru_prompt",
    )(xrg, xrg, w_conv, b_conv.reshape(1, -1), wa_bd, wx_bd,
      b_a.reshape(1, -1), b_x.reshape(1, -1), lam.reshape(1, -1))


def _rglru_sample_kernel(xrg_ref, cp_ref, h0_ref, wc_ref, bc_ref, wa_ref, wx_ref, ba_ref, bx_ref, lam_ref,
                         y_ref, cn_ref, hl_ref, *, n_tok, start0):
    d_rnn = h0_ref.shape[1]
    gw = wa_ref.shape[1]
    wc = wc_ref[...]
    xp = [cp_ref[:, k * d_rnn:(k + 1) * d_rnn] for k in range(CONV_W - 1)]
    xp += [xrg_ref[:, t * 2 * d_rnn:t * 2 * d_rnn + d_rnn] for t in range(n_tok)]
    h = h0_ref[...]
    for t in range(n_tok):
        conv = wc[0:1] * xp[t]
        for k in range(1, CONV_W):
            conv = conv + wc[k:k + 1] * xp[t + k]
        xc = bc_ref[...] + conv
        a_parts, u_parts = [], []
        for g in range(d_rnn // gw):
            sl = slice(g * gw, (g + 1) * gw)
            a_g, u_g = _rglru_gates(xc[:, sl], wa_ref[g], wx_ref[g], ba_ref[:, sl], bx_ref[:, sl],
                                    lam_ref[:, sl], True if (start0 and t == 0) else None)
            a_parts.append(a_g)
            u_parts.append(u_g)
        a = jnp.concatenate(a_parts, axis=1)
        u = jnp.concatenate(u_parts, axis=1)
        h = a * h + u
        gate = xrg_ref[:, t * 2 * d_rnn + d_rnn:(t + 1) * 2 * d_rnn]
        y_ref[:, t * d_rnn:(t + 1) * d_rnn] = (jax.nn.gelu(gate) * h).astype(BF)
    hl_ref[...] = h
    tail = xp[-(CONV_W - 1):]
    for k in range(CONV_W - 1):
        cn_ref[:, k * d_rnn:(k + 1) * d_rnn] = tail[k]


def _rglru_sample(xrg, conv_prev, h0, w_conv, b_conv, wa_bd, wx_bd, b_a, b_x, lam, start0):
    b, d_rnn = h0.shape
    n_tok = xrg.shape[1] // (2 * d_rnn)
    args = (xrg, conv_prev, h0, w_conv, b_conv.reshape(1, -1), wa_bd, wx_bd,
            b_a.reshape(1, -1), b_x.reshape(1, -1), lam.reshape(1, -1))
    full = lambda a: pl.BlockSpec(a.shape, lambda i: (0,) * a.ndim)
    outs = [jax.ShapeDtypeStruct((b, n_tok * d_rnn), BF),
            jax.ShapeDtypeStruct((b, (CONV_W - 1) * d_rnn), F32),
            jax.ShapeDtypeStruct((b, d_rnn), F32)]
    return pl.pallas_call(
        functools.partial(_rglru_sample_kernel, n_tok=n_tok, start0=start0),
        grid=(1,),
        in_specs=[full(a) for a in args],
        out_specs=[full(o) for o in outs],
        out_shape=outs,
        compiler_params=_params("arbitrary"),
        name="rglru_sample",
    )(*args)


def _compress_kernel(x_ref, pe_ref, w1_ref, w2_ref, pk_ref, o_ref, scr):
    x = x_ref[...]
    rows = x.shape[0]
    xb = (x.reshape(rows // SUBLANES, SUBLANES, CMP_ROW_W) + pe_ref[...][None]).reshape(rows, CMP_ROW_W)
    full = _dot(xb.astype(BF), w1_ref[...])
    hid = full[:, :KV_W] + pltpu.roll(full[:, KV_W:], rows - 1, 0)
    out = _dot(jax.nn.gelu(hid).astype(BF), w2_ref[...])
    nb = rows // 4
    ob = out.astype(BF)
    for k in range(KV_HEADS):
        scr[...] = _dot(ob, pk_ref[k])
        o_ref[k, 0] = scr[pl.ds(0, nb, stride=4), :].astype(BF)
        o_ref[k, 1] = scr[pl.ds(2, nb, stride=4), :].astype(BF)


def _compress_prompt(kv, pe2, w1, w2, pk):
    t = kv.shape[0]
    rows = 128
    x = kv.reshape(t // CMP_ROW, CMP_ROW_W)
    nb = rows // 4
    nbs = t // SEL_BLOCK
    const = lambda a: pl.BlockSpec(a.shape, lambda i: (0,) * a.ndim)
    return pl.pallas_call(
        _compress_kernel,
        grid=(x.shape[0] // rows,),
        in_specs=[pl.BlockSpec((rows, CMP_ROW_W), lambda i: (i, 0)),
                  const(pe2), const(w1), const(w2), const(pk)],
        out_specs=pl.BlockSpec((KV_HEADS, 2, nb, LANES), lambda i: (0, 0, i, 0)),
        out_shape=jax.ShapeDtypeStruct((KV_HEADS, 2, nbs, LANES), BF),
        scratch_shapes=[pltpu.VMEM((rows, LANES), F32)],
        compiler_params=_params("parallel"),
        name="compress_prompt",
    )(x, pe2, w1, w2, pk)


def _compress_paged_kernel(pt_ref, *refs, pages):
    del pt_ref
    pe_ref = refs[2 * pages]
    wk1_ref, wk2_ref, wv1_ref, wv2_ref, ok_ref, ov_ref, xk_scr, xv_scr, outk_scr, outv_scr = refs[2 * pages + 1:]
    _compress_pages(refs[:pages], pe_ref, wk1_ref, wk2_ref, ok_ref, xk_scr, outk_scr)
    _compress_pages(refs[pages:2 * pages], pe_ref, wv1_ref, wv2_ref, ov_ref, xv_scr, outv_scr)


def _compress_pages(x_refs, pe_ref, w1_ref, w2_ref, o_ref, xs_scr, out_scr):
    pages = len(x_refs)
    halves = KV_W // LANES
    for p in range(pages):
        xt = (x_refs[p][0] + pe_ref[...]).astype(BF).T
        for h in range(halves):
            xs_scr[h, p * PAGE_SIZE:(p + 1) * PAGE_SIZE, :] = xt[:, h * LANES:(h + 1) * LANES]
    nblk = pages * (PAGE_SIZE // CMP_BLOCK)
    by_tok = [pltpu.einshape("mld->lmd", xs_scr[h].reshape(nblk, CMP_BLOCK, LANES)) for h in range(halves)]
    acc = None
    for l2 in range(CMP_BLOCK // 2):
        parts = [jnp.concatenate([by_tok[h][2 * l2], by_tok[h][2 * l2 + 1]], axis=1) for h in range(halves)]
        d = _dot(jnp.concatenate(parts, axis=0).astype(BF), w1_ref[l2])
        acc = d if acc is None else acc + d
    out_scr[...] = _dot(jax.nn.gelu(acc).astype(BF), w2_ref[...])
    for h in range(halves):
        for par in range(2):
            o_ref[0, par, :, h * LANES:(h + 1) * LANES] = (
                out_scr[pl.ds(h * nblk + par, nblk // 2, stride=2), :].astype(BF))


def _compress_paged(cache_k, cache_v, page_table, pe_tok, wk1, wk2, wv1, wv2):
    b, n_pages = page_table.shape
    pages = CMP_PAGES
    view = lambda c: jnp.transpose(c, (0, 2, 3, 1)).reshape(c.shape[0], KV_W, PAGE_SIZE)
    xk, xv = view(cache_k), view(cache_v)
    nblk = pages * (PAGE_SIZE // CMP_BLOCK)
    halves = KV_W // LANES
    nbs = n_pages * PAGE_SIZE // SEL_BLOCK
    const = lambda a: pl.BlockSpec(a.shape, lambda i, s, pt: (0,) * a.ndim)
    page_specs = [
        pl.BlockSpec((1, KV_W, PAGE_SIZE),
                     functools.partial(lambda i, s, pt, p: (pt[i, s * pages + p], 0, 0), p=p))
        for p in range(pages)]
    return pl.pallas_call(
        functools.partial(_compress_paged_kernel, pages=pages),
        grid_spec=pltpu.PrefetchScalarGridSpec(
            num_scalar_prefetch=1,
            grid=(b, n_pages // pages),
            in_specs=page_specs + page_specs + [const(a) for a in (pe_tok, wk1, wk2, wv1, wv2)],
            out_specs=[pl.BlockSpec((1, 2, nblk // 2, KV_W), lambda i, s, pt: (i, 0, s, 0))] * 2,
            scratch_shapes=[pltpu.VMEM((halves, pages * PAGE_SIZE, LANES), BF)] * 2
                           + [pltpu.VMEM((halves * nblk, LANES), F32)] * 2,
        ),
        out_shape=[jax.ShapeDtypeStruct((b, 2, nbs, KV_W), BF)] * 2,
        compiler_params=_params("parallel", "arbitrary"),
        name="compress_paged",
    )(page_table, *([xk] * pages), *([xv] * pages), pe_tok, wk1, wk2, wv1, wv2)


def _topk_columns(imp_t, n_pick, quota=None):
    nblk = imp_t.shape[0]
    idx = lax.broadcasted_iota(jnp.int32, imp_t.shape, 0)
    left = imp_t
    for n in range(n_pick):
        mx = jnp.max(left, axis=0, keepdims=True)
        first = jnp.min(jnp.where(left == mx, idx, nblk), axis=0, keepdims=True)
        if quota is not None:
            first = jnp.where(n < quota, first, nblk)
        left = jnp.where(idx == first, -jnp.inf, left)
    return jnp.where((left == -jnp.inf) & (imp_t != -jnp.inf), 1.0, 0.0)


def _softmax_cols(s):
    m = jnp.max(s, axis=0, keepdims=True)
    m = jnp.where(m == -jnp.inf, 0.0, m)
    e = jnp.exp2(s - m)
    den = jnp.sum(e, axis=0, keepdims=True)
    return e * (1.0 / jnp.maximum(den, 1e-30))


def _nsa_prompt_kernel(*refs):
    (q_ref, kc_ref, vc_ref, ks_ref, vst_ref) = refs[:5]
    wb = WINDOW // Q_BLOCK + 1
    kw_refs = refs[5:5 + wb]
    vwt_refs = refs[5 + wb:5 + 2 * wb]
    (gn_ref, pa_ref, o_ref,
     bias_scr, gate_scr, sa_scr, sb_scr, m_scr, acc_scr, oc_scr, ow_scr) = refs[5 + 2 * wb:]
    kvh = pl.program_id(0)
    i = pl.program_id(1)
    nq = Q_BLOCK
    r = GROUP * nq
    hd = HEAD_DIM
    q = q_ref[...].reshape(r, LANES)
    lane = lax.broadcasted_iota(jnp.int32, (1, r), 1)
    qpos = i * nq + lane % nq

    nbs = kc_ref.shape[1] // 2

    def window_and_gates():
        kw = jnp.concatenate([ref[0] for ref in kw_refs], axis=0)
        vwt = jnp.concatenate([ref[0] for ref in vwt_refs], axis=1)
        s_w = _dot_nt(kw, q)
        nwb = len(kw_refs)
        krow = lax.broadcasted_iota(jnp.int32, (nq, 1), 0)
        parts = []
        for b in range(nwb):
            blk_pos = (i - (nwb - 1) + b) * nq
            mask = blk_pos >= 0
            if b == 0:
                mask = mask & (qpos - (blk_pos + krow) <= WINDOW)
            if b == nwb - 1:
                mask = mask & (blk_pos + krow <= qpos)
            parts.append(jnp.where(mask, s_w[b * nq:(b + 1) * nq], -jnp.inf))
        s_w = jnp.concatenate(parts, axis=0)
        e_w = jnp.exp2(s_w - jnp.max(s_w, axis=0, keepdims=True))
        acc_w = _dot(vwt, e_w.astype(BF))
        ow_scr[...] = acc_w[:hd] * (1.0 / jnp.maximum(acc_w[hd:hd + 1], 1e-30))
        gate_scr[...] = jax.nn.sigmoid(gn_ref[...]).T

    def compressed_and_select(nv, first_quarter):
        window_and_gates()
        kc = jnp.concatenate([kc_ref[0, 0:nv], kc_ref[0, nbs:nbs + nv]], axis=0)
        vc = jnp.concatenate([vc_ref[0, 0:nv], vc_ref[0, nbs:nbs + nv]], axis=0)
        rowc = lax.broadcasted_iota(jnp.int32, (2 * nv, 1), 0)
        blk = 2 * (rowc % nv) + rowc // nv
        p_c = _softmax_cols(jnp.where(blk * CMP_BLOCK + (CMP_BLOCK - 1) <= qpos, _dot_nt(kc, q), -jnp.inf))
        vct = vc.astype(F32).T[:hd].astype(BF)
        oc_scr[...] = _dot(vct, p_c.astype(BF))
        imp = p_c[:, 0:nq]
        for g in range(1, GROUP):
            imp = imp + p_c[:, g * nq:(g + 1) * nq]
        imp = imp[:nv] + imp[nv:]
        qp = i * nq + lax.broadcasted_iota(jnp.int32, (1, nq), 1)
        jb = lax.broadcasted_iota(jnp.int32, (nv, 1), 0)
        cur = qp // SEL_BLOCK
        valid = jb * SEL_BLOCK <= qp
        forced = (jb == 0) | (jb == cur) | (jb == cur - 1)
        n_forced = 1 + jnp.where(cur >= 1, 1, 0) + jnp.where(cur >= 2, 1, 0)
        n_sel = min(N_SEL, nbs)
        n_iter = n_sel - (1 if first_quarter else 3)
        picks = _topk_columns(jnp.where(valid & ~forced, imp, -jnp.inf), n_iter, n_sel - n_forced)
        sel = jnp.where(forced, 1.0, picks)
        bias = jnp.where(valid & (sel > 0.5), 0.0, NEG).T.astype(BF)
        nt = nv // BLOCKS_PER_TILE
        bias_all = _dot(bias, pa_ref[0:nv, 0:nt * LANES]).astype(BF)
        for t in range(nt):
            bias_scr[t] = bias_all[:, t * LANES:(t + 1) * LANES]

    n_var = 4
    variant = (i * n_var) // pl.num_programs(1)
    for v in range(n_var):
        pl.when(variant == v)(functools.partial(compressed_and_select, nbs * (v + 1) // n_var, v == 0))
    o_c = oc_scr[...]

    def scores(t):
        k0 = pl.multiple_of(t * SEL_TILE, SEL_TILE)
        kt = ks_ref[0, pl.ds(k0, SEL_TILE), :]
        qa = q + jnp.concatenate([bias_scr[t]] * GROUP, axis=0)
        return _dot_nt(kt, qa)

    def update(t, s_ref, diagonal):
        vt = vst_ref[0, t]
        cw = 2 * LANES
        for c in range(r // cw):
            sl = slice(c * cw, (c + 1) * cw)
            sg = s_ref[:, sl]
            if diagonal:
                kpos = t * SEL_TILE + lax.broadcasted_iota(jnp.int32, (SEL_TILE, 1), 0)
                sg = jnp.where(kpos <= qpos[:, sl], sg, NEG)
            m_old = m_scr[:, sl]
            m_new = jnp.maximum(m_old, jnp.max(sg, axis=0, keepdims=True))
            p = jnp.exp2(sg - m_new)
            acc_scr[:, sl] = jnp.exp2(m_old - m_new) * acc_scr[:, sl] + _dot(vt, p.astype(BF))
            m_scr[:, sl] = m_new

    def pair(t0):
        sb_scr[...] = scores(t0 + 1)
        update(t0, sa_scr, False)
        sa_scr[...] = scores(t0 + 2)
        update(t0 + 1, sb_scr, False)

    def octet(u, _):
        for n in range(4):
            pair(8 * u + 2 * n)
        return 0

    t_diag = (i * nq) // SEL_TILE
    m_scr[...] = jnp.full(m_scr.shape, NEG, F32)
    acc_scr[...] = jnp.zeros_like(acc_scr)
    sa_scr[...] = scores(0)
    lax.fori_loop(0, t_diag // 8, octet, 0)

    @pl.when(t_diag % 8 >= 4)
    def _():
        pair(t_diag - t_diag % 8)
        pair(t_diag - t_diag % 8 + 2)

    @pl.when(t_diag % 4 >= 2)
    def _():
        pair(t_diag - t_diag % 4)

    @pl.when(t_diag % 2 == 1)
    def _():
        sb_scr[...] = scores(t_diag)
        update(t_diag - 1, sa_scr, False)
        update(t_diag, sb_scr, True)

    @pl.when(t_diag % 2 == 0)
    def _():
        update(t_diag, sa_scr, True)

    o_s = acc_scr[:hd] * (1.0 / acc_scr[hd:hd + 1])

    o_w = ow_scr[...]

    def gate(br):
        return jnp.concatenate(
            [gate_scr[pl.ds(br * N_HEADS + kvh * GROUP + g, 1), :] for g in range(GROUP)], axis=1)

    comb = (gate(0) * o_c + gate(1) * o_s + gate(2) * o_w).astype(BF)
    o_ref[...] = jnp.concatenate([comb[:, g * nq:(g + 1) * nq] for g in range(GROUP)], axis=0)


def _nsa_prompt(qa, kc, vc, ksa, vst, kwa, vwt, gn, pall):
    t = qa.shape[1]
    n_tiles = t // SEL_TILE
    wb = WINDOW // Q_BLOCK + 1
    kv_full = lambda a: pl.BlockSpec((1,) + a.shape[1:], lambda k, i: (k,) + (0,) * (a.ndim - 1))
    wblk = lambda i, m: jnp.maximum(i - (wb - 1) + m, 0)
    win_k = [pl.BlockSpec((1, Q_BLOCK, LANES), functools.partial(lambda k, i, m: (k, wblk(i, m), 0), m=m))
             for m in range(wb)]
    win_v = [pl.BlockSpec((1, SEL_V_ROWS, Q_BLOCK), functools.partial(lambda k, i, m: (k, 0, wblk(i, m)), m=m))
             for m in range(wb)]
    const = lambda a: pl.BlockSpec(a.shape, lambda k, i: (0,) * a.ndim)
    return pl.pallas_call(
        _nsa_prompt_kernel,
        grid=(KV_HEADS, t // Q_BLOCK),
        in_specs=[pl.BlockSpec((GROUP, Q_BLOCK, LANES), lambda k, i: (k, i, 0)),
                  kv_full(kc), kv_full(vc), kv_full(ksa), kv_full(vst)]
                 + win_k + win_v
                 + [pl.BlockSpec((Q_BLOCK, LANES), lambda k, i: (i, 0)), const(pall)],
        out_specs=pl.BlockSpec((KV_W, Q_BLOCK), lambda k, i: (k, i)),
        out_shape=jax.ShapeDtypeStruct((ATTN_W, t), BF),
        scratch_shapes=[pltpu.VMEM((n_tiles, Q_BLOCK, LANES), BF), pltpu.VMEM((LANES, Q_BLOCK), F32),
                        pltpu.VMEM((SEL_TILE, GROUP * Q_BLOCK), F32), pltpu.VMEM((SEL_TILE, GROUP * Q_BLOCK), F32),
                        pltpu.VMEM((1, GROUP * Q_BLOCK), F32),
                        pltpu.VMEM((SEL_V_ROWS, GROUP * Q_BLOCK), F32),
                        pltpu.VMEM((HEAD_DIM, GROUP * Q_BLOCK), F32), pltpu.VMEM((HEAD_DIM, GROUP * Q_BLOCK), F32)],
        compiler_params=_params("parallel", "arbitrary"),
        name="nsa_prompt",
    )(qa, kc, vc, ksa, vst, *([kwa] * wb), *([vwt] * wb), gn, pall)


def _nsa_sample_kernel(*refs, pages, past, n_tok):
    pt_ref = refs[0]
    q_ref, kc_ref, vc_ref = refs[1:4]
    ks_refs = refs[4:4 + pages]
    vs_refs = refs[4 + pages:4 + 2 * pages]
    (ksn_ref, vsn_ref, kwn_ref, vwn_ref, wk_ref, wv_ref, gn_ref,
     pg_ref, rp_ref, eg_ref, o_ref, q_scr, bias_scr, m_scr, l_scr, acc_scr, oc_scr) = refs[4 + 2 * pages:]
    del pt_ref
    s_id = pl.program_id(1)
    tp = SUBLANES
    r = N_HEADS * tp
    row = lax.broadcasted_iota(jnp.int32, (r, 1), 0)
    qpos = past + row % tp
    nbs = bias_scr.shape[1]

    @pl.when(s_id == 0)
    def _():
        qb = q_ref[0].astype(BF)
        for h in range(N_HEADS):
            k = h // GROUP
            q_scr[h * tp:(h + 1) * tp, :] = _dot(qb[:, k * KV_W:(k + 1) * KV_W], pg_ref[h]).astype(BF)
        q = q_scr[...]
        kc = kc_ref[0].reshape(2 * nbs, KV_W)
        vc = vc_ref[0].reshape(2 * nbs, KV_W)
        col = lax.broadcasted_iota(jnp.int32, (1, 2 * nbs), 1)
        blk = 2 * (col % nbs) + col // nbs
        p_c = _masked_softmax(_dot_nt(q, kc), blk * CMP_BLOCK + (CMP_BLOCK - 1) <= qpos)
        oc_scr[...] = _dot(p_c.astype(BF), vc)
        imps = []
        for k in range(KV_HEADS):
            base = k * GROUP * tp
            imp = p_c[base:base + tp]
            for g in range(1, GROUP):
                imp = imp + p_c[base + g * tp:base + (g + 1) * tp]
            imps.append(imp)
        imp = jnp.concatenate(imps, axis=0)
        imp = imp[:, :nbs] + imp[:, nbs:]
        jb = lax.broadcasted_iota(jnp.int32, (1, nbs), 1)
        imp = jnp.where((jb == 0) | (jb == nbs - 1), jnp.inf, imp)
        idx = lax.broadcasted_iota(jnp.int32, imp.shape, 1)
        sel = jnp.zeros(imp.shape, F32)
        for _ in range(min(N_SEL, nbs + 1) - 1):
            mx = jnp.max(imp, axis=1, keepdims=True)
            first = jnp.min(jnp.where(imp == mx, idx, nbs), axis=1, keepdims=True)
            hit = idx == first
            sel = jnp.where(hit, 1.0, sel)
            imp = jnp.where(hit, -jnp.inf, imp)
        bias = jnp.where(sel > 0.5, 0.0, NEG).astype(BF)
        for k in range(KV_HEADS):
            for g in range(GROUP):
                h = k * GROUP + g
                bias_scr[h * tp:(h + 1) * tp, :] = bias[k * tp:(k + 1) * tp]
        m_scr[...] = jnp.full(m_scr.shape, NEG, F32)
        l_scr[...] = jnp.zeros_like(l_scr)
        acc_scr[...] = jnp.zeros_like(acc_scr)

    def online(st, v, v_transposed):
        m = m_scr[...]
        m_new = jnp.maximum(m, jnp.max(st, axis=1, keepdims=True))
        scale = jnp.exp(m - m_new)
        p = jnp.exp(st - m_new)
        l_scr[...] = scale * l_scr[...] + jnp.sum(p, axis=1, keepdims=True)
        pv = _dot_nt(p.astype(BF), v) if v_transposed else _dot(p.astype(BF), v)
        acc_scr[...] = scale * acc_scr[...] + pv
        m_scr[...] = m_new

    q = q_scr[...]
    kt = jnp.concatenate([ref[0].astype(BF) for ref in ks_refs], axis=1)
    vt = jnp.concatenate([ref[0].astype(BF) for ref in vs_refs], axis=1)
    n_keys = kt.shape[1]
    jb = lax.broadcasted_iota(jnp.int32, (nbs, 1), 0)
    key_blk = s_id * (n_keys // SEL_BLOCK) + lax.broadcasted_iota(jnp.int32, (1, n_keys), 1) // SEL_BLOCK
    expand = jnp.where(jb == key_blk, 1.0, 0.0).astype(BF)
    online(_dot(q, kt) + _dot(bias_scr[...], expand), vt, True)

    @pl.when(s_id == pl.num_programs(1) - 1)
    def _():
        zeros = jnp.zeros((LANES - tp, KV_W), BF)
        tok = lax.broadcasted_iota(jnp.int32, (1, LANES), 1)
        ksn = jnp.concatenate([ksn_ref[0].astype(BF), zeros], axis=0)
        vsn = jnp.concatenate([vsn_ref[0].astype(BF), zeros], axis=0)
        online(jnp.where((past + tok <= qpos) & (tok < n_tok), _dot_nt(q, ksn), NEG), vsn, False)
        o_s = acc_scr[...] / l_scr[...]
        kw = jnp.concatenate([wk_ref[0].astype(BF), kwn_ref[0].astype(BF), zeros], axis=0)
        vw = jnp.concatenate([wv_ref[0].astype(BF), vwn_ref[0].astype(BF), zeros], axis=0)
        n_w = kw.shape[0]
        wcol = lax.broadcasted_iota(jnp.int32, (1, n_w), 1)
        kwpos = past - WINDOW + wcol
        dpos = qpos - kwpos
        mask_w = (dpos >= 0) & (dpos <= WINDOW) & (kwpos >= 0) & (wcol < WINDOW + n_tok)
        p_w = _masked_softmax(_dot_nt(q, kw), mask_w)
        o_w = _dot(p_w.astype(BF), vw)
        sg = jax.nn.sigmoid(gn_ref[0])
        gexp = _expand_f32(sg, eg_ref[...])

        def gate(br):
            return jnp.concatenate(
                [gexp[:, (br * N_HEADS + h) * KV_W:(br * N_HEADS + h + 1) * KV_W] for h in range(N_HEADS)],
                axis=0)

        comb = (gate(0) * oc_scr[...] + gate(1) * o_s + gate(2) * o_w).astype(BF)
        for k in range(KV_HEADS):
            out = None
            for g in range(GROUP):
                h = k * GROUP + g
                part = _dot(comb[h * tp:(h + 1) * tp], rp_ref[h])
                out = part if out is None else out + part
            o_ref[0, :, k * KV_W:(k + 1) * KV_W] = out


def _nsa_sample(page_table, q8, kc, vc, cache_ks, cache_vs, new8, win_k, win_v, gn8, pgk, rkg, egate, n_tok):
    b, n_pages = page_table.shape
    past = n_pages * PAGE_SIZE
    pages = SEL_PAGES
    tp = SUBLANES
    r = N_HEADS * tp
    nbs = past // SEL_BLOCK
    ck = jnp.transpose(cache_ks, (0, 2, 3, 1)).reshape(cache_ks.shape[0], KV_W, PAGE_SIZE)
    cv = jnp.transpose(cache_vs, (0, 2, 3, 1)).reshape(cache_vs.shape[0], KV_W, PAGE_SIZE)
    per_seq = lambda a: pl.BlockSpec((1,) + a.shape[1:], lambda i, s, pt: (i,) + (0,) * (a.ndim - 1))
    const = lambda a: pl.BlockSpec(a.shape, lambda i, s, pt: (0,) * a.ndim)
    page_specs = [
        pl.BlockSpec((1, KV_W, PAGE_SIZE),
                     functools.partial(lambda i, s, pt, p: (pt[i, s * pages + p], 0, 0), p=p))
        for p in range(pages)]
    return pl.pallas_call(
        functools.partial(_nsa_sample_kernel, pages=pages, past=past, n_tok=n_tok),
        grid_spec=pltpu.PrefetchScalarGridSpec(
            num_scalar_prefetch=1,
            grid=(b, n_pages // pages),
            in_specs=[per_seq(q8), per_seq(kc), per_seq(vc)] + page_specs + page_specs
                     + [per_seq(a) for a in new8] + [per_seq(win_k), per_seq(win_v), per_seq(gn8),
                                                     const(pgk), const(rkg), const(egate)],
            out_specs=pl.BlockSpec((1, tp, ATTN_W), lambda i, s, pt: (i, 0, 0)),
            scratch_shapes=[pltpu.VMEM((r, KV_W), BF), pltpu.VMEM((r, nbs), BF),
                            pltpu.VMEM((r, 1), F32), pltpu.VMEM((r, 1), F32),
                            pltpu.VMEM((r, KV_W), F32), pltpu.VMEM((r, KV_W), F32)],
        ),
        out_shape=jax.ShapeDtypeStruct((b, tp, ATTN_W), F32),
        compiler_params=_params("parallel", "arbitrary"),
        name="nsa_sample",
    )(page_table, q8, kc, vc, *([ck] * pages), *([cv] * pages), *new8, win_k, win_v, gn8, pgk, rkg, egate)


def _merge_kernel(x_ref, y_ref, o_ref, wgr_ref, wga_ref, wr_ref, wa_ref, wo_ref, g_ref, b_ref,
                  out_ref, xb_ref, acc_ref, *ob_ref):
    j = pl.program_id(1)

    @pl.when(j == 0)
    def _():
        xb_ref[...] = x_ref[...].astype(BF)
        acc_ref[...] = jnp.zeros_like(acc_ref)
        if ob_ref:
            ob_ref[0][...] = o_ref[...].astype(F32).T.astype(BF)

    xb = xb_ref[...]
    rec = _dot(y_ref[...], wr_ref[...])
    att = _dot(ob_ref[0][...] if ob_ref else o_ref[...], wa_ref[...])
    u = jax.nn.sigmoid(_dot(xb, wgr_ref[...])) * rec + jax.nn.sigmoid(_dot(xb, wga_ref[...])) * att
    acc_ref[...] += _dot(u.astype(BF), wo_ref[...])

    @pl.when(j == pl.num_programs(1) - 1)
    def _():
        out_ref[...] = _layer_norm(ALPHA * x_ref[...] + acc_ref[...], g_ref[...], b_ref[...])


def _merge(x, y_rec, o_attn, w_grec, w_gatt, w_rec_o, w_attn_o, w_out, g, b, o_transposed):
    m, d = x.shape
    tm = min(512, m)
    tn = 512
    col = lambda a: pl.BlockSpec((a.shape[0], tn), lambda i, j: (0, j))
    rowb = lambda a: pl.BlockSpec((tm, a.shape[1]), lambda i, j: (i, 0))
    vec = pl.BlockSpec((1, d), lambda i, j: (0, 0))
    if o_transposed:
        o_spec = pl.BlockSpec((o_attn.shape[0], tm), lambda i, j: (0, i))
        o_scratch = [pltpu.VMEM((tm, o_attn.shape[0]), BF)]
    else:
        o_spec, o_scratch = rowb(o_attn), []
    return pl.pallas_call(
        _merge_kernel,
        grid=(m // tm, d // tn),
        in_specs=[rowb(x), rowb(y_rec), o_spec, col(w_grec), col(w_gatt), col(w_rec_o), col(w_attn_o),
                  pl.BlockSpec((tn, d), lambda i, j: (j, 0)), vec, vec],
        out_specs=pl.BlockSpec((tm, d), lambda i, j: (i, 0)),
        out_shape=jax.ShapeDtypeStruct((m, d), F32),
        scratch_shapes=[pltpu.VMEM((tm, d), BF), pltpu.VMEM((tm, d), F32)] + o_scratch,
        compiler_params=_params("parallel", "arbitrary"),
        name="merge_out",
    )(x, y_rec, o_attn, w_grec, w_gatt, w_rec_o, w_attn_o, w_out, g.reshape(1, d), b.reshape(1, d))


def _place(n_rows, n_cols, src0, dst0, width, value=1.0):
    m = np.zeros((n_rows, n_cols), np.float32)
    m[src0 + np.arange(width), dst0 + np.arange(width)] = value
    return m


def _layout_constants(n_sel_blocks_prompt):
    hd = HEAD_DIM
    scale = hd ** -0.5
    pq = np.stack([_place(LANES, LANES, e * hd, 0, hd) for e in range(2)])
    pk = np.stack([_place(KV_W, LANES, k * hd, 0, hd) for k in range(KV_HEADS)])
    pkt = np.transpose(pk, (0, 2, 1))[:, :SEL_V_ROWS]
    nbs = n_sel_blocks_prompt
    n_tiles = -(-nbs // BLOCKS_PER_TILE)
    pall = np.zeros((nbs, n_tiles * LANES), np.float32)
    j = np.arange(nbs)
    pall[j, (j // BLOCKS_PER_TILE) * LANES + hd + j % BLOCKS_PER_TILE] = 1.0
    pgk = np.stack([_place(KV_W, KV_W, (h % GROUP) * hd, (h // GROUP) * hd, hd, scale) for h in range(N_HEADS)])
    rkg = np.stack([_place(KV_W, KV_W, (h // GROUP) * hd, (h % GROUP) * hd, hd) for h in range(N_HEADS)])
    egs = np.zeros((LANES, N_NSA_BRANCH * N_HEADS * KV_W), np.float32)
    for c in range(N_NSA_BRANCH * N_HEADS):
        egs[c, c * KV_W:(c + 1) * KV_W] = 1.0
    as_bf = lambda a: jnp.asarray(a, BF)
    return dict(pq=as_bf(pq), pk=as_bf(pk), pkt=as_bf(pkt), pall=as_bf(pall),
                pgk=as_bf(pgk), rkg=as_bf(rkg), egs=as_bf(egs))


def _block_diag(w, per_group):
    nb, c, _ = w.shape
    eye = jnp.eye(per_group, dtype=w.dtype)
    wg = w.reshape(nb // per_group, per_group, c, c)
    return jnp.einsum('gpcd,pq->gpcqd', wg, eye).reshape(nb // per_group, per_group * c, per_group * c)


def _compress_weights(w1, w2, pe):
    eye = jnp.eye(KV_HEADS, dtype=w1.dtype)
    big = jnp.einsum('lde,kq->lkdqe', w1, eye).reshape(CMP_BLOCK * KV_W, KV_HEADS * w1.shape[2])
    half = big.shape[0] // 2
    w1cat = jnp.concatenate([big[:half], big[half:]], axis=1).astype(BF)
    w2bd = _block_diag(jnp.broadcast_to(w2, (KV_HEADS,) + w2.shape), KV_HEADS)[0].astype(BF)
    pe_flat = jnp.broadcast_to(pe[:, None, :], (CMP_BLOCK, KV_HEADS, HEAD_DIM)).reshape(2, half)
    pe2 = jnp.tile(pe_flat, (SUBLANES // 2, 1))
    return w1cat, w2bd, pe2


def _compress_weights_paged(w1, w2, pe):
    per_half = LANES // HEAD_DIM
    eye = jnp.eye(per_half, dtype=w1.dtype)
    w1h = jnp.einsum('lde,kq->lkdqe', w1, eye).reshape(CMP_BLOCK, LANES, LANES)
    w1p = w1h.reshape(CMP_BLOCK // 2, 2 * LANES, LANES).astype(BF)
    w2p = jnp.einsum('ed,kq->keqd', w2, eye).reshape(LANES, LANES).astype(BF)
    pe_tok = jnp.tile(pe, (PAGE_SIZE // CMP_BLOCK, KV_HEADS)).T
    return w1p, w2p, pe_tok


def kernel(x_prompt, x_sample, cache_k_cmp, cache_v_cmp, cache_k_sel, cache_v_sel, page_table,
           state_win_k, state_win_v, state_conv, state_h,
           ln1_g, ln1_b, w_ffn1_up, w_ffn1_down, w_in, w_conv, b_conv, w_rg_a, b_rg_a, w_rg_x, b_rg_x,
           rg_lambda, cmp_pe, w_ck1, w_ck2, w_cv1, w_cv2, w_rec_o, w_attn_o, w_out,
           ln2_g, ln2_b, w_ffn2_up, w_ffn2_down, ln3_g, ln3_b):
    bp, t, d = x_prompt.shape
    bs, n_tok, _ = x_sample.shape
    d_rnn = w_conv.shape[1]
    assert bp == 1 and t % (128 * CMP_ROW) == 0 and t >= WINDOW
    past = page_table.shape[1] * PAGE_SIZE
    assert page_table.shape[1] % max(CMP_PAGES, SEL_PAGES) == 0 and CONV_W - 1 <= n_tok <= SUBLANES
    assert state_win_k.shape[1] == WINDOW

    o_q = 2 * d_rnn
    o_kv = o_q + ATTN_W
    o_gn = o_kv + 6 * KV_W
    o_gr = o_gn + N_NSA_BRANCH * N_HEADS
    o_ga = o_gr + d
    w_in_b = w_in.astype(BF)
    w_rnn = w_in_b[:, :o_q]
    w_q = w_in_b[:, o_q:o_kv]
    w_kv = w_in_b[:, o_kv:o_gn]
    w_gn = jnp.pad(w_in_b[:, o_gn:o_gr], ((0, 0), (0, LANES - (o_gr - o_gn))))
    w_grec = w_in_b[:, o_gr:o_ga]
    w_gatt = w_in_b[:, o_ga:]
    w_attn_all = jnp.concatenate([w_in_b[:, :o_gn], jnp.pad(w_gn, ((0, 0), (0, 512 - LANES)))], axis=1)
    up1, down1 = w_ffn1_up.astype(BF), w_ffn1_down.astype(BF)
    up2, down2 = w_ffn2_up.astype(BF), w_ffn2_down.astype(BF)
    per_group = 2 * LANES // (d_rnn // RNN_BLOCKS)
    wa_bd = _block_diag(w_rg_a, per_group).astype(BF)
    wx_bd = _block_diag(w_rg_x, per_group).astype(BF)
    ck1, ck2, pe2 = _compress_weights(w_ck1, w_ck2, cmp_pe)
    cv1, cv2, _ = _compress_weights(w_cv1, w_cv2, cmp_pe)
    w_rec_b, w_attn_b, w_out_b = w_rec_o.astype(BF), w_attn_o.astype(BF), w_out.astype(BF)
    cst = _layout_constants(t // SEL_BLOCK)

    xp = _ffn_ln(x_prompt.reshape(t, d), up1, down1, ln1_g, ln1_b)
    xrg = _matmul(xp, w_rnn, w_rnn.shape[1])
    (p_kc, p_vc, kct, vct, kst, vst32, kwt, vwt32, p_gn, qa, ksa, vst, kwa, vwt) = _proj_attn(
        xp, w_q, w_kv, w_gn, cst['pq'], cst['pk'], cst['pkt'])
    y_rec, p_tail, p_h = _rglru_prompt(xrg, w_conv, b_conv, wa_bd, wx_bd, b_rg_a, b_rg_x, rg_lambda)
    nbs = t // SEL_BLOCK
    kc = _compress_prompt(p_kc, pe2, ck1, ck2, cst['pk']).reshape(KV_HEADS, 2 * nbs, LANES)
    vc = _compress_prompt(p_vc, pe2, cv1, cv2, cst['pk']).reshape(KV_HEADS, 2 * nbs, LANES)
    o_attn_t = _nsa_prompt(qa, kc, vc, ksa, vst, kwa, vwt, p_gn, cst['pall'])
    x2 = _merge(xp, y_rec, o_attn_t, w_grec, w_gatt, w_rec_b, w_attn_b, w_out_b, ln2_g, ln2_b, True)
    y_prompt = _ffn_ln(x2, up2, down2, ln3_g, ln3_b).reshape(bp, t, d)

    kvh = lambda a: jnp.transpose(a.reshape(bp, KV_HEADS, HEAD_DIM, -1), (0, 3, 1, 2))
    p_states = (kvh(kct), kvh(vct), kvh(kst), kvh(vst32), kvh(kwt[:, t - WINDOW:]), kvh(vwt32[:, t - WINDOW:]),
                p_tail[SUBLANES - (CONV_W - 1):].reshape(bp, CONV_W - 1, d_rnn), p_h.reshape(bp, d_rnn))

    m_s = bs * n_tok
    xs = _ffn_ln(x_sample.reshape(m_s, d), up1, down1, ln1_g, ln1_b)
    zs = _matmul(xs, w_attn_all, 512)
    s_xrg = zs[:, :o_q].reshape(bs, n_tok * o_q)
    seq = lambda a: a.reshape(bs, n_tok, a.shape[-1])
    s_q = seq(zs[:, o_q:o_kv])
    s_kv = [seq(zs[:, o_kv + n * KV_W:o_kv + (n + 1) * KV_W]) for n in range(6)]
    s_gn = seq(zs[:, o_gn:o_gn + LANES])
    pad8 = lambda a: jnp.pad(a, ((0, 0), (0, SUBLANES - n_tok), (0, 0)))
    ys_rec, s_conv, s_h = _rglru_sample(
        s_xrg, state_conv.reshape(bs, -1), state_h, w_conv, b_conv, wa_bd, wx_bd, b_rg_a, b_rg_x, rg_lambda,
        start0=(past == 0))
    ck1p, ck2p, pe_tok = _compress_weights_paged(w_ck1, w_ck2, cmp_pe)
    cv1p, cv2p, _ = _compress_weights_paged(w_cv1, w_cv2, cmp_pe)
    kc_s, vc_s = _compress_paged(cache_k_cmp, cache_v_cmp, page_table, pe_tok, ck1p, ck2p, cv1p, cv2p)
    o8 = _nsa_sample(page_table, pad8(s_q), kc_s, vc_s, cache_k_sel, cache_v_sel,
                     [pad8(s_kv[n]) for n in (2, 3, 4, 5)],
                     state_win_k.reshape(bs, WINDOW, KV_W), state_win_v.reshape(bs, WINDOW, KV_W),
                     pad8(s_gn), cst['pgk'], cst['rkg'], cst['egs'], n_tok)
    os_attn = o8[:, :n_tok].reshape(m_s, ATTN_W).astype(BF)
    x2s = _merge(xs, ys_rec.reshape(m_s, d_rnn), os_attn, w_grec, w_gatt, w_rec_b, w_attn_b, w_out_b, ln2_g, ln2_b,
                 False)
    y_sample = _ffn_ln(x2s, up2, down2, ln3_g, ln3_b).reshape(bs, n_tok, d)

    kvs = lambda a: a.reshape(bs, n_tok, KV_HEADS, HEAD_DIM)
    win = lambda old, new: jnp.concatenate([old, kvs(new)], axis=1)[:, -WINDOW:]
    s_states = (kvs(s_kv[0]), kvs(s_kv[1]), kvs(s_kv[2]), kvs(s_kv[3]),
                win(state_win_k, s_kv[4]), win(state_win_v, s_kv[5]),
                s_conv.reshape(bs, CONV_W - 1, d_rnn), s_h)

    return (y_prompt, y_sample) + p_states + s_states
```

```python
import functools

import numpy as np
import jax
import jax.numpy as jnp
from jax import lax
from jax.experimental import pallas as pl
from jax.experimental.pallas import tpu as pltpu

F32 = jnp.float32
BF = jnp.bfloat16

DEPTH = 1
ALPHA = (2.0 * DEPTH) ** 0.25
N_HEADS = 16
HEAD_DIM = 64
KV_HEADS = 4
GROUP = N_HEADS // KV_HEADS
KV_W = KV_HEADS * HEAD_DIM
ATTN_W = N_HEADS * HEAD_DIM
N_NSA_BRANCH = 3
CMP_BLOCK = 32
SEL_BLOCK = 64
N_SEL = 16
WINDOW = 512
Q_BLOCK = 256
CONV_W = 4
LRU_C = 8.0
RNN_BLOCKS = 16
LN_EPS = 1e-5
PAGE_SIZE = 128

LANES = 128
SUBLANES = 8
VMEM_LIMIT = 56 * 1024 * 1024

NEG = -1e30
SEL_TILE = 512
BLOCKS_PER_TILE = SEL_TILE // SEL_BLOCK
SEL_V_ROWS = HEAD_DIM + 16
CMP_ROW = 16
CMP_PAGES = 16
SEL_PAGES = 32
CMP_ROW_W = CMP_ROW * KV_W

_NT = (((1,), (1,)), ((), ()))


def _dot(a, b):
    return jnp.dot(a, b, preferred_element_type=F32)


def _dot_nt(a, b):
    return lax.dot_general(a, b, _NT, preferred_element_type=F32)


def _params(*sem):
    return pltpu.CompilerParams(dimension_semantics=sem, vmem_limit_bytes=VMEM_LIMIT)


def _layer_norm(y, g, b):
    mu = jnp.mean(y, axis=-1, keepdims=True)
    d = y - mu
    var = jnp.mean(d * d, axis=-1, keepdims=True)
    return d * lax.rsqrt(var + LN_EPS) * g + b


def _masked_softmax(s, mask):
    s = jnp.where(mask, s, -jnp.inf)
    m = jnp.max(s, axis=-1, keepdims=True)
    m = jnp.where(m == -jnp.inf, 0.0, m)
    e = jnp.where(mask, jnp.exp(s - m), 0.0)
    den = jnp.sum(e, axis=-1, keepdims=True)
    return e / jnp.maximum(den, 1e-30)


def _split3(x):
    hi = x.astype(BF)
    r1 = x - hi.astype(F32)
    mid = r1.astype(BF)
    lo = (r1 - mid.astype(F32)).astype(BF)
    return hi, mid, lo


def _expand_f32(x, e):
    hi, mid, lo = _split3(x)
    return _dot(hi, e) + _dot(mid, e) + _dot(lo, e)


def _ffn_kernel(x_ref, wg_ref, wu_ref, wd_ref, g_ref, b_ref, o_ref, xb_ref, acc_ref):
    j = pl.program_id(1)

    @pl.when(j == 0)
    def _():
        xb_ref[...] = x_ref[...].astype(BF)
        acc_ref[...] = jnp.zeros_like(acc_ref)

    xb = xb_ref[...]
    gate = _dot(xb, wg_ref[...])
    up = _dot(xb, wu_ref[...])
    h = (gate * jax.nn.sigmoid(gate) * up).astype(BF)
    acc_ref[...] += _dot(h, wd_ref[...])

    @pl.when(j == pl.num_programs(1) - 1)
    def _():
        y = ALPHA * x_ref[...] + 0.5 * acc_ref[...]
        o_ref[...] = _layer_norm(y, g_ref[...], b_ref[...])


def _ffn_ln(x, w_up, w_down, g, b):
    m, d = x.shape
    f = w_down.shape[0]
    tm = min(512, m)
    tn = 512
    nj = f // tn
    return pl.pallas_call(
        _ffn_kernel,
        grid=(m // tm, nj),
        in_specs=[
            pl.BlockSpec((tm, d), lambda i, j: (i, 0)),
            pl.BlockSpec((d, tn), lambda i, j: (0, j)),
            pl.BlockSpec((d, tn), lambda i, j: (0, j + nj)),
            pl.BlockSpec((tn, d), lambda i, j: (j, 0)),
            pl.BlockSpec((1, d), lambda i, j: (0, 0)),
            pl.BlockSpec((1, d), lambda i, j: (0, 0)),
        ],
        out_specs=pl.BlockSpec((tm, d), lambda i, j: (i, 0)),
        out_shape=jax.ShapeDtypeStruct((m, d), F32),
        scratch_shapes=[pltpu.VMEM((tm, d), BF), pltpu.VMEM((tm, d), F32)],
        compiler_params=_params("parallel", "arbitrary"),
        name="ffn_ln",
    )(x, w_up, w_up, w_down, g.reshape(1, d), b.reshape(1, d))


def _mm_kernel(x_ref, w_ref, o_ref):
    o_ref[...] = _dot(x_ref[...].astype(BF), w_ref[...])


def _matmul(x, w, tn):
    m, d = x.shape
    n = w.shape[1]
    tm = min(512, m)
    return pl.pallas_call(
        _mm_kernel,
        grid=(m // tm, n // tn),
        in_specs=[pl.BlockSpec((tm, d), lambda i, j: (i, 0)),
                  pl.BlockSpec((d, tn), lambda i, j: (0, j))],
        out_specs=pl.BlockSpec((tm, tn), lambda i, j: (i, j)),
        out_shape=jax.ShapeDtypeStruct((m, n), F32),
        compiler_params=_params("parallel", "arbitrary"),
        name="proj",
    )(x, w)


def _proj_attn_kernel(x_ref, wq_ref, wkv_ref, wgn_ref, pq_ref, pk_ref, pkt_ref,
                      kc_ref, vc_ref, kct_ref, vct_ref, kst_ref, vst32_ref, kwt_ref, vwt32_ref, gn_ref,
                      qa_ref, ksa_ref, vst_ref, kwa_ref, vwt_ref):
    tm = x_ref.shape[0]
    xb = x_ref[...].astype(BF)
    zq = (_dot(xb, wq_ref[...]) * (HEAD_DIM ** -0.5 * np.log2(np.e))).astype(BF)
    for h in range(N_HEADS):
        pair = zq[:, (h // 2) * LANES:(h // 2 + 1) * LANES]
        qa_ref[h] = _dot(pair, pq_ref[h % 2]).astype(BF)
    zkv = _dot(xb, wkv_ref[...])
    parts = [zkv[:, n * KV_W:(n + 1) * KV_W] for n in range(6)]
    kc_ref[...] = parts[0]
    vc_ref[...] = parts[1]
    for ref, part in zip((kct_ref, vct_ref, kst_ref, vst32_ref, kwt_ref, vwt32_ref), parts):
        ref[...] = part.T
    gn_ref[...] = _dot(xb, wgn_ref[...])
    t = pl.program_id(0) * tm + lax.broadcasted_iota(jnp.int32, (tm, 1), 0)
    lane = lax.broadcasted_iota(jnp.int32, (1, LANES), 1)
    onehot = jnp.where(lane == HEAD_DIM + (t // SEL_BLOCK) % BLOCKS_PER_TILE, 1.0, 0.0)
    ksb = parts[2].astype(BF)
    kwb = parts[4].astype(BF)
    vsb = parts[3].astype(BF)
    vwb = parts[5].astype(BF)
    for k in range(KV_HEADS):
        ksa_ref[k] = (_dot(ksb, pk_ref[k]) + onehot).astype(BF)
        kwa_ref[k] = _dot(kwb, pk_ref[k]).astype(BF)
        ones_row = jnp.where(lax.broadcasted_iota(jnp.int32, (SEL_V_ROWS, 1), 0) == HEAD_DIM, 1.0, 0.0)
        vst_ref[k, 0] = (_dot_nt(pkt_ref[k], vsb) + ones_row).astype(BF)
        vwt_ref[k] = (_dot_nt(pkt_ref[k], vwb) + ones_row).astype(BF)


def _proj_attn(x, wq, wkv, wgn, pq, pk, pkt):
    m, d = x.shape
    tm = 256
    per_tile = SEL_TILE // tm
    row = lambda w: pl.BlockSpec((tm, w), lambda i: (i, 0))
    const = lambda a: pl.BlockSpec(a.shape, lambda i: (0,) * a.ndim)
    head = lambda n: pl.BlockSpec((n, tm, LANES), lambda i: (0, i, 0))
    f32 = lambda w: jax.ShapeDtypeStruct((m, w), F32)
    aug = lambda n: jax.ShapeDtypeStruct((n, m, LANES), BF)
    return pl.pallas_call(
        _proj_attn_kernel,
        grid=(m // tm,),
        in_specs=[row(d), const(wq), const(wkv), const(wgn), const(pq), const(pk), const(pkt)],
        out_specs=[row(KV_W)] * 2 + [pl.BlockSpec((KV_W, tm), lambda i: (0, i))] * 6
                  + [row(LANES), head(N_HEADS), head(KV_HEADS),
                                     pl.BlockSpec((KV_HEADS, 1, SEL_V_ROWS, tm),
                                                  lambda i: (0, i // per_tile, 0, i % per_tile)),
                                     head(KV_HEADS),
                                     pl.BlockSpec((KV_HEADS, SEL_V_ROWS, tm), lambda i: (0, 0, i))],
        out_shape=[f32(KV_W)] * 2 + [jax.ShapeDtypeStruct((KV_W, m), F32)] * 6
                  + [f32(LANES), aug(N_HEADS), aug(KV_HEADS),
                                     jax.ShapeDtypeStruct((KV_HEADS, m // SEL_TILE, SEL_V_ROWS, SEL_TILE), BF),
                                     aug(KV_HEADS),
                                     jax.ShapeDtypeStruct((KV_HEADS, SEL_V_ROWS, m), BF)],
        compiler_params=_params("parallel"),
        name="proj_attn",
    )(x, wq, wkv, wgn, pq, pk, pkt)


def _rglru_gates(xc, wa, wx, ba, bx, lam, is_start):
    xcb = xc.astype(BF)
    r = jax.nn.sigmoid(_dot(xcb, wa) + ba)
    ig = jax.nn.sigmoid(_dot(xcb, wx) + bx)
    log_a = -LRU_C * r * jax.nn.softplus(-lam)
    a = jnp.exp(log_a)
    if is_start is True:
        return a, ig * xc
    th = jnp.tanh(log_a)
    mult = jnp.sqrt(-2.0 * th / (1.0 - th))
    if is_start is not None:
        mult = jnp.where(is_start, 1.0, mult)
    return a, mult * (ig * xc)


def _rglru_prompt_kernel(x_ref, gate_ref, wc_ref, bc_ref, wa_ref, wx_ref, ba_ref, bx_ref, lam_ref,
                         y_ref, tail_ref, hl_ref, h_scr, tail_scr):
    c = pl.program_id(1)
    tc = x_ref.shape[0]

    @pl.when(c == 0)
    def _():
        h_scr[...] = jnp.zeros_like(h_scr)
        tail_scr[...] = jnp.zeros_like(tail_scr)

    x = x_ref[...]
    prev = tail_scr[...]
    row8 = lax.broadcasted_iota(jnp.int32, (SUBLANES, 1), 0)
    row = lax.broadcasted_iota(jnp.int32, (tc, 1), 0)

    def shifted(s):
        rolled = pltpu.roll(x, s, 0)
        top = jnp.where(row8 < s, pltpu.roll(prev, s, 0), rolled[:SUBLANES])
        return jnp.concatenate([top, rolled[SUBLANES:]], axis=0)

    wc = wc_ref[...]
    conv = wc[0:1] * shifted(3)
    conv = conv + wc[1:2] * shifted(2)
    conv = conv + wc[2:3] * shifted(1)
    conv = conv + wc[3:4] * x
    xc = bc_ref[...] + conv

    is_start = (row + c * tc) == 0
    a, u = _rglru_gates(xc, wa_ref[0], wx_ref[0], ba_ref[...], bx_ref[...], lam_ref[...], is_start)

    d = 1
    while d < SUBLANES:
        keep = row % SUBLANES >= d
        a_sh = jnp.where(keep, pltpu.roll(a, d, 0), 1.0)
        u_sh = jnp.where(keep, pltpu.roll(u, d, 0), 0.0)
        u = a * u_sh + u
        a = a * a_sh
        d *= 2
    carry = h_scr[...]
    groups = []
    for g in range(tc // SUBLANES):
        rows = slice(g * SUBLANES, (g + 1) * SUBLANES)
        hg = a[rows] * carry + u[rows]
        carry = hg[SUBLANES - 1:SUBLANES]
        groups.append(hg)
    h = jnp.concatenate(groups, axis=0)

    y_ref[...] = (jax.nn.gelu(gate_ref[...]) * h).astype(BF)
    h_last = h[tc - 1:tc]
    h_scr[...] = h_last
    tail_scr[...] = x[tc - SUBLANES:]
    hl_ref[...] = h_last
    tail_ref[...] = x[tc - SUBLANES:]


def _rglru_prompt(xrg, w_conv, b_conv, wa_bd, wx_bd, b_a, b_x, lam):
    t = xrg.shape[0]
    d_rnn = w_conv.shape[1]
    gw = 2 * LANES
    ng = d_rnn // gw
    tc = min(512, t)
    vec = lambda: pl.BlockSpec((1, gw), lambda g, c: (0, g))
    return pl.pallas_call(
        _rglru_prompt_kernel,
        grid=(ng, t // tc),
        in_specs=[
            pl.BlockSpec((tc, gw), lambda g, c: (c, g)),
            pl.BlockSpec((tc, gw), lambda g, c: (c, g + ng)),
            pl.BlockSpec((CONV_W, gw), lambda g, c: (0, g)),
            vec(),
            pl.BlockSpec((1, gw, gw), lambda g, c: (g, 0, 0)),
            pl.BlockSpec((1, gw, gw), lambda g, c: (g, 0, 0)),
            vec(), vec(), vec(),
        ],
        out_specs=[
            pl.BlockSpec((tc, gw), lambda g, c: (c, g)),
            pl.BlockSpec((SUBLANES, gw), lambda g, c: (0, g)),
            pl.BlockSpec((1, gw), lambda g, c: (0, g)),
        ],
        out_shape=[
            jax.ShapeDtypeStruct((t, d_rnn), BF),
            jax.ShapeDtypeStruct((SUBLANES, d_rnn), F32),
            jax.ShapeDtypeStruct((1, d_rnn), F32),
        ],
        scratch_shapes=[pltpu.VMEM((1, gw), F32), pltpu.VMEM((SUBLANES, gw), F32)],
        compiler_params=_params("parallel", "arbitrary"),
        name="rglru_prompt",
    )(xrg, xrg, w_conv, b_conv.reshape(1, -1), wa_bd, wx_bd,
      b_a.reshape(1, -1), b_x.reshape(1, -1), lam.reshape(1, -1))


def _rglru_sample_kernel(xrg_ref, cp_ref, h0_ref, wc_ref, bc_ref, wa_ref, wx_ref, ba_ref, bx_ref, lam_ref,
                         y_ref, cn_ref, hl_ref, *, n_tok, start0):
    d_rnn = h0_ref.shape[1]
    gw = wa_ref.shape[1]
    wc = wc_ref[...]
    xp = [cp_ref[:, k * d_rnn:(k + 1) * d_rnn] for k in range(CONV_W - 1)]
    xp += [xrg_ref[:, t * 2 * d_rnn:t * 2 * d_rnn + d_rnn] for t in range(n_tok)]
    h = h0_ref[...]
    for t in range(n_tok):
        conv = wc[0:1] * xp[t]
        for k in range(1, CONV_W):
            conv = conv + wc[k:k + 1] * xp[t + k]
        xc = bc_ref[...] + conv
        a_parts, u_parts = [], []
        for g in range(d_rnn // gw):
            sl = slice(g * gw, (g + 1) * gw)
            a_g, u_g = _rglru_gates(xc[:, sl], wa_ref[g], wx_ref[g], ba_ref[:, sl], bx_ref[:, sl],
                                    lam_ref[:, sl], True if (start0 and t == 0) else None)
            a_parts.append(a_g)
            u_parts.append(u_g)
        a = jnp.concatenate(a_parts, axis=1)
        u = jnp.concatenate(u_parts, axis=1)
        h = a * h + u
        gate = xrg_ref[:, t * 2 * d_rnn + d_rnn:(t + 1) * 2 * d_rnn]
        y_ref[:, t * d_rnn:(t + 1) * d_rnn] = (jax.nn.gelu(gate) * h).astype(BF)
    hl_ref[...] = h
    tail = xp[-(CONV_W - 1):]
    for k in range(CONV_W - 1):
        cn_ref[:, k * d_rnn:(k + 1) * d_rnn] = tail[k]


def _rglru_sample(xrg, conv_prev, h0, w_conv, b_conv, wa_bd, wx_bd, b_a, b_x, lam, start0):
    b, d_rnn = h0.shape
    n_tok = xrg.shape[1] // (2 * d_rnn)
    args = (xrg, conv_prev, h0, w_conv, b_conv.reshape(1, -1), wa_bd, wx_bd,
            b_a.reshape(1, -1), b_x.reshape(1, -1), lam.reshape(1, -1))
    full = lambda a: pl.BlockSpec(a.shape, lambda i: (0,) * a.ndim)
    outs = [jax.ShapeDtypeStruct((b, n_tok * d_rnn), BF),
            jax.ShapeDtypeStruct((b, (CONV_W - 1) * d_rnn), F32),
            jax.ShapeDtypeStruct((b, d_rnn), F32)]
    return pl.pallas_call(
        functools.partial(_rglru_sample_kernel, n_tok=n_tok, start0=start0),
        grid=(1,),
        in_specs=[full(a) for a in args],
        out_specs=[full(o) for o in outs],
        out_shape=outs,
        compiler_params=_params("arbitrary"),
        name="rglru_sample",
    )(*args)


def _compress_kernel(x_ref, pe_ref, w1_ref, w2_ref, pk_ref, o_ref, scr):
    x = x_ref[...]
    rows = x.shape[0]
    xb = (x.reshape(rows // SUBLANES, SUBLANES, CMP_ROW_W) + pe_ref[...][None]).reshape(rows, CMP_ROW_W)
    full = _dot(xb.astype(BF), w1_ref[...])
    hid = full[:, :KV_W] + pltpu.roll(full[:, KV_W:], rows - 1, 0)
    out = _dot(jax.nn.gelu(hid).astype(BF), w2_ref[...])
    nb = rows // 4
    ob = out.astype(BF)
    for k in range(KV_HEADS):
        scr[...] = _dot(ob, pk_ref[k])
        o_ref[k, 0] = scr[pl.ds(0, nb, stride=4), :].astype(BF)
        o_ref[k, 1] = scr[pl.ds(2, nb, stride=4), :].astype(BF)


def _compress_prompt(kv, pe2, w1, w2, pk):
    t = kv.shape[0]
    rows = 128
    x = kv.reshape(t // CMP_ROW, CMP_ROW_W)
    nb = rows // 4
    nbs = t // SEL_BLOCK
    const = lambda a: pl.BlockSpec(a.shape, lambda i: (0,) * a.ndim)
    return pl.pallas_call(
        _compress_kernel,
        grid=(x.shape[0] // rows,),
        in_specs=[pl.BlockSpec((rows, CMP_ROW_W), lambda i: (i, 0)),
                  const(pe2), const(w1), const(w2), const(pk)],
        out_specs=pl.BlockSpec((KV_HEADS, 2, nb, LANES), lambda i: (0, 0, i, 0)),
        out_shape=jax.ShapeDtypeStruct((KV_HEADS, 2, nbs, LANES), BF),
        scratch_shapes=[pltpu.VMEM((rows, LANES), F32)],
        compiler_params=_params("parallel"),
        name="compress_prompt",
    )(x, pe2, w1, w2, pk)


def _compress_paged_kernel(pt_ref, *refs, pages):
    del pt_ref
    pe_ref = refs[2 * pages]
    wk1_ref, wk2_ref, wv1_ref, wv2_ref, ok_ref, ov_ref, xk_scr, xv_scr, outk_scr, outv_scr = refs[2 * pages + 1:]
    _compress_pages(refs[:pages], pe_ref, wk1_ref, wk2_ref, ok_ref, xk_scr, outk_scr)
    _compress_pages(refs[pages:2 * pages], pe_ref, wv1_ref, wv2_ref, ov_ref, xv_scr, outv_scr)


def _compress_pages(x_refs, pe_ref, w1_ref, w2_ref, o_ref, xs_scr, out_scr):
    pages = len(x_refs)
    halves = KV_W // LANES
    for p in range(pages):
        xt = (x_refs[p][0] + pe_ref[...]).astype(BF).T
        for h in range(halves):
            xs_scr[h, p * PAGE_SIZE:(p + 1) * PAGE_SIZE, :] = xt[:, h * LANES:(h + 1) * LANES]
    nblk = pages * (PAGE_SIZE // CMP_BLOCK)
    by_tok = [pltpu.einshape("mld->lmd", xs_scr[h].reshape(nblk, CMP_BLOCK, LANES)) for h in range(halves)]
    acc = None
    for l2 in range(CMP_BLOCK // 2):
        parts = [jnp.concatenate([by_tok[h][2 * l2], by_tok[h][2 * l2 + 1]], axis=1) for h in range(halves)]
        d = _dot(jnp.concatenate(parts, axis=0).astype(BF), w1_ref[l2])
        acc = d if acc is None else acc + d
    out_scr[...] = _dot(jax.nn.gelu(acc).astype(BF), w2_ref[...])
    for h in range(halves):
        for par in range(2):
            o_ref[0, par, :, h * LANES:(h + 1) * LANES] = (
                out_scr[pl.ds(h * nblk + par, nblk // 2, stride=2), :].astype(BF))


def _compress_paged(cache_k, cache_v, page_table, pe_tok, wk1, wk2, wv1, wv2):
    b, n_pages = page_table.shape
    pages = CMP_PAGES
    view = lambda c: jnp.transpose(c, (0, 2, 3, 1)).reshape(c.shape[0], KV_W, PAGE_SIZE)
    xk, xv = view(cache_k), view(cache_v)
    nblk = pages * (PAGE_SIZE // CMP_BLOCK)
    halves = KV_W // LANES
    nbs = n_pages * PAGE_SIZE // SEL_BLOCK
    const = lambda a: pl.BlockSpec(a.shape, lambda i, s, pt: (0,) * a.ndim)
    page_specs = [
        pl.BlockSpec((1, KV_W, PAGE_SIZE),
                     functools.partial(lambda i, s, pt, p: (pt[i, s * pages + p], 0, 0), p=p))
        for p in range(pages)]
    return pl.pallas_call(
        functools.partial(_compress_paged_kernel, pages=pages),
        grid_spec=pltpu.PrefetchScalarGridSpec(
            num_scalar_prefetch=1,
            grid=(b, n_pages // pages),
            in_specs=page_specs + page_specs + [const(a) for a in (pe_tok, wk1, wk2, wv1, wv2)],
            out_specs=[pl.BlockSpec((1, 2, nblk // 2, KV_W), lambda i, s, pt: (i, 0, s, 0))] * 2,
            scratch_shapes=[pltpu.VMEM((halves, pages * PAGE_SIZE, LANES), BF)] * 2
                           + [pltpu.VMEM((halves * nblk, LANES), F32)] * 2,
        ),
        out_shape=[jax.ShapeDtypeStruct((b, 2, nbs, KV_W), BF)] * 2,
        compiler_params=_params("parallel", "arbitrary"),
        name="compress_paged",
    )(page_table, *([xk] * pages), *([xv] * pages), pe_tok, wk1, wk2, wv1, wv2)


def _topk_columns(imp_t, n_pick, quota=None):
    nblk = imp_t.shape[0]
    idx = lax.broadcasted_iota(jnp.int32, imp_t.shape, 0)
    left = imp_t
    for n in range(n_pick):
        mx = jnp.max(left, axis=0, keepdims=True)
        first = jnp.min(jnp.where(left == mx, idx, nblk), axis=0, keepdims=True)
        if quota is not None:
            first = jnp.where(n < quota, first, nblk)
        left = jnp.where(idx == first, -jnp.inf, left)
    return jnp.where((left == -jnp.inf) & (imp_t != -jnp.inf), 1.0, 0.0)


def _softmax_cols(s):
    m = jnp.max(s, axis=0, keepdims=True)
    m = jnp.where(m == -jnp.inf, 0.0, m)
    e = jnp.exp2(s - m)
    den = jnp.sum(e, axis=0, keepdims=True)
    return e * (1.0 / jnp.maximum(den, 1e-30))


def _nsa_prompt_kernel(*refs):
    (q_ref, kc_ref, vc_ref, ks_ref, vst_ref) = refs[:5]
    wb = WINDOW // Q_BLOCK + 1
    kw_refs = refs[5:5 + wb]
    vwt_refs = refs[5 + wb:5 + 2 * wb]
    (gn_ref, pa_ref, o_ref,
     bias_scr, gate_scr, sa_scr, sb_scr, m_scr, acc_scr, oc_scr, ow_scr) = refs[5 + 2 * wb:]
    kvh = pl.program_id(0)
    i = pl.program_id(1)
    nq = Q_BLOCK
    r = GROUP * nq
    hd = HEAD_DIM
    q = q_ref[...].reshape(r, LANES)
    lane = lax.broadcasted_iota(jnp.int32, (1, r), 1)
    qpos = i * nq + lane % nq

    nbs = kc_ref.shape[1] // 2

    def window_and_gates():
        kw = jnp.concatenate([ref[0] for ref in kw_refs], axis=0)
        vwt = jnp.concatenate([ref[0] for ref in vwt_refs], axis=1)
        s_w = _dot_nt(kw, q)
        nwb = len(kw_refs)
        krow = lax.broadcasted_iota(jnp.int32, (nq, 1), 0)
        parts = []
        for b in range(nwb):
            blk_pos = (i - (nwb - 1) + b) * nq
            mask = blk_pos >= 0
            if b == 0:
                mask = mask & (qpos - (blk_pos + krow) <= WINDOW)
            if b == nwb - 1:
                mask = mask & (blk_pos + krow <= qpos)
            parts.append(jnp.where(mask, s_w[b * nq:(b + 1) * nq], -jnp.inf))
        s_w = jnp.concatenate(parts, axis=0)
        e_w = jnp.exp2(s_w - jnp.max(s_w, axis=0, keepdims=True))
        acc_w = _dot(vwt, e_w.astype(BF))
        ow_scr[...] = acc_w[:hd] * (1.0 / jnp.maximum(acc_w[hd:hd + 1], 1e-30))
        gate_scr[...] = jax.nn.sigmoid(gn_ref[...]).T

    def compressed_and_select(nv, first_quarter):
        window_and_gates()
        kc = jnp.concatenate([kc_ref[0, 0:nv], kc_ref[0, nbs:nbs + nv]], axis=0)
        vc = jnp.concatenate([vc_ref[0, 0:nv], vc_ref[0, nbs:nbs + nv]], axis=0)
        rowc = lax.broadcasted_iota(jnp.int32, (2 * nv, 1), 0)
        blk = 2 * (rowc % nv) + rowc // nv
        p_c = _softmax_cols(jnp.where(blk * CMP_BLOCK + (CMP_BLOCK - 1) <= qpos, _dot_nt(kc, q), -jnp.inf))
        vct = vc.astype(F32).T[:hd].astype(BF)
        oc_scr[...] = _dot(vct, p_c.astype(BF))
        imp = p_c[:, 0:nq]
        for g in range(1, GROUP):
            imp = imp + p_c[:, g * nq:(g + 1) * nq]
        imp = imp[:nv] + imp[nv:]
        qp = i * nq + lax.broadcasted_iota(jnp.int32, (1, nq), 1)
        jb = lax.broadcasted_iota(jnp.int32, (nv, 1), 0)
        cur = qp // SEL_BLOCK
        valid = jb * SEL_BLOCK <= qp
        forced = (jb == 0) | (jb == cur) | (jb == cur - 1)
        n_forced = 1 + jnp.where(cur >= 1, 1, 0) + jnp.where(cur >= 2, 1, 0)
        n_sel = min(N_SEL, nbs)
        n_iter = n_sel - (1 if first_quarter else 3)
        picks = _topk_columns(jnp.where(valid & ~forced, imp, -jnp.inf), n_iter, n_sel - n_forced)
        sel = jnp.where(forced, 1.0, picks)
        bias = jnp.where(valid & (sel > 0.5), 0.0, NEG).T.astype(BF)
        nt = nv // BLOCKS_PER_TILE
        bias_all = _dot(bias, pa_ref[0:nv, 0:nt * LANES]).astype(BF)
        for t in range(nt):
            bias_scr[t] = bias_all[:, t * LANES:(t + 1) * LANES]

    n_var = 4
    variant = (i * n_var) // pl.num_programs(1)
    for v in range(n_var):
        pl.when(variant == v)(functools.partial(compressed_and_select, nbs * (v + 1) // n_var, v == 0))
    o_c = oc_scr[...]

    def scores(t):
        k0 = pl.multiple_of(t * SEL_TILE, SEL_TILE)
        kt = ks_ref[0, pl.ds(k0, SEL_TILE), :]
        qa = q + jnp.concatenate([bias_scr[t]] * GROUP, axis=0)
        return _dot_nt(kt, qa)

    def update(t, s_ref, diagonal):
        vt = vst_ref[0, t]
        cw = 2 * LANES
        for c in range(r // cw):
            sl = slice(c * cw, (c + 1) * cw)
            sg = s_ref[:, sl]
            if diagonal:
                kpos = t * SEL_TILE + lax.broadcasted_iota(jnp.int32, (SEL_TILE, 1), 0)
                sg = jnp.where(kpos <= qpos[:, sl], sg, NEG)
            m_old = m_scr[:, sl]
            m_new = jnp.maximum(m_old, jnp.max(sg, axis=0, keepdims=True))
            p = jnp.exp2(sg - m_new)
            acc_scr[:, sl] = jnp.exp2(m_old - m_new) * acc_scr[:, sl] + _dot(vt, p.astype(BF))
            m_scr[:, sl] = m_new

    def pair(t0):
        sb_scr[...] = scores(t0 + 1)
        update(t0, sa_scr, False)
        sa_scr[...] = scores(t0 + 2)
        update(t0 + 1, sb_scr, False)

    def octet(u, _):
        for n in range(4):
            pair(8 * u + 2 * n)
        return 0

    t_diag = (i * nq) // SEL_TILE
    m_scr[...] = jnp.full(m_scr.shape, NEG, F32)
    acc_scr[...] = jnp.zeros_like(acc_scr)
    sa_scr[...] = scores(0)
    lax.fori_loop(0, t_diag // 8, octet, 0)

    @pl.when(t_diag % 8 >= 4)
    def _():
        pair(t_diag - t_diag % 8)
        pair(t_diag - t_diag % 8 + 2)

    @pl.when(t_diag % 4 >= 2)
    def _():
        pair(t_diag - t_diag % 4)

    @pl.when(t_diag % 2 == 1)
    def _():
        sb_scr[...] = scores(t_diag)
        update(t_diag - 1, sa_scr, False)
        update(t_diag, sb_scr, True)

    @pl.when(t_diag % 2 == 0)
    def _():
        update(t_diag, sa_scr, True)

    o_s = acc_scr[:hd] * (1.0 / acc_scr[hd:hd + 1])

    o_w = ow_scr[...]

    def gate(br):
        return jnp.concatenate(
            [gate_scr[pl.ds(br * N_HEADS + kvh * GROUP + g, 1), :] for g in range(GROUP)], axis=1)

    comb = (gate(0) * o_c + gate(1) * o_s + gate(2) * o_w).astype(BF)
    o_ref[...] = jnp.concatenate([comb[:, g * nq:(g + 1) * nq] for g in range(GROUP)], axis=0)


def _nsa_prompt(qa, kc, vc, ksa, vst, kwa, vwt, gn, pall):
    t = qa.shape[1]
    n_tiles = t // SEL_TILE
    wb = WINDOW // Q_BLOCK + 1
    kv_full = lambda a: pl.BlockSpec((1,) + a.shape[1:], lambda k, i: (k,) + (0,) * (a.ndim - 1))
    wblk = lambda i, m: jnp.maximum(i - (wb - 1) + m, 0)
    win_k = [pl.BlockSpec((1, Q_BLOCK, LANES), functools.partial(lambda k, i, m: (k, wblk(i, m), 0), m=m))
             for m in range(wb)]
    win_v = [pl.BlockSpec((1, SEL_V_ROWS, Q_BLOCK), functools.partial(lambda k, i, m: (k, 0, wblk(i, m)), m=m))
             for m in range(wb)]
    const = lambda a: pl.BlockSpec(a.shape, lambda k, i: (0,) * a.ndim)
    return pl.pallas_call(
        _nsa_prompt_kernel,
        grid=(KV_HEADS, t // Q_BLOCK),
        in_specs=[pl.BlockSpec((GROUP, Q_BLOCK, LANES), lambda k, i: (k, i, 0)),
                  kv_full(kc), kv_full(vc), kv_full(ksa), kv_full(vst)]
                 + win_k + win_v
                 + [pl.BlockSpec((Q_BLOCK, LANES), lambda k, i: (i, 0)), const(pall)],
        out_specs=pl.BlockSpec((KV_W, Q_BLOCK), lambda k, i: (k, i)),
        out_shape=jax.ShapeDtypeStruct((ATTN_W, t), BF),
        scratch_shapes=[pltpu.VMEM((n_tiles, Q_BLOCK, LANES), BF), pltpu.VMEM((LANES, Q_BLOCK), F32),
                        pltpu.VMEM((SEL_TILE, GROUP * Q_BLOCK), F32), pltpu.VMEM((SEL_TILE, GROUP * Q_BLOCK), F32),
                        pltpu.VMEM((1, GROUP * Q_BLOCK), F32),
                        pltpu.VMEM((SEL_V_ROWS, GROUP * Q_BLOCK), F32),
                        pltpu.VMEM((HEAD_DIM, GROUP * Q_BLOCK), F32), pltpu.VMEM((HEAD_DIM, GROUP * Q_BLOCK), F32)],
        compiler_params=_params("parallel", "arbitrary"),
        name="nsa_prompt",
    )(qa, kc, vc, ksa, vst, *([kwa] * wb), *([vwt] * wb), gn, pall)


def _nsa_sample_kernel(*refs, pages, past, n_tok):
    pt_ref = refs[0]
    q_ref, kc_ref, vc_ref, ck_ref, cv_ref = refs[1:6]
    (ksn_ref, vsn_ref, kwn_ref, vwn_ref, wk_ref, wv_ref, gn_ref,
     pg_ref, rp_ref, eg_ref, o_ref, q_scr, bias_scr, m_scr, l_scr, acc_scr, oc_scr,
     kbuf, vbuf, ksem, vsem) = refs[6:]
    s_id = pl.program_id(1)
    n_steps = pl.num_programs(1)

    def page_copies(seq, step, slot):
        out = []
        for p in range(pages):
            page = pt_ref[seq, step * pages + p]
            out.append(pltpu.make_async_copy(ck_ref.at[page], kbuf.at[slot, p], ksem.at[slot]))
            out.append(pltpu.make_async_copy(cv_ref.at[page], vbuf.at[slot, p], vsem.at[slot]))
        return out

    n_glob = pl.program_id(0) * n_steps + s_id
    slot = n_glob % 2

    @pl.when(n_glob == 0)
    def _():
        for c in page_copies(0, 0, 0):
            c.start()

    @pl.when(n_glob + 1 < pl.num_programs(0) * n_steps)
    def _():
        nxt = n_glob + 1
        for c in page_copies(nxt // n_steps, nxt % n_steps, 1 - slot):
            c.start()

    for c in page_copies(pl.program_id(0), s_id, slot):
        c.wait()
    tp = SUBLANES
    r = N_HEADS * tp
    row = lax.broadcasted_iota(jnp.int32, (r, 1), 0)
    qpos = past + row % tp
    nbs = bias_scr.shape[1]

    @pl.when(s_id == 0)
    def _():
        qb = q_ref[0].astype(BF)
        for h in range(N_HEADS):
            k = h // GROUP
            q_scr[h * tp:(h + 1) * tp, :] = _dot(qb[:, k * KV_W:(k + 1) * KV_W], pg_ref[h]).astype(BF)
        q = q_scr[...]
        kc = kc_ref[0].reshape(2 * nbs, KV_W)
        vc = vc_ref[0].reshape(2 * nbs, KV_W)
        col = lax.broadcasted_iota(jnp.int32, (1, 2 * nbs), 1)
        blk = 2 * (col % nbs) + col // nbs
        p_c = _masked_softmax(_dot_nt(q, kc), blk * CMP_BLOCK + (CMP_BLOCK - 1) <= qpos)
        oc_scr[...] = _dot(p_c.astype(BF), vc)
        imps = []
        for k in range(KV_HEADS):
            base = k * GROUP * tp
            imp = p_c[base:base + tp]
            for g in range(1, GROUP):
                imp = imp + p_c[base + g * tp:base + (g + 1) * tp]
            imps.append(imp)
        imp = jnp.concatenate(imps, axis=0)
        imp = imp[:, :nbs] + imp[:, nbs:]
        jb = lax.broadcasted_iota(jnp.int32, (1, nbs), 1)
        imp = jnp.where((jb == 0) | (jb == nbs - 1), jnp.inf, imp)
        idx = lax.broadcasted_iota(jnp.int32, imp.shape, 1)
        sel = jnp.zeros(imp.shape, F32)
        for _ in range(min(N_SEL, nbs + 1) - 1):
            mx = jnp.max(imp, axis=1, keepdims=True)
            first = jnp.min(jnp.where(imp == mx, idx, nbs), axis=1, keepdims=True)
            hit = idx == first
            sel = jnp.where(hit, 1.0, sel)
            imp = jnp.where(hit, -jnp.inf, imp)
        bias = jnp.where(sel > 0.5, 0.0, NEG).astype(BF)
        for k in range(KV_HEADS):
            for g in range(GROUP):
                h = k * GROUP + g
                bias_scr[h * tp:(h + 1) * tp, :] = bias[k * tp:(k + 1) * tp]
        m_scr[...] = jnp.full(m_scr.shape, NEG, F32)
        l_scr[...] = jnp.zeros_like(l_scr)
        acc_scr[...] = jnp.zeros_like(acc_scr)

    def online(st, v, v_transposed):
        m = m_scr[...]
        m_new = jnp.maximum(m, jnp.max(st, axis=1, keepdims=True))
        scale = jnp.exp(m - m_new)
        p = jnp.exp(st - m_new)
        l_scr[...] = scale * l_scr[...] + jnp.sum(p, axis=1, keepdims=True)
        pv = _dot_nt(p.astype(BF), v) if v_transposed else _dot(p.astype(BF), v)
        acc_scr[...] = scale * acc_scr[...] + pv
        m_scr[...] = m_new

    q = q_scr[...]
    kt = jnp.concatenate([kbuf[slot, p].astype(BF) for p in range(pages)], axis=1)
    vt = jnp.concatenate([vbuf[slot, p].astype(BF) for p in range(pages)], axis=1)
    n_keys = kt.shape[1]
    jb = lax.broadcasted_iota(jnp.int32, (nbs, 1), 0)
    key_blk = s_id * (n_keys // SEL_BLOCK) + lax.broadcasted_iota(jnp.int32, (1, n_keys), 1) // SEL_BLOCK
    expand = jnp.where(jb == key_blk, 1.0, 0.0).astype(BF)
    online(_dot(q, kt) + _dot(bias_scr[...], expand), vt, True)

    @pl.when(s_id == pl.num_programs(1) - 1)
    def _():
        zeros = jnp.zeros((LANES - tp, KV_W), BF)
        tok = lax.broadcasted_iota(jnp.int32, (1, LANES), 1)
        ksn = jnp.concatenate([ksn_ref[0].astype(BF), zeros], axis=0)
        vsn = jnp.concatenate([vsn_ref[0].astype(BF), zeros], axis=0)
        online(jnp.where((past + tok <= qpos) & (tok < n_tok), _dot_nt(q, ksn), NEG), vsn, False)
        o_s = acc_scr[...] / l_scr[...]
        kw = jnp.concatenate([wk_ref[0].astype(BF), kwn_ref[0].astype(BF), zeros], axis=0)
        vw = jnp.concatenate([wv_ref[0].astype(BF), vwn_ref[0].astype(BF), zeros], axis=0)
        n_w = kw.shape[0]
        wcol = lax.broadcasted_iota(jnp.int32, (1, n_w), 1)
        kwpos = past - WINDOW + wcol
        dpos = qpos - kwpos
        mask_w = (dpos >= 0) & (dpos <= WINDOW) & (kwpos >= 0) & (wcol < WINDOW + n_tok)
        p_w = _masked_softmax(_dot_nt(q, kw), mask_w)
        o_w = _dot(p_w.astype(BF), vw)
        sg = jax.nn.sigmoid(gn_ref[0])
        gexp = _expand_f32(sg, eg_ref[...])

        def gate(br):
            return jnp.concatenate(
                [gexp[:, (br * N_HEADS + h) * KV_W:(br * N_HEADS + h + 1) * KV_W] for h in range(N_HEADS)],
                axis=0)

        comb = (gate(0) * oc_scr[...] + gate(1) * o_s + gate(2) * o_w).astype(BF)
        for k in range(KV_HEADS):
            out = None
            for g in range(GROUP):
                h = k * GROUP + g
                part = _dot(comb[h * tp:(h + 1) * tp], rp_ref[h])
                out = part if out is None else out + part
            o_ref[0, :, k * KV_W:(k + 1) * KV_W] = out


def _nsa_sample(page_table, q8, kc, vc, cache_ks, cache_vs, new8, win_k, win_v, gn8, pgk, rkg, egate, n_tok):
    b, n_pages = page_table.shape
    past = n_pages * PAGE_SIZE
    pages = SEL_PAGES
    tp = SUBLANES
    r = N_HEADS * tp
    nbs = past // SEL_BLOCK
    ck = jnp.transpose(cache_ks, (0, 2, 3, 1)).reshape(cache_ks.shape[0], KV_W, PAGE_SIZE)
    cv = jnp.transpose(cache_vs, (0, 2, 3, 1)).reshape(cache_vs.shape[0], KV_W, PAGE_SIZE)
    per_seq = lambda a: pl.BlockSpec((1,) + a.shape[1:], lambda i, s, pt: (i,) + (0,) * (a.ndim - 1))
    const = lambda a: pl.BlockSpec(a.shape, lambda i, s, pt: (0,) * a.ndim)
    hbm = pl.BlockSpec(memory_space=pl.ANY)
    return pl.pallas_call(
        functools.partial(_nsa_sample_kernel, pages=pages, past=past, n_tok=n_tok),
        grid_spec=pltpu.PrefetchScalarGridSpec(
            num_scalar_prefetch=1,
            grid=(b, n_pages // pages),
            in_specs=[per_seq(q8), per_seq(kc), per_seq(vc), hbm, hbm]
                     + [per_seq(a) for a in new8] + [per_seq(win_k), per_seq(win_v), per_seq(gn8),
                                                     const(pgk), const(rkg), const(egate)],
            out_specs=pl.BlockSpec((1, tp, ATTN_W), lambda i, s, pt: (i, 0, 0)),
            scratch_shapes=[pltpu.VMEM((r, KV_W), BF), pltpu.VMEM((r, nbs), BF),
                            pltpu.VMEM((r, 1), F32), pltpu.VMEM((r, 1), F32),
                            pltpu.VMEM((r, KV_W), F32), pltpu.VMEM((r, KV_W), F32),
                            pltpu.VMEM((2, pages, KV_W, PAGE_SIZE), F32),
                            pltpu.VMEM((2, pages, KV_W, PAGE_SIZE), F32),
                            pltpu.SemaphoreType.DMA((2,)), pltpu.SemaphoreType.DMA((2,))],
        ),
        out_shape=jax.ShapeDtypeStruct((b, tp, ATTN_W), F32),
        compiler_params=_params("arbitrary", "arbitrary"),
        name="nsa_sample",
    )(page_table, q8, kc, vc, ck, cv, *new8, win_k, win_v, gn8, pgk, rkg, egate)


def _merge_kernel(x_ref, y_ref, o_ref, wgr_ref, wga_ref, wr_ref, wa_ref, wo_ref, g_ref, b_ref,
                  out_ref, xb_ref, acc_ref, *ob_ref):
    j = pl.program_id(1)

    @pl.when(j == 0)
    def _():
        xb_ref[...] = x_ref[...].astype(BF)
        acc_ref[...] = jnp.zeros_like(acc_ref)
        if ob_ref:
            ob_ref[0][...] = o_ref[...].astype(F32).T.astype(BF)

    xb = xb_ref[...]
    rec = _dot(y_ref[...], wr_ref[...])
    att = _dot(ob_ref[0][...] if ob_ref else o_ref[...], wa_ref[...])
    u = jax.nn.sigmoid(_dot(xb, wgr_ref[...])) * rec + jax.nn.sigmoid(_dot(xb, wga_ref[...])) * att
    acc_ref[...] += _dot(u.astype(BF), wo_ref[...])

    @pl.when(j == pl.num_programs(1) - 1)
    def _():
        out_ref[...] = _layer_norm(ALPHA * x_ref[...] + acc_ref[...], g_ref[...], b_ref[...])


def _merge(x, y_rec, o_attn, w_grec, w_gatt, w_rec_o, w_attn_o, w_out, g, b, o_transposed):
    m, d = x.shape
    tm = min(512, m)
    tn = 512
    col = lambda a: pl.BlockSpec((a.shape[0], tn), lambda i, j: (0, j))
    rowb = lambda a: pl.BlockSpec((tm, a.shape[1]), lambda i, j: (i, 0))
    vec = pl.BlockSpec((1, d), lambda i, j: (0, 0))
    if o_transposed:
        o_spec = pl.BlockSpec((o_attn.shape[0], tm), lambda i, j: (0, i))
        o_scratch = [pltpu.VMEM((tm, o_attn.shape[0]), BF)]
    else:
        o_spec, o_scratch = rowb(o_attn), []
    return pl.pallas_call(
        _merge_kernel,
        grid=(m // tm, d // tn),
        in_specs=[rowb(x), rowb(y_rec), o_spec, col(w_grec), col(w_gatt), col(w_rec_o), col(w_attn_o),
                  pl.BlockSpec((tn, d), lambda i, j: (j, 0)), vec, vec],
        out_specs=pl.BlockSpec((tm, d), lambda i, j: (i, 0)),
        out_shape=jax.ShapeDtypeStruct((m, d), F32),
        scratch_shapes=[pltpu.VMEM((tm, d), BF), pltpu.VMEM((tm, d), F32)] + o_scratch,
        compiler_params=_params("parallel", "arbitrary"),
        name="merge_out",
    )(x, y_rec, o_attn, w_grec, w_gatt, w_rec_o, w_attn_o, w_out, g.reshape(1, d), b.reshape(1, d))


def _place(n_rows, n_cols, src0, dst0, width, value=1.0):
    m = np.zeros((n_rows, n_cols), np.float32)
    m[src0 + np.arange(width), dst0 + np.arange(width)] = value
    return m


def _layout_constants(n_sel_blocks_prompt):
    hd = HEAD_DIM
    scale = hd ** -0.5
    pq = np.stack([_place(LANES, LANES, e * hd, 0, hd) for e in range(2)])
    pk = np.stack([_place(KV_W, LANES, k * hd, 0, hd) for k in range(KV_HEADS)])
    pkt = np.transpose(pk, (0, 2, 1))[:, :SEL_V_ROWS]
    nbs = n_sel_blocks_prompt
    n_tiles = -(-nbs // BLOCKS_PER_TILE)
    pall = np.zeros((nbs, n_tiles * LANES), np.float32)
    j = np.arange(nbs)
    pall[j, (j // BLOCKS_PER_TILE) * LANES + hd + j % BLOCKS_PER_TILE] = 1.0
    pgk = np.stack([_place(KV_W, KV_W, (h % GROUP) * hd, (h // GROUP) * hd, hd, scale) for h in range(N_HEADS)])
    rkg = np.stack([_place(KV_W, KV_W, (h // GROUP) * hd, (h % GROUP) * hd, hd) for h in range(N_HEADS)])
    egs = np.zeros((LANES, N_NSA_BRANCH * N_HEADS * KV_W), np.float32)
    for c in range(N_NSA_BRANCH * N_HEADS):
        egs[c, c * KV_W:(c + 1) * KV_W] = 1.0
    as_bf = lambda a: jnp.asarray(a, BF)
    return dict(pq=as_bf(pq), pk=as_bf(pk), pkt=as_bf(pkt), pall=as_bf(pall),
                pgk=as_bf(pgk), rkg=as_bf(rkg), egs=as_bf(egs))


def _block_diag(w, per_group):
    nb, c, _ = w.shape
    eye = jnp.eye(per_group, dtype=w.dtype)
    wg = w.reshape(nb // per_group, per_group, c, c)
    return jnp.einsum('gpcd,pq->gpcqd', wg, eye).reshape(nb // per_group, per_group * c, per_group * c)


def _compress_weights(w1, w2, pe):
    eye = jnp.eye(KV_HEADS, dtype=w1.dtype)
    big = jnp.einsum('lde,kq->lkdqe', w1, eye).reshape(CMP_BLOCK * KV_W, KV_HEADS * w1.shape[2])
    half = big.shape[0] // 2
    w1cat = jnp.concatenate([big[:half], big[half:]], axis=1).astype(BF)
    w2bd = _block_diag(jnp.broadcast_to(w2, (KV_HEADS,) + w2.shape), KV_HEADS)[0].astype(BF)
    pe_flat = jnp.broadcast_to(pe[:, None, :], (CMP_BLOCK, KV_HEADS, HEAD_DIM)).reshape(2, half)
    pe2 = jnp.tile(pe_flat, (SUBLANES // 2, 1))
    return w1cat, w2bd, pe2


def _compress_weights_paged(w1, w2, pe):
    per_half = LANES // HEAD_DIM
    eye = jnp.eye(per_half, dtype=w1.dtype)
    w1h = jnp.einsum('lde,kq->lkdqe', w1, eye).reshape(CMP_BLOCK, LANES, LANES)
    w1p = w1h.reshape(CMP_BLOCK // 2, 2 * LANES, LANES).astype(BF)
    w2p = jnp.einsum('ed,kq->keqd', w2, eye).reshape(LANES, LANES).astype(BF)
    pe_tok = jnp.tile(pe, (PAGE_SIZE // CMP_BLOCK, KV_HEADS)).T
    return w1p, w2p, pe_tok


def kernel(x_prompt, x_sample, cache_k_cmp, cache_v_cmp, cache_k_sel, cache_v_sel, page_table,
           state_win_k, state_win_v, state_conv, state_h,
           ln1_g, ln1_b, w_ffn1_up, w_ffn1_down, w_in, w_conv, b_conv, w_rg_a, b_rg_a, w_rg_x, b_rg_x,
           rg_lambda, cmp_pe, w_ck1, w_ck2, w_cv1, w_cv2, w_rec_o, w_attn_o, w_out,
           ln2_g, ln2_b, w_ffn2_up, w_ffn2_down, ln3_g, ln3_b):
    bp, t, d = x_prompt.shape
    bs, n_tok, _ = x_sample.shape
    d_rnn = w_conv.shape[1]
    assert bp == 1 and t % (128 * CMP_ROW) == 0 and t >= WINDOW
    past = page_table.shape[1] * PAGE_SIZE
    assert page_table.shape[1] % max(CMP_PAGES, SEL_PAGES) == 0 and CONV_W - 1 <= n_tok <= SUBLANES
    assert state_win_k.shape[1] == WINDOW

    o_q = 2 * d_rnn
    o_kv = o_q + ATTN_W
    o_gn = o_kv + 6 * KV_W
    o_gr = o_gn + N_NSA_BRANCH * N_HEADS
    o_ga = o_gr + d
    w_in_b = w_in.astype(BF)
    w_rnn = w_in_b[:, :o_q]
    w_q = w_in_b[:, o_q:o_kv]
    w_kv = w_in_b[:, o_kv:o_gn]
    w_gn = jnp.pad(w_in_b[:, o_gn:o_gr], ((0, 0), (0, LANES - (o_gr - o_gn))))
    w_grec = w_in_b[:, o_gr:o_ga]
    w_gatt = w_in_b[:, o_ga:]
    w_attn_all = jnp.concatenate([w_in_b[:, :o_gn], jnp.pad(w_gn, ((0, 0), (0, 512 - LANES)))], axis=1)
    up1, down1 = w_ffn1_up.astype(BF), w_ffn1_down.astype(BF)
    up2, down2 = w_ffn2_up.astype(BF), w_ffn2_down.astype(BF)
    per_group = 2 * LANES // (d_rnn // RNN_BLOCKS)
    wa_bd = _block_diag(w_rg_a, per_group).astype(BF)
    wx_bd = _block_diag(w_rg_x, per_group).astype(BF)
    ck1, ck2, pe2 = _compress_weights(w_ck1, w_ck2, cmp_pe)
    cv1, cv2, _ = _compress_weights(w_cv1, w_cv2, cmp_pe)
    w_rec_b, w_attn_b, w_out_b = w_rec_o.astype(BF), w_attn_o.astype(BF), w_out.astype(BF)
    cst = _layout_constants(t // SEL_BLOCK)

    xp = _ffn_ln(x_prompt.reshape(t, d), up1, down1, ln1_g, ln1_b)
    xrg = _matmul(xp, w_rnn, w_rnn.shape[1])
    (p_kc, p_vc, kct, vct, kst, vst32, kwt, vwt32, p_gn, qa, ksa, vst, kwa, vwt) = _proj_attn(
        xp, w_q, w_kv, w_gn, cst['pq'], cst['pk'], cst['pkt'])
    y_rec, p_tail, p_h = _rglru_prompt(xrg, w_conv, b_conv, wa_bd, wx_bd, b_rg_a, b_rg_x, rg_lambda)
    nbs = t // SEL_BLOCK
    kc = _compress_prompt(p_kc, pe2, ck1, ck2, cst['pk']).reshape(KV_HEADS, 2 * nbs, LANES)
    vc = _compress_prompt(p_vc, pe2, cv1, cv2, cst['pk']).reshape(KV_HEADS, 2 * nbs, LANES)
    o_attn_t = _nsa_prompt(qa, kc, vc, ksa, vst, kwa, vwt, p_gn, cst['pall'])
    x2 = _merge(xp, y_rec, o_attn_t, w_grec, w_gatt, w_rec_b, w_attn_b, w_out_b, ln2_g, ln2_b, True)
    y_prompt = _ffn_ln(x2, up2, down2, ln3_g, ln3_b).reshape(bp, t, d)

    kvh = lambda a: jnp.transpose(a.reshape(bp, KV_HEADS, HEAD_DIM, -1), (0, 3, 1, 2))
    p_states = (kvh(kct), kvh(vct), kvh(kst), kvh(vst32), kvh(kwt[:, t - WINDOW:]), kvh(vwt32[:, t - WINDOW:]),
                p_tail[SUBLANES - (CONV_W - 1):].reshape(bp, CONV_W - 1, d_rnn), p_h.reshape(bp, d_rnn))

    m_s = bs * n_tok
    xs = _ffn_ln(x_sample.reshape(m_s, d), up1, down1, ln1_g, ln1_b)
    zs = _matmul(xs, w_attn_all, 512)
    s_xrg = zs[:, :o_q].reshape(bs, n_tok * o_q)
    seq = lambda a: a.reshape(bs, n_tok, a.shape[-1])
    s_q = seq(zs[:, o_q:o_kv])
    s_kv = [seq(zs[:, o_kv + n * KV_W:o_kv + (n + 1) * KV_W]) for n in range(6)]
    s_gn = seq(zs[:, o_gn:o_gn + LANES])
    pad8 = lambda a: jnp.pad(a, ((0, 0), (0, SUBLANES - n_tok), (0, 0)))
    ys_rec, s_conv, s_h = _rglru_sample(
        s_xrg, state_conv.reshape(bs, -1), state_h, w_conv, b_conv, wa_bd, wx_bd, b_rg_a, b_rg_x, rg_lambda,
        start0=(past == 0))
    ck1p, ck2p, pe_tok = _compress_weights_paged(w_ck1, w_ck2, cmp_pe)
    cv1p, cv2p, _ = _compress_weights_paged(w_cv1, w_cv2, cmp_pe)
    kc_s, vc_s = _compress_paged(cache_k_cmp, cache_v_cmp, page_table, pe_tok, ck1p, ck2p, cv1p, cv2p)
    o8 = _nsa_sample(page_table, pad8(s_q), kc_s, vc_s, cache_k_sel, cache_v_sel,
                     [pad8(s_kv[n]) for n in (2, 3, 4, 5)],
                     state_win_k.reshape(bs, WINDOW, KV_W), state_win_v.reshape(bs, WINDOW, KV_W),
                     pad8(s_gn), cst['pgk'], cst['rkg'], cst['egs'], n_tok)
    os_attn = o8[:, :n_tok].reshape(m_s, ATTN_W).astype(BF)
    x2s = _merge(xs, ys_rec.reshape(m_s, d_rnn), os_attn, w_grec, w_gatt, w_rec_b, w_attn_b, w_out_b, ln2_g, ln2_b,
                 False)
    y_sample = _ffn_ln(x2s, up2, down2, ln3_g, ln3_b).reshape(bs, n_tok, d)

    kvs = lambda a: a.reshape(bs, n_tok, KV_HEADS, HEAD_DIM)
    win = lambda old, new: jnp.concatenate([old, kvs(new)], axis=1)[:, -WINDOW:]
    s_states = (kvs(s_kv[0]), kvs(s_kv[1]), kvs(s_kv[2]), kvs(s_kv[3]),
                win(state_win_k, s_kv[4]), win(state_win_v, s_kv[5]),
                s_conv.reshape(bs, CONV_W - 1, d_rnn), s_h)

    return (y_prompt, y_sample) + p_states + s_states
```
